```python
import math
import jax
import jax.numpy as jnp
from jax import lax
import numpy as np

D_MODEL = 1024
BATCH = 4
SEQ = 4096
DEPTH = 2
DEC_BATCH = 128
DEC_SEQ = 4
PAST_LEN = 2048
PAGE_SIZE = 128

SSM_WIDTH = 256
SSM_GROUP = 16
SSM_GROUPS = SSM_WIDTH // SSM_GROUP
SSM_STATE = 64
ATT_HEADS = 4
ATT_HEAD_DIM = 64
ATT_V_DIM = 2 * ATT_HEAD_DIM
QK_WIDTH = ATT_HEADS * 2 * ATT_HEAD_DIM
ATT_WIDTH = ATT_HEADS * ATT_V_DIM
Q_BLOCK = 128
CONV_WIDTH = 256
CONV_K = 3
N_BRANCH = 3
IN_SIZES = (SSM_WIDTH, QK_WIDTH, QK_WIDTH, ATT_WIDTH, CONV_WIDTH, CONV_WIDTH, CONV_WIDTH, N_BRANCH * D_MODEL)
IN_COLS = sum(IN_SIZES)
IN_SPLITS = tuple(int(s) for s in np.cumsum(IN_SIZES)[:-1])
D_FF = 2816
N_EXPERTS = 8
TOP_K = 2
D_FF_EXPERT = 3584
MOE_BLOCK = 128
N_DENSE = (DEPTH + 1) // 2
N_MOE = DEPTH // 2
EPS = 1e-6
NEG_INF = -1e30

kernel_name = "hybrid_s5_diffattn_shortconv_gated_decode_step"


def rmsnorm(x, g):
    xf = x.astype(jnp.float32)
    y = xf * lax.rsqrt(jnp.mean(xf * xf, axis=-1, keepdims=True) + EPS)
    return (y * g.astype(jnp.float32)).astype(x.dtype)


def swiglu(h, w1, w3, w2):
    return (jax.nn.silu(h @ w1) * (h @ w3)) @ w2


def moe_ffn(h, router_w, w1, w3, w2):
    lead = h.shape[:-1]
    t = h.reshape(-1, D_MODEL)
    n_tok = t.shape[0]
    n_assign = n_tok * TOP_K
    logits = t.astype(jnp.float32) @ router_w.astype(jnp.float32)
    top_logit, top_idx = lax.top_k(logits, TOP_K)
    gate = jax.nn.softmax(top_logit, axis=-1)
    e_flat = top_idx.reshape(-1)
    tok_flat = jnp.arange(n_assign, dtype=jnp.int32) // TOP_K
    g_flat = gate.reshape(-1)
    order = jnp.argsort(e_flat)
    e_sorted = e_flat[order]
    tok_sorted = tok_flat[order]
    g_sorted = g_flat[order]
    counts = jnp.bincount(e_flat, length=N_EXPERTS)
    starts = jnp.cumsum(counts) - counts
    padded = (counts + MOE_BLOCK - 1) // MOE_BLOCK * MOE_BLOCK
    pad_ends = jnp.cumsum(padded)
    pad_starts = pad_ends - padded
    dest = pad_starts[e_sorted] + (jnp.arange(n_assign) - starts[e_sorted])
    n_blocks = -(-(n_assign + N_EXPERTS * (MOE_BLOCK - 1)) // MOE_BLOCK)
    cap = n_blocks * MOE_BLOCK
    slot_tok = jnp.full((cap,), n_tok, jnp.int32).at[dest].set(tok_sorted)
    slot_gate = jnp.zeros((cap,), jnp.float32).at[dest].set(g_sorted)
    block_expert = jnp.minimum(
        jnp.searchsorted(pad_ends, jnp.arange(n_blocks) * MOE_BLOCK, side="right"), N_EXPERTS - 1)
    t_pad = jnp.concatenate([t, jnp.zeros((1, D_MODEL), t.dtype)], axis=0)

    def expert_block(args):
        tok, e = args
        return swiglu(t_pad[tok], w1[e], w3[e], w2[e])

    out = lax.map(expert_block, (slot_tok.reshape(n_blocks, MOE_BLOCK), block_expert))
    out = out.reshape(cap, D_MODEL) * slot_gate[:, None].astype(out.dtype)
    y = jnp.zeros((n_tok + 1, D_MODEL), out.dtype).at[slot_tok].add(out)[:n_tok]
    return y.reshape(*lead, D_MODEL)


def ssm_discretize(a_re, a_im, log_dt, b_re, b_im):
    a_re = a_re.astype(jnp.float32)
    a_im = a_im.astype(jnp.float32)
    b_re = b_re.astype(jnp.float32)
    b_im = b_im.astype(jnp.float32)
    dt = jnp.exp(log_dt.astype(jnp.float32))[:, None]
    mag = jnp.exp(a_re * dt)
    ab_re = mag * jnp.cos(a_im * dt)
    ab_im = mag * jnp.sin(a_im * dt)
    den = a_re * a_re + a_im * a_im
    nr = ab_re - 1.0
    c_re = (nr * a_re + ab_im * a_im) / den
    c_im = (ab_im * a_re - nr * a_im) / den
    bb_re = c_re[..., None] * b_re - c_im[..., None] * b_im
    bb_im = c_re[..., None] * b_im + c_im[..., None] * b_re
    return ab_re, ab_im, bb_re, bb_im


def complex_affine_combine(e1, e2):
    a1r, a1i, b1r, b1i = e1
    a2r, a2i, b2r, b2i = e2
    return (a2r * a1r - a2i * a1i,
            a2r * a1i + a2i * a1r,
            a2r * b1r - a2i * b1i + b2r,
            a2r * b1i + a2i * b1r + b2i)


def s5_mixer(u, s0_re, s0_im, a_re, a_im, log_dt, b_re, b_im, c_re, c_im, d, w_glu, b_glu):
    n, L, _ = u.shape
    ab_re, ab_im, bb_re, bb_im = ssm_discretize(a_re, a_im, log_dt, b_re, b_im)
    uf = u.astype(jnp.float32)
    ug = uf.reshape(n, L, SSM_GROUPS, SSM_GROUP)
    bu_re = jnp.einsum('nlgj,gpj->nlgp', ug, bb_re)
    bu_im = jnp.einsum('nlgj,gpj->nlgp', ug, bb_im)
    ar = jnp.broadcast_to(ab_re, bu_re.shape)
    ai = jnp.broadcast_to(ab_im, bu_re.shape)
    acr, aci, s_re, s_im = lax.associative_scan(complex_affine_combine, (ar, ai, bu_re, bu_im), axis=1)
    s0r = s0_re.astype(jnp.float32)[:, None]
    s0i = s0_im.astype(jnp.float32)[:, None]
    s_re = s_re + acr * s0r - aci * s0i
    s_im = s_im + acr * s0i + aci * s0r
    y = (jnp.einsum('gjp,nlgp->nlgj', c_re.astype(jnp.float32), s_re)
         - jnp.einsum('gjp,nlgp->nlgj', c_im.astype(jnp.float32), s_im))
    y = y.reshape(n, L, SSM_WIDTH) + d.astype(jnp.float32) * uf
    y = jax.nn.gelu(y)
    y = y * jax.nn.sigmoid(y @ w_glu.astype(jnp.float32) + b_glu.astype(jnp.float32))
    return y.astype(u.dtype), s_re[:, -1], s_im[:, -1]


def diff_attend(q, k, v, q_pos, k_pos, lam):
    s = jnp.einsum('nqhcd,nkhcd->nhcqk', q, k).astype(jnp.float32) * (ATT_HEAD_DIM ** -0.5)
    mask = k_pos[None, :] <= q_pos[:, None]
    s = jnp.where(mask, s, NEG_INF)
    p = jax.nn.softmax(s, axis=-1)
    w = p[:, :, 0] - lam * p[:, :, 1]
    return jnp.einsum('nhqk,nkhe->nqhe', w.astype(v.dtype), v)


def diff_attention_causal_blocks(q, k, v, lam):
    n, L = q.shape[:2]
    nb = L // Q_BLOCK
    qb = q.reshape(n, nb, Q_BLOCK, ATT_HEADS, 2, ATT_HEAD_DIM).swapaxes(0, 1)
    k_pos = jnp.arange(L)

    def one_block(args):
        q_blk, i = args
        q_pos = i * Q_BLOCK + jnp.arange(Q_BLOCK)
        return diff_attend(q_blk, k, v, q_pos, k_pos, lam)

    o = lax.map(one_block, (qb, jnp.arange(nb)))
    return o.swapaxes(0, 1).reshape(n, L, ATT_HEADS, ATT_V_DIM)


def trunk_layer(x, l, p, past_k, past_v, s0_re, s0_im, conv_buf):
    n, L, _ = x.shape
    h = rmsnorm(x, p['norm_mix_g'][l])
    z = h @ p['w_in'][l]
    u, q, k, v, cb, cc, ch, gl = jnp.split(z, IN_SPLITS, axis=-1)

    ya, s_re, s_im = s5_mixer(u, s0_re, s0_im, p['ssm_a_re'][l], p['ssm_a_im'][l], p['ssm_log_dt'][l],
                              p['ssm_b_re'][l], p['ssm_b_im'][l], p['ssm_c_re'][l], p['ssm_c_im'][l],
                              p['ssm_d'][l], p['ssm_w_glu'][l], p['ssm_b_glu'][l])

    q = rmsnorm(q.reshape(n, L, ATT_HEADS, 2, ATT_HEAD_DIM), p['q_norm_g'][l])
    k = rmsnorm(k.reshape(n, L, ATT_HEADS, 2, ATT_HEAD_DIM), p['k_norm_g'][l])
    v = v.reshape(n, L, ATT_HEADS, ATT_V_DIM)
    lam_init = 0.8 - 0.6 * math.exp(-0.3 * l)
    f32 = jnp.float32
    lam = (jnp.exp(jnp.sum(p['lambda_q1'][l].astype(f32) * p['lambda_k1'][l].astype(f32)))
           - jnp.exp(jnp.sum(p['lambda_q2'][l].astype(f32) * p['lambda_k2'][l].astype(f32))) + lam_init)
    if past_k is None:
        o = diff_attention_causal_blocks(q, k, v, lam)
    else:
        past = past_k.shape[1]
        k_all = jnp.concatenate([past_k.astype(k.dtype), k], axis=1)
        v_all = jnp.concatenate([past_v.astype(v.dtype), v], axis=1)
        o = diff_attend(q, k_all, v_all, past + jnp.arange(L), jnp.arange(past + L), lam)
    yb = (rmsnorm(o, p['head_norm_g'][l]) * (1.0 - lam_init)).reshape(n, L, ATT_WIDTH)

    vin = cc * ch
    ext = jnp.concatenate([conv_buf.astype(vin.dtype), vin], axis=1)
    cw = p['conv_w'][l]
    conv = ext[:, 0:L] * cw[0]
    for j in range(1, CONV_K):
        conv = conv + ext[:, j:j + L] * cw[j]
    yc = cb * conv
    new_buf = ext[:, -(CONV_K - 1):]

    g = jax.nn.sigmoid(gl).reshape(n, L, N_BRANCH, D_MODEL)
    merged = (g[:, :, 0] * (ya @ p['w_br_ssm'][l])
              + g[:, :, 1] * (yb @ p['w_br_att'][l])
              + g[:, :, 2] * (yc @ p['w_br_conv'][l]))
    x = x + merged @ p['w_out'][l]

    h2 = rmsnorm(x, p['norm_ffn_g'][l])
    if l % 2 == 0:
        i = l // 2
        f = swiglu(h2, p['ffn_w1'][i], p['ffn_w3'][i], p['ffn_w2'][i])
    else:
        i = l // 2
        f = moe_ffn(h2, p['router_w'][i], p['moe_w1'][i], p['moe_w3'][i], p['moe_w2'][i])
    x = x + f
    return x, k, v, s_re, s_im, new_buf


def setup_inputs(seed: int = 0) -> dict:
    key = jax.random.key(seed)
    keys = iter(jax.random.split(key, 48))
    nrm = lambda shape, scale: scale * jax.random.normal(next(keys), shape, jnp.float32)
    n_pages = PAST_LEN // PAGE_SIZE
    n_used = DEC_BATCH * n_pages
    n_pool = n_used + n_used // 4
    page_table = jax.random.permutation(next(keys), n_pool)[:n_used].reshape(DEC_BATCH, n_pages).astype(jnp.int32)
    a_im = jnp.pi * jnp.arange(SSM_STATE, dtype=jnp.float32)
    return {
        'x_prompt': nrm((BATCH, SEQ, D_MODEL), 1.0),
        'x_sample': nrm((DEC_BATCH, DEC_SEQ, D_MODEL), 1.0),
        'cache_k': nrm((DEPTH, n_pool, PAGE_SIZE, ATT_HEADS, 2, ATT_HEAD_DIM), 1.0),
        'cache_v': nrm((DEPTH, n_pool, PAGE_SIZE, ATT_HEADS, ATT_V_DIM), 1.0),
        'page_table': page_table,
        'state_ssm_re': nrm((DEPTH, DEC_BATCH, SSM_GROUPS, SSM_STATE), 0.5),
        'state_ssm_im': nrm((DEPTH, DEC_BATCH, SSM_GROUPS, SSM_STATE), 0.5),
        'state_conv': nrm((DEPTH, DEC_BATCH, CONV_K - 1, CONV_WIDTH), 1.0),
        'norm_mix_g': 1.0 + nrm((DEPTH, D_MODEL), 0.01),
        'norm_ffn_g': 1.0 + nrm((DEPTH, D_MODEL), 0.01),
        'w_in': nrm((DEPTH, D_MODEL, IN_COLS), D_MODEL ** -0.5),
        'ssm_a_re': -0.5 + nrm((DEPTH, SSM_GROUPS, SSM_STATE), 0.01),
        'ssm_a_im': a_im + nrm((DEPTH, SSM_GROUPS, SSM_STATE), 0.01),
        'ssm_log_dt': jax.random.uniform(next(keys), (DEPTH, SSM_GROUPS), jnp.float32,
                                         minval=math.log(1e-3), maxval=math.log(1e-1)),
        'ssm_b_re': nrm((DEPTH, SSM_GROUPS, SSM_STATE, SSM_GROUP), (2 * SSM_GROUP) ** -0.5),
        'ssm_b_im': nrm((DEPTH, SSM_GROUPS, SSM_STATE, SSM_GROUP), (2 * SSM_GROUP) ** -0.5),
        'ssm_c_re': nrm((DEPTH, SSM_GROUPS, SSM_GROUP, SSM_STATE), SSM_STATE ** -0.5),
        'ssm_c_im': nrm((DEPTH, SSM_GROUPS, SSM_GROUP, SSM_STATE), SSM_STATE ** -0.5),
        'ssm_d': nrm((DEPTH, SSM_WIDTH), 1.0),
        'ssm_w_glu': nrm((DEPTH, SSM_WIDTH, SSM_WIDTH), SSM_WIDTH ** -0.5),
        'ssm_b_glu': nrm((DEPTH, SSM_WIDTH), 0.01),
        'q_norm_g': 1.0 + nrm((DEPTH, ATT_HEAD_DIM), 0.01),
        'k_norm_g': 1.0 + nrm((DEPTH, ATT_HEAD_DIM), 0.01),
        'lambda_q1': nrm((DEPTH, ATT_HEAD_DIM), 0.1),
        'lambda_k1': nrm((DEPTH, ATT_HEAD_DIM), 0.1),
        'lambda_q2': nrm((DEPTH, ATT_HEAD_DIM), 0.1),
        'lambda_k2': nrm((DEPTH, ATT_HEAD_DIM), 0.1),
        'head_norm_g': 1.0 + nrm((DEPTH, ATT_V_DIM), 0.01),
        'conv_w': nrm((DEPTH, CONV_K, CONV_WIDTH), CONV_K ** -0.5),
        'w_br_ssm': nrm((DEPTH, SSM_WIDTH, D_MODEL), SSM_WIDTH ** -0.5),
        'w_br_att': nrm((DEPTH, ATT_WIDTH, D_MODEL), ATT_WIDTH ** -0.5),
        'w_br_conv': nrm((DEPTH, CONV_WIDTH, D_MODEL), CONV_WIDTH ** -0.5),
        'w_out': nrm((DEPTH, D_MODEL, D_MODEL), D_MODEL ** -0.5),
        'ffn_w1': nrm((N_DENSE, D_MODEL, D_FF), D_MODEL ** -0.5),
        'ffn_w3': nrm((N_DENSE, D_MODEL, D_FF), D_MODEL ** -0.5),
        'ffn_w2': nrm((N_DENSE, D_FF, D_MODEL), D_FF ** -0.5),
        'router_w': nrm((N_MOE, D_MODEL, N_EXPERTS), D_MODEL ** -0.5),
        'moe_w1': nrm((N_MOE, N_EXPERTS, D_MODEL, D_FF_EXPERT), D_MODEL ** -0.5),
        'moe_w3': nrm((N_MOE, N_EXPERTS, D_MODEL, D_FF_EXPERT), D_MODEL ** -0.5),
        'moe_w2': nrm((N_MOE, N_EXPERTS, D_FF_EXPERT, D_MODEL), D_FF_EXPERT ** -0.5),
    }


def reference(x_prompt, x_sample, cache_k, cache_v, page_table, state_ssm_re, state_ssm_im, state_conv,
              norm_mix_g, norm_ffn_g, w_in, ssm_a_re, ssm_a_im, ssm_log_dt, ssm_b_re, ssm_b_im,
              ssm_c_re, ssm_c_im, ssm_d, ssm_w_glu, ssm_b_glu, q_norm_g, k_norm_g,
              lambda_q1, lambda_k1, lambda_q2, lambda_k2, head_norm_g, conv_w,
              w_br_ssm, w_br_att, w_br_conv, w_out, ffn_w1, ffn_w3, ffn_w2,
              router_w, moe_w1, moe_w3, moe_w2):
    p = dict(norm_mix_g=norm_mix_g, norm_ffn_g=norm_ffn_g, w_in=w_in,
             ssm_a_re=ssm_a_re, ssm_a_im=ssm_a_im, ssm_log_dt=ssm_log_dt,
             ssm_b_re=ssm_b_re, ssm_b_im=ssm_b_im, ssm_c_re=ssm_c_re, ssm_c_im=ssm_c_im,
             ssm_d=ssm_d, ssm_w_glu=ssm_w_glu, ssm_b_glu=ssm_b_glu,
             q_norm_g=q_norm_g, k_norm_g=k_norm_g, lambda_q1=lambda_q1, lambda_k1=lambda_k1,
             lambda_q2=lambda_q2, lambda_k2=lambda_k2, head_norm_g=head_norm_g, conv_w=conv_w,
             w_br_ssm=w_br_ssm, w_br_att=w_br_att, w_br_conv=w_br_conv, w_out=w_out,
             ffn_w1=ffn_w1, ffn_w3=ffn_w3, ffn_w2=ffn_w2,
             router_w=router_w, moe_w1=moe_w1, moe_w3=moe_w3, moe_w2=moe_w2)
    n_p = x_prompt.shape[0]
    n_s, n_pages = page_table.shape
    past_len = n_pages * cache_k.shape[2]
    yp = x_prompt
    ys = x_sample
    kp, vp, srp, sip, cvp = [], [], [], [], []
    ks, vs, srs, sis, cvs = [], [], [], [], []
    for l in range(DEPTH):
        zero_s = jnp.zeros((n_p, SSM_GROUPS, SSM_STATE), jnp.float32)
        zero_buf = jnp.zeros((n_p, CONV_K - 1, CONV_WIDTH), x_prompt.dtype)
        yp, k_new, v_new, s_re, s_im, buf = trunk_layer(yp, l, p, None, None, zero_s, zero_s, zero_buf)
        kp.append(k_new); vp.append(v_new); srp.append(s_re); sip.append(s_im); cvp.append(buf)

        past_k = cache_k[l, page_table].reshape(n_s, past_len, ATT_HEADS, 2, ATT_HEAD_DIM)
        past_v = cache_v[l, page_table].reshape(n_s, past_len, ATT_HEADS, ATT_V_DIM)
        ys, k_new, v_new, s_re, s_im, buf = trunk_layer(ys, l, p, past_k, past_v,
                                                        state_ssm_re[l], state_ssm_im[l], state_conv[l])
        ks.append(k_new); vs.append(v_new); srs.append(s_re); sis.append(s_im); cvs.append(buf)
    return (yp, ys,
            jnp.stack(kp), jnp.stack(vp), jnp.stack(srp), jnp.stack(sip), jnp.stack(cvp),
            jnp.stack(ks), jnp.stack(vs), jnp.stack(srs), jnp.stack(sis), jnp.stack(cvs))
```

```python
import functools
import math

import jax
import jax.numpy as jnp
from jax import lax
from jax.experimental import pallas as pl
from jax.experimental.pallas import tpu as pltpu

F32 = jnp.float32
BF16 = jnp.bfloat16

D_MODEL = 1024
SSM_WIDTH = 256
SSM_GROUP = 16
SSM_GROUPS = SSM_WIDTH // SSM_GROUP
SSM_STATE = 64
ATT_HEADS = 4
ATT_HEAD_DIM = 64
ATT_V_DIM = 2 * ATT_HEAD_DIM
QK_WIDTH = ATT_HEADS * 2 * ATT_HEAD_DIM
ATT_WIDTH = ATT_HEADS * ATT_V_DIM
CONV_WIDTH = 256
CONV_K = 3
N_BRANCH = 3
N_EXPERTS = 8
TOP_K = 2
EPS = 1e-6
NEG_INF = -1e30

C_U = 0
C_Q = C_U + SSM_WIDTH
C_K = C_Q + QK_WIDTH
C_V = C_K + QK_WIDTH
C_CB = C_V + ATT_WIDTH
C_CC = C_CB + CONV_WIDTH
C_CH = C_CC + CONV_WIDTH
C_G = C_CH + CONV_WIDTH
IN_COLS = C_G + N_BRANCH * D_MODEL

SSM_CHUNK = 32
SUBLANES = 8
ROW_TILE = 512
ATT_TILE = 512
MOE_TILE = 256
VMEM_LIMIT = 56 * 1024 * 1024


def _cparams(sem):
    return pltpu.CompilerParams(dimension_semantics=sem, vmem_limit_bytes=VMEM_LIMIT)


def _const_spec(shape):
    nd = len(shape)
    return pl.BlockSpec(shape, lambda *_: (0,) * nd)


def _bdot(a, b):
    return jnp.dot(a, b, preferred_element_type=F32)


def _rms_rows(x, g):
    ms = jnp.mean(x * x, axis=-1, keepdims=True)
    return x * lax.rsqrt(ms + EPS) * g


def _segment_rms(z, g, seg):
    sq = z * z
    hi = sq.astype(BF16)
    lo = (sq - hi.astype(F32)).astype(BF16)
    ms = _bdot(hi, seg) + _bdot(lo, seg)
    return z * lax.rsqrt(ms + EPS) * g


def _inproj_kernel(x_ref, g_ref, w_ref, qg_ref, kg_ref, seg_ref,
                   u_ref, q_ref, kf_ref, kb_ref, vf_ref, vb_ref, cb_ref, vin_ref, gate_ref):
    h = _rms_rows(x_ref[...], g_ref[...]).astype(BF16)

    def proj(a, b):
        return _bdot(h, w_ref[:, a:b])

    seg = seg_ref[...]
    u_ref[...] = proj(C_U, C_Q)
    qn = _segment_rms(proj(C_Q, C_K), qg_ref[...], seg)
    q_ref[...] = (qn * (ATT_HEAD_DIM ** -0.5)).astype(BF16)
    kn = _segment_rms(proj(C_K, C_V), kg_ref[...], seg)
    kf_ref[...] = kn
    kb_ref[...] = kn.astype(BF16)
    v = proj(C_V, C_CB)
    vf_ref[...] = v
    vb_ref[...] = v.astype(BF16)
    cb_ref[...] = proj(C_CB, C_CC)
    vin_ref[...] = proj(C_CC, C_CH) * proj(C_CH, C_G)
    for j in range(N_BRANCH):
        a = C_G + j * D_MODEL
        gate_ref[:, j * D_MODEL:(j + 1) * D_MODEL] = jax.nn.sigmoid(proj(a, a + D_MODEL))


def _inproj(x, g, w_bf, qg, kg, seg):
    t = x.shape[0]
    tm = ROW_TILE
    row = lambda w: pl.BlockSpec((tm, w), lambda i: (i, 0))
    outs = [
        (SSM_WIDTH, F32), (QK_WIDTH, BF16), (QK_WIDTH, F32), (QK_WIDTH, BF16),
        (ATT_WIDTH, F32), (ATT_WIDTH, BF16), (CONV_WIDTH, F32), (CONV_WIDTH, F32),
        (N_BRANCH * D_MODEL, F32),
    ]
    return pl.pallas_call(
        _inproj_kernel,
        out_shape=[jax.ShapeDtypeStruct((t, w), d) for w, d in outs],
        grid=(t // tm,),
        in_specs=[row(D_MODEL), _const_spec((1, D_MODEL)), _const_spec((D_MODEL, IN_COLS)),
                  _const_spec((1, QK_WIDTH)), _const_spec((1, QK_WIDTH)),
                  _const_spec((QK_WIDTH, QK_WIDTH))],
        out_specs=[row(w) for w, _ in outs],
        compiler_params=_cparams(("parallel",)),
        name="inproj",
    )(x, g, w_bf, qg, kg, seg)


def _s5_kernel(up_ref, us_ref, s0r_ref, s0i_ref, m_ref, pre_ref, pim_ref, qre_ref, qim_ref, a_ref,
               yp_ref, ys_ref, fpr_ref, fpi_ref, fsr_ref, fsi_ref,
               wre, wim, sre, sim, *, n_chunks, dec_cols):
    up = up_ref[...]
    wre[...] = _bdot(up, pre_ref[...])
    wim[...] = _bdot(up, pim_ref[...])
    ar = a_ref[0:1, :]
    ai = a_ref[1:2, :]

    def step(c, carry):
        s_re, s_im = carry
        r0 = pl.multiple_of(c * SUBLANES, SUBLANES)
        sre[pl.ds(r0, SUBLANES), :] = s_re
        sim[pl.ds(r0, SUBLANES), :] = s_im
        n_re = ar * s_re - ai * s_im + wre[pl.ds(r0, SUBLANES), :]
        n_im = ar * s_im + ai * s_re + wim[pl.ds(r0, SUBLANES), :]
        return n_re, n_im

    zero = jnp.zeros((SUBLANES, SSM_STATE), F32)
    s_re, s_im = lax.fori_loop(0, n_chunks, step, (zero, zero))
    fpr_ref[...] = s_re
    fpi_ref[...] = s_im
    yp_ref[...] = (_bdot(up, m_ref[...])
                   + _bdot(sre[...].astype(BF16), qre_ref[...])
                   + _bdot(sim[...].astype(BF16), qim_ref[...]))

    us = us_ref[...]
    k0 = pre_ref.shape[0] - dec_cols
    s0r = s0r_ref[...]
    s0i = s0i_ref[...]
    asr = a_ref[2:3, :]
    asi = a_ref[3:4, :]
    fsr_ref[...] = asr * s0r - asi * s0i + _bdot(us, pre_ref[k0:, :])
    fsi_ref[...] = asr * s0i + asi * s0r + _bdot(us, pim_ref[k0:, :])
    ys_ref[...] = (_bdot(us, m_ref[0:dec_cols, 0:dec_cols])
                   + _bdot(s0r.astype(BF16), qre_ref[:, 0:dec_cols])
                   + _bdot(s0i.astype(BF16), qim_ref[:, 0:dec_cols]))


def _s5_tables(a_re, a_im, log_dt, b_re, b_im, c_re, c_im, dec):
    hp = lax.Precision.HIGHEST
    tc = SSM_CHUNK
    dt = jnp.exp(log_dt)[:, None]
    den = a_re * a_re + a_im * a_im

    def apow(m):
        mag = jnp.exp(a_re * dt * m)
        ang = a_im * dt * m
        return mag * jnp.cos(ang), mag * jnp.sin(ang)

    ab_re, ab_im = apow(1.0)
    nr = ab_re - 1.0
    cr = (nr * a_re + ab_im * a_im) / den
    ci = (ab_im * a_re - nr * a_im) / den
    bb_re = cr[..., None] * b_re - ci[..., None] * b_im
    bb_im = cr[..., None] * b_im + ci[..., None] * b_re
    ms = jnp.arange(tc + 1, dtype=F32)[None, :, None]
    mag = jnp.exp(a_re[:, None, :] * dt[:, None, :] * ms)
    ang = a_im[:, None, :] * dt[:, None, :] * ms
    pw_re = mag * jnp.cos(ang)
    pw_im = mag * jnp.sin(ang)
    cbr =c_re.transpose(0, 2, 1)[:, :, None, :] * bb_re[:, :, :, None] \
        - c_im.transpose(0, 2, 1)[:, :, None, :] * bb_im[:, :, :, None]
    cbi = c_re.transpose(0, 2, 1)[:, :, None, :] * bb_im[:, :, :, None] \
        + c_im.transpose(0, 2, 1)[:, :, None, :] * bb_re[:, :, :, None]
    kern = (jnp.einsum('gmp,gpji->gmji', pw_re[:, :tc], cbr, precision=hp)
            - jnp.einsum('gmp,gpji->gmji', pw_im[:, :tc], cbi, precision=hp))
    kk = jnp.arange(tc)
    lag = kk[None, :] - kk[:, None]
    toe = jnp.where((lag >= 0)[None, :, :, None, None], kern[:, jnp.clip(lag, 0, tc - 1)], 0.0)
    toe = toe.transpose(0, 1, 3, 2, 4).reshape(SSM_GROUPS, tc * SSM_GROUP, tc * SSM_GROUP)
    rev_re = pw_re[:, tc - 1::-1][:, :tc]
    rev_im = pw_im[:, tc - 1::-1][:, :tc]
    bbt_re = bb_re.transpose(0, 2, 1)
    bbt_im = bb_im.transpose(0, 2, 1)
    p_re = (rev_re[:, :, None, :] * bbt_re[:, None] - rev_im[:, :, None, :] * bbt_im[:, None])
    p_im = (rev_re[:, :, None, :] * bbt_im[:, None] + rev_im[:, :, None, :] * bbt_re[:, None])
    p_re = p_re.reshape(SSM_GROUPS, tc * SSM_GROUP, SSM_STATE)
    p_im = p_im.reshape(SSM_GROUPS, tc * SSM_GROUP, SSM_STATE)
    ct_re = c_re.transpose(0, 2, 1)
    ct_im = c_im.transpose(0, 2, 1)
    nx_re = pw_re[:, 1:].transpose(0, 2, 1)
    nx_im = pw_im[:, 1:].transpose(0, 2, 1)
    q_re = nx_re[..., None] * ct_re[:, :, None, :] - nx_im[..., None] * ct_im[:, :, None, :]
    q_im = -(nx_re[..., None] * ct_im[:, :, None, :] + nx_im[..., None] * ct_re[:, :, None, :])
    q_re = q_re.reshape(SSM_GROUPS, SSM_STATE, tc * SSM_GROUP)
    q_im = q_im.reshape(SSM_GROUPS, SSM_STATE, tc * SSM_GROUP)
    adec = jnp.stack([pw_re[:, tc], pw_im[:, tc], pw_re[:, dec], pw_im[:, dec]], axis=1)
    return (toe.astype(BF16), p_re.astype(BF16), p_im.astype(BF16),
            q_re.astype(BF16), q_im.astype(BF16), adec)


def _s5(u, s0_re, s0_im, tables, n_p, seq, n_s, dec):
    toe, p_re, p_im, q_re, q_im, adec = tables
    tc = SSM_CHUNK
    g, j, p = SSM_GROUPS, SSM_GROUP, SSM_STATE
    nc = seq // tc
    tp = n_p * seq
    bp = -(-n_p // SUBLANES) * SUBLANES
    assert bp == SUBLANES
    rows = nc * bp
    cw = tc * j
    dcols = dec * j
    up = u[:tp].reshape(n_p, nc, tc, g, j).transpose(3, 1, 0, 2, 4)
    up = jnp.pad(up, ((0, 0), (0, 0), (0, bp - n_p), (0, 0), (0, 0)))
    up = up.reshape(g, rows, cw).astype(BF16)
    us = u[tp:].reshape(n_s, dec, g, j).transpose(2, 0, 1, 3).reshape(g, n_s, dcols).astype(BF16)
    s0r = s0_re.transpose(1, 0, 2)
    s0i = s0_im.transpose(1, 0, 2)

    grp = lambda a, b: pl.BlockSpec((None, a, b), lambda i: (i, 0, 0))
    yp, ys, fpr, fpi, fsr, fsi = pl.pallas_call(
        functools.partial(_s5_kernel, n_chunks=nc, dec_cols=dcols),
        out_shape=[jax.ShapeDtypeStruct((g, rows, cw), F32),
                   jax.ShapeDtypeStruct((g, n_s, dcols), F32),
                   jax.ShapeDtypeStruct((g, bp, p), F32), jax.ShapeDtypeStruct((g, bp, p), F32),
                   jax.ShapeDtypeStruct((g, n_s, p), F32), jax.ShapeDtypeStruct((g, n_s, p), F32)],
        grid=(g,),
        in_specs=[grp(rows, cw), grp(n_s, dcols), grp(n_s, p), grp(n_s, p),
                  grp(cw, cw), grp(cw, p), grp(cw, p), grp(p, cw), grp(p, cw), grp(4, p)],
        out_specs=[grp(rows, cw), grp(n_s, dcols), grp(bp, p), grp(bp, p), grp(n_s, p), grp(n_s, p)],
        scratch_shapes=[pltpu.VMEM((rows, p), F32)] * 4,
        compiler_params=_cparams(("parallel",)),
        name="s5",
    )(up, us, s0r, s0i, toe, p_re, p_im, q_re, q_im, adec)
    yp = yp.reshape(g, nc, bp, tc, j)[:, :, :n_p].transpose(2, 1, 3, 0, 4).reshape(tp, g * j)
    ys = ys.reshape(g, n_s, dec, j).transpose(1, 2, 0, 3).reshape(n_s * dec, g * j)
    y = jnp.concatenate([yp, ys], axis=0)
    st = lambda a: a.transpose(1, 0, 2)
    return y, st(fpr)[:n_p], st(fpi)[:n_p], st(fsr), st(fsi)


def _lambda(lp_ref, lam_init):
    lp = lp_ref[...]
    s1 = jnp.sum(lp[0:1, :] * lp[1:2, :], axis=-1, keepdims=True)
    s2 = jnp.sum(lp[2:3, :] * lp[3:4, :], axis=-1, keepdims=True)
    return jnp.exp(s1) - jnp.exp(s2) + lam_init


def _attn_kernel(q_ref, k_ref, v_ref, lp_ref, hg_ref, o_ref, m1, l1, a1, m2, l2, a2, *, tile, lam_init):
    qi = pl.program_id(2)
    ki = pl.program_id(3)

    @pl.when(ki == 0)
    def _():
        m1[...] = jnp.full(m1.shape, NEG_INF, F32)
        m2[...] = jnp.full(m2.shape, NEG_INF, F32)
        l1[...] = jnp.zeros(l1.shape, F32)
        l2[...] = jnp.zeros(l2.shape, F32)
        a1[...] = jnp.zeros(a1.shape, F32)
        a2[...] = jnp.zeros(a2.shape, F32)

    @pl.when(ki <= qi)
    def _():
        q = q_ref[...]
        k = k_ref[...]
        v = v_ref[...]
        lane = lax.broadcasted_iota(jnp.int32, q.shape, 1)
        zero = jnp.zeros_like(q)
        row = qi * tile + lax.broadcasted_iota(jnp.int32, (tile, tile), 0)
        col = ki * tile + lax.broadcasted_iota(jnp.int32, (tile, tile), 1)
        mask = col <= row
        nt = (((1,), (1,)), ((), ()))
        for qm, m_ref, l_ref, a_ref in ((jnp.where(lane < ATT_HEAD_DIM, q, zero), m1, l1, a1),
                                        (jnp.where(lane >= ATT_HEAD_DIM, q, zero), m2, l2, a2)):
            s = lax.dot_general(qm, k, nt, preferred_element_type=F32)
            s = jnp.where(mask, s, NEG_INF)
            m_old = m_ref[...]
            m_new = jnp.maximum(m_old, jnp.max(s, axis=-1, keepdims=True))
            alpha = jnp.exp(m_old - m_new)
            p = jnp.exp(s - m_new)
            l_ref[...] = alpha * l_ref[...] + jnp.sum(p, axis=-1, keepdims=True)
            a_ref[...] = alpha * a_ref[...] + _bdot(p.astype(BF16), v)
            m_ref[...] = m_new

    @pl.when(ki == qi)
    def _():
        lam = _lambda(lp_ref, lam_init)
        o = a1[...] / l1[...] - lam * (a2[...] / l2[...])
        o_ref[...] = (_rms_rows(o, hg_ref[...]) * (1.0 - lam_init)).astype(o_ref.dtype)


def _attn_prompt(q, k, v, lp, hg, n_p, seq, lam_init):
    tile = min(ATT_TILE, seq)
    nq = seq // tile
    qspec = pl.BlockSpec((tile, ATT_V_DIM), lambda b, h, i, j: (b * nq + i, h))
    kspec = pl.BlockSpec((tile, ATT_V_DIM), lambda b, h, i, j: (b * nq + jnp.minimum(i, j), h))
    return pl.pallas_call(
        functools.partial(_attn_kernel, tile=tile, lam_init=lam_init),
        out_shape=jax.ShapeDtypeStruct((n_p * seq, ATT_WIDTH), BF16),
        grid=(n_p, ATT_HEADS, nq, nq),
        in_specs=[qspec, kspec, kspec, _const_spec((4, ATT_HEAD_DIM)), _const_spec((1, ATT_V_DIM))],
        out_specs=qspec,
        scratch_shapes=[pltpu.VMEM((tile, 1), F32), pltpu.VMEM((tile, 1), F32),
                        pltpu.VMEM((tile, ATT_V_DIM), F32)] * 2,
        compiler_params=_cparams(("parallel", "parallel", "parallel", "arbitrary")),
        name="attn_prompt",
    )(q, k, v, lp, hg)


def _attn_sample_kernel(pt_ref, q_ref, kn_ref, vn_ref, lp_ref, hg_ref, *rest, n_pages, dec, lam_init):
    del pt_ref
    kp = rest[:n_pages]
    vp = rest[n_pages:2 * n_pages]
    o_ref = rest[2 * n_pages]
    n_rows = ATT_HEADS * 2 * dec
    q = q_ref[...].astype(F32)
    qt = jnp.concatenate([q] * (ATT_HEADS * 2), axis=0)
    r = lax.broadcasted_iota(jnp.int32, (n_rows, QK_WIDTH), 0)
    c = lax.broadcasted_iota(jnp.int32, (n_rows, QK_WIDTH), 1)
    qb = jnp.where(r // dec == c // ATT_HEAD_DIM, qt, 0.0).astype(BF16)
    nt = (((1,), (1,)), ((), ()))
    s_pages = [lax.dot_general(qb, kp[j][...].astype(BF16), nt, preferred_element_type=F32)
               for j in range(n_pages)]
    s_new = lax.dot_general(qb, kn_ref[...], nt, preferred_element_type=F32)
    rn = lax.broadcasted_iota(jnp.int32, (n_rows, dec), 0) % dec
    cn = lax.broadcasted_iota(jnp.int32, (n_rows, dec), 1)
    s_new = jnp.where(cn <= rn, s_new, NEG_INF)
    m = jnp.max(s_new, axis=-1, keepdims=True)
    for s in s_pages:
        m = jnp.maximum(m, jnp.max(s, axis=-1, keepdims=True))
    p_new = jnp.exp(s_new - m)
    l = jnp.sum(p_new, axis=-1, keepdims=True)
    acc = _bdot(p_new.astype(BF16), vn_ref[...])
    for j in range(n_pages):
        p = jnp.exp(s_pages[j] - m)
        l = l + jnp.sum(p, axis=-1, keepdims=True)
        acc = acc + _bdot(p.astype(BF16), vp[j][...].astype(BF16))
    acc = acc / l
    lam = _lambda(lp_ref, lam_init)
    hg = hg_ref[...]
    for h in range(ATT_HEADS):
        r0 = h * 2 * dec
        cols = slice(h * ATT_V_DIM, (h + 1) * ATT_V_DIM)
        o = acc[r0:r0 + dec, cols] - lam * acc[r0 + dec:r0 + 2 * dec, cols]
        o_ref[:, cols] = (_rms_rows(o, hg) * (1.0 - lam_init)).astype(o_ref.dtype)


def _attn_sample(q, kn, vn, cache_k_l, cache_v_l, page_table, lp, hg, lam_init):
    n_s, dec, _ = q.shape
    n_pages = page_table.shape[1]
    page = cache_k_l.shape[1]
    tok = pl.BlockSpec((None, dec, QK_WIDTH), lambda n, pt: (n, 0, 0))
    pages = [pl.BlockSpec((None, page, QK_WIDTH), lambda n, pt, j=j: (pt[n, j], 0, 0))
             for j in range(n_pages)]
    grid_spec = pltpu.PrefetchScalarGridSpec(
        num_scalar_prefetch=1,
        grid=(n_s,),
        in_specs=[tok, tok, tok,
                  pl.BlockSpec((4, ATT_HEAD_DIM), lambda n, pt: (0, 0)),
                  pl.BlockSpec((1, ATT_V_DIM), lambda n, pt: (0, 0))] + pages + pages,
        out_specs=tok,
    )
    return pl.pallas_call(
        functools.partial(_attn_sample_kernel, n_pages=n_pages, dec=dec, lam_init=lam_init),
        out_shape=jax.ShapeDtypeStruct((n_s, dec, ATT_WIDTH), BF16),
        grid_spec=grid_spec,
        compiler_params=_cparams(("parallel",)),
        name="attn_sample",
    )(page_table, q, kn, vn, lp, hg, *([cache_k_l] * n_pages), *([cache_v_l] * n_pages))


def _merge_kernel(x_ref, yraw_ref, u_ref, d_ref, wglu_ref, bglu_ref, yb_ref,
                  cb_ref, vin_ref, vm1_ref, vm2_ref, cw_ref, gate_ref,
                  wssm_ref, watt_ref, wconv_ref, wout_ref, o_ref):
    y = jax.nn.gelu(yraw_ref[...] + d_ref[...] * u_ref[...])
    ya = y * jax.nn.sigmoid(_bdot(y.astype(BF16), wglu_ref[...]) + bglu_ref[...])
    conv = vm2_ref[...] * cw_ref[0:1, :] + vm1_ref[...] * cw_ref[1:2, :] + vin_ref[...] * cw_ref[2:3, :]
    yc = cb_ref[...] * conv
    merged = (gate_ref[:, 0:D_MODEL] * _bdot(ya.astype(BF16), wssm_ref[...])
              + gate_ref[:, D_MODEL:2 * D_MODEL] * _bdot(yb_ref[...], watt_ref[...])
              + gate_ref[:, 2 * D_MODEL:3 * D_MODEL] * _bdot(yc.astype(BF16), wconv_ref[...]))
    o_ref[...] = x_ref[...] + _bdot(merged.astype(BF16), wout_ref[...])


def _merge(x, yraw, u, d, wglu, bglu, yb, cb, vin, vm1, vm2, cw, gates, wssm, watt, wconv, wout):
    t = x.shape[0]
    tm = ROW_TILE
    row = lambda w: pl.BlockSpec((tm, w), lambda i: (i, 0))
    return pl.pallas_call(
        _merge_kernel,
        out_shape=jax.ShapeDtypeStruct((t, D_MODEL), F32),
        grid=(t // tm,),
        in_specs=[row(D_MODEL), row(SSM_WIDTH), row(SSM_WIDTH), _const_spec((1, SSM_WIDTH)),
                  _const_spec((SSM_WIDTH, SSM_WIDTH)), _const_spec((1, SSM_WIDTH)), row(ATT_WIDTH),
                  row(CONV_WIDTH), row(CONV_WIDTH), row(CONV_WIDTH), row(CONV_WIDTH),
                  _const_spec((CONV_K, CONV_WIDTH)), row(N_BRANCH * D_MODEL),
                  _const_spec((SSM_WIDTH, D_MODEL)), _const_spec((ATT_WIDTH, D_MODEL)),
                  _const_spec((CONV_WIDTH, D_MODEL)), _const_spec((D_MODEL, D_MODEL))],
        out_specs=row(D_MODEL),
        compiler_params=_cparams(("parallel",)),
        name="merge",
    )(x, yraw, u, d, wglu, bglu, yb, cb, vin, vm1, vm2, cw, gates, wssm, watt, wconv, wout)


def _swiglu(h, w1, w3, w2):
    a = _bdot(h, w1)
    b = _bdot(h, w3)
    return _bdot((jax.nn.silu(a) * b).astype(BF16), w2)


def _ffn_kernel(x_ref, g_ref, w1_ref, w3_ref, w2_ref, o_ref):
    x = x_ref[...]
    h = _rms_rows(x, g_ref[...]).astype(BF16)
    o_ref[...] = x + _swiglu(h, w1_ref[...], w3_ref[...], w2_ref[...])


def _ffn(x, g, w1, w3, w2):
    t = x.shape[0]
    tm = ROW_TILE
    d_ff = w1.shape[1]
    row = pl.BlockSpec((tm, D_MODEL), lambda i: (i, 0))
    return pl.pallas_call(
        _ffn_kernel,
        out_shape=jax.ShapeDtypeStruct((t, D_MODEL), F32),
        grid=(t // tm,),
        in_specs=[row, _const_spec((1, D_MODEL)), _const_spec((D_MODEL, d_ff)),
                  _const_spec((D_MODEL, d_ff)), _const_spec((d_ff, D_MODEL))],
        out_specs=row,
        compiler_params=_cparams(("parallel",)),
        name="ffn",
    )(x, g, w1, w3, w2)


def _router_kernel(x_ref, g_ref, wr_ref, h_ref, idx_ref, gate_ref):
    h = _rms_rows(x_ref[...], g_ref[...])
    h_ref[...] = h.astype(BF16)
    logits = jnp.dot(h, wr_ref[...], preferred_element_type=F32, precision=lax.Precision.HIGHEST)
    lane = lax.broadcasted_iota(jnp.int32, logits.shape, 1)
    logits = jnp.where(lane < N_EXPERTS, logits, -jnp.inf)
    big = jnp.int32(logits.shape[1])
    m1 = jnp.max(logits, axis=-1, keepdims=True)
    i1 = jnp.min(jnp.where(logits == m1, lane, big), axis=-1, keepdims=True)
    rest = jnp.where(lane == i1, -jnp.inf, logits)
    m2 = jnp.max(rest, axis=-1, keepdims=True)
    i2 = jnp.min(jnp.where(rest == m2, lane, big), axis=-1, keepdims=True)
    e = jnp.exp(m2 - m1)
    g1 = 1.0 / (1.0 + e)
    g2 = e / (1.0 + e)
    idx_ref[...] = jnp.where(lane == 0, i1, jnp.where(lane == 1, i2, 0))
    gate_ref[...] = jnp.where(lane == 0, g1, jnp.where(lane == 1, g2, 0.0))


def _router(x, g, wr_pad):
    t = x.shape[0]
    tm = ROW_TILE
    lanes = wr_pad.shape[1]
    row = lambda w: pl.BlockSpec((tm, w), lambda i: (i, 0))
    return pl.pallas_call(
        _router_kernel,
        out_shape=[jax.ShapeDtypeStruct((t, D_MODEL), BF16),
                   jax.ShapeDtypeStruct((t, lanes), jnp.int32),
                   jax.ShapeDtypeStruct((t, lanes), F32)],
        grid=(t // tm,),
        in_specs=[row(D_MODEL), _const_spec((1, D_MODEL)), _const_spec((D_MODEL, lanes))],
        out_specs=[row(D_MODEL), row(lanes), row(lanes)],
        compiler_params=_cparams(("parallel",)),
        name="router",
    )(x, g, wr_pad)


def _moe_kernel(be_ref, nv_ref, x_ref, gate_ref, w1_ref, w3_ref, w2_ref, o_ref):
    del be_ref

    @pl.when(pl.program_id(0) < nv_ref[0])
    def _():
        o_ref[...] = _swiglu(x_ref[...], w1_ref[...], w3_ref[...], w2_ref[...]) * gate_ref[...]


def _moe_blocks(xs, slot_gate, block_expert, n_valid, w1, w3, w2):
    cap = xs.shape[0]
    tb = MOE_TILE
    d_ff = w1.shape[2]
    nb = cap // tb
    rows = lambda w: pl.BlockSpec((tb, w), lambda i, be, nv: (jnp.minimum(i, nv[0] - 1), 0))
    grid_spec = pltpu.PrefetchScalarGridSpec(
        num_scalar_prefetch=2,
        grid=(nb,),
        in_specs=[rows(D_MODEL), rows(1),
                  pl.BlockSpec((None, D_MODEL, d_ff), lambda i, be, nv: (be[i], 0, 0)),
                  pl.BlockSpec((None, D_MODEL, d_ff), lambda i, be, nv: (be[i], 0, 0)),
                  pl.BlockSpec((None, d_ff, D_MODEL), lambda i, be, nv: (be[i], 0, 0))],
        out_specs=pl.BlockSpec((tb, D_MODEL), lambda i, be, nv: (i, 0)),
    )
    return pl.pallas_call(
        _moe_kernel,
        out_shape=jax.ShapeDtypeStruct((cap, D_MODEL), F32),
        grid_spec=grid_spec,
        compiler_params=_cparams(("arbitrary",)),
        name="moe",
    )(block_expert, n_valid, xs, slot_gate, w1, w3, w2)


def _moe(x, g, wr, w1, w3, w2):
    t = x.shape[0]
    tb = MOE_TILE
    lanes = 128
    wr_pad = jnp.pad(wr, ((0, 0), (0, lanes - N_EXPERTS)))
    h, idx, gate = _router(x, g, wr_pad)
    n_assign = t * TOP_K
    e_flat = idx[:, :TOP_K].reshape(-1)
    g_flat = gate[:, :TOP_K].reshape(-1)
    tok_flat = jnp.arange(n_assign, dtype=jnp.int32) // TOP_K
    order = jnp.argsort(e_flat, stable=True)
    e_sorted = e_flat[order]
    counts = jnp.bincount(e_flat, length=N_EXPERTS).astype(jnp.int32)
    starts = jnp.cumsum(counts) - counts
    padded = (counts + tb - 1) // tb * tb
    pad_ends = jnp.cumsum(padded)
    pad_starts = pad_ends - padded
    dest_sorted = pad_starts[e_sorted] + (jnp.arange(n_assign, dtype=jnp.int32) - starts[e_sorted])
    nb = -(-(n_assign + N_EXPERTS * (tb - 1)) // tb)
    cap = nb * tb
    slot_tok = jnp.zeros((cap,), jnp.int32).at[dest_sorted].set(tok_flat[order])
    slot_gate = jnp.zeros((cap,), F32).at[dest_sorted].set(g_flat[order])
    dest = jnp.zeros((n_assign,), jnp.int32).at[order].set(dest_sorted).reshape(t, TOP_K)
    block_expert = jnp.minimum(
        jnp.searchsorted(pad_ends, jnp.arange(nb, dtype=jnp.int32) * tb, side="right"),
        N_EXPERTS - 1).astype(jnp.int32)
    n_valid = (pad_ends[-1:] // tb).astype(jnp.int32)
    out = _moe_blocks(h[slot_tok], slot_gate[:, None], block_expert, n_valid, w1, w3, w2)
    return x + out[dest[:, 0]] + out[dest[:, 1]]


def _shifted(vin, buf, n, length, shift):
    v = vin.reshape(n, length, CONV_WIDTH)
    ext = jnp.concatenate([buf, v], axis=1)
    lo = CONV_K - 1 - shift
    return ext[:, lo:lo + length].reshape(n * length, CONV_WIDTH)


def kernel(x_prompt, x_sample, cache_k, cache_v, page_table, state_ssm_re, state_ssm_im, state_conv,
           norm_mix_g, norm_ffn_g, w_in, ssm_a_re, ssm_a_im, ssm_log_dt, ssm_b_re, ssm_b_im,
           ssm_c_re, ssm_c_im, ssm_d, ssm_w_glu, ssm_b_glu, q_norm_g, k_norm_g,
           lambda_q1, lambda_k1, lambda_q2, lambda_k2, head_norm_g, conv_w,
           w_br_ssm, w_br_att, w_br_conv, w_out, ffn_w1, ffn_w3, ffn_w2,
           router_w, moe_w1, moe_w3, moe_w2):
    n_p, seq, _ = x_prompt.shape
    n_s, dec, _ = x_sample.shape
    depth = w_in.shape[0]
    tp = n_p * seq
    ts = n_s * dec
    page = cache_k.shape[2]
    ck = cache_k.reshape(depth, cache_k.shape[1], page, QK_WIDTH)
    cv = cache_v.reshape(depth, cache_v.shape[1], page, ATT_WIDTH)
    seg = jnp.kron(jnp.eye(QK_WIDTH // ATT_HEAD_DIM, dtype=F32),
                   jnp.full((ATT_HEAD_DIM, ATT_HEAD_DIM), 1.0 / ATT_HEAD_DIM, F32)).astype(BF16)
    n_rep = QK_WIDTH // ATT_HEAD_DIM
    zero_buf = jnp.zeros((n_p, CONV_K - 1, CONV_WIDTH), F32)

    x = jnp.concatenate([x_prompt.reshape(tp, D_MODEL), x_sample.reshape(ts, D_MODEL)], axis=0)
    kp, vp, srp, sip, cvp = [], [], [], [], []
    ks, vs, srs, sis, cvs = [], [], [], [], []
    for l in range(depth):
        lam_init = 0.8 - 0.6 * math.exp(-0.3 * l)
        u, q, kf, kb, vf, vb, cb, vin, gates = _inproj(
            x, norm_mix_g[l][None], w_in[l].astype(BF16),
            jnp.tile(q_norm_g[l], n_rep)[None], jnp.tile(k_norm_g[l], n_rep)[None], seg)

        tables = _s5_tables(ssm_a_re[l], ssm_a_im[l], ssm_log_dt[l], ssm_b_re[l], ssm_b_im[l],
                            ssm_c_re[l], ssm_c_im[l], dec)
        yraw, p_re, p_im, s_re, s_im = _s5(u, state_ssm_re[l], state_ssm_im[l], tables, n_p, seq, n_s, dec)

        lp = jnp.stack([lambda_q1[l], lambda_k1[l], lambda_q2[l], lambda_k2[l]])
        hg = head_norm_g[l][None]
        yb_p = _attn_prompt(q[:tp], kb[:tp], vb[:tp], lp, hg, n_p, seq, lam_init)
        yb_s = _attn_sample(q[tp:].reshape(n_s, dec, QK_WIDTH), kb[tp:].reshape(n_s, dec, QK_WIDTH),
                            vb[tp:].reshape(n_s, dec, ATT_WIDTH), ck[l], cv[l], page_table, lp, hg, lam_init)
        yb = jnp.concatenate([yb_p, yb_s.reshape(ts, ATT_WIDTH)], axis=0)

        vm = [jnp.concatenate([_shifted(vin[:tp], zero_buf, n_p, seq, s),
                               _shifted(vin[tp:], state_conv[l], n_s, dec, s)], axis=0) for s in (1, 2)]
        x = _merge(x, yraw, u, ssm_d[l][None], ssm_w_glu[l].astype(BF16), ssm_b_glu[l][None], yb,
                   cb, vin, vm[0], vm[1], conv_w[l], gates,
                   w_br_ssm[l].astype(BF16), w_br_att[l].astype(BF16), w_br_conv[l].astype(BF16),
                   w_out[l].astype(BF16))

        i = l // 2
        if l % 2 == 0:
            x = _ffn(x, norm_ffn_g[l][None], ffn_w1[i].astype(BF16), ffn_w3[i].astype(BF16),
                     ffn_w2[i].astype(BF16))
        else:
            x = _moe(x, norm_ffn_g[l][None], router_w[i], moe_w1[i].astype(BF16),
                     moe_w3[i].astype(BF16), moe_w2[i].astype(BF16))

        ext_p = jnp.concatenate([zero_buf, vin[:tp].reshape(n_p, seq, CONV_WIDTH)], axis=1)
        ext_s = jnp.concatenate([state_conv[l], vin[tp:].reshape(n_s, dec, CONV_WIDTH)], axis=1)
        kp.append(kf[:tp].reshape(n_p, seq, ATT_HEADS, 2, ATT_HEAD_DIM))
        vp.append(vf[:tp].reshape(n_p, seq, ATT_HEADS, ATT_V_DIM))
        srp.append(p_re); sip.append(p_im); cvp.append(ext_p[:, -(CONV_K - 1):])
        ks.append(kf[tp:].reshape(n_s, dec, ATT_HEADS, 2, ATT_HEAD_DIM))
        vs.append(vf[tp:].reshape(n_s, dec, ATT_HEADS, ATT_V_DIM))
        srs.append(s_re); sis.append(s_im); cvs.append(ext_s[:, -(CONV_K - 1):])

    return (x[:tp].reshape(n_p, seq, D_MODEL), x[tp:].reshape(n_s, dec, D_MODEL),
            jnp.stack(kp), jnp.stack(vp), jnp.stack(srp), jnp.stack(sip), jnp.stack(cvp),
            jnp.stack(ks), jnp.stack(vs), jnp.stack(srs), jnp.stack(sis), jnp.stack(cvs))
```

```python
import functools
import math

import jax
import jax.numpy as jnp
from jax import lax
from jax.experimental import pallas as pl
from jax.experimental.pallas import tpu as pltpu

F32 = jnp.float32
BF16 = jnp.bfloat16

D_MODEL = 1024
SSM_WIDTH = 256
SSM_GROUP = 16
SSM_GROUPS = SSM_WIDTH // SSM_GROUP
SSM_STATE = 64
ATT_HEADS = 4
ATT_HEAD_DIM = 64
ATT_V_DIM = 2 * ATT_HEAD_DIM
QK_WIDTH = ATT_HEADS * 2 * ATT_HEAD_DIM
ATT_WIDTH = ATT_HEADS * ATT_V_DIM
CONV_WIDTH = 256
CONV_K = 3
N_BRANCH = 3
N_EXPERTS = 8
TOP_K = 2
EPS = 1e-6
NEG_INF = -1e30

C_U = 0
C_Q = C_U + SSM_WIDTH
C_K = C_Q + QK_WIDTH
C_V = C_K + QK_WIDTH
C_CB = C_V + ATT_WIDTH
C_CC = C_CB + CONV_WIDTH
C_CH = C_CC + CONV_WIDTH
C_G = C_CH + CONV_WIDTH
IN_COLS = C_G + N_BRANCH * D_MODEL

SSM_CHUNK = 8
SUBLANES = 8
ROW_TILE = 512
ATT_TILE = 512
MOE_TILE = 256
VMEM_LIMIT = 56 * 1024 * 1024


def _cparams(sem):
    return pltpu.CompilerParams(dimension_semantics=sem, vmem_limit_bytes=VMEM_LIMIT)


def _const_spec(shape):
    nd = len(shape)
    return pl.BlockSpec(shape, lambda *_: (0,) * nd)


def _bdot(a, b):
    return jnp.dot(a, b, preferred_element_type=F32)


def _rms_rows(x, g):
    ms = jnp.mean(x * x, axis=-1, keepdims=True)
    return x * lax.rsqrt(ms + EPS) * g


def _segment_rms(z, g, seg):
    sq = z * z
    hi = sq.astype(BF16)
    lo = (sq - hi.astype(F32)).astype(BF16)
    ms = _bdot(hi, seg) + _bdot(lo, seg)
    return z * lax.rsqrt(ms + EPS) * g


def _inproj_kernel(x_ref, g_ref, w_ref, qg_ref, kg_ref, seg_ref,
                   u_ref, q_ref, kf_ref, kb_ref, vf_ref, vb_ref, cb_ref, vin_ref, gate_ref):
    h = _rms_rows(x_ref[...], g_ref[...]).astype(BF16)

    def proj(a, b):
        return _bdot(h, w_ref[:, a:b])

    seg = seg_ref[...]
    u_ref[...] = proj(C_U, C_Q)
    qn = _segment_rms(proj(C_Q, C_K), qg_ref[...], seg)
    q_ref[...] = (qn * (ATT_HEAD_DIM ** -0.5)).astype(BF16)
    kn = _segment_rms(proj(C_K, C_V), kg_ref[...], seg)
    kf_ref[...] = kn
    kb_ref[...] = kn.astype(BF16)
    v = proj(C_V, C_CB)
    vf_ref[...] = v
    vb_ref[...] = v.astype(BF16)
    cb_ref[...] = proj(C_CB, C_CC)
    vin_ref[...] = proj(C_CC, C_CH) * proj(C_CH, C_G)
    for j in range(N_BRANCH):
        a = C_G + j * D_MODEL
        gate_ref[:, j * D_MODEL:(j + 1) * D_MODEL] = jax.nn.sigmoid(proj(a, a + D_MODEL))


def _inproj(x, g, w_bf, qg, kg, seg):
    t = x.shape[0]
    tm = ROW_TILE
    row = lambda w: pl.BlockSpec((tm, w), lambda i: (i, 0))
    outs = [
        (SSM_WIDTH, F32), (QK_WIDTH, BF16), (QK_WIDTH, F32), (QK_WIDTH, BF16),
        (ATT_WIDTH, F32), (ATT_WIDTH, BF16), (CONV_WIDTH, F32), (CONV_WIDTH, F32),
        (N_BRANCH * D_MODEL, F32),
    ]
    return pl.pallas_call(
        _inproj_kernel,
        out_shape=[jax.ShapeDtypeStruct((t, w), d) for w, d in outs],
        grid=(t // tm,),
        in_specs=[row(D_MODEL), _const_spec((1, D_MODEL)), _const_spec((D_MODEL, IN_COLS)),
                  _const_spec((1, QK_WIDTH)), _const_spec((1, QK_WIDTH)),
                  _const_spec((QK_WIDTH, QK_WIDTH))],
        out_specs=[row(w) for w, _ in outs],
        compiler_params=_cparams(("parallel",)),
        name="inproj",
    )(x, g, w_bf, qg, kg, seg)


def _s5_intra(u, kb_ref, tc):
    rowmod = lax.broadcasted_iota(jnp.int32, u.shape, 0) % tc
    y = _bdot(u.astype(BF16), kb_ref[0])
    for m in range(1, tc):
        um = jnp.where(rowmod >= m, pltpu.roll(u, m, axis=0), 0.0)
        y = y + _bdot(um.astype(BF16), kb_ref[m])
    return y


def _s5_inject(u_refs, pb_ref, k0, n_chunks, tc):
    w = None
    for k in range(tc):
        rows = pl.ds(k, n_chunks, stride=tc)
        uk = jnp.concatenate([r[rows, :] for r in u_refs], axis=1).astype(BF16)
        d = _bdot(uk, pb_ref[k0 + k])
        w = d if w is None else w + d
    return w


def _s5_readout(y_ref, y_scrs, y, s_in, qb_ref, n_chunks, tc):
    lanes = y_scrs[0].shape[1]
    for h, scr in enumerate(y_scrs):
        scr[...] = y[:, h * lanes:(h + 1) * lanes]
    sb = s_in.astype(BF16)
    for k in range(tc):
        rows = pl.ds(k, n_chunks, stride=tc)
        yk = _bdot(sb, qb_ref[k])
        for h, scr in enumerate(y_scrs):
            scr[rows, :] = scr[rows, :] + yk[:, h * lanes:(h + 1) * lanes]
    y_ref[...] = jnp.concatenate([scr[...] for scr in y_scrs], axis=1)


def _cmul_add(a_re, a_im, s, w, half):
    s_re, s_im = s[:, :half], s[:, half:]
    return jnp.concatenate([a_re * s_re - a_im * s_im, a_re * s_im + a_im * s_re], axis=1) + w


def _s5_prompt_kernel(ua_ref, ub_ref, kb_ref, pb_ref, qb_ref, a_ref, y_ref, fs_ref,
                      carry, w_scr, s_scr, ya_scr, yb_scr, *, tc, tiles_per_seq):
    tm = ua_ref.shape[0]
    n_chunks = tm // tc
    half = a_ref.shape[1]

    @pl.when(pl.program_id(0) % tiles_per_seq == 0)
    def _():
        carry[...] = jnp.zeros(carry.shape, F32)

    y = _s5_intra(jnp.concatenate([ua_ref[...], ub_ref[...]], axis=1), kb_ref, tc)
    w_scr[...] = _s5_inject((ua_ref, ub_ref), pb_ref, 0, n_chunks, tc)
    a_re = a_ref[0:1, :]
    a_im = a_ref[1:2, :]
    s = carry[...]
    for c in range(n_chunks):
        s_scr[c:c + 1, :] = s
        s = _cmul_add(a_re, a_im, s, w_scr[c:c + 1, :], half)
    carry[...] = s
    fs_ref[...] = s
    _s5_readout(y_ref, (ya_scr, yb_scr), y, s_scr[...], qb_ref, n_chunks, tc)


def _s5_sample_kernel(ua_ref, ub_ref, s0_ref, kb_ref, pb_ref, qb_ref, a_ref, y_ref, fs_ref,
                      ya_scr, yb_scr, *, tc, k0):
    n_chunks = ua_ref.shape[0] // tc
    half = a_ref.shape[1]
    y = _s5_intra(jnp.concatenate([ua_ref[...], ub_ref[...]], axis=1), kb_ref, tc)
    w = _s5_inject((ua_ref, ub_ref), pb_ref, k0, n_chunks, tc)
    s0 = s0_ref[...]
    fs_ref[...] = _cmul_add(a_ref[2:3, :], a_ref[3:4, :], s0, w, half)
    _s5_readout(y_ref, (ya_scr, yb_scr), y, s0, qb_ref, n_chunks, tc)


def _s5_tables(a_re, a_im, log_dt, b_re, b_im, c_re, c_im, dec):
    hp = lax.Precision.HIGHEST
    tc = SSM_CHUNK
    g, p, j = SSM_GROUPS, SSM_STATE, SSM_GROUP
    dt = jnp.exp(log_dt)[:, None]
    den = a_re * a_re + a_im * a_im
    ms = jnp.arange(tc + 1, dtype=F32)[None, :, None]
    mag = jnp.exp(a_re[:, None, :] * dt[:, None, :] * ms)
    ang = a_im[:, None, :] * dt[:, None, :] * ms
    pw_re = mag * jnp.cos(ang)
    pw_im = mag * jnp.sin(ang)
    ab_re, ab_im = pw_re[:, 1], pw_im[:, 1]
    nr = ab_re - 1.0
    cr = (nr * a_re + ab_im * a_im) / den
    ci = (ab_im * a_re - nr * a_im) / den
    bb_re = cr[..., None] * b_re - ci[..., None] * b_im
    bb_im = cr[..., None] * b_im + ci[..., None] * b_re
    eye = jnp.eye(g, dtype=F32)
    ct_re = c_re.transpose(0, 2, 1)
    ct_im = c_im.transpose(0, 2, 1)
    cbr = ct_re[:, :, None, :] * bb_re[:, :, :, None] - ct_im[:, :, None, :] * bb_im[:, :, :, None]
    cbi = ct_re[:, :, None, :] * bb_im[:, :, :, None] + ct_im[:, :, None, :] * bb_re[:, :, :, None]
    kern = (jnp.einsum('gmp,gpji->mgji', pw_re[:, :tc], cbr, precision=hp)
            - jnp.einsum('gmp,gpji->mgji', pw_im[:, :tc], cbi, precision=hp))
    kb = jnp.einsum('mgji,gh->mgjhi', kern, eye).reshape(tc, g * j, g * j)
    rv_re = pw_re[:, tc - 1::-1][:, :tc].transpose(1, 0, 2)
    rv_im = pw_im[:, tc - 1::-1][:, :tc].transpose(1, 0, 2)
    bt_re = bb_re.transpose(0, 2, 1)
    bt_im = bb_im.transpose(0, 2, 1)
    pr = rv_re[:, :, None, :] * bt_re[None] - rv_im[:, :, None, :] * bt_im[None]
    pi = rv_re[:, :, None, :] * bt_im[None] + rv_im[:, :, None, :] * bt_re[None]
    blk = lambda t: jnp.einsum('kgjp,gh->kgjhp', t, eye).reshape(tc, g * j, g * p)
    pb = jnp.concatenate([blk(pr), blk(pi)], axis=2)
    nx_re = pw_re[:, 1:].transpose(1, 0, 2)
    nx_im = pw_im[:, 1:].transpose(1, 0, 2)
    qr = nx_re[..., None] * ct_re[None] - nx_im[..., None] * ct_im[None]
    qi = -(nx_re[..., None] * ct_im[None] + nx_im[..., None] * ct_re[None])
    blq = lambda t: jnp.einsum('kgpi,gh->kgphi', t, eye).reshape(tc, g * p, g * j)
    qb = jnp.concatenate([blq(qr), blq(qi)], axis=1)
    flat = lambda t: t.reshape(1, g * p)
    adec = jnp.concatenate([flat(pw_re[:, tc]), flat(pw_im[:, tc]),
                            flat(pw_re[:, dec]), flat(pw_im[:, dec])], axis=0)
    return kb.astype(BF16), pb.astype(BF16), qb.astype(BF16), adec


def _s5(u, s0_re, s0_im, tables, n_p, seq, n_s, dec):
    kb, pb, qb, adec = tables
    tc = SSM_CHUNK
    g, p = SSM_GROUPS, SSM_STATE
    sw = 2 * g * p
    tm = ROW_TILE
    tp = n_p * seq
    ts = n_s * dec
    tiles_per_seq = seq // tm
    hw = SSM_WIDTH // 2
    assert dec <= tc and seq % tm == 0 and tm % tc == 0
    yp, fsp = pl.pallas_call(
        functools.partial(_s5_prompt_kernel, tc=tc, tiles_per_seq=tiles_per_seq),
        out_shape=[jax.ShapeDtypeStruct((tp, SSM_WIDTH), F32),
                   jax.ShapeDtypeStruct((n_p, 1, sw), F32)],
        grid=(tp // tm,),
        in_specs=[pl.BlockSpec((tm, hw), lambda i: (i, 0)), pl.BlockSpec((tm, hw), lambda i: (i, 1)),
                  _const_spec(kb.shape), _const_spec(pb.shape), _const_spec(qb.shape),
                  _const_spec(adec.shape)],
        out_specs=[pl.BlockSpec((tm, SSM_WIDTH), lambda i: (i, 0)),
                   pl.BlockSpec((None, 1, sw), lambda i: (i // tiles_per_seq, 0, 0))],
        scratch_shapes=[pltpu.VMEM((1, sw), F32), pltpu.VMEM((tm // tc, sw), F32),
                        pltpu.VMEM((tm // tc, sw), F32),
                        pltpu.VMEM((tm, hw), F32), pltpu.VMEM((tm, hw), F32)],
        compiler_params=_cparams(("arbitrary",)),
        name="s5_prompt",
    )(u, u, kb, pb, qb, adec)
    s0 = jnp.concatenate([s0_re.reshape(n_s, g * p), s0_im.reshape(n_s, g * p)], axis=1)
    us = u[tp:]
    ys, fss = pl.pallas_call(
        functools.partial(_s5_sample_kernel, tc=dec, k0=tc - dec),
        out_shape=[jax.ShapeDtypeStruct((ts, SSM_WIDTH), F32), jax.ShapeDtypeStruct((n_s, sw), F32)],
        grid=(1,),
        in_specs=[pl.BlockSpec((ts, hw), lambda i: (0, 0)), pl.BlockSpec((ts, hw), lambda i: (0, 1)),
                  _const_spec((n_s, sw)),
                  _const_spec(kb.shape), _const_spec(pb.shape), _const_spec(qb.shape),
                  _const_spec(adec.shape)],
        out_specs=[_const_spec((ts, SSM_WIDTH)), _const_spec((n_s, sw))],
        scratch_shapes=[pltpu.VMEM((ts, hw), F32), pltpu.VMEM((ts, hw), F32)],
        compiler_params=_cparams(("arbitrary",)),
        name="s5_sample",
    )(us, us, s0, kb, pb, qb, adec)
    half = g * p
    fsp = fsp.reshape(n_p, sw)
    st = lambda a, n: a.reshape(n, g, p)
    return (yp, ys, st(fsp[:, :half], n_p), st(fsp[:, half:], n_p),
            st(fss[:, :half], n_s), st(fss[:, half:], n_s))


def _lambda(lp_ref, lam_init):
    lp = lp_ref[...]
    s1 = jnp.sum(lp[0:1, :] * lp[1:2, :], axis=-1, keepdims=True)
    s2 = jnp.sum(lp[2:3, :] * lp[3:4, :], axis=-1, keepdims=True)
    return jnp.exp(s1) - jnp.exp(s2) + lam_init


def _attn_kernel(qi_ref, ki_ref, q_ref, k_ref, v_ref, lp_ref, hg_ref, o_ref, m_scr, acc_scr,
                 *, tile, lam_init):
    t = pl.program_id(1)
    qi = qi_ref[t]
    ki = ki_ref[t]
    vd = ATT_V_DIM

    @pl.when(ki == 0)
    def _():
        m_scr[...] = jnp.full(m_scr.shape, NEG_INF, F32)
        acc_scr[...] = jnp.zeros(acc_scr.shape, F32)

    def accumulate(masked):
        lane = lax.broadcasted_iota(jnp.int32, (tile, vd), 1)
        ones = jnp.ones((tile, vd), BF16)
        if masked:
            mask = (lax.broadcasted_iota(jnp.int32, (tile, tile), 1)
                    <= lax.broadcasted_iota(jnp.int32, (tile, tile), 0))
        nt = (((1,), (1,)), ((), ()))
        for h in range(ATT_HEADS):
            cols = slice(h * vd, (h + 1) * vd)
            q = q_ref[:, cols]
            k = k_ref[:, cols]
            v1 = jnp.concatenate([v_ref[:, cols], ones], axis=1)
            zero = jnp.zeros_like(q)
            for c in range(2):
                qm = jnp.where((lane >= ATT_HEAD_DIM) == bool(c), q, zero)
                s = lax.dot_general(qm, k, nt, preferred_element_type=F32)
                if masked:
                    s = jnp.where(mask, s, NEG_INF)
                idx = 2 * h + c
                m_old = m_scr[idx]
                m_new = jnp.maximum(m_old, jnp.max(s, axis=-1, keepdims=True))
                p = jnp.exp(s - m_new).astype(BF16)
                acc_scr[idx] = jnp.exp(m_old - m_new) * acc_scr[idx] + _bdot(p, v1)
                m_scr[idx] = m_new

    @pl.when(ki < qi)
    def _():
        accumulate(False)

    @pl.when(ki == qi)
    def _():
        accumulate(True)
        lam = _lambda(lp_ref, lam_init)
        hg = hg_ref[...]
        for h in range(ATT_HEADS):
            a1 = acc_scr[2 * h]
            a2 = acc_scr[2 * h + 1]
            o = a1[:, :vd] / a1[:, vd:] - lam * (a2[:, :vd] / a2[:, vd:])
            o_ref[:, h * vd:(h + 1) * vd] = (_rms_rows(o, hg) * (1.0 - lam_init)).astype(o_ref.dtype)


def _attn_prompt(q, k, v, lp, hg, n_p, seq, lam_init):
    tile = min(ATT_TILE, seq)
    nq = seq // tile
    pairs = [(i, j) for i in range(nq) for j in range(i + 1)]
    qi_tab = jnp.asarray([a for a, _ in pairs], jnp.int32)
    ki_tab = jnp.asarray([b for _, b in pairs], jnp.int32)
    qspec = pl.BlockSpec((tile, ATT_WIDTH), lambda b, t, qi, ki: (b * nq + qi[t], 0))
    kspec = pl.BlockSpec((tile, ATT_WIDTH), lambda b, t, qi, ki: (b * nq + ki[t], 0))
    grid_spec = pltpu.PrefetchScalarGridSpec(
        num_scalar_prefetch=2,
        grid=(n_p, len(pairs)),
        in_specs=[qspec, kspec, kspec,
                  pl.BlockSpec((4, ATT_HEAD_DIM), lambda b, t, qi, ki: (0, 0)),
                  pl.BlockSpec((1, ATT_V_DIM), lambda b, t, qi, ki: (0, 0))],
        out_specs=qspec,
        scratch_shapes=[pltpu.VMEM((2 * ATT_HEADS, tile, 1), F32),
                        pltpu.VMEM((2 * ATT_HEADS, tile, 2 * ATT_V_DIM), F32)],
    )
    return pl.pallas_call(
        functools.partial(_attn_kernel, tile=tile, lam_init=lam_init),
        out_shape=jax.ShapeDtypeStruct((n_p * seq, ATT_WIDTH), BF16),
        grid_spec=grid_spec,
        compiler_params=_cparams(("parallel", "arbitrary")),
        name="attn_prompt",
    )(qi_tab, ki_tab, q, k, v, lp, hg)


def _attn_sample_kernel(pt_ref, q_ref, kn_ref, vn_ref, lp_ref, hg_ref, *rest, n_pages, dec, lam_init):
    del pt_ref
    kp = rest[:n_pages]
    vp = rest[n_pages:2 * n_pages]
    o_ref = rest[2 * n_pages]
    page = kp[0].shape[1]
    grp = 2 * dec
    n_rows = ATT_HEADS * grp
    q = q_ref[...].astype(F32)
    qt = jnp.concatenate([q] * (ATT_HEADS * 2), axis=0)
    r = lax.broadcasted_iota(jnp.int32, (n_rows, QK_WIDTH), 0)
    c = lax.broadcasted_iota(jnp.int32, (n_rows, QK_WIDTH), 1)
    qb = jnp.where(r // dec == c // ATT_HEAD_DIM, qt, 0.0).astype(BF16)
    nt = (((1,), (1,)), ((), ()))
    s_pages = [_bdot(qb, kp[j][...].astype(BF16)) for j in range(n_pages)]
    s_new = lax.dot_general(qb, kn_ref[...], nt, preferred_element_type=F32)
    rn = lax.broadcasted_iota(jnp.int32, (n_rows, dec), 0) % dec
    cn = lax.broadcasted_iota(jnp.int32, (n_rows, dec), 1)
    s_new = jnp.where(cn <= rn, s_new, NEG_INF)
    m = jnp.max(s_new, axis=-1, keepdims=True)
    for s in s_pages:
        m = jnp.maximum(m, jnp.max(s, axis=-1, keepdims=True))
    p_new = jnp.exp(s_new - m)
    l = jnp.sum(p_new, axis=-1, keepdims=True)
    acc_new = _bdot(p_new.astype(BF16), vn_ref[...])
    ps = []
    for j in range(n_pages):
        p = jnp.exp(s_pages[j] - m)
        l = l + jnp.sum(p, axis=-1, keepdims=True)
        ps.append(p)
    lam = _lambda(lp_ref, lam_init)
    hg = hg_ref[...]
    for h in range(ATT_HEADS):
        rows = slice(h * grp, (h + 1) * grp)
        cols = slice(h * ATT_V_DIM, (h + 1) * ATT_V_DIM)
        acc = acc_new[rows, cols]
        for j in range(n_pages):
            vh = vp[j][pl.ds(h, page, stride=ATT_HEADS), :].astype(BF16)
            acc = acc + _bdot(ps[j][rows, :].astype(BF16), vh)
        acc = acc / l[rows, :]
        o = acc[:dec] - lam * acc[dec:]
        o_ref[:, cols] = (_rms_rows(o, hg) * (1.0 - lam_init)).astype(o_ref.dtype)


def _attn_sample(q, kn, vn, cache_kt, cache_vr, layer, page_table, lp, hg, lam_init):
    n_s, dec, _ = q.shape
    n_pages = page_table.shape[1]
    page = cache_kt.shape[3]
    tok = pl.BlockSpec((None, dec, QK_WIDTH), lambda n, pt: (n, 0, 0))
    kpages = [pl.BlockSpec((None, None, QK_WIDTH, page), lambda n, pt, j=j: (layer, pt[n, j], 0, 0))
              for j in range(n_pages)]
    vpages = [pl.BlockSpec((None, None, page * ATT_HEADS, ATT_V_DIM),
                           lambda n, pt, j=j: (layer, pt[n, j], 0, 0)) for j in range(n_pages)]
    grid_spec = pltpu.PrefetchScalarGridSpec(
        num_scalar_prefetch=1,
        grid=(n_s,),
        in_specs=[tok, tok, tok,
                  pl.BlockSpec((4, ATT_HEAD_DIM), lambda n, pt: (0, 0)),
                  pl.BlockSpec((1, ATT_V_DIM), lambda n, pt: (0, 0))] + kpages + vpages,
        out_specs=tok,
    )
    return pl.pallas_call(
        functools.partial(_attn_sample_kernel, n_pages=n_pages, dec=dec, lam_init=lam_init),
        out_shape=jax.ShapeDtypeStruct((n_s, dec, ATT_WIDTH), BF16),
        grid_spec=grid_spec,
        compiler_params=_cparams(("parallel",)),
        name="attn_sample",
    )(page_table, q, kn, vn, lp, hg, *([cache_kt] * n_pages), *([cache_vr] * n_pages))


def _merge_kernel(x_ref, u_ref, cb_ref, vin_ref, halo_ref, gate_ref,
                  yrp_ref, yrs_ref, ybp_ref, ybs_ref, vm1s_ref, vm2s_ref,
                  d_ref, wglu_ref, bglu_ref, cw_ref, wssm_ref, watt_ref, wconv_ref, wout_ref,
                  o_ref, *, n_prompt_tiles, tiles_per_seq):
    i = pl.program_id(0)
    is_prompt = i < n_prompt_tiles
    yraw = jnp.where(is_prompt, yrp_ref[...], yrs_ref[...])
    yb = jnp.where(is_prompt, ybp_ref[...], ybs_ref[...])
    y = jax.nn.gelu(yraw + d_ref[...] * u_ref[...])
    ya = y * jax.nn.sigmoid(_bdot(y.astype(BF16), wglu_ref[...]) + bglu_ref[...])

    vin = vin_ref[...]
    row = lax.broadcasted_iota(jnp.int32, vin.shape, 0)
    halo = jnp.where(i % tiles_per_seq == 0, 0.0, halo_ref[...])
    h1 = jnp.broadcast_to(halo[SUBLANES - 1:SUBLANES, :], vin.shape)
    h2 = jnp.broadcast_to(halo[SUBLANES - 2:SUBLANES - 1, :], vin.shape)
    vm1 = jnp.where(row == 0, h1, pltpu.roll(vin, 1, axis=0))
    vm2 = jnp.where(row == 0, h2, jnp.where(row == 1, h1, pltpu.roll(vin, 2, axis=0)))
    vm1 = jnp.where(is_prompt, vm1, vm1s_ref[...])
    vm2 = jnp.where(is_prompt, vm2, vm2s_ref[...])
    conv = vm2 * cw_ref[0:1, :] + vm1 * cw_ref[1:2, :] + vin * cw_ref[2:3, :]
    yc = cb_ref[...] * conv
    merged = (gate_ref[:, 0:D_MODEL] * _bdot(ya.astype(BF16), wssm_ref[...])
              + gate_ref[:, D_MODEL:2 * D_MODEL] * _bdot(yb, watt_ref[...])
              + gate_ref[:, 2 * D_MODEL:3 * D_MODEL] * _bdot(yc.astype(BF16), wconv_ref[...]))
    o_ref[...] = x_ref[...] + _bdot(merged.astype(BF16), wout_ref[...])


def _merge(x, u, cb, vin, gates, yr_p, yr_s, yb_p, yb_s, vm1_s, vm2_s,
           d, wglu, bglu, cw, wssm, watt, wconv, wout, n_prompt_tiles, tiles_per_seq):
    t = x.shape[0]
    tm = ROW_TILE
    npt = n_prompt_tiles
    row = lambda w: pl.BlockSpec((tm, w), lambda i: (i, 0))
    prow = lambda w: pl.BlockSpec((tm, w), lambda i: (jnp.minimum(i, npt - 1), 0))
    srow = lambda w: pl.BlockSpec((tm, w), lambda i: (jnp.maximum(i - npt, 0), 0))
    halo = pl.BlockSpec((SUBLANES, CONV_WIDTH), lambda i: (jnp.maximum(i * (tm // SUBLANES) - 1, 0), 0))
    return pl.pallas_call(
        functools.partial(_merge_kernel, n_prompt_tiles=npt, tiles_per_seq=tiles_per_seq),
        out_shape=jax.ShapeDtypeStruct((t, D_MODEL), F32),
        grid=(t // tm,),
        in_specs=[row(D_MODEL), row(SSM_WIDTH), row(CONV_WIDTH), row(CONV_WIDTH), halo,
                  row(N_BRANCH * D_MODEL),
                  prow(SSM_WIDTH), srow(SSM_WIDTH), prow(ATT_WIDTH), srow(ATT_WIDTH),
                  srow(CONV_WIDTH), srow(CONV_WIDTH),
                  _const_spec((1, SSM_WIDTH)), _const_spec((SSM_WIDTH, SSM_WIDTH)),
                  _const_spec((1, SSM_WIDTH)), _const_spec((CONV_K, CONV_WIDTH)),
                  _const_spec((SSM_WIDTH, D_MODEL)), _const_spec((ATT_WIDTH, D_MODEL)),
                  _const_spec((CONV_WIDTH, D_MODEL)), _const_spec((D_MODEL, D_MODEL))],
        out_specs=row(D_MODEL),
        compiler_params=_cparams(("parallel",)),
        name="merge",
    )(x, u, cb, vin, vin, gates, yr_p, yr_s, yb_p, yb_s, vm1_s, vm2_s,
      d, wglu, bglu, cw, wssm, watt, wconv, wout)


def _swiglu(h, w1, w3, w2):
    a = _bdot(h, w1)
    b = _bdot(h, w3)
    return _bdot((jax.nn.silu(a) * b).astype(BF16), w2)


def _ffn_kernel(x_ref, g_ref, w1_ref, w3_ref, w2_ref, o_ref):
    x = x_ref[...]
    h = _rms_rows(x, g_ref[...]).astype(BF16)
    o_ref[...] = x + _swiglu(h, w1_ref[...], w3_ref[...], w2_ref[...])


def _ffn(x, g, w1, w3, w2):
    t = x.shape[0]
    tm = ROW_TILE
    d_ff = w1.shape[1]
    row = pl.BlockSpec((tm, D_MODEL), lambda i: (i, 0))
    return pl.pallas_call(
        _ffn_kernel,
        out_shape=jax.ShapeDtypeStruct((t, D_MODEL), F32),
        grid=(t // tm,),
        in_specs=[row, _const_spec((1, D_MODEL)), _const_spec((D_MODEL, d_ff)),
                  _const_spec((D_MODEL, d_ff)), _const_spec((d_ff, D_MODEL))],
        out_specs=row,
        compiler_params=_cparams(("parallel",)),
        name="ffn",
    )(x, g, w1, w3, w2)


def _router_kernel(x_ref, g_ref, wr_ref, h_ref, idx_ref, gate_ref):
    h = _rms_rows(x_ref[...], g_ref[...])
    h_ref[...] = h.astype(BF16)
    logits = jnp.dot(h, wr_ref[...], preferred_element_type=F32, precision=lax.Precision.HIGHEST)
    lane = lax.broadcasted_iota(jnp.int32, logits.shape, 1)
    logits = jnp.where(lane < N_EXPERTS, logits, -jnp.inf)
    big = jnp.int32(logits.shape[1])
    m1 = jnp.max(logits, axis=-1, keepdims=True)
    i1 = jnp.min(jnp.where(logits == m1, lane, big), axis=-1, keepdims=True)
    rest = jnp.where(lane == i1, -jnp.inf, logits)
    m2 = jnp.max(rest, axis=-1, keepdims=True)
    i2 = jnp.min(jnp.where(rest == m2, lane, big), axis=-1, keepdims=True)
    e = jnp.exp(m2 - m1)
    g1 = 1.0 / (1.0 + e)
    g2 = e / (1.0 + e)
    idx_ref[...] = jnp.where(lane == 0, i1, jnp.where(lane == 1, i2, 0))
    gate_ref[...] = jnp.where(lane == 0, g1, jnp.where(lane == 1, g2, 0.0))


def _router(x, g, wr_pad):
    t = x.shape[0]
    tm = ROW_TILE
    lanes = wr_pad.shape[1]
    row = lambda w: pl.BlockSpec((tm, w), lambda i: (i, 0))
    return pl.pallas_call(
        _router_kernel,
        out_shape=[jax.ShapeDtypeStruct((t, D_MODEL), BF16),
                   jax.ShapeDtypeStruct((t, lanes), jnp.int32),
                   jax.ShapeDtypeStruct((t, lanes), F32)],
        grid=(t // tm,),
        in_specs=[row(D_MODEL), _const_spec((1, D_MODEL)), _const_spec((D_MODEL, lanes))],
        out_specs=[row(D_MODEL), row(lanes), row(lanes)],
        compiler_params=_cparams(("parallel",)),
        name="router",
    )(x, g, wr_pad)


def _moe_kernel(be_ref, nv_ref, x_ref, gate_ref, w1_ref, w3_ref, w2_ref, o_ref):
    del be_ref

    @pl.when(pl.program_id(0) < nv_ref[0])
    def _():
        o_ref[...] = _swiglu(x_ref[...], w1_ref[...], w3_ref[...], w2_ref[...]) * gate_ref[...]


def _moe_blocks(xs, slot_gate, block_expert, n_valid, w1, w3, w2):
    cap = xs.shape[0]
    tb = MOE_TILE
    d_ff = w1.shape[2]
    nb = cap // tb
    rows = lambda w: pl.BlockSpec((tb, w), lambda i, be, nv: (jnp.minimum(i, nv[0] - 1), 0))
    grid_spec = pltpu.PrefetchScalarGridSpec(
        num_scalar_prefetch=2,
        grid=(nb,),
        in_specs=[rows(D_MODEL), rows(1),
                  pl.BlockSpec((None, D_MODEL, d_ff), lambda i, be, nv: (be[i], 0, 0)),
                  pl.BlockSpec((None, D_MODEL, d_ff), lambda i, be, nv: (be[i], 0, 0)),
                  pl.BlockSpec((None, d_ff, D_MODEL), lambda i, be, nv: (be[i], 0, 0))],
        out_specs=pl.BlockSpec((tb, D_MODEL), lambda i, be, nv: (i, 0)),
    )
    return pl.pallas_call(
        _moe_kernel,
        out_shape=jax.ShapeDtypeStruct((cap, D_MODEL), F32),
        grid_spec=grid_spec,
        compiler_params=_cparams(("arbitrary",)),
        name="moe",
    )(block_expert, n_valid, xs, slot_gate, w1, w3, w2)


def _moe(x, g, wr, w1, w3, w2):
    t = x.shape[0]
    tb = MOE_TILE
    lanes = 128
    wr_pad = jnp.pad(wr, ((0, 0), (0, lanes - N_EXPERTS)))
    h, idx, gate = _router(x, g, wr_pad)
    n_assign = t * TOP_K
    e_flat = idx[:, :TOP_K].reshape(-1)
    g_flat = gate[:, :TOP_K].reshape(-1)
    tok_flat = jnp.arange(n_assign, dtype=jnp.int32) // TOP_K
    order = jnp.argsort(e_flat, stable=True)
    e_sorted = e_flat[order]
    counts = jnp.bincount(e_flat, length=N_EXPERTS).astype(jnp.int32)
    starts = jnp.cumsum(counts) - counts
    padded = (counts + tb - 1) // tb * tb
    pad_ends = jnp.cumsum(padded)
    pad_starts = pad_ends - padded
    dest_sorted = pad_starts[e_sorted] + (jnp.arange(n_assign, dtype=jnp.int32) - starts[e_sorted])
    nb = -(-(n_assign + N_EXPERTS * (tb - 1)) // tb)
    cap = nb * tb
    slot_tok = jnp.zeros((cap,), jnp.int32).at[dest_sorted].set(tok_flat[order])
    slot_gate = jnp.zeros((cap,), F32).at[dest_sorted].set(g_flat[order])
    dest = jnp.zeros((n_assign,), jnp.int32).at[order].set(dest_sorted).reshape(t, TOP_K)
    block_expert = jnp.minimum(
        jnp.searchsorted(pad_ends, jnp.arange(nb, dtype=jnp.int32) * tb, side="right"),
        N_EXPERTS - 1).astype(jnp.int32)
    n_valid = (pad_ends[-1:] // tb).astype(jnp.int32)
    out = _moe_blocks(h[slot_tok], slot_gate[:, None], block_expert, n_valid, w1, w3, w2)
    return x + out[dest[:, 0]] + out[dest[:, 1]]


def kernel(x_prompt, x_sample, cache_k, cache_v, page_table, state_ssm_re, state_ssm_im, state_conv,
           norm_mix_g, norm_ffn_g, w_in, ssm_a_re, ssm_a_im, ssm_log_dt, ssm_b_re, ssm_b_im,
           ssm_c_re, ssm_c_im, ssm_d, ssm_w_glu, ssm_b_glu, q_norm_g, k_norm_g,
           lambda_q1, lambda_k1, lambda_q2, lambda_k2, head_norm_g, conv_w,
           w_br_ssm, w_br_att, w_br_conv, w_out, ffn_w1, ffn_w3, ffn_w2,
           router_w, moe_w1, moe_w3, moe_w2):
    n_p, seq, _ = x_prompt.shape
    n_s, dec, _ = x_sample.shape
    depth = w_in.shape[0]
    tp = n_p * seq
    ts = n_s * dec
    assert seq % ROW_TILE == 0 and ts % ROW_TILE == 0 and seq >= CONV_K - 1
    pool, page = cache_k.shape[1], cache_k.shape[2]
    cache_kt = cache_k.reshape(depth, pool, page, QK_WIDTH).transpose(0, 1, 3, 2)
    cache_vr = cache_v.reshape(depth, pool, page * ATT_HEADS, ATT_V_DIM)
    seg = jnp.kron(jnp.eye(QK_WIDTH // ATT_HEAD_DIM, dtype=F32),
                   jnp.full((ATT_HEAD_DIM, ATT_HEAD_DIM), 1.0 / ATT_HEAD_DIM, F32)).astype(BF16)
    n_rep = QK_WIDTH // ATT_HEAD_DIM

    x = jnp.concatenate([x_prompt.reshape(tp, D_MODEL), x_sample.reshape(ts, D_MODEL)], axis=0)
    kp, vp, srp, sip, cvp = [], [], [], [], []
    ks, vs, srs, sis, cvs = [], [], [], [], []
    for l in range(depth):
        lam_init = 0.8 - 0.6 * math.exp(-0.3 * l)
        u, q, kf, kb, vf, vb, cb, vin, gates = _inproj(
            x, norm_mix_g[l][None], w_in[l].astype(BF16),
            jnp.tile(q_norm_g[l], n_rep)[None], jnp.tile(k_norm_g[l], n_rep)[None], seg)

        tables = _s5_tables(ssm_a_re[l], ssm_a_im[l], ssm_log_dt[l], ssm_b_re[l], ssm_b_im[l],
                            ssm_c_re[l], ssm_c_im[l], dec)
        yr_p, yr_s, p_re, p_im, s_re, s_im = _s5(u, state_ssm_re[l], state_ssm_im[l], tables,
                                                  n_p, seq, n_s, dec)

        lp = jnp.stack([lambda_q1[l], lambda_k1[l], lambda_q2[l], lambda_k2[l]])
        hg = head_norm_g[l][None]
        yb_p = _attn_prompt(q, kb, vb, lp, hg, n_p, seq, lam_init)
        yb_s = _attn_sample(q[tp:].reshape(n_s, dec, QK_WIDTH), kb[tp:].reshape(n_s, dec, QK_WIDTH),
                            vb[tp:].reshape(n_s, dec, ATT_WIDTH), cache_kt, cache_vr, l, page_table,
                            lp, hg, lam_init).reshape(ts, ATT_WIDTH)

        ext_s = jnp.concatenate([state_conv[l], vin[tp:].reshape(n_s, dec, CONV_WIDTH)], axis=1)
        vm1_s = ext_s[:, 1:1 + dec].reshape(ts, CONV_WIDTH)
        vm2_s = ext_s[:, 0:dec].reshape(ts, CONV_WIDTH)
        x = _merge(x, u, cb, vin, gates, yr_p, yr_s, yb_p, yb_s, vm1_s, vm2_s,
                   ssm_d[l][None], ssm_w_glu[l].astype(BF16), ssm_b_glu[l][None], conv_w[l],
                   w_br_ssm[l].astype(BF16), w_br_att[l].astype(BF16), w_br_conv[l].astype(BF16),
                   w_out[l].astype(BF16), tp // ROW_TILE, seq // ROW_TILE)

        i = l // 2
        if l % 2 == 0:
            x = _ffn(x, norm_ffn_g[l][None], ffn_w1[i].astype(BF16), ffn_w3[i].astype(BF16),
                     ffn_w2[i].astype(BF16))
        else:
            x = _moe(x, norm_ffn_g[l][None], router_w[i], moe_w1[i].astype(BF16),
                     moe_w3[i].astype(BF16), moe_w2[i].astype(BF16))

        kp.append(kf[:tp].reshape(n_p, seq, ATT_HEADS, 2, ATT_HEAD_DIM))
        vp.append(vf[:tp].reshape(n_p, seq, ATT_HEADS, ATT_V_DIM))
        srp.append(p_re); sip.append(p_im)
        cvp.append(vin[:tp].reshape(n_p, seq, CONV_WIDTH)[:, seq - (CONV_K - 1):])
        ks.append(kf[tp:].reshape(n_s, dec, ATT_HEADS, 2, ATT_HEAD_DIM))
        vs.append(vf[tp:].reshape(n_s, dec, ATT_HEADS, ATT_V_DIM))
        srs.append(s_re); sis.append(s_im); cvs.append(ext_s[:, dec:])

    return (x[:tp].reshape(n_p, seq, D_MODEL), x[tp:].reshape(n_s, dec, D_MODEL),
            jnp.stack(kp), jnp.stack(vp), jnp.stack(srp), jnp.stack(sip), jnp.stack(cvp),
            jnp.stack(ks), jnp.stack(vs), jnp.stack(srs), jnp.stack(sis), jnp.stack(cvs))
```

```python
import functools
import math

import jax
import jax.numpy as jnp
from jax import lax
from jax.experimental import pallas as pl
from jax.experimental.pallas import tpu as pltpu
from jax.experimental.pallas import tpu_sc as plsc

F32 = jnp.float32
BF16 = jnp.bfloat16

D_MODEL = 1024
SSM_WIDTH = 256
SSM_GROUP = 16
SSM_GROUPS = SSM_WIDTH // SSM_GROUP
SSM_STATE = 64
ATT_HEADS = 4
ATT_HEAD_DIM = 64
ATT_V_DIM = 2 * ATT_HEAD_DIM
QK_WIDTH = ATT_HEADS * 2 * ATT_HEAD_DIM
ATT_WIDTH = ATT_HEADS * ATT_V_DIM
CONV_WIDTH = 256
CONV_K = 3
N_BRANCH = 3
N_EXPERTS = 8
TOP_K = 2
EPS = 1e-6
NEG_INF = -1e30

C_U = 0
C_Q = C_U + SSM_WIDTH
C_K = C_Q + QK_WIDTH
C_V = C_K + QK_WIDTH
C_CB = C_V + ATT_WIDTH
C_CC = C_CB + CONV_WIDTH
C_CH = C_CC + CONV_WIDTH
C_G = C_CH + CONV_WIDTH
IN_COLS = C_G + N_BRANCH * D_MODEL

SSM_CHUNK = 8
SUBLANES = 8
ROW_TILE = 512
ATT_TILE = 512
MOE_TILE = 256
SC_CORES = 2
SC_WORKERS = SC_CORES * 16
SC_ROWS = 48
VMEM_LIMIT = 56 * 1024 * 1024


def _cparams(sem):
    return pltpu.CompilerParams(dimension_semantics=sem, vmem_limit_bytes=VMEM_LIMIT)


def _const_spec(shape):
    nd = len(shape)
    return pl.BlockSpec(shape, lambda *_: (0,) * nd)


def _bdot(a, b):
    return jnp.dot(a, b, preferred_element_type=F32)


def _rms_rows(x, g):
    ms = jnp.mean(x * x, axis=-1, keepdims=True)
    return x * lax.rsqrt(ms + EPS) * g


def _segment_rms(z, g, seg):
    sq = z * z
    hi = sq.astype(BF16)
    lo = (sq - hi.astype(F32)).astype(BF16)
    ms = _bdot(hi, seg) + _bdot(lo, seg)
    return z * lax.rsqrt(ms + EPS) * g


def _inproj_kernel(x_ref, g_ref, w_ref, qg_ref, kg_ref, seg_ref,
                   u_ref, q_ref, kf_ref, kb_ref, vf_ref, vb_ref, cb_ref, vin_ref, gate_ref):
    h = _rms_rows(x_ref[...], g_ref[...]).astype(BF16)

    def proj(a, b):
        return _bdot(h, w_ref[:, a:b])

    seg = seg_ref[...]
    u_ref[...] = proj(C_U, C_Q)
    qn = _segment_rms(proj(C_Q, C_K), qg_ref[...], seg)
    q_ref[...] = (qn * (ATT_HEAD_DIM ** -0.5)).astype(BF16)
    kn = _segment_rms(proj(C_K, C_V), kg_ref[...], seg)
    kf_ref[...] = kn
    kb_ref[...] = kn.astype(BF16)
    v = proj(C_V, C_CB)
    vf_ref[...] = v
    vb_ref[...] = v.astype(BF16)
    cb_ref[...] = proj(C_CB, C_CC)
    vin_ref[...] = proj(C_CC, C_CH) * proj(C_CH, C_G)
    for j in range(N_BRANCH):
        a = C_G + j * D_MODEL
        gate_ref[:, j * D_MODEL:(j + 1) * D_MODEL] = jax.nn.sigmoid(proj(a, a + D_MODEL))


def _inproj(x, g, w_bf, qg, kg, seg):
    t = x.shape[0]
    tm = ROW_TILE
    row = lambda w: pl.BlockSpec((tm, w), lambda i: (i, 0))
    outs = [
        (SSM_WIDTH, F32), (QK_WIDTH, BF16), (QK_WIDTH, F32), (QK_WIDTH, BF16),
        (ATT_WIDTH, F32), (ATT_WIDTH, BF16), (CONV_WIDTH, F32), (CONV_WIDTH, F32),
        (N_BRANCH * D_MODEL, F32),
    ]
    return pl.pallas_call(
        _inproj_kernel,
        out_shape=[jax.ShapeDtypeStruct((t, w), d) for w, d in outs],
        grid=(t // tm,),
        in_specs=[row(D_MODEL), _const_spec((1, D_MODEL)), _const_spec((D_MODEL, IN_COLS)),
                  _const_spec((1, QK_WIDTH)), _const_spec((1, QK_WIDTH)),
                  _const_spec((QK_WIDTH, QK_WIDTH))],
        out_specs=[row(w) for w, _ in outs],
        compiler_params=_cparams(("parallel",)),
        name="inproj",
    )(x, g, w_bf, qg, kg, seg)


def _s5_intra(u, kb_ref, tc):
    rowmod = lax.broadcasted_iota(jnp.int32, u.shape, 0) % tc
    y = _bdot(u.astype(BF16), kb_ref[0])
    for m in range(1, tc):
        um = jnp.where(rowmod >= m, pltpu.roll(u, m, axis=0), 0.0)
        y = y + _bdot(um.astype(BF16), kb_ref[m])
    return y


def _s5_inject(u_refs, pb_ref, k0, n_chunks, tc):
    w = None
    for k in range(tc):
        rows = pl.ds(k, n_chunks, stride=tc)
        uk = jnp.concatenate([r[rows, :] for r in u_refs], axis=1).astype(BF16)
        d = _bdot(uk, pb_ref[k0 + k])
        w = d if w is None else w + d
    return w


def _s5_readout(y_ref, y_scrs, y, s_in, qb_ref, n_chunks, tc):
    lanes = y_scrs[0].shape[1]
    for h, scr in enumerate(y_scrs):
        scr[...] = y[:, h * lanes:(h + 1) * lanes]
    sb = s_in.astype(BF16)
    for k in range(tc):
        rows = pl.ds(k, n_chunks, stride=tc)
        yk = _bdot(sb, qb_ref[k])
        for h, scr in enumerate(y_scrs):
            scr[rows, :] = scr[rows, :] + yk[:, h * lanes:(h + 1) * lanes]
    y_ref[...] = jnp.concatenate([scr[...] for scr in y_scrs], axis=1)


def _cmul_add(a_re, a_im, s, w, half):
    s_re, s_im = s[:, :half], s[:, half:]
    return jnp.concatenate([a_re * s_re - a_im * s_im, a_re * s_im + a_im * s_re], axis=1) + w


def _s5_prompt_kernel(ua_ref, ub_ref, kb_ref, pb_ref, qb_ref, a_ref, y_ref, fs_ref,
                      carry, w_scr, s_scr, ya_scr, yb_scr, *, tc, tiles_per_seq):
    tm = ua_ref.shape[0]
    n_chunks = tm // tc
    half = a_ref.shape[1]

    @pl.when(pl.program_id(0) % tiles_per_seq == 0)
    def _():
        carry[...] = jnp.zeros(carry.shape, F32)

    y = _s5_intra(jnp.concatenate([ua_ref[...], ub_ref[...]], axis=1), kb_ref, tc)
    w_scr[...] = _s5_inject((ua_ref, ub_ref), pb_ref, 0, n_chunks, tc)
    a_re = a_ref[0:1, :]
    a_im = a_ref[1:2, :]
    s = carry[...]
    for c in range(n_chunks):
        s_scr[c:c + 1, :] = s
        s = _cmul_add(a_re, a_im, s, w_scr[c:c + 1, :], half)
    carry[...] = s
    fs_ref[...] = s
    _s5_readout(y_ref, (ya_scr, yb_scr), y, s_scr[...], qb_ref, n_chunks, tc)


def _s5_sample_kernel(ua_ref, ub_ref, s0_ref, kb_ref, pb_ref, qb_ref, a_ref, y_ref, fs_ref,
                      ya_scr, yb_scr, *, tc, k0):
    n_chunks = ua_ref.shape[0] // tc
    half = a_ref.shape[1]
    y = _s5_intra(jnp.concatenate([ua_ref[...], ub_ref[...]], axis=1), kb_ref, tc)
    w = _s5_inject((ua_ref, ub_ref), pb_ref, k0, n_chunks, tc)
    s0 = s0_ref[...]
    fs_ref[...] = _cmul_add(a_ref[2:3, :], a_ref[3:4, :], s0, w, half)
    _s5_readout(y_ref, (ya_scr, yb_scr), y, s0, qb_ref, n_chunks, tc)


def _s5_tables(a_re, a_im, log_dt, b_re, b_im, c_re, c_im, dec):
    hp = lax.Precision.HIGHEST
    tc = SSM_CHUNK
    g, p, j = SSM_GROUPS, SSM_STATE, SSM_GROUP
    dt = jnp.exp(log_dt)[:, None]
    den = a_re * a_re + a_im * a_im
    ms = jnp.arange(tc + 1, dtype=F32)[None, :, None]
    mag = jnp.exp(a_re[:, None, :] * dt[:, None, :] * ms)
    ang = a_im[:, None, :] * dt[:, None, :] * ms
    pw_re = mag * jnp.cos(ang)
    pw_im = mag * jnp.sin(ang)
    ab_re, ab_im = pw_re[:, 1], pw_im[:, 1]
    nr = ab_re - 1.0
    cr = (nr * a_re + ab_im * a_im) / den
    ci = (ab_im * a_re - nr * a_im) / den
    bb_re = cr[..., None] * b_re - ci[..., None] * b_im
    bb_im = cr[..., None] * b_im + ci[..., None] * b_re
    eye = jnp.eye(g, dtype=F32)
    ct_re = c_re.transpose(0, 2, 1)
    ct_im = c_im.transpose(0, 2, 1)
    cbr = ct_re[:, :, None, :] * bb_re[:, :, :, None] - ct_im[:, :, None, :] * bb_im[:, :, :, None]
    cbi = ct_re[:, :, None, :] * bb_im[:, :, :, None] + ct_im[:, :, None, :] * bb_re[:, :, :, None]
    kern = (jnp.einsum('gmp,gpji->mgji', pw_re[:, :tc], cbr, precision=hp)
            - jnp.einsum('gmp,gpji->mgji', pw_im[:, :tc], cbi, precision=hp))
    kb = jnp.einsum('mgji,gh->mgjhi', kern, eye).reshape(tc, g * j, g * j)
    rv_re = pw_re[:, tc - 1::-1][:, :tc].transpose(1, 0, 2)
    rv_im = pw_im[:, tc - 1::-1][:, :tc].transpose(1, 0, 2)
    bt_re = bb_re.transpose(0, 2, 1)
    bt_im = bb_im.transpose(0, 2, 1)
    pr = rv_re[:, :, None, :] * bt_re[None] - rv_im[:, :, None, :] * bt_im[None]
    pi = rv_re[:, :, None, :] * bt_im[None] + rv_im[:, :, None, :] * bt_re[None]
    blk = lambda t: jnp.einsum('kgjp,gh->kgjhp', t, eye).reshape(tc, g * j, g * p)
    pb = jnp.concatenate([blk(pr), blk(pi)], axis=2)
    nx_re = pw_re[:, 1:].transpose(1, 0, 2)
    nx_im = pw_im[:, 1:].transpose(1, 0, 2)
    qr = nx_re[..., None] * ct_re[None] - nx_im[..., None] * ct_im[None]
    qi = -(nx_re[..., None] * ct_im[None] + nx_im[..., None] * ct_re[None])
    blq = lambda t: jnp.einsum('kgpi,gh->kgphi', t, eye).reshape(tc, g * p, g * j)
    qb = jnp.concatenate([blq(qr), blq(qi)], axis=1)
    flat = lambda t: t.reshape(1, g * p)
    adec = jnp.concatenate([flat(pw_re[:, tc]), flat(pw_im[:, tc]),
                            flat(pw_re[:, dec]), flat(pw_im[:, dec])], axis=0)
    return kb.astype(BF16), pb.astype(BF16), qb.astype(BF16), adec


def _s5(u, s0_re, s0_im, tables, n_p, seq, n_s, dec):
    kb, pb, qb, adec = tables
    tc = SSM_CHUNK
    g, p = SSM_GROUPS, SSM_STATE
    sw = 2 * g * p
    tm = ROW_TILE
    tp = n_p * seq
    ts = n_s * dec
    tiles_per_seq = seq // tm
    hw = SSM_WIDTH // 2
    assert dec <= tc and seq % tm == 0 and tm % tc == 0
    yp, fsp = pl.pallas_call(
        functools.partial(_s5_prompt_kernel, tc=tc, tiles_per_seq=tiles_per_seq),
        out_shape=[jax.ShapeDtypeStruct((tp, SSM_WIDTH), F32),
                   jax.ShapeDtypeStruct((n_p, 1, sw), F32)],
        grid=(tp // tm,),
        in_specs=[pl.BlockSpec((tm, hw), lambda i: (i, 0)), pl.BlockSpec((tm, hw), lambda i: (i, 1)),
                  _const_spec(kb.shape), _const_spec(pb.shape), _const_spec(qb.shape),
                  _const_spec(adec.shape)],
        out_specs=[pl.BlockSpec((tm, SSM_WIDTH), lambda i: (i, 0)),
                   pl.BlockSpec((None, 1, sw), lambda i: (i // tiles_per_seq, 0, 0))],
        scratch_shapes=[pltpu.VMEM((1, sw), F32), pltpu.VMEM((tm // tc, sw), F32),
                        pltpu.VMEM((tm // tc, sw), F32),
                        pltpu.VMEM((tm, hw), F32), pltpu.VMEM((tm, hw), F32)],
        compiler_params=_cparams(("arbitrary",)),
        name="s5_prompt",
    )(u, u, kb, pb, qb, adec)
    s0 = jnp.concatenate([s0_re.reshape(n_s, g * p), s0_im.reshape(n_s, g * p)], axis=1)
    us = u[tp:]
    ys, fss = pl.pallas_call(
        functools.partial(_s5_sample_kernel, tc=dec, k0=tc - dec),
        out_shape=[jax.ShapeDtypeStruct((ts, SSM_WIDTH), F32), jax.ShapeDtypeStruct((n_s, sw), F32)],
        grid=(1,),
        in_specs=[pl.BlockSpec((ts, hw), lambda i: (0, 0)), pl.BlockSpec((ts, hw), lambda i: (0, 1)),
                  _const_spec((n_s, sw)),
                  _const_spec(kb.shape), _const_spec(pb.shape), _const_spec(qb.shape),
                  _const_spec(adec.shape)],
        out_specs=[_const_spec((ts, SSM_WIDTH)), _const_spec((n_s, sw))],
        scratch_shapes=[pltpu.VMEM((ts, hw), F32), pltpu.VMEM((ts, hw), F32)],
        compiler_params=_cparams(("arbitrary",)),
        name="s5_sample",
    )(us, us, s0, kb, pb, qb, adec)
    half = g * p
    fsp = fsp.reshape(n_p, sw)
    st = lambda a, n: a.reshape(n, g, p)
    return (yp, ys, st(fsp[:, :half], n_p), st(fsp[:, half:], n_p),
            st(fss[:, :half], n_s), st(fss[:, half:], n_s))


def _lambda(lp_ref, lam_init):
    lp = lp_ref[...]
    s1 = jnp.sum(lp[0:1, :] * lp[1:2, :], axis=-1, keepdims=True)
    s2 = jnp.sum(lp[2:3, :] * lp[3:4, :], axis=-1, keepdims=True)
    return jnp.exp(s1) - jnp.exp(s2) + lam_init


def _attn_kernel(qi_ref, ki_ref, q_ref, k_ref, v_ref, lp_ref, hg_ref, o_ref, m_scr, acc_scr,
                 *, tile, lam_init):
    t = pl.program_id(1)
    qi = qi_ref[t]
    ki = ki_ref[t]
    vd = ATT_V_DIM

    @pl.when(ki == 0)
    def _():
        m_scr[...] = jnp.full(m_scr.shape, NEG_INF, F32)
        acc_scr[...] = jnp.zeros(acc_scr.shape, F32)

    def accumulate(masked):
        lane = lax.broadcasted_iota(jnp.int32, (tile, vd), 1)
        ones = jnp.ones((tile, vd), BF16)
        if masked:
            mask = (lax.broadcasted_iota(jnp.int32, (tile, tile), 1)
                    <= lax.broadcasted_iota(jnp.int32, (tile, tile), 0))
        nt = (((1,), (1,)), ((), ()))
        for h in range(ATT_HEADS):
            cols = slice(h * vd, (h + 1) * vd)
            q = q_ref[:, cols]
            k = k_ref[:, cols]
            v1 = jnp.concatenate([v_ref[:, cols], ones], axis=1)
            zero = jnp.zeros_like(q)
            for c in range(2):
                qm = jnp.where((lane >= ATT_HEAD_DIM) == bool(c), q, zero)
                s = lax.dot_general(qm, k, nt, preferred_element_type=F32)
                if masked:
                    s = jnp.where(mask, s, NEG_INF)
                idx = 2 * h + c
                m_old = m_scr[idx]
                m_new = jnp.maximum(m_old, jnp.max(s, axis=-1, keepdims=True))
                p = jnp.exp(s - m_new).astype(BF16)
                acc_scr[idx] = jnp.exp(m_old - m_new) * acc_scr[idx] + _bdot(p, v1)
                m_scr[idx] = m_new

    @pl.when(ki < qi)
    def _():
        accumulate(False)

    @pl.when(ki == qi)
    def _():
        accumulate(True)
        lam = _lambda(lp_ref, lam_init)
        hg = hg_ref[...]
        for h in range(ATT_HEADS):
            a1 = acc_scr[2 * h]
            a2 = acc_scr[2 * h + 1]
            o = a1[:, :vd] / a1[:, vd:] - lam * (a2[:, :vd] / a2[:, vd:])
            o_ref[:, h * vd:(h + 1) * vd] = (_rms_rows(o, hg) * (1.0 - lam_init)).astype(o_ref.dtype)


def _attn_prompt(q, k, v, lp, hg, n_p, seq, lam_init):
    tile = min(ATT_TILE, seq)
    nq = seq // tile
    pairs = [(i, j) for i in range(nq) for j in range(i + 1)]
    qi_tab = jnp.asarray([a for a, _ in pairs], jnp.int32)
    ki_tab = jnp.asarray([b for _, b in pairs], jnp.int32)
    qspec = pl.BlockSpec((tile, ATT_WIDTH), lambda b, t, qi, ki: (b * nq + qi[t], 0))
    kspec = pl.BlockSpec((tile, ATT_WIDTH), lambda b, t, qi, ki: (b * nq + ki[t], 0))
    grid_spec = pltpu.PrefetchScalarGridSpec(
        num_scalar_prefetch=2,
        grid=(n_p, len(pairs)),
        in_specs=[qspec, kspec, kspec,
                  pl.BlockSpec((4, ATT_HEAD_DIM), lambda b, t, qi, ki: (0, 0)),
                  pl.BlockSpec((1, ATT_V_DIM), lambda b, t, qi, ki: (0, 0))],
        out_specs=qspec,
        scratch_shapes=[pltpu.VMEM((2 * ATT_HEADS, tile, 1), F32),
                        pltpu.VMEM((2 * ATT_HEADS, tile, 2 * ATT_V_DIM), F32)],
    )
    return pl.pallas_call(
        functools.partial(_attn_kernel, tile=tile, lam_init=lam_init),
        out_shape=jax.ShapeDtypeStruct((n_p * seq, ATT_WIDTH), BF16),
        grid_spec=grid_spec,
        compiler_params=_cparams(("parallel", "arbitrary")),
        name="attn_prompt",
    )(qi_tab, ki_tab, q, k, v, lp, hg)


def _attn_sample_kernel(pt_ref, q_ref, kn_ref, vn_ref, lp_ref, hg_ref, *rest, n_pages, dec, lam_init):
    del pt_ref
    kp = rest[:n_pages]
    vp = rest[n_pages:2 * n_pages]
    o_ref = rest[2 * n_pages]
    page = kp[0].shape[1]
    grp = 2 * dec
    n_rows = ATT_HEADS * grp
    q = q_ref[...].astype(F32)
    qt = jnp.concatenate([q] * (ATT_HEADS * 2), axis=0)
    r = lax.broadcasted_iota(jnp.int32, (n_rows, QK_WIDTH), 0)
    c = lax.broadcasted_iota(jnp.int32, (n_rows, QK_WIDTH), 1)
    qb = jnp.where(r // dec == c // ATT_HEAD_DIM, qt, 0.0).astype(BF16)
    nt = (((1,), (1,)), ((), ()))
    s_pages = [_bdot(qb, kp[j][...].astype(BF16)) for j in range(n_pages)]
    s_new = lax.dot_general(qb, kn_ref[...], nt, preferred_element_type=F32)
    rn = lax.broadcasted_iota(jnp.int32, (n_rows, dec), 0) % dec
    cn = lax.broadcasted_iota(jnp.int32, (n_rows, dec), 1)
    s_new = jnp.where(cn <= rn, s_new, NEG_INF)
    m = jnp.max(s_new, axis=-1, keepdims=True)
    for s in s_pages:
        m = jnp.maximum(m, jnp.max(s, axis=-1, keepdims=True))
    p_new = jnp.exp(s_new - m)
    l = jnp.sum(p_new, axis=-1, keepdims=True)
    acc_new = _bdot(p_new.astype(BF16), vn_ref[...])
    ps = []
    for j in range(n_pages):
        p = jnp.exp(s_pages[j] - m)
        l = l + jnp.sum(p, axis=-1, keepdims=True)
        ps.append(p)
    lam = _lambda(lp_ref, lam_init)
    hg = hg_ref[...]
    for h in range(ATT_HEADS):
        rows = slice(h * grp, (h + 1) * grp)
        cols = slice(h * ATT_V_DIM, (h + 1) * ATT_V_DIM)
        acc = acc_new[rows, cols]
        for j in range(n_pages):
            vh = vp[j][pl.ds(h, page, stride=ATT_HEADS), :].astype(BF16)
            acc = acc + _bdot(ps[j][rows, :].astype(BF16), vh)
        acc = acc / l[rows, :]
        o = acc[:dec] - lam * acc[dec:]
        o_ref[:, cols] = (_rms_rows(o, hg) * (1.0 - lam_init)).astype(o_ref.dtype)


def _attn_sample(q, kn, vn, cache_kt, cache_vr, layer, page_table, lp, hg, lam_init):
    n_s, dec, _ = q.shape
    n_pages = page_table.shape[1]
    page = cache_kt.shape[3]
    tok = pl.BlockSpec((None, dec, QK_WIDTH), lambda n, pt: (n, 0, 0))
    kpages = [pl.BlockSpec((None, None, QK_WIDTH, page), lambda n, pt, j=j: (layer, pt[n, j], 0, 0))
              for j in range(n_pages)]
    vpages = [pl.BlockSpec((None, None, page * ATT_HEADS, ATT_V_DIM),
                           lambda n, pt, j=j: (layer, pt[n, j], 0, 0)) for j in range(n_pages)]
    grid_spec = pltpu.PrefetchScalarGridSpec(
        num_scalar_prefetch=1,
        grid=(n_s,),
        in_specs=[tok, tok, tok,
                  pl.BlockSpec((4, ATT_HEAD_DIM), lambda n, pt: (0, 0)),
                  pl.BlockSpec((1, ATT_V_DIM), lambda n, pt: (0, 0))] + kpages + vpages,
        out_specs=tok,
    )
    return pl.pallas_call(
        functools.partial(_attn_sample_kernel, n_pages=n_pages, dec=dec, lam_init=lam_init),
        out_shape=jax.ShapeDtypeStruct((n_s, dec, ATT_WIDTH), BF16),
        grid_spec=grid_spec,
        compiler_params=_cparams(("parallel",)),
        name="attn_sample",
    )(page_table, q, kn, vn, lp, hg, *([cache_kt] * n_pages), *([cache_vr] * n_pages))


def _merge_kernel(x_ref, u_ref, cb_ref, vin_ref, halo_ref, gate_ref,
                  yrp_ref, yrs_ref, ybp_ref, ybs_ref, vm1s_ref, vm2s_ref,
                  d_ref, wglu_ref, bglu_ref, cw_ref, wssm_ref, watt_ref, wconv_ref, wout_ref,
                  o_ref, *, n_prompt_tiles, tiles_per_seq):
    i = pl.program_id(0)
    is_prompt = i < n_prompt_tiles
    yraw = jnp.where(is_prompt, yrp_ref[...], yrs_ref[...])
    yb = jnp.where(is_prompt, ybp_ref[...], ybs_ref[...])
    y = jax.nn.gelu(yraw + d_ref[...] * u_ref[...])
    ya = y * jax.nn.sigmoid(_bdot(y.astype(BF16), wglu_ref[...]) + bglu_ref[...])

    vin = vin_ref[...]
    row = lax.broadcasted_iota(jnp.int32, vin.shape, 0)
    halo = jnp.where(i % tiles_per_seq == 0, 0.0, halo_ref[...])
    h1 = jnp.broadcast_to(halo[SUBLANES - 1:SUBLANES, :], vin.shape)
    h2 = jnp.broadcast_to(halo[SUBLANES - 2:SUBLANES - 1, :], vin.shape)
    vm1 = jnp.where(row == 0, h1, pltpu.roll(vin, 1, axis=0))
    vm2 = jnp.where(row == 0, h2, jnp.where(row == 1, h1, pltpu.roll(vin, 2, axis=0)))
    vm1 = jnp.where(is_prompt, vm1, vm1s_ref[...])
    vm2 = jnp.where(is_prompt, vm2, vm2s_ref[...])
    conv = vm2 * cw_ref[0:1, :] + vm1 * cw_ref[1:2, :] + vin * cw_ref[2:3, :]
    yc = cb_ref[...] * conv
    merged = (gate_ref[:, 0:D_MODEL] * _bdot(ya.astype(BF16), wssm_ref[...])
              + gate_ref[:, D_MODEL:2 * D_MODEL] * _bdot(yb, watt_ref[...])
              + gate_ref[:, 2 * D_MODEL:3 * D_MODEL] * _bdot(yc.astype(BF16), wconv_ref[...]))
    o_ref[...] = x_ref[...] + _bdot(merged.astype(BF16), wout_ref[...])


def _merge(x, u, cb, vin, gates, yr_p, yr_s, yb_p, yb_s, vm1_s, vm2_s,
           d, wglu, bglu, cw, wssm, watt, wconv, wout, n_prompt_tiles, tiles_per_seq):
    t = x.shape[0]
    tm = ROW_TILE
    npt = n_prompt_tiles
    row = lambda w: pl.BlockSpec((tm, w), lambda i: (i, 0))
    prow = lambda w: pl.BlockSpec((tm, w), lambda i: (jnp.minimum(i, npt - 1), 0))
    srow = lambda w: pl.BlockSpec((tm, w), lambda i: (jnp.maximum(i - npt, 0), 0))
    halo = pl.BlockSpec((SUBLANES, CONV_WIDTH), lambda i: (jnp.maximum(i * (tm // SUBLANES) - 1, 0), 0))
    return pl.pallas_call(
        functools.partial(_merge_kernel, n_prompt_tiles=npt, tiles_per_seq=tiles_per_seq),
        out_shape=jax.ShapeDtypeStruct((t, D_MODEL), F32),
        grid=(t // tm,),
        in_specs=[row(D_MODEL), row(SSM_WIDTH), row(CONV_WIDTH), row(CONV_WIDTH), halo,
                  row(N_BRANCH * D_MODEL),
                  prow(SSM_WIDTH), srow(SSM_WIDTH), prow(ATT_WIDTH), srow(ATT_WIDTH),
                  srow(CONV_WIDTH), srow(CONV_WIDTH),
                  _const_spec((1, SSM_WIDTH)), _const_spec((SSM_WIDTH, SSM_WIDTH)),
                  _const_spec((1, SSM_WIDTH)), _const_spec((CONV_K, CONV_WIDTH)),
                  _const_spec((SSM_WIDTH, D_MODEL)), _const_spec((ATT_WIDTH, D_MODEL)),
                  _const_spec((CONV_WIDTH, D_MODEL)), _const_spec((D_MODEL, D_MODEL))],
        out_specs=row(D_MODEL),
        compiler_params=_cparams(("parallel",)),
        name="merge",
    )(x, u, cb, vin, vin, gates, yr_p, yr_s, yb_p, yb_s, vm1_s, vm2_s,
      d, wglu, bglu, cw, wssm, watt, wconv, wout)


def _swiglu(h, w1, w3, w2):
    a = _bdot(h, w1)
    b = _bdot(h, w3)
    return _bdot((jax.nn.silu(a) * b).astype(BF16), w2)


def _ffn_kernel(x_ref, g_ref, w1_ref, w3_ref, w2_ref, o_ref):
    x = x_ref[...]
    h = _rms_rows(x, g_ref[...]).astype(BF16)
    o_ref[...] = x + _swiglu(h, w1_ref[...], w3_ref[...], w2_ref[...])


def _ffn(x, g, w1, w3, w2):
    t = x.shape[0]
    tm = ROW_TILE
    d_ff = w1.shape[1]
    row = pl.BlockSpec((tm, D_MODEL), lambda i: (i, 0))
    return pl.pallas_call(
        _ffn_kernel,
        out_shape=jax.ShapeDtypeStruct((t, D_MODEL), F32),
        grid=(t // tm,),
        in_specs=[row, _const_spec((1, D_MODEL)), _const_spec((D_MODEL, d_ff)),
                  _const_spec((D_MODEL, d_ff)), _const_spec((d_ff, D_MODEL))],
        out_specs=row,
        compiler_params=_cparams(("parallel",)),
        name="ffn",
    )(x, g, w1, w3, w2)


def _router_kernel(x_ref, g_ref, wr_ref, tri_ref, h_ref, meta_ref, gate_ref, cnt_ref, carry):
    @pl.when(pl.program_id(0) == 0)
    def _():
        carry[...] = jnp.zeros(carry.shape, F32)

    h = _rms_rows(x_ref[...], g_ref[...])
    h_ref[...] = h
    logits = jnp.dot(h, wr_ref[...], preferred_element_type=F32, precision=lax.Precision.HIGHEST)
    lane = lax.broadcasted_iota(jnp.int32, logits.shape, 1)
    logits = jnp.where(lane < N_EXPERTS, logits, -jnp.inf)
    big = jnp.int32(logits.shape[1])
    m1 = jnp.max(logits, axis=-1, keepdims=True)
    i1 = jnp.min(jnp.where(logits == m1, lane, big), axis=-1, keepdims=True)
    rest = jnp.where(lane == i1, -jnp.inf, logits)
    m2 = jnp.max(rest, axis=-1, keepdims=True)
    i2 = jnp.min(jnp.where(rest == m2, lane, big), axis=-1, keepdims=True)
    e = jnp.exp(m2 - m1)
    g1 = 1.0 / (1.0 + e)
    g2 = e / (1.0 + e)
    o1 = lane == i1
    o2 = lane == i2
    chosen = jnp.where(o1 | o2, 1.0, 0.0)
    base = _bdot(tri_ref[...], chosen.astype(BF16)) + carry[...]
    r1 = jnp.sum(jnp.where(o1, base, 0.0), axis=-1, keepdims=True).astype(jnp.int32)
    r2 = jnp.sum(jnp.where(o2, base, 0.0), axis=-1, keepdims=True).astype(jnp.int32)
    carry[...] = carry[...] + jnp.sum(chosen, axis=0, keepdims=True)
    cnt_ref[...] = carry[...]
    meta_ref[...] = jnp.where(lane == 0, i1, jnp.where(lane == 1, i2,
                              jnp.where(lane == 2, r1, jnp.where(lane == 3, r2, 0))))
    gate_ref[...] = jnp.where(lane == 0, g1, jnp.where(lane == 1, g2, 0.0))


def _router(x, g, wr_pad):
    t = x.shape[0]
    tm = ROW_TILE
    lanes = wr_pad.shape[1]
    tri = jnp.tri(tm, k=-1, dtype=BF16)
    row = lambda w: pl.BlockSpec((tm, w), lambda i: (i, 0))
    return pl.pallas_call(
        _router_kernel,
        out_shape=[jax.ShapeDtypeStruct((t, D_MODEL), F32),
                   jax.ShapeDtypeStruct((t, lanes), jnp.int32),
                   jax.ShapeDtypeStruct((t, lanes), F32),
                   jax.ShapeDtypeStruct((1, lanes), F32)],
        grid=(t // tm,),
        in_specs=[row(D_MODEL), _const_spec((1, D_MODEL)), _const_spec((D_MODEL, lanes)),
                  _const_spec((tm, tm))],
        out_specs=[row(D_MODEL), row(lanes), row(lanes), _const_spec((1, lanes))],
        scratch_shapes=[pltpu.VMEM((1, lanes), F32)],
        compiler_params=_cparams(("arbitrary",)),
        name="router",
    )(x, g, wr_pad, tri)


def _sc_mesh():
    return plsc.VectorSubcoreMesh(core_axis_name="c", subcore_axis_name="s")


def _sc_worker_base(per_worker):
    return (lax.axis_index("s") * SC_CORES + lax.axis_index("c")) * per_worker


def _sc_scatter_rows(x, idx0, idx1, n_out):
    n, w = x.shape
    per_worker = n // SC_WORKERS
    assert n % (SC_WORKERS * SC_ROWS) == 0

    @functools.partial(
        pl.kernel, mesh=_sc_mesh(), out_type=jax.ShapeDtypeStruct((n_out, w), x.dtype),
        scratch_types=[pltpu.VMEM((SC_ROWS,), jnp.int32), pltpu.VMEM((SC_ROWS,), jnp.int32),
                       pltpu.VMEM((SC_ROWS, w), x.dtype), pltpu.SemaphoreType.DMA])
    def scatter(x_hbm, i0_hbm, i1_hbm, out_hbm, i0_v, i1_v, rows_v, sem):
        start = _sc_worker_base(per_worker)

        @pl.loop(0, per_worker // SC_ROWS)
        def _(c):
            rows = pl.ds(pl.multiple_of(start + c * SC_ROWS, SUBLANES), SC_ROWS)
            pltpu.sync_copy(x_hbm.at[rows], rows_v)
            pltpu.sync_copy(i0_hbm.at[rows], i0_v)
            pltpu.sync_copy(i1_hbm.at[rows], i1_v)
            pltpu.async_copy(rows_v, out_hbm.at[i0_v], sem).wait()
            pltpu.async_copy(rows_v, out_hbm.at[i1_v], sem).wait()

    return scatter(x, idx0, idx1)


def _sc_gather_rows(table, idx):
    n, w = idx.shape[0], table.shape[1]
    per_worker = n // SC_WORKERS
    assert n % (SC_WORKERS * SC_ROWS) == 0

    @functools.partial(
        pl.kernel, mesh=_sc_mesh(), out_type=jax.ShapeDtypeStruct((n, w), table.dtype),
        scratch_types=[pltpu.VMEM((SC_ROWS,), jnp.int32), pltpu.VMEM((SC_ROWS, w), table.dtype),
                       pltpu.SemaphoreType.DMA])
    def gather(table_hbm, idx_hbm, out_hbm, idx_v, rows_v, sem):
        start = _sc_worker_base(per_worker)

        @pl.loop(0, per_worker // SC_ROWS)
        def _(c):
            rows = pl.ds(pl.multiple_of(start + c * SC_ROWS, SUBLANES), SC_ROWS)
            pltpu.sync_copy(idx_hbm.at[rows], idx_v)
            pltpu.async_copy(table_hbm.at[idx_v], rows_v, sem).wait()
            pltpu.sync_copy(rows_v, out_hbm.at[rows])

    return gather(table, idx)


def _moe_kernel(be_ref, nv_ref, x_ref, w1_ref, w3_ref, w2_ref, o_ref):
    del be_ref

    @pl.when(pl.program_id(0) < nv_ref[0])
    def _():
        o_ref[...] = _swiglu(x_ref[...].astype(BF16), w1_ref[...], w3_ref[...], w2_ref[...])


def _moe_blocks(xs, block_expert, n_valid, w1, w3, w2):
    cap = xs.shape[0]
    tb = MOE_TILE
    d_ff = w1.shape[2]
    grid_spec = pltpu.PrefetchScalarGridSpec(
        num_scalar_prefetch=2,
        grid=(cap // tb,),
        in_specs=[pl.BlockSpec((tb, D_MODEL), lambda i, be, nv: (jnp.minimum(i, nv[0] - 1), 0)),
                  pl.BlockSpec((None, D_MODEL, d_ff), lambda i, be, nv: (be[i], 0, 0)),
                  pl.BlockSpec((None, D_MODEL, d_ff), lambda i, be, nv: (be[i], 0, 0)),
                  pl.BlockSpec((None, d_ff, D_MODEL), lambda i, be, nv: (be[i], 0, 0))],
        out_specs=pl.BlockSpec((tb, D_MODEL), lambda i, be, nv: (i, 0)),
    )
    return pl.pallas_call(
        _moe_kernel,
        out_shape=jax.ShapeDtypeStruct((cap, D_MODEL), F32),
        grid_spec=grid_spec,
        compiler_params=_cparams(("arbitrary",)),
        name="moe",
    )(block_expert, n_valid, xs, w1, w3, w2)


def _combine_kernel(x_ref, a0_ref, a1_ref, gate_ref, o_ref):
    gate = gate_ref[...]
    o_ref[...] = x_ref[...] + gate[:, 0:1] * a0_ref[...] + gate[:, 1:2] * a1_ref[...]


def _combine(x, picked, gate):
    t = x.shape[0]
    tm = ROW_TILE
    nt = t // tm
    return pl.pallas_call(
        _combine_kernel,
        out_shape=jax.ShapeDtypeStruct((t, D_MODEL), F32),
        grid=(nt,),
        in_specs=[pl.BlockSpec((tm, D_MODEL), lambda i: (i, 0)),
                  pl.BlockSpec((tm, D_MODEL), lambda i: (i, 0)),
                  pl.BlockSpec((tm, D_MODEL), lambda i: (i + nt, 0)),
                  pl.BlockSpec((tm, gate.shape[1]), lambda i: (i, 0))],
        out_specs=pl.BlockSpec((tm, D_MODEL), lambda i: (i, 0)),
        compiler_params=_cparams(("parallel",)),
        name="moe_combine",
    )(x, picked, picked, gate)


def _moe(x, g, wr, w1, w3, w2):
    t = x.shape[0]
    tb = MOE_TILE
    lanes = 128
    wr_pad = jnp.pad(wr, ((0, 0), (0, lanes - N_EXPERTS)))
    h, meta, gate, cnt = _router(x, g, wr_pad)
    counts = cnt[0, :N_EXPERTS].astype(jnp.int32)
    padded = (counts + tb - 1) // tb * tb
    pad_ends = jnp.cumsum(padded)
    pad_starts = pad_ends - padded
    experts = jnp.arange(N_EXPERTS, dtype=jnp.int32)
    slot = lambda e, r: jnp.sum(jnp.where(e[:, None] == experts, pad_starts, 0), axis=1) + r
    dest0 = slot(meta[:, 0], meta[:, 2])
    dest1 = slot(meta[:, 1], meta[:, 3])
    nb = -(-(t * TOP_K + N_EXPERTS * (tb - 1)) // tb)
    block_expert = jnp.minimum(
        jnp.sum(pad_ends[None, :] <= (jnp.arange(nb, dtype=jnp.int32) * tb)[:, None], axis=1),
        N_EXPERTS - 1).astype(jnp.int32)
    n_valid = (pad_ends[-1:] // tb).astype(jnp.int32)
    xs = _sc_scatter_rows(h, dest0, dest1, nb * tb)
    out = _moe_blocks(xs, block_expert, n_valid, w1, w3, w2)
    picked = _sc_gather_rows(out, jnp.concatenate([dest0, dest1]))
    return _combine(x, picked, gate)


def kernel(x_prompt, x_sample, cache_k, cache_v, page_table, state_ssm_re, state_ssm_im, state_conv,
           norm_mix_g, norm_ffn_g, w_in, ssm_a_re, ssm_a_im, ssm_log_dt, ssm_b_re, ssm_b_im,
           ssm_c_re, ssm_c_im, ssm_d, ssm_w_glu, ssm_b_glu, q_norm_g, k_norm_g,
           lambda_q1, lambda_k1, lambda_q2, lambda_k2, head_norm_g, conv_w,
           w_br_ssm, w_br_att, w_br_conv, w_out, ffn_w1, ffn_w3, ffn_w2,
           router_w, moe_w1, moe_w3, moe_w2):
    n_p, seq, _ = x_prompt.shape
    n_s, dec, _ = x_sample.shape
    depth = w_in.shape[0]
    tp = n_p * seq
    ts = n_s * dec
    assert seq % ROW_TILE == 0 and ts % ROW_TILE == 0 and seq >= CONV_K - 1
    pool, page = cache_k.shape[1], cache_k.shape[2]
    cache_kt = cache_k.reshape(depth, pool, page, QK_WIDTH).transpose(0, 1, 3, 2)
    cache_vr = cache_v.reshape(depth, pool, page * ATT_HEADS, ATT_V_DIM)
    seg = jnp.kron(jnp.eye(QK_WIDTH // ATT_HEAD_DIM, dtype=F32),
                   jnp.full((ATT_HEAD_DIM, ATT_HEAD_DIM), 1.0 / ATT_HEAD_DIM, F32)).astype(BF16)
    n_rep = QK_WIDTH // ATT_HEAD_DIM

    x = jnp.concatenate([x_prompt.reshape(tp, D_MODEL), x_sample.reshape(ts, D_MODEL)], axis=0)
    kp, vp, srp, sip, cvp = [], [], [], [], []
    ks, vs, srs, sis, cvs = [], [], [], [], []
    for l in range(depth):
        lam_init = 0.8 - 0.6 * math.exp(-0.3 * l)
        u, q, kf, kb, vf, vb, cb, vin, gates = _inproj(
            x, norm_mix_g[l][None], w_in[l].astype(BF16),
            jnp.tile(q_norm_g[l], n_rep)[None], jnp.tile(k_norm_g[l], n_rep)[None], seg)

        tables = _s5_tables(ssm_a_re[l], ssm_a_im[l], ssm_log_dt[l], ssm_b_re[l], ssm_b_im[l],
                            ssm_c_re[l], ssm_c_im[l], dec)
        yr_p, yr_s, p_re, p_im, s_re, s_im = _s5(u, state_ssm_re[l], state_ssm_im[l], tables,
                                                  n_p, seq, n_s, dec)

        lp = jnp.stack([lambda_q1[l], lambda_k1[l], lambda_q2[l], lambda_k2[l]])
        hg = head_norm_g[l][None]
        yb_p = _attn_prompt(q, kb, vb, lp, hg, n_p, seq, lam_init)
        yb_s = _attn_sample(q[tp:].reshape(n_s, dec, QK_WIDTH), kb[tp:].reshape(n_s, dec, QK_WIDTH),
                            vb[tp:].reshape(n_s, dec, ATT_WIDTH), cache_kt, cache_vr, l, page_table,
                            lp, hg, lam_init).reshape(ts, ATT_WIDTH)

        ext_s = jnp.concatenate([state_conv[l], vin[tp:].reshape(n_s, dec, CONV_WIDTH)], axis=1)
        vm1_s = ext_s[:, 1:1 + dec].reshape(ts, CONV_WIDTH)
        vm2_s = ext_s[:, 0:dec].reshape(ts, CONV_WIDTH)
        x = _merge(x, u, cb, vin, gates, yr_p, yr_s, yb_p, yb_s, vm1_s, vm2_s,
                   ssm_d[l][None], ssm_w_glu[l].astype(BF16), ssm_b_glu[l][None], conv_w[l],
                   w_br_ssm[l].astype(BF16), w_br_att[l].astype(BF16), w_br_conv[l].astype(BF16),
                   w_out[l].astype(BF16), tp // ROW_TILE, seq // ROW_TILE)

        i = l // 2
        if l % 2 == 0:
            x = _ffn(x, norm_ffn_g[l][None], ffn_w1[i].astype(BF16), ffn_w3[i].astype(BF16),
                     ffn_w2[i].astype(BF16))
        else:
            x = _moe(x, norm_ffn_g[l][None], router_w[i], moe_w1[i].astype(BF16),
                     moe_w3[i].astype(BF16), moe_w2[i].astype(BF16))

        kp.append(kf[:tp].reshape(n_p, seq, ATT_HEADS, 2, ATT_HEAD_DIM))
        vp.append(vf[:tp].reshape(n_p, seq, ATT_HEADS, ATT_V_DIM))
        srp.append(p_re); sip.append(p_im)
        cvp.append(vin[:tp].reshape(n_p, seq, CONV_WIDTH)[:, seq - (CONV_K - 1):])
        ks.append(kf[tp:].reshape(n_s, dec, ATT_HEADS, 2, ATT_HEAD_DIM))
        vs.append(vf[tp:].reshape(n_s, dec, ATT_HEADS, ATT_V_DIM))
        srs.append(s_re); sis.append(s_im); cvs.append(ext_s[:, dec:])

    return (x[:tp].reshape(n_p, seq, D_MODEL), x[tp:].reshape(n_s, dec, D_MODEL),
            jnp.stack(kp), jnp.stack(vp), jnp.stack(srp), jnp.stack(sip), jnp.stack(cvp),
            jnp.stack(ks), jnp.stack(vs), jnp.stack(srs), jnp.stack(sis), jnp.stack(cvs))
```

```python
import functools
import math

import jax
import jax.numpy as jnp
from jax import lax
from jax.experimental import pallas as pl
from jax.experimental.pallas import tpu as pltpu
from jax.experimental.pallas import tpu_sc as plsc

F32 = jnp.float32
BF16 = jnp.bfloat16

D_MODEL = 1024
SSM_WIDTH = 256
SSM_GROUP = 16
SSM_GROUPS = SSM_WIDTH // SSM_GROUP
SSM_STATE = 64
ATT_HEADS = 4
ATT_HEAD_DIM = 64
ATT_V_DIM = 2 * ATT_HEAD_DIM
QK_WIDTH = ATT_HEADS * 2 * ATT_HEAD_DIM
ATT_WIDTH = ATT_HEADS * ATT_V_DIM
CONV_WIDTH = 256
CONV_K = 3
N_BRANCH = 3
N_EXPERTS = 8
TOP_K = 2
EPS = 1e-6
NEG_INF = -1e30

C_U = 0
C_Q = C_U + SSM_WIDTH
C_K = C_Q + QK_WIDTH
C_V = C_K + QK_WIDTH
C_CB = C_V + ATT_WIDTH
C_CC = C_CB + CONV_WIDTH
C_CH = C_CC + CONV_WIDTH
C_G = C_CH + CONV_WIDTH
IN_COLS = C_G + N_BRANCH * D_MODEL

SSM_CHUNK = 8
SUBLANES = 8
ROW_TILE = 512
ATT_TILE = 512
MOE_TILE = 256
SC_CORES = 2
SC_WORKERS = SC_CORES * 16
SC_ROWS = 48
VMEM_LIMIT = 56 * 1024 * 1024


def _cparams(sem):
    return pltpu.CompilerParams(dimension_semantics=sem, vmem_limit_bytes=VMEM_LIMIT)


def _const_spec(shape):
    nd = len(shape)
    return pl.BlockSpec(shape, lambda *_: (0,) * nd)


def _bdot(a, b):
    return jnp.dot(a, b, preferred_element_type=F32)


def _rms_rows(x, g):
    ms = jnp.mean(x * x, axis=-1, keepdims=True)
    return x * lax.rsqrt(ms + EPS) * g


def _group_specs(tm, w, n_prompt_tiles):
    return [pl.BlockSpec((tm, w), lambda i: (jnp.minimum(i, n_prompt_tiles - 1), 0)),
            pl.BlockSpec((tm, w), lambda i: (jnp.maximum(i - n_prompt_tiles, 0), 0))]


def _read_group(p_ref, s_ref, is_prompt):
    return jnp.where(is_prompt, p_ref[...], s_ref[...])


def _write_group(p_ref, s_ref, is_prompt, val):
    @pl.when(is_prompt)
    def _():
        p_ref[...] = val

    @pl.when(jnp.logical_not(is_prompt))
    def _():
        s_ref[...] = val


def _segment_rms(z, g, seg):
    ms = _bdot((z * z).astype(BF16), seg)
    return z * lax.rsqrt(ms + EPS) * g


def _inproj_kernel(xp_ref, xs_ref, g_ref, w_ref, qg_ref, kg_ref, seg_ref, *rest, n_prompt_tiles, n_prev):
    (u_ref, q_ref, kb_ref, vb_ref, cb_ref, vin_ref, gate_ref,
     kt_ref, vr_ref, ks_ref, vs_ref) = rest[n_prev:]
    tm = u_ref.shape[0]
    is_prompt = pl.program_id(0) < n_prompt_tiles
    h = _rms_rows(_read_group(xp_ref, xs_ref, is_prompt), g_ref[...]).astype(BF16)

    def proj(a, b):
        return _bdot(h, w_ref[:, a:b])

    seg = seg_ref[...]
    u_ref[...] = proj(C_U, C_Q)
    qn = _segment_rms(proj(C_Q, C_K), qg_ref[...], seg)
    q_ref[...] = (qn * (ATT_HEAD_DIM ** -0.5)).astype(BF16)
    kn = _segment_rms(proj(C_K, C_V), kg_ref[...], seg)
    kb_ref[...] = kn.astype(BF16)
    v = proj(C_V, C_CB)
    vb_ref[...] = v.astype(BF16)
    cb_ref[...] = proj(C_CB, C_CC)
    vin_ref[...] = proj(C_CC, C_CH) * proj(C_CH, C_G)
    for j in range(N_BRANCH):
        a = C_G + j * D_MODEL
        gate_ref[:, j * D_MODEL:(j + 1) * D_MODEL] = jax.nn.sigmoid(proj(a, a + D_MODEL))

    @pl.when(is_prompt)
    def _():
        kt_ref[...] = kn.T
        for hd in range(ATT_HEADS):
            vr_ref[pl.ds(hd, tm, stride=ATT_HEADS), :] = v[:, hd * ATT_V_DIM:(hd + 1) * ATT_V_DIM]

    @pl.when(jnp.logical_not(is_prompt))
    def _():
        ks_ref[...] = kn
        vs_ref[...] = v


def _inproj(xp, xs, g, w_bf, qg, kg, seg, layer, depth, n_p, seq, prev):
    tm = ROW_TILE
    tp, ts = xp.shape[0], xs.shape[0]
    t = tp + ts
    npt = tp // tm
    tps = seq // tm
    row = lambda w: pl.BlockSpec((tm, w), lambda i: (i, 0))
    pc = lambda i: jnp.minimum(i, npt - 1)
    outs = [(SSM_WIDTH, F32), (QK_WIDTH, BF16), (QK_WIDTH, BF16), (ATT_WIDTH, BF16),
            (CONV_WIDTH, F32), (CONV_WIDTH, F32), (N_BRANCH * D_MODEL, F32)]
    out_shape = ([jax.ShapeDtypeStruct((t, w), d) for w, d in outs]
                 + [jax.ShapeDtypeStruct((depth, n_p, QK_WIDTH, seq), F32),
                    jax.ShapeDtypeStruct((depth, tp * ATT_HEADS, ATT_V_DIM), F32),
                    jax.ShapeDtypeStruct((ts, QK_WIDTH), F32), jax.ShapeDtypeStruct((ts, ATT_WIDTH), F32)])
    srow = _group_specs(tm, QK_WIDTH, npt)[1]
    out_specs = ([row(w) for w, _ in outs]
                 + [pl.BlockSpec((None, None, QK_WIDTH, tm), lambda i: (layer, pc(i) // tps, 0, pc(i) % tps)),
                    pl.BlockSpec((None, tm * ATT_HEADS, ATT_V_DIM), lambda i: (layer, pc(i), 0)),
                    srow, srow])
    prev = () if prev is None else tuple(prev)
    n_in = 7
    return pl.pallas_call(
        functools.partial(_inproj_kernel, n_prompt_tiles=npt, n_prev=len(prev)),
        out_shape=out_shape,
        grid=(t // tm,),
        in_specs=(_group_specs(tm, D_MODEL, npt)
                  + [_const_spec((1, D_MODEL)), _const_spec((D_MODEL, IN_COLS)),
                     _const_spec((1, QK_WIDTH)), _const_spec((1, QK_WIDTH)),
                     _const_spec((QK_WIDTH, QK_WIDTH))]
                  + [pl.BlockSpec(memory_space=pl.ANY)] * len(prev)),
        out_specs=out_specs,
        input_output_aliases={n_in + j: len(outs) + j for j in range(len(prev))},
        compiler_params=_cparams(("arbitrary",)),
        name="inproj",
    )(xp, xs, g, w_bf, qg, kg, seg, *prev)


def _s5_intra(u, kb_ref, tc):
    rowmod = lax.broadcasted_iota(jnp.int32, u.shape, 0) % tc
    y = _bdot(u.astype(BF16), kb_ref[0])
    for m in range(1, tc):
        um = jnp.where(rowmod >= m, pltpu.roll(u, m, axis=0), 0.0)
        y = y + _bdot(um.astype(BF16), kb_ref[m])
    return y


def _s5_inject(u_refs, pb_ref, k0, n_chunks, tc):
    w = None
    for k in range(tc):
        rows = pl.ds(k, n_chunks, stride=tc)
        uk = jnp.concatenate([r[rows, :] for r in u_refs], axis=1).astype(BF16)
        d = _bdot(uk, pb_ref[k0 + k])
        w = d if w is None else w + d
    return w


def _s5_readout(y_ref, y_scrs, y, s_in, qb_ref, n_chunks, tc):
    lanes = y_scrs[0].shape[1]
    for h, scr in enumerate(y_scrs):
        scr[...] = y[:, h * lanes:(h + 1) * lanes]
    sb = s_in.astype(BF16)
    for k in range(tc):
        rows = pl.ds(k, n_chunks, stride=tc)
        yk = _bdot(sb, qb_ref[k])
        for h, scr in enumerate(y_scrs):
            scr[rows, :] = scr[rows, :] + yk[:, h * lanes:(h + 1) * lanes]
    y_ref[...] = jnp.concatenate([scr[...] for scr in y_scrs], axis=1)


def _cmul_add(a_re, a_im, s, w, half):
    s_re, s_im = s[:, :half], s[:, half:]
    return jnp.concatenate([a_re * s_re - a_im * s_im, a_re * s_im + a_im * s_re], axis=1) + w


def _s5_prompt_kernel(ua_ref, ub_ref, kb_ref, pb_ref, qb_ref, a_ref, y_ref, fs_ref,
                      carry, w_scr, s_scr, ya_scr, yb_scr, *, tc, tiles_per_seq):
    tm = ua_ref.shape[0]
    n_chunks = tm // tc
    half = a_ref.shape[1]

    @pl.when(pl.program_id(0) % tiles_per_seq == 0)
    def _():
        carry[...] = jnp.zeros(carry.shape, F32)

    y = _s5_intra(jnp.concatenate([ua_ref[...], ub_ref[...]], axis=1), kb_ref, tc)
    w_scr[...] = _s5_inject((ua_ref, ub_ref), pb_ref, 0, n_chunks, tc)
    a_re = a_ref[0:1, :]
    a_im = a_ref[1:2, :]
    s = carry[...]
    for c in range(n_chunks):
        s_scr[c:c + 1, :] = s
        s = _cmul_add(a_re, a_im, s, w_scr[c:c + 1, :], half)
    carry[...] = s
    fs_ref[...] = s
    _s5_readout(y_ref, (ya_scr, yb_scr), y, s_scr[...], qb_ref, n_chunks, tc)


def _s5_sample_kernel(ua_ref, ub_ref, s0_ref, kb_ref, pb_ref, qb_ref, a_ref, y_ref, fs_ref,
                      ya_scr, yb_scr, *, tc, k0):
    n_chunks = ua_ref.shape[0] // tc
    half = a_ref.shape[1]
    y = _s5_intra(jnp.concatenate([ua_ref[...], ub_ref[...]], axis=1), kb_ref, tc)
    w = _s5_inject((ua_ref, ub_ref), pb_ref, k0, n_chunks, tc)
    s0 = s0_ref[...]
    fs_ref[...] = _cmul_add(a_ref[2:3, :], a_ref[3:4, :], s0, w, half)
    _s5_readout(y_ref, (ya_scr, yb_scr), y, s0, qb_ref, n_chunks, tc)


def _s5_tables(a_re, a_im, log_dt, b_re, b_im, c_re, c_im, dec):
    hp = lax.Precision.HIGHEST
    tc = SSM_CHUNK
    g, p, j = SSM_GROUPS, SSM_STATE, SSM_GROUP
    dt = jnp.exp(log_dt)[:, None]
    den = a_re * a_re + a_im * a_im
    ms = jnp.arange(tc + 1, dtype=F32)[None, :, None]
    mag = jnp.exp(a_re[:, None, :] * dt[:, None, :] * ms)
    ang = a_im[:, None, :] * dt[:, None, :] * ms
    pw_re = mag * jnp.cos(ang)
    pw_im = mag * jnp.sin(ang)
    ab_re, ab_im = pw_re[:, 1], pw_im[:, 1]
    nr = ab_re - 1.0
    cr = (nr * a_re + ab_im * a_im) / den
    ci = (ab_im * a_re - nr * a_im) / den
    bb_re = cr[..., None] * b_re - ci[..., None] * b_im
    bb_im = cr[..., None] * b_im + ci[..., None] * b_re
    eye = jnp.eye(g, dtype=F32)
    ct_re = c_re.transpose(0, 2, 1)
    ct_im = c_im.transpose(0, 2, 1)
    cbr = ct_re[:, :, None, :] * bb_re[:, :, :, None] - ct_im[:, :, None, :] * bb_im[:, :, :, None]
    cbi = ct_re[:, :, None, :] * bb_im[:, :, :, None] + ct_im[:, :, None, :] * bb_re[:, :, :, None]
    kern = (jnp.einsum('gmp,gpji->mgji', pw_re[:, :tc], cbr, precision=hp)
            - jnp.einsum('gmp,gpji->mgji', pw_im[:, :tc], cbi, precision=hp))
    kb = jnp.einsum('mgji,gh->mgjhi', kern, eye).reshape(tc, g * j, g * j)
    rv_re = pw_re[:, tc - 1::-1][:, :tc].transpose(1, 0, 2)
    rv_im = pw_im[:, tc - 1::-1][:, :tc].transpose(1, 0, 2)
    bt_re = bb_re.transpose(0, 2, 1)
    bt_im = bb_im.transpose(0, 2, 1)
    pr = rv_re[:, :, None, :] * bt_re[None] - rv_im[:, :, None, :] * bt_im[None]
    pi = rv_re[:, :, None, :] * bt_im[None] + rv_im[:, :, None, :] * bt_re[None]
    blk = lambda t: jnp.einsum('kgjp,gh->kgjhp', t, eye).reshape(tc, g * j, g * p)
    pb = jnp.concatenate([blk(pr), blk(pi)], axis=2)
    nx_re = pw_re[:, 1:].transpose(1, 0, 2)
    nx_im = pw_im[:, 1:].transpose(1, 0, 2)
    qr = nx_re[..., None] * ct_re[None] - nx_im[..., None] * ct_im[None]
    qi = -(nx_re[..., None] * ct_im[None] + nx_im[..., None] * ct_re[None])
    blq = lambda t: jnp.einsum('kgpi,gh->kgphi', t, eye).reshape(tc, g * p, g * j)
    qb = jnp.concatenate([blq(qr), blq(qi)], axis=1)
    flat = lambda t: t.reshape(1, g * p)
    adec = jnp.concatenate([flat(pw_re[:, tc]), flat(pw_im[:, tc]),
                            flat(pw_re[:, dec]), flat(pw_im[:, dec])], axis=0)
    return kb.astype(BF16), pb.astype(BF16), qb.astype(BF16), adec


def _s5(u, s0_re, s0_im, tables, n_p, seq, n_s, dec):
    kb, pb, qb, adec = tables
    tc = SSM_CHUNK
    g, p = SSM_GROUPS, SSM_STATE
    sw = 2 * g * p
    tm = ROW_TILE
    tp = n_p * seq
    ts = n_s * dec
    tiles_per_seq = seq // tm
    hw = SSM_WIDTH // 2
    assert dec <= tc and seq % tm == 0 and tm % tc == 0
    yp, fsp = pl.pallas_call(
        functools.partial(_s5_prompt_kernel, tc=tc, tiles_per_seq=tiles_per_seq),
        out_shape=[jax.ShapeDtypeStruct((tp, SSM_WIDTH), F32),
                   jax.ShapeDtypeStruct((n_p, 1, sw), F32)],
        grid=(tp // tm,),
        in_specs=[pl.BlockSpec((tm, hw), lambda i: (i, 0)), pl.BlockSpec((tm, hw), lambda i: (i, 1)),
                  _const_spec(kb.shape), _const_spec(pb.shape), _const_spec(qb.shape),
                  _const_spec(adec.shape)],
        out_specs=[pl.BlockSpec((tm, SSM_WIDTH), lambda i: (i, 0)),
                   pl.BlockSpec((None, 1, sw), lambda i: (i // tiles_per_seq, 0, 0))],
        scratch_shapes=[pltpu.VMEM((1, sw), F32), pltpu.VMEM((tm // tc, sw), F32),
                        pltpu.VMEM((tm // tc, sw), F32),
                        pltpu.VMEM((tm, hw), F32), pltpu.VMEM((tm, hw), F32)],
        compiler_params=_cparams(("arbitrary",)),
        name="s5_prompt",
    )(u, u, kb, pb, qb, adec)
    s0 = jnp.concatenate([s0_re.reshape(n_s, g * p), s0_im.reshape(n_s, g * p)], axis=1)
    us = u[tp:]
    ys, fss = pl.pallas_call(
        functools.partial(_s5_sample_kernel, tc=dec, k0=tc - dec),
        out_shape=[jax.ShapeDtypeStruct((ts, SSM_WIDTH), F32), jax.ShapeDtypeStruct((n_s, sw), F32)],
        grid=(1,),
        in_specs=[pl.BlockSpec((ts, hw), lambda i: (0, 0)), pl.BlockSpec((ts, hw), lambda i: (0, 1)),
                  _const_spec((n_s, sw)),
                  _const_spec(kb.shape), _const_spec(pb.shape), _const_spec(qb.shape),
                  _const_spec(adec.shape)],
        out_specs=[_const_spec((ts, SSM_WIDTH)), _const_spec((n_s, sw))],
        scratch_shapes=[pltpu.VMEM((ts, hw), F32), pltpu.VMEM((ts, hw), F32)],
        compiler_params=_cparams(("arbitrary",)),
        name="s5_sample",
    )(us, us, s0, kb, pb, qb, adec)
    half = g * p
    fsp = fsp.reshape(n_p, sw)
    st = lambda a, n: a.reshape(n, g, p)
    return (yp, ys, st(fsp[:, :half], n_p), st(fsp[:, half:], n_p),
            st(fss[:, :half], n_s), st(fss[:, half:], n_s))


def _lambda(lp_ref, lam_init):
    lp = lp_ref[...]
    s1 = jnp.sum(lp[0:1, :] * lp[1:2, :], axis=-1, keepdims=True)
    s2 = jnp.sum(lp[2:3, :] * lp[3:4, :], axis=-1, keepdims=True)
    return jnp.exp(s1) - jnp.exp(s2) + lam_init


def _attn_kernel(qi_ref, ki_ref, q_ref, k_ref, v_ref, lp_ref, hg_ref, o_ref, m_scr, acc_scr,
                 *, tile, lam_init):
    t = pl.program_id(1)
    qi = qi_ref[t]
    ki = ki_ref[t]
    vd = ATT_V_DIM

    @pl.when(ki == 0)
    def _():
        m_scr[...] = jnp.full(m_scr.shape, NEG_INF, F32)
        acc_scr[...] = jnp.zeros(acc_scr.shape, F32)

    def accumulate(masked):
        lane = lax.broadcasted_iota(jnp.int32, (tile, vd), 1)
        ones = jnp.ones((tile, vd), BF16)
        if masked:
            mask = (lax.broadcasted_iota(jnp.int32, (tile, tile), 1)
                    <= lax.broadcasted_iota(jnp.int32, (tile, tile), 0))
        nt = (((1,), (1,)), ((), ()))
        for h in range(ATT_HEADS):
            cols = slice(h * vd, (h + 1) * vd)
            q = q_ref[:, cols]
            k = k_ref[:, cols]
            v1 = jnp.concatenate([v_ref[:, cols], ones], axis=1)
            zero = jnp.zeros_like(q)
            for c in range(2):
                qm = jnp.where((lane >= ATT_HEAD_DIM) == bool(c), q, zero)
                s = lax.dot_general(qm, k, nt, preferred_element_type=F32)
                if masked:
                    s = jnp.where(mask, s, NEG_INF)
                idx = 2 * h + c
                m_old = m_scr[idx]
                m_row = jnp.max(s, axis=-1, keepdims=True)
                m_new = jnp.maximum(m_old, jnp.broadcast_to(m_row, m_old.shape))
                alpha = jnp.exp(m_old - m_new)
                p = jnp.exp(s - jnp.concatenate([m_new] * (tile // vd), axis=1)).astype(BF16)
                acc_scr[idx] = jnp.concatenate([alpha, alpha], axis=1) * acc_scr[idx] + _bdot(p, v1)
                m_scr[idx] = m_new

    @pl.when(ki < qi)
    def _():
        accumulate(False)

    @pl.when(ki == qi)
    def _():
        accumulate(True)
        lam = _lambda(lp_ref, lam_init)
        hg = hg_ref[...]
        for h in range(ATT_HEADS):
            a1 = acc_scr[2 * h]
            a2 = acc_scr[2 * h + 1]
            o = a1[:, :vd] / a1[:, vd:] - lam * (a2[:, :vd] / a2[:, vd:])
            o_ref[:, h * vd:(h + 1) * vd] = (_rms_rows(o, hg) * (1.0 - lam_init)).astype(o_ref.dtype)


def _attn_prompt(q, k, v, lp, hg, n_p, seq, lam_init):
    tile = min(ATT_TILE, seq)
    assert seq % tile == 0 and tile % ATT_V_DIM == 0
    nq = seq // tile
    pairs = [(i, j) for i in range(nq) for j in range(i + 1)]
    qi_tab = jnp.asarray([a for a, _ in pairs], jnp.int32)
    ki_tab = jnp.asarray([b for _, b in pairs], jnp.int32)
    qspec = pl.BlockSpec((tile, ATT_WIDTH), lambda b, t, qi, ki: (b * nq + qi[t], 0))
    kspec = pl.BlockSpec((tile, ATT_WIDTH), lambda b, t, qi, ki: (b * nq + ki[t], 0))
    grid_spec = pltpu.PrefetchScalarGridSpec(
        num_scalar_prefetch=2,
        grid=(n_p, len(pairs)),
        in_specs=[qspec, kspec, kspec,
                  pl.BlockSpec((4, ATT_HEAD_DIM), lambda b, t, qi, ki: (0, 0)),
                  pl.BlockSpec((1, ATT_V_DIM), lambda b, t, qi, ki: (0, 0))],
        out_specs=qspec,
        scratch_shapes=[pltpu.VMEM((2 * ATT_HEADS, tile, ATT_V_DIM), F32),
                        pltpu.VMEM((2 * ATT_HEADS, tile, 2 * ATT_V_DIM), F32)],
    )
    return pl.pallas_call(
        functools.partial(_attn_kernel, tile=tile, lam_init=lam_init),
        out_shape=jax.ShapeDtypeStruct((n_p * seq, ATT_WIDTH), BF16),
        grid_spec=grid_spec,
        compiler_params=_cparams(("parallel", "arbitrary")),
        name="attn_prompt",
    )(qi_tab, ki_tab, q, k, v, lp, hg)


def _attn_sample_kernel(pt_ref, q_ref, kn_ref, vn_ref, lp_ref, hg_ref, *rest, n_pages, dec, lam_init):
    del pt_ref
    kp = rest[:n_pages]
    vp = rest[n_pages:2 * n_pages]
    o_ref = rest[2 * n_pages]
    page = kp[0].shape[1]
    grp = 2 * dec
    n_rows = ATT_HEADS * grp
    q = q_ref[...].astype(F32)
    qt = jnp.concatenate([q] * (ATT_HEADS * 2), axis=0)
    r = lax.broadcasted_iota(jnp.int32, (n_rows, QK_WIDTH), 0)
    c = lax.broadcasted_iota(jnp.int32, (n_rows, QK_WIDTH), 1)
    qb = jnp.where(r // dec == c // ATT_HEAD_DIM, qt, 0.0).astype(BF16)
    nt = (((1,), (1,)), ((), ()))
    s_pages = [_bdot(qb, kp[j][...].astype(BF16)) for j in range(n_pages)]
    s_new = lax.dot_general(qb, kn_ref[...], nt, preferred_element_type=F32)
    rn = lax.broadcasted_iota(jnp.int32, (n_rows, dec), 0) % dec
    cn = lax.broadcasted_iota(jnp.int32, (n_rows, dec), 1)
    s_new = jnp.where(cn <= rn, s_new, NEG_INF)
    m = jnp.max(s_new, axis=-1, keepdims=True)
    for s in s_pages:
        m = jnp.maximum(m, jnp.max(s, axis=-1, keepdims=True))
    p_new = jnp.exp(s_new - m)
    l = jnp.sum(p_new, axis=-1, keepdims=True)
    acc_new = _bdot(p_new.astype(BF16), vn_ref[...])
    ps = []
    for j in range(n_pages):
        p = jnp.exp(s_pages[j] - m)
        l = l + jnp.sum(p, axis=-1, keepdims=True)
        ps.append(p)
    lam = _lambda(lp_ref, lam_init)
    hg = hg_ref[...]
    for h in range(ATT_HEADS):
        rows = slice(h * grp, (h + 1) * grp)
        cols = slice(h * ATT_V_DIM, (h + 1) * ATT_V_DIM)
        acc = acc_new[rows, cols]
        for j in range(n_pages):
            vh = vp[j][pl.ds(h, page, stride=ATT_HEADS), :].astype(BF16)
            acc = acc + _bdot(ps[j][rows, :].astype(BF16), vh)
        acc = acc / l[rows, :]
        o = acc[:dec] - lam * acc[dec:]
        o_ref[:, cols] = (_rms_rows(o, hg) * (1.0 - lam_init)).astype(o_ref.dtype)


def _attn_sample(q, kn, vn, cache_kt, cache_vr, layer, page_table, lp, hg, lam_init):
    n_s, dec, _ = q.shape
    n_pages = page_table.shape[1]
    page = cache_kt.shape[3]
    tok = pl.BlockSpec((None, dec, QK_WIDTH), lambda n, pt: (n, 0, 0))
    kpages = [pl.BlockSpec((None, None, QK_WIDTH, page), lambda n, pt, j=j: (layer, pt[n, j], 0, 0))
              for j in range(n_pages)]
    vpages = [pl.BlockSpec((None, None, page * ATT_HEADS, ATT_V_DIM),
                           lambda n, pt, j=j: (layer, pt[n, j], 0, 0)) for j in range(n_pages)]
    grid_spec = pltpu.PrefetchScalarGridSpec(
        num_scalar_prefetch=1,
        grid=(n_s,),
        in_specs=[tok, tok, tok,
                  pl.BlockSpec((4, ATT_HEAD_DIM), lambda n, pt: (0, 0)),
                  pl.BlockSpec((1, ATT_V_DIM), lambda n, pt: (0, 0))] + kpages + vpages,
        out_specs=tok,
    )
    return pl.pallas_call(
        functools.partial(_attn_sample_kernel, n_pages=n_pages, dec=dec, lam_init=lam_init),
        out_shape=jax.ShapeDtypeStruct((n_s, dec, ATT_WIDTH), BF16),
        grid_spec=grid_spec,
        compiler_params=_cparams(("parallel",)),
        name="attn_sample",
    )(page_table, q, kn, vn, lp, hg, *([cache_kt] * n_pages), *([cache_vr] * n_pages))


def _merge_kernel(xp_ref, xs_ref, u_ref, cb_ref, vin_ref, halo_ref, gate_ref,
                  yrp_ref, yrs_ref, ybp_ref, ybs_ref, vm1s_ref, vm2s_ref,
                  d_ref, wglu_ref, bglu_ref, cw_ref, wssm_ref, watt_ref, wconv_ref, wout_ref,
                  op_ref, os_ref, *, n_prompt_tiles, tiles_per_seq):
    i = pl.program_id(0)
    is_prompt = i < n_prompt_tiles
    yraw = jnp.where(is_prompt, yrp_ref[...], yrs_ref[...])
    yb = jnp.where(is_prompt, ybp_ref[...], ybs_ref[...])
    y = jax.nn.gelu(yraw + d_ref[...] * u_ref[...])
    ya = y * jax.nn.sigmoid(_bdot(y.astype(BF16), wglu_ref[...]) + bglu_ref[...])

    vin = vin_ref[...]
    row = lax.broadcasted_iota(jnp.int32, vin.shape, 0)
    halo = jnp.where(i % tiles_per_seq == 0, 0.0, halo_ref[...])
    h1 = jnp.broadcast_to(halo[SUBLANES - 1:SUBLANES, :], vin.shape)
    h2 = jnp.broadcast_to(halo[SUBLANES - 2:SUBLANES - 1, :], vin.shape)
    vm1 = jnp.where(row == 0, h1, pltpu.roll(vin, 1, axis=0))
    vm2 = jnp.where(row == 0, h2, jnp.where(row == 1, h1, pltpu.roll(vin, 2, axis=0)))
    vm1 = jnp.where(is_prompt, vm1, vm1s_ref[...])
    vm2 = jnp.where(is_prompt, vm2, vm2s_ref[...])
    conv = vm2 * cw_ref[0:1, :] + vm1 * cw_ref[1:2, :] + vin * cw_ref[2:3, :]
    yc = cb_ref[...] * conv
    merged = (gate_ref[:, 0:D_MODEL] * _bdot(ya.astype(BF16), wssm_ref[...])
              + gate_ref[:, D_MODEL:2 * D_MODEL] * _bdot(yb, watt_ref[...])
              + gate_ref[:, 2 * D_MODEL:3 * D_MODEL] * _bdot(yc.astype(BF16), wconv_ref[...]))
    x_new = _read_group(xp_ref, xs_ref, is_prompt) + _bdot(merged.astype(BF16), wout_ref[...])
    _write_group(op_ref, os_ref, is_prompt, x_new)


def _merge(xp, xs, u, cb, vin, gates, yr_p, yr_s, yb_p, yb_s, vm1_s, vm2_s,
           d, wglu, bglu, cw, wssm, watt, wconv, wout, tiles_per_seq):
    tm = ROW_TILE
    tp, ts = xp.shape[0], xs.shape[0]
    npt = tp // tm
    row = lambda w: pl.BlockSpec((tm, w), lambda i: (i, 0))
    prow = lambda w: _group_specs(tm, w, npt)[0]
    srow = lambda w: _group_specs(tm, w, npt)[1]
    halo = pl.BlockSpec((SUBLANES, CONV_WIDTH), lambda i: (jnp.maximum(i * (tm // SUBLANES) - 1, 0), 0))
    return pl.pallas_call(
        functools.partial(_merge_kernel, n_prompt_tiles=npt, tiles_per_seq=tiles_per_seq),
        out_shape=[jax.ShapeDtypeStruct((tp, D_MODEL), F32), jax.ShapeDtypeStruct((ts, D_MODEL), F32)],
        grid=((tp + ts) // tm,),
        in_specs=[prow(D_MODEL), srow(D_MODEL), row(SSM_WIDTH), row(CONV_WIDTH), row(CONV_WIDTH), halo,
                  row(N_BRANCH * D_MODEL),
                  prow(SSM_WIDTH), srow(SSM_WIDTH), prow(ATT_WIDTH), srow(ATT_WIDTH),
                  srow(CONV_WIDTH), srow(CONV_WIDTH),
                  _const_spec((1, SSM_WIDTH)), _const_spec((SSM_WIDTH, SSM_WIDTH)),
                  _const_spec((1, SSM_WIDTH)), _const_spec((CONV_K, CONV_WIDTH)),
                  _const_spec((SSM_WIDTH, D_MODEL)), _const_spec((ATT_WIDTH, D_MODEL)),
                  _const_spec((CONV_WIDTH, D_MODEL)), _const_spec((D_MODEL, D_MODEL))],
        out_specs=[prow(D_MODEL), srow(D_MODEL)],
        compiler_params=_cparams(("arbitrary",)),
        name="merge",
    )(xp, xs, u, cb, vin, vin, gates, yr_p, yr_s, yb_p, yb_s, vm1_s, vm2_s,
      d, wglu, bglu, cw, wssm, watt, wconv, wout)


def _swiglu(h, w1, w3, w2):
    a = _bdot(h, w1)
    b = _bdot(h, w3)
    return _bdot((jax.nn.silu(a) * b).astype(BF16), w2)


def _ffn_kernel(xp_ref, xs_ref, g_ref, w1_ref, w3_ref, w2_ref, op_ref, os_ref, *, n_prompt_tiles):
    is_prompt = pl.program_id(0) < n_prompt_tiles
    x = _read_group(xp_ref, xs_ref, is_prompt)
    h = _rms_rows(x, g_ref[...]).astype(BF16)
    _write_group(op_ref, os_ref, is_prompt, x + _swiglu(h, w1_ref[...], w3_ref[...], w2_ref[...]))


def _ffn(xp, xs, g, w1, w3, w2):
    tm = ROW_TILE
    tp, ts = xp.shape[0], xs.shape[0]
    npt = tp // tm
    d_ff = w1.shape[1]
    rows = _group_specs(tm, D_MODEL, npt)
    return pl.pallas_call(
        functools.partial(_ffn_kernel, n_prompt_tiles=npt),
        out_shape=[jax.ShapeDtypeStruct((tp, D_MODEL), F32), jax.ShapeDtypeStruct((ts, D_MODEL), F32)],
        grid=((tp + ts) // tm,),
        in_specs=rows + [_const_spec((1, D_MODEL)), _const_spec((D_MODEL, d_ff)),
                         _const_spec((D_MODEL, d_ff)), _const_spec((d_ff, D_MODEL))],
        out_specs=rows,
        compiler_params=_cparams(("arbitrary",)),
        name="ffn",
    )(xp, xs, g, w1, w3, w2)


def _router_kernel(xp_ref, xs_ref, g_ref, wr_ref, tri_ref, h_ref, meta_ref, gate_ref, cnt_ref, carry,
                   *, n_prompt_tiles):
    @pl.when(pl.program_id(0) == 0)
    def _():
        carry[...] = jnp.zeros(carry.shape, F32)

    x = _read_group(xp_ref, xs_ref, pl.program_id(0) < n_prompt_tiles)
    h = _rms_rows(x, g_ref[...])
    h_ref[...] = h
    logits = jnp.dot(h, wr_ref[...], preferred_element_type=F32, precision=lax.Precision.HIGHEST)
    lane = lax.broadcasted_iota(jnp.int32, logits.shape, 1)
    logits = jnp.where(lane < N_EXPERTS, logits, -jnp.inf)
    big = jnp.int32(logits.shape[1])
    m1 = jnp.max(logits, axis=-1, keepdims=True)
    i1 = jnp.min(jnp.where(logits == m1, lane, big), axis=-1, keepdims=True)
    rest = jnp.where(lane == i1, -jnp.inf, logits)
    m2 = jnp.max(rest, axis=-1, keepdims=True)
    i2 = jnp.min(jnp.where(rest == m2, lane, big), axis=-1, keepdims=True)
    e = jnp.exp(m2 - m1)
    g1 = 1.0 / (1.0 + e)
    g2 = e / (1.0 + e)
    o1 = lane == i1
    o2 = lane == i2
    chosen = jnp.where(o1 | o2, 1.0, 0.0)
    base = _bdot(tri_ref[...], chosen.astype(BF16)) + carry[...]
    r1 = jnp.sum(jnp.where(o1, base, 0.0), axis=-1, keepdims=True).astype(jnp.int32)
    r2 = jnp.sum(jnp.where(o2, base, 0.0), axis=-1, keepdims=True).astype(jnp.int32)
    carry[...] = carry[...] + jnp.sum(chosen, axis=0, keepdims=True)
    cnt_ref[...] = carry[...]
    meta_ref[...] = jnp.where(lane == 0, i1, jnp.where(lane == 1, i2,
                              jnp.where(lane == 2, r1, jnp.where(lane == 3, r2, 0))))
    gate_ref[...] = jnp.where(lane == 0, g1, jnp.where(lane == 1, g2, 0.0))


def _router(xp, xs, g, wr_pad):
    tm = ROW_TILE
    t = xp.shape[0] + xs.shape[0]
    npt = xp.shape[0] // tm
    lanes = wr_pad.shape[1]
    tri = jnp.tri(tm, k=-1, dtype=BF16)
    row = lambda w: pl.BlockSpec((tm, w), lambda i: (i, 0))
    return pl.pallas_call(
        functools.partial(_router_kernel, n_prompt_tiles=npt),
        out_shape=[jax.ShapeDtypeStruct((t, D_MODEL), F32),
                   jax.ShapeDtypeStruct((t, lanes), jnp.int32),
                   jax.ShapeDtypeStruct((t, lanes), F32),
                   jax.ShapeDtypeStruct((1, lanes), F32)],
        grid=(t // tm,),
        in_specs=_group_specs(tm, D_MODEL, npt) + [_const_spec((1, D_MODEL)), _const_spec((D_MODEL, lanes)),
                                                   _const_spec((tm, tm))],
        out_specs=[row(D_MODEL), row(lanes), row(lanes), _const_spec((1, lanes))],
        scratch_shapes=[pltpu.VMEM((1, lanes), F32)],
        compiler_params=_cparams(("arbitrary",)),
        name="router",
    )(xp, xs, g, wr_pad, tri)


def _sc_mesh():
    return plsc.VectorSubcoreMesh(core_axis_name="c", subcore_axis_name="s")


def _sc_worker_base(per_worker):
    return (lax.axis_index("s") * SC_CORES + lax.axis_index("c")) * per_worker


def _sc_scatter_rows(x, idx0, idx1, n_out):
    n, w = x.shape
    per_worker = n // SC_WORKERS
    assert n % (SC_WORKERS * SC_ROWS) == 0

    @functools.partial(
        pl.kernel, mesh=_sc_mesh(), out_type=jax.ShapeDtypeStruct((n_out, w), x.dtype),
        scratch_types=[pltpu.VMEM((SC_ROWS,), jnp.int32), pltpu.VMEM((SC_ROWS,), jnp.int32),
                       pltpu.VMEM((SC_ROWS, w), x.dtype), pltpu.SemaphoreType.DMA])
    def scatter(x_hbm, i0_hbm, i1_hbm, out_hbm, i0_v, i1_v, rows_v, sem):
        start = _sc_worker_base(per_worker)

        @pl.loop(0, per_worker // SC_ROWS)
        def _(c):
            rows = pl.ds(pl.multiple_of(start + c * SC_ROWS, SUBLANES), SC_ROWS)
            pltpu.sync_copy(x_hbm.at[rows], rows_v)
            pltpu.sync_copy(i0_hbm.at[rows], i0_v)
            pltpu.sync_copy(i1_hbm.at[rows], i1_v)
            pltpu.async_copy(rows_v, out_hbm.at[i0_v], sem).wait()
            pltpu.async_copy(rows_v, out_hbm.at[i1_v], sem).wait()

    return scatter(x, idx0, idx1)


def _sc_gather_rows(table, idx):
    n, w = idx.shape[0], table.shape[1]
    per_worker = n // SC_WORKERS
    assert n % (SC_WORKERS * SC_ROWS) == 0

    @functools.partial(
        pl.kernel, mesh=_sc_mesh(), out_type=jax.ShapeDtypeStruct((n, w), table.dtype),
        scratch_types=[pltpu.VMEM((SC_ROWS,), jnp.int32), pltpu.VMEM((SC_ROWS, w), table.dtype),
                       pltpu.SemaphoreType.DMA])
    def gather(table_hbm, idx_hbm, out_hbm, idx_v, rows_v, sem):
        start = _sc_worker_base(per_worker)

        @pl.loop(0, per_worker // SC_ROWS)
        def _(c):
            rows = pl.ds(pl.multiple_of(start + c * SC_ROWS, SUBLANES), SC_ROWS)
            pltpu.sync_copy(idx_hbm.at[rows], idx_v)
            pltpu.async_copy(table_hbm.at[idx_v], rows_v, sem).wait()
            pltpu.sync_copy(rows_v, out_hbm.at[rows])

    return gather(table, idx)


def _moe_kernel(be_ref, nv_ref, x_ref, w1_ref, w3_ref, w2_ref, o_ref):
    del be_ref

    @pl.when(pl.program_id(0) < nv_ref[0])
    def _():
        o_ref[...] = _swiglu(x_ref[...].astype(BF16), w1_ref[...], w3_ref[...], w2_ref[...])


def _moe_blocks(xs, block_expert, n_valid, w1, w3, w2):
    cap = xs.shape[0]
    tb = MOE_TILE
    d_ff = w1.shape[2]
    grid_spec = pltpu.PrefetchScalarGridSpec(
        num_scalar_prefetch=2,
        grid=(cap // tb,),
        in_specs=[pl.BlockSpec((tb, D_MODEL), lambda i, be, nv: (jnp.minimum(i, nv[0] - 1), 0)),
                  pl.BlockSpec((None, D_MODEL, d_ff), lambda i, be, nv: (be[i], 0, 0)),
                  pl.BlockSpec((None, D_MODEL, d_ff), lambda i, be, nv: (be[i], 0, 0)),
                  pl.BlockSpec((None, d_ff, D_MODEL), lambda i, be, nv: (be[i], 0, 0))],
        out_specs=pl.BlockSpec((tb, D_MODEL), lambda i, be, nv: (i, 0)),
    )
    return pl.pallas_call(
        _moe_kernel,
        out_shape=jax.ShapeDtypeStruct((cap, D_MODEL), F32),
        grid_spec=grid_spec,
        compiler_params=_cparams(("arbitrary",)),
        name="moe",
    )(block_expert, n_valid, xs, w1, w3, w2)


def _combine_kernel(xp_ref, xs_ref, a0_ref, a1_ref, gate_ref, op_ref, os_ref, *, n_prompt_tiles):
    is_prompt = pl.program_id(0) < n_prompt_tiles
    gate = gate_ref[...]
    y = (_read_group(xp_ref, xs_ref, is_prompt)
         + gate[:, 0:1] * a0_ref[...] + gate[:, 1:2] * a1_ref[...])
    _write_group(op_ref, os_ref, is_prompt, y)


def _combine(xp, xs, picked, gate):
    tm = ROW_TILE
    tp, ts = xp.shape[0], xs.shape[0]
    nt = (tp + ts) // tm
    rows = _group_specs(tm, D_MODEL, tp // tm)
    return pl.pallas_call(
        functools.partial(_combine_kernel, n_prompt_tiles=tp // tm),
        out_shape=[jax.ShapeDtypeStruct((tp, D_MODEL), F32), jax.ShapeDtypeStruct((ts, D_MODEL), F32)],
        grid=(nt,),
        in_specs=rows + [pl.BlockSpec((tm, D_MODEL), lambda i: (i, 0)),
                         pl.BlockSpec((tm, D_MODEL), lambda i: (i + nt, 0)),
                         pl.BlockSpec((tm, gate.shape[1]), lambda i: (i, 0))],
        out_specs=rows,
        compiler_params=_cparams(("arbitrary",)),
        name="moe_combine",
    )(xp, xs, picked, picked, gate)


def _moe(xp, xs, g, wr, w1, w3, w2):
    t = xp.shape[0] + xs.shape[0]
    tb = MOE_TILE
    lanes = 128
    wr_pad = jnp.pad(wr, ((0, 0), (0, lanes - N_EXPERTS)))
    h, meta, gate, cnt = _router(xp, xs, g, wr_pad)
    counts = cnt[0, :N_EXPERTS].astype(jnp.int32)
    padded = (counts + tb - 1) // tb * tb
    pad_ends = jnp.cumsum(padded)
    pad_starts = pad_ends - padded
    experts = jnp.arange(N_EXPERTS, dtype=jnp.int32)
    slot = lambda e, r: jnp.sum(jnp.where(e[:, None] == experts, pad_starts, 0), axis=1) + r
    dest0 = slot(meta[:, 0], meta[:, 2])
    dest1 = slot(meta[:, 1], meta[:, 3])
    nb = -(-(t * TOP_K + N_EXPERTS * (tb - 1)) // tb)
    block_expert = jnp.minimum(
        jnp.sum(pad_ends[None, :] <= (jnp.arange(nb, dtype=jnp.int32) * tb)[:, None], axis=1),
        N_EXPERTS - 1).astype(jnp.int32)
    n_valid = (pad_ends[-1:] // tb).astype(jnp.int32)
    slots = _sc_scatter_rows(h, dest0, dest1, nb * tb)
    out = _moe_blocks(slots, block_expert, n_valid, w1, w3, w2)
    picked = _sc_gather_rows(out, jnp.concatenate([dest0, dest1]))
    return _combine(xp, xs, picked, gate)


def kernel(x_prompt, x_sample, cache_k, cache_v, page_table, state_ssm_re, state_ssm_im, state_conv,
           norm_mix_g, norm_ffn_g, w_in, ssm_a_re, ssm_a_im, ssm_log_dt, ssm_b_re, ssm_b_im,
           ssm_c_re, ssm_c_im, ssm_d, ssm_w_glu, ssm_b_glu, q_norm_g, k_norm_g,
           lambda_q1, lambda_k1, lambda_q2, lambda_k2, head_norm_g, conv_w,
           w_br_ssm, w_br_att, w_br_conv, w_out, ffn_w1, ffn_w3, ffn_w2,
           router_w, moe_w1, moe_w3, moe_w2):
    n_p, seq, _ = x_prompt.shape
    n_s, dec, _ = x_sample.shape
    depth = w_in.shape[0]
    tp = n_p * seq
    ts = n_s * dec
    assert seq % ROW_TILE == 0 and ts % ROW_TILE == 0 and seq >= CONV_K - 1
    pool, page = cache_k.shape[1], cache_k.shape[2]
    cache_kt = cache_k.reshape(depth, pool, page, QK_WIDTH).transpose(0, 1, 3, 2)
    cache_vr = cache_v.reshape(depth, pool, page * ATT_HEADS, ATT_V_DIM)
    seg = jnp.kron(jnp.eye(QK_WIDTH // ATT_HEAD_DIM, dtype=F32),
                   jnp.full((ATT_HEAD_DIM, ATT_HEAD_DIM), 1.0 / ATT_HEAD_DIM, F32)).astype(BF16)
    n_rep = QK_WIDTH // ATT_HEAD_DIM

    xp = x_prompt.reshape(tp, D_MODEL)
    xs = x_sample.reshape(ts, D_MODEL)
    kv_prompt = None
    srp, sip, cvp = [], [], []
    ks, vs, srs, sis, cvs = [], [], [], [], []
    for l in range(depth):
        lam_init = 0.8 - 0.6 * math.exp(-0.3 * l)
        u, q, kb, vb, cb, vin, gates, kt, vr, k_s, v_s = _inproj(
            xp, xs, norm_mix_g[l][None], w_in[l].astype(BF16),
            jnp.tile(q_norm_g[l], n_rep)[None], jnp.tile(k_norm_g[l], n_rep)[None], seg,
            l, depth, n_p, seq, kv_prompt)
        kv_prompt = (kt, vr)

        tables = _s5_tables(ssm_a_re[l], ssm_a_im[l], ssm_log_dt[l], ssm_b_re[l], ssm_b_im[l],
                            ssm_c_re[l], ssm_c_im[l], dec)
        yr_p, yr_s, p_re, p_im, s_re, s_im = _s5(u, state_ssm_re[l], state_ssm_im[l], tables,
                                                  n_p, seq, n_s, dec)

        lp = jnp.stack([lambda_q1[l], lambda_k1[l], lambda_q2[l], lambda_k2[l]])
        hg = head_norm_g[l][None]
        yb_p = _attn_prompt(q, kb, vb, lp, hg, n_p, seq, lam_init)
        yb_s = _attn_sample(q[tp:].reshape(n_s, dec, QK_WIDTH), kb[tp:].reshape(n_s, dec, QK_WIDTH),
                            vb[tp:].reshape(n_s, dec, ATT_WIDTH), cache_kt, cache_vr, l, page_table,
                            lp, hg, lam_init).reshape(ts, ATT_WIDTH)

        ext_s = jnp.concatenate([state_conv[l], vin[tp:].reshape(n_s, dec, CONV_WIDTH)], axis=1)
        vm1_s = ext_s[:, 1:1 + dec].reshape(ts, CONV_WIDTH)
        vm2_s = ext_s[:, 0:dec].reshape(ts, CONV_WIDTH)
        xp, xs = _merge(xp, xs, u, cb, vin, gates, yr_p, yr_s, yb_p, yb_s, vm1_s, vm2_s,
                        ssm_d[l][None], ssm_w_glu[l].astype(BF16), ssm_b_glu[l][None], conv_w[l],
                        w_br_ssm[l].astype(BF16), w_br_att[l].astype(BF16), w_br_conv[l].astype(BF16),
                        w_out[l].astype(BF16), seq // ROW_TILE)

        i = l // 2
        if l % 2 == 0:
            xp, xs = _ffn(xp, xs, norm_ffn_g[l][None], ffn_w1[i].astype(BF16), ffn_w3[i].astype(BF16),
                          ffn_w2[i].astype(BF16))
        else:
            xp, xs = _moe(xp, xs, norm_ffn_g[l][None], router_w[i], moe_w1[i].astype(BF16),
                          moe_w3[i].astype(BF16), moe_w2[i].astype(BF16))

        srp.append(p_re); sip.append(p_im)
        cvp.append(vin[:tp].reshape(n_p, seq, CONV_WIDTH)[:, seq - (CONV_K - 1):])
        ks.append(k_s.reshape(n_s, dec, ATT_HEADS, 2, ATT_HEAD_DIM))
        vs.append(v_s.reshape(n_s, dec, ATT_HEADS, ATT_V_DIM))
        srs.append(s_re); sis.append(s_im); cvs.append(ext_s[:, dec:])

    kt, vr = kv_prompt
    k_prompt = kt.reshape(depth, n_p, ATT_HEADS, 2, ATT_HEAD_DIM, seq).transpose(0, 1, 5, 2, 3, 4)
    v_prompt = vr.reshape(depth, n_p, seq, ATT_HEADS, ATT_V_DIM)
    return (xp.reshape(n_p, seq, D_MODEL), xs.reshape(n_s, dec, D_MODEL),
            k_prompt, v_prompt, jnp.stack(srp), jnp.stack(sip), jnp.stack(cvp),
            jnp.stack(ks), jnp.stack(vs), jnp.stack(srs), jnp.stack(sis), jnp.stack(cvs))
```

```python
import functools
import math

import jax
import jax.numpy as jnp
from jax import lax
from jax.experimental import pallas as pl
from jax.experimental.pallas import tpu as pltpu
from jax.experimental.pallas import tpu_sc as plsc

F32 = jnp.float32
BF16 = jnp.bfloat16

D_MODEL = 1024
SSM_WIDTH = 256
SSM_GROUP = 16
SSM_GROUPS = SSM_WIDTH // SSM_GROUP
SSM_STATE = 64
ATT_HEADS = 4
ATT_HEAD_DIM = 64
ATT_V_DIM = 2 * ATT_HEAD_DIM
QK_WIDTH = ATT_HEADS * 2 * ATT_HEAD_DIM
ATT_WIDTH = ATT_HEADS * ATT_V_DIM
CONV_WIDTH = 256
CONV_K = 3
N_BRANCH = 3
N_EXPERTS = 8
TOP_K = 2
EPS = 1e-6
NEG_INF = -1e30

C_U = 0
C_Q = C_U + SSM_WIDTH
C_K = C_Q + QK_WIDTH
C_V = C_K + QK_WIDTH
C_CB = C_V + ATT_WIDTH
C_CC = C_CB + CONV_WIDTH
C_CH = C_CC + CONV_WIDTH
C_G = C_CH + CONV_WIDTH
IN_COLS = C_G + N_BRANCH * D_MODEL

SSM_CHUNK = 8
SUBLANES = 8
BF16_SUBLANES = 16
ROW_TILE = 512
ATT_TILE = 512
MOE_TILE = 256
SC_CORES = 2
SC_WORKERS = SC_CORES * 16
SC_ROWS = 48
VMEM_LIMIT = 56 * 1024 * 1024


def _cparams(sem):
    return pltpu.CompilerParams(dimension_semantics=sem, vmem_limit_bytes=VMEM_LIMIT)


def _const_spec(shape):
    nd = len(shape)
    return pl.BlockSpec(shape, lambda *_: (0,) * nd)


def _bdot(a, b):
    return jnp.dot(a, b, preferred_element_type=F32)


def _rms_rows(x, g):
    ms = jnp.mean(x * x, axis=-1, keepdims=True)
    return x * lax.rsqrt(ms + EPS) * g


def _group_specs(tm, w, n_prompt_tiles):
    return [pl.BlockSpec((tm, w), lambda i: (jnp.minimum(i, n_prompt_tiles - 1), 0)),
            pl.BlockSpec((tm, w), lambda i: (jnp.maximum(i - n_prompt_tiles, 0), 0))]


def _read_group(p_ref, s_ref, is_prompt):
    return jnp.where(is_prompt, p_ref[...], s_ref[...])


def _write_group(p_ref, s_ref, is_prompt, val):
    @pl.when(is_prompt)
    def _():
        p_ref[...] = val

    @pl.when(jnp.logical_not(is_prompt))
    def _():
        s_ref[...] = val


def _segment_rms(z, g, seg):
    ms = _bdot((z * z).astype(BF16), seg)
    return z * lax.rsqrt(ms + EPS) * g


def _inproj_kernel(xp_ref, xs_ref, g_ref, w_ref, qg_ref, kg_ref, seg_ref, *rest,
                   n_prompt_tiles, n_prev, layer):
    (u_ref, q_ref, kb_ref, vb_ref, cb_ref, vin_ref, gate_ref,
     kt_ref, vr_ref, ks_ref, vs_ref) = rest[n_prev:]
    tm = u_ref.shape[0]
    is_prompt = pl.program_id(0) < n_prompt_tiles
    h = _rms_rows(_read_group(xp_ref, xs_ref, is_prompt), g_ref[...]).astype(BF16)

    def proj(a, b):
        return _bdot(h, w_ref[:, a:b])

    seg = seg_ref[...]
    u_ref[...] = proj(C_U, C_Q)
    qn = _segment_rms(proj(C_Q, C_K), qg_ref[...], seg)
    q_ref[...] = (qn * (ATT_HEAD_DIM ** -0.5)).astype(BF16)
    kn = _segment_rms(proj(C_K, C_V), kg_ref[...], seg)
    kb_ref[...] = kn.astype(BF16)
    v = proj(C_V, C_CB)
    vb_ref[...] = v.astype(BF16)
    cb_ref[...] = proj(C_CB, C_CC)
    vin_ref[...] = proj(C_CC, C_CH) * proj(C_CH, C_G)
    for j in range(N_BRANCH):
        a = C_G + j * D_MODEL
        gate_ref[:, j * D_MODEL:(j + 1) * D_MODEL] = jax.nn.sigmoid(proj(a, a + D_MODEL)).astype(gate_ref.dtype)

    @pl.when(is_prompt)
    def _():
        if n_prev:
            kt_l, vr_l = kt_ref, vr_ref
        else:
            kt_l, vr_l = kt_ref.at[layer], vr_ref.at[layer]
            for other in range(kt_ref.shape[0]):
                if other != layer:
                    kt_ref[other] = jnp.zeros(kt_ref.shape[1:], F32)
                    vr_ref[other] = jnp.zeros(vr_ref.shape[1:], F32)
        kt_l[...] = kn.T
        for hd in range(ATT_HEADS):
            vr_l[pl.ds(hd, tm, stride=ATT_HEADS), :] = v[:, hd * ATT_V_DIM:(hd + 1) * ATT_V_DIM]

    @pl.when(jnp.logical_not(is_prompt))
    def _():
        ks_ref[...] = kn
        vs_ref[...] = v


def _inproj(xp, xs, g, w_bf, qg, kg, seg, layer, depth, n_p, seq, prev):
    tm = ROW_TILE
    tp, ts = xp.shape[0], xs.shape[0]
    t = tp + ts
    npt = tp // tm
    tps = seq // tm
    row = lambda w: pl.BlockSpec((tm, w), lambda i: (i, 0))
    pc = lambda i: jnp.minimum(i, npt - 1)
    outs = [(SSM_WIDTH, F32), (QK_WIDTH, BF16), (QK_WIDTH, BF16), (ATT_WIDTH, BF16),
            (CONV_WIDTH, F32), (CONV_WIDTH, F32), (N_BRANCH * D_MODEL, BF16)]
    out_shape = ([jax.ShapeDtypeStruct((t, w), d) for w, d in outs]
                 + [jax.ShapeDtypeStruct((depth, n_p, QK_WIDTH, seq), F32),
                    jax.ShapeDtypeStruct((depth, tp * ATT_HEADS, ATT_V_DIM), F32),
                    jax.ShapeDtypeStruct((ts, QK_WIDTH), F32), jax.ShapeDtypeStruct((ts, ATT_WIDTH), F32)])
    srow = _group_specs(tm, QK_WIDTH, npt)[1]
    prev = () if prev is None else tuple(prev)
    lead, at = (None, layer) if prev else (depth, 0)
    out_specs = ([row(w) for w, _ in outs]
                 + [pl.BlockSpec((lead, None, QK_WIDTH, tm), lambda i: (at, pc(i) // tps, 0, pc(i) % tps)),
                    pl.BlockSpec((lead, tm * ATT_HEADS, ATT_V_DIM), lambda i: (at, pc(i), 0)),
                    srow, srow])
    n_in = 7
    return pl.pallas_call(
        functools.partial(_inproj_kernel, n_prompt_tiles=npt, n_prev=len(prev), layer=layer),
        out_shape=out_shape,
        grid=(t // tm,),
        in_specs=(_group_specs(tm, D_MODEL, npt)
                  + [_const_spec((1, D_MODEL)), _const_spec((D_MODEL, IN_COLS)),
                     _const_spec((1, QK_WIDTH)), _const_spec((1, QK_WIDTH)),
                     _const_spec((QK_WIDTH, QK_WIDTH))]
                  + [pl.BlockSpec(memory_space=pl.ANY)] * len(prev)),
        out_specs=out_specs,
        input_output_aliases={n_in + j: len(outs) + j for j in range(len(prev))},
        compiler_params=_cparams(("arbitrary",)),
        name="inproj",
    )(xp, xs, g, w_bf, qg, kg, seg, *prev)


def _s5_intra(u, kb_ref, tc):
    rowmod = lax.broadcasted_iota(jnp.int32, u.shape, 0) % tc
    y = _bdot(u.astype(BF16), kb_ref[0])
    for m in range(1, tc):
        um = jnp.where(rowmod >= m, pltpu.roll(u, m, axis=0), 0.0)
        y = y + _bdot(um.astype(BF16), kb_ref[m])
    return y


def _s5_inject(u_refs, pb_ref, k0, n_chunks, tc):
    w = None
    for k in range(tc):
        rows = pl.ds(k, n_chunks, stride=tc)
        uk = jnp.concatenate([r[rows, :] for r in u_refs], axis=1).astype(BF16)
        d = _bdot(uk, pb_ref[k0 + k])
        w = d if w is None else w + d
    return w


def _s5_readout(y_ref, y_scrs, y, s_in, qb_ref, n_chunks, tc):
    lanes = y_scrs[0].shape[1]
    for h, scr in enumerate(y_scrs):
        scr[...] = y[:, h * lanes:(h + 1) * lanes]
    sb = s_in.astype(BF16)
    for k in range(tc):
        rows = pl.ds(k, n_chunks, stride=tc)
        yk = _bdot(sb, qb_ref[k])
        for h, scr in enumerate(y_scrs):
            scr[rows, :] = scr[rows, :] + yk[:, h * lanes:(h + 1) * lanes]
    y_ref[...] = jnp.concatenate([scr[...] for scr in y_scrs], axis=1)


def _cmul_add(a_re, a_im, s, w, half):
    s_re, s_im = s[:, :half], s[:, half:]
    return jnp.concatenate([a_re * s_re - a_im * s_im, a_re * s_im + a_im * s_re], axis=1) + w


def _s5_prompt_kernel(ua_ref, ub_ref, kb_ref, pb_ref, qb_ref, a_ref, y_ref, fs_ref,
                      carry, w_scr, s_scr, ya_scr, yb_scr, *, tc, tiles_per_seq):
    tm = ua_ref.shape[0]
    n_chunks = tm // tc
    half = a_ref.shape[1]

    @pl.when(pl.program_id(0) % tiles_per_seq == 0)
    def _():
        carry[...] = jnp.zeros(carry.shape, F32)

    y = _s5_intra(jnp.concatenate([ua_ref[...], ub_ref[...]], axis=1), kb_ref, tc)
    w_scr[...] = _s5_inject((ua_ref, ub_ref), pb_ref, 0, n_chunks, tc)
    a_re = a_ref[0:1, :]
    a_im = a_ref[1:2, :]
    s = carry[...]
    for c in range(n_chunks):
        s_scr[c:c + 1, :] = s
        s = _cmul_add(a_re, a_im, s, w_scr[c:c + 1, :], half)
    carry[...] = s
    fs_ref[...] = s
    _s5_readout(y_ref, (ya_scr, yb_scr), y, s_scr[...], qb_ref, n_chunks, tc)


def _s5_sample_kernel(ua_ref, ub_ref, s0_ref, kb_ref, pb_ref, qb_ref, a_ref, y_ref, fs_ref,
                      ya_scr, yb_scr, *, tc, k0):
    n_chunks = ua_ref.shape[0] // tc
    half = a_ref.shape[1]
    y = _s5_intra(jnp.concatenate([ua_ref[...], ub_ref[...]], axis=1), kb_ref, tc)
    w = _s5_inject((ua_ref, ub_ref), pb_ref, k0, n_chunks, tc)
    s0 = s0_ref[...]
    fs_ref[...] = _cmul_add(a_ref[2:3, :], a_ref[3:4, :], s0, w, half)
    _s5_readout(y_ref, (ya_scr, yb_scr), y, s0, qb_ref, n_chunks, tc)


def _s5_tables_kernel(are_ref, aim_ref, ldt_ref, bre_ref, bim_ref, cre_ref, cim_ref,
                      kb_ref, pb_ref, qb_ref, a_ref, *, tc, dec):
    are = are_ref[...]
    aim = aim_ref[...]
    dt = jnp.exp(ldt_ref[...])

    def power(m):
        mag = jnp.exp(are * dt * m)
        ang = aim * dt * m
        return mag * jnp.cos(ang), mag * jnp.sin(ang)

    ab_re, ab_im = power(1.0)
    den = are * are + aim * aim
    nr = ab_re - 1.0
    cr = (nr * are + ab_im * aim) / den
    ci = (ab_im * are - nr * aim) / den
    gj, gp = bre_ref.shape
    row_g = lax.broadcasted_iota(jnp.int32, (gj, gp), 0) // SSM_GROUP
    col_g = lax.broadcasted_iota(jnp.int32, (gj, gp), 1) // SSM_STATE
    diag = row_g == col_g
    bre = bre_ref[...]
    bim = bim_ref[...]
    bb_re = jnp.where(diag, cr * bre - ci * bim, 0.0)
    bb_im = jnp.where(diag, cr * bim + ci * bre, 0.0)
    cc_re = jnp.where(diag, cre_ref[...], 0.0)
    cc_im = jnp.where(diag, cim_ref[...], 0.0)
    c_blk = jnp.concatenate([cc_re, -cc_im], axis=1).T
    for m in range(tc):
        pr, pi = power(float(m))
        pm = jnp.concatenate([pr * bb_re - pi * bb_im, pr * bb_im + pi * bb_re], axis=1)
        pb_ref[tc - 1 - m] = pm.astype(pb_ref.dtype)
        kb_ref[m] = jnp.dot(pm, c_blk, preferred_element_type=F32,
                            precision=lax.Precision.HIGHEST).astype(kb_ref.dtype)
        qr, qi = power(float(m + 1))
        qm = jnp.concatenate([qr * cc_re - qi * cc_im, -(qr * cc_im + qi * cc_re)], axis=1)
        qb_ref[m] = qm.T.astype(qb_ref.dtype)
    a_ref[...] = jnp.concatenate(list(power(float(tc)) + power(float(dec))), axis=0)


def _s5_tables(a_re, a_im, log_dt, b_re, b_im, c_re, c_im, dec):
    depth = a_re.shape[0]
    tc = SSM_CHUNK
    g, p, j = SSM_GROUPS, SSM_STATE, SSM_GROUP
    gp, gj = g * p, g * j
    rowv = lambda t: t.reshape(depth, 1, gp)
    ldt = jnp.repeat(log_dt, p, axis=1).reshape(depth, 1, gp)
    bt = lambda t: jnp.tile(t.transpose(0, 1, 3, 2).reshape(depth, gj, p), (1, 1, g))
    ct = lambda t: jnp.tile(t.reshape(depth, gj, p), (1, 1, g))
    lay = lambda *shape: pl.BlockSpec((None,) + shape, lambda l: (l,) + (0,) * len(shape))
    return pl.pallas_call(
        functools.partial(_s5_tables_kernel, tc=tc, dec=dec),
        out_shape=[jax.ShapeDtypeStruct((depth, tc, gj, gj), BF16),
                   jax.ShapeDtypeStruct((depth, tc, gj, 2 * gp), BF16),
                   jax.ShapeDtypeStruct((depth, tc, 2 * gp, gj), BF16),
                   jax.ShapeDtypeStruct((depth, 4, gp), F32)],
        grid=(depth,),
        in_specs=[lay(1, gp)] * 3 + [lay(gj, gp)] * 4,
        out_specs=[lay(tc, gj, gj), lay(tc, gj, 2 * gp), lay(tc, 2 * gp, gj), lay(4, gp)],
        compiler_params=_cparams(("arbitrary",)),
        name="s5_tables",
    )(rowv(a_re), rowv(a_im), ldt, bt(b_re), bt(b_im), ct(c_re), ct(c_im))


def _s5(u, s0_re, s0_im, tables, n_p, seq, n_s, dec):
    kb, pb, qb, adec = tables
    tc = SSM_CHUNK
    g, p = SSM_GROUPS, SSM_STATE
    sw = 2 * g * p
    tm = ROW_TILE
    tp = n_p * seq
    ts = n_s * dec
    tiles_per_seq = seq // tm
    hw = SSM_WIDTH // 2
    assert dec <= tc and seq % tm == 0 and tm % tc == 0
    yp, fsp = pl.pallas_call(
        functools.partial(_s5_prompt_kernel, tc=tc, tiles_per_seq=tiles_per_seq),
        out_shape=[jax.ShapeDtypeStruct((tp, SSM_WIDTH), F32),
                   jax.ShapeDtypeStruct((n_p, 1, sw), F32)],
        grid=(tp // tm,),
        in_specs=[pl.BlockSpec((tm, hw), lambda i: (i, 0)), pl.BlockSpec((tm, hw), lambda i: (i, 1)),
                  _const_spec(kb.shape), _const_spec(pb.shape), _const_spec(qb.shape),
                  _const_spec(adec.shape)],
        out_specs=[pl.BlockSpec((tm, SSM_WIDTH), lambda i: (i, 0)),
                   pl.BlockSpec((None, 1, sw), lambda i: (i // tiles_per_seq, 0, 0))],
        scratch_shapes=[pltpu.VMEM((1, sw), F32), pltpu.VMEM((tm // tc, sw), F32),
                        pltpu.VMEM((tm // tc, sw), F32),
                        pltpu.VMEM((tm, hw), F32), pltpu.VMEM((tm, hw), F32)],
        compiler_params=_cparams(("arbitrary",)),
        name="s5_prompt",
    )(u, u, kb, pb, qb, adec)
    s0 = jnp.concatenate([s0_re.reshape(n_s, g * p), s0_im.reshape(n_s, g * p)], axis=1)
    us = u[tp:]
    ys, fss = pl.pallas_call(
        functools.partial(_s5_sample_kernel, tc=dec, k0=tc - dec),
        out_shape=[jax.ShapeDtypeStruct((ts, SSM_WIDTH), F32), jax.ShapeDtypeStruct((n_s, sw), F32)],
        grid=(1,),
        in_specs=[pl.BlockSpec((ts, hw), lambda i: (0, 0)), pl.BlockSpec((ts, hw), lambda i: (0, 1)),
                  _const_spec((n_s, sw)),
                  _const_spec(kb.shape), _const_spec(pb.shape), _const_spec(qb.shape),
                  _const_spec(adec.shape)],
        out_specs=[_const_spec((ts, SSM_WIDTH)), _const_spec((n_s, sw))],
        scratch_shapes=[pltpu.VMEM((ts, hw), F32), pltpu.VMEM((ts, hw), F32)],
        compiler_params=_cparams(("arbitrary",)),
        name="s5_sample",
    )(us, us, s0, kb, pb, qb, adec)
    half = g * p
    fsp = fsp.reshape(n_p, sw)
    st = lambda a, n: a.reshape(n, g, p)
    return (yp, ys, st(fsp[:, :half], n_p), st(fsp[:, half:], n_p),
            st(fss[:, :half], n_s), st(fss[:, half:], n_s))


def _lambda(lp_ref, lam_init):
    lp = lp_ref[...]
    s1 = jnp.sum(lp[0:1, :] * lp[1:2, :], axis=-1, keepdims=True)
    s2 = jnp.sum(lp[2:3, :] * lp[3:4, :], axis=-1, keepdims=True)
    return jnp.exp(s1) - jnp.exp(s2) + lam_init


def _attn_kernel(qi_ref, ki_ref, q_ref, k_ref, v_ref, lp_ref, hg_ref, *rest, tile, lam_init, cast_blocks):
    n_cast = len(cast_blocks)
    cast_in = rest[:n_cast]
    o_ref = rest[n_cast]
    cast_out = rest[n_cast + 1:2 * n_cast + 1]
    m_scr, acc_scr = rest[2 * n_cast + 1:]
    t = pl.program_id(1)
    qi = qi_ref[t]
    ki = ki_ref[t]
    vd = ATT_V_DIM

    step = pl.program_id(0) * pl.num_programs(1) + t
    for src, dst, n_blocks in zip(cast_in, cast_out, cast_blocks):
        @pl.when(step < n_blocks)
        def _(src=src, dst=dst):
            dst[...] = src[...].astype(dst.dtype)

    @pl.when(ki == 0)
    def _():
        m_scr[...] = jnp.full(m_scr.shape, NEG_INF, F32)
        acc_scr[...] = jnp.zeros(acc_scr.shape, F32)

    def accumulate(masked):
        lane = lax.broadcasted_iota(jnp.int32, (tile, vd), 1)
        ones = jnp.ones((tile, vd), BF16)
        if masked:
            mask = (lax.broadcasted_iota(jnp.int32, (tile, tile), 1)
                    <= lax.broadcasted_iota(jnp.int32, (tile, tile), 0))
        nt = (((1,), (1,)), ((), ()))
        for h in range(ATT_HEADS):
            cols = slice(h * vd, (h + 1) * vd)
            q = q_ref[:, cols]
            k = k_ref[:, cols]
            v1 = jnp.concatenate([v_ref[:, cols], ones], axis=1)
            zero = jnp.zeros_like(q)
            for c in range(2):
                qm = jnp.where((lane >= ATT_HEAD_DIM) == bool(c), q, zero)
                s = lax.dot_general(qm, k, nt, preferred_element_type=F32)
                if masked:
                    s = jnp.where(mask, s, NEG_INF)
                idx = 2 * h + c
                m_old = m_scr[idx]
                m_row = jnp.max(s, axis=-1, keepdims=True)
                m_new = jnp.maximum(m_old, jnp.broadcast_to(m_row, m_old.shape))
                alpha = jnp.exp(m_old - m_new)
                p = jnp.exp(s - jnp.concatenate([m_new] * (tile // vd), axis=1)).astype(BF16)
                acc_scr[idx] = jnp.concatenate([alpha, alpha], axis=1) * acc_scr[idx] + _bdot(p, v1)
                m_scr[idx] = m_new

    @pl.when(ki < qi)
    def _():
        accumulate(False)

    @pl.when(ki == qi)
    def _():
        accumulate(True)
        lam = _lambda(lp_ref, lam_init)
        hg = hg_ref[...]
        for h in range(ATT_HEADS):
            a1 = acc_scr[2 * h]
            a2 = acc_scr[2 * h + 1]
            o = a1[:, :vd] / a1[:, vd:] - lam * (a2[:, :vd] / a2[:, vd:])
            o_ref[:, h * vd:(h + 1) * vd] = (_rms_rows(o, hg) * (1.0 - lam_init)).astype(o_ref.dtype)


def _cast_block_rows(rows, n_steps):
    for br in range(BF16_SUBLANES, rows + 1, BF16_SUBLANES):
        if rows % br == 0 and rows // br <= n_steps:
            return br
    raise ValueError((rows, n_steps))


def _attn_prompt(q, k, v, lp, hg, n_p, seq, lam_init, casts=()):
    tile = min(ATT_TILE, seq)
    assert seq % tile == 0 and tile % ATT_V_DIM == 0
    nq = seq // tile
    pairs = [(i, j) for i in range(nq) for j in range(i + 1)]
    n_steps = n_p * len(pairs)
    qi_tab = jnp.asarray([a for a, _ in pairs], jnp.int32)
    ki_tab = jnp.asarray([b for _, b in pairs], jnp.int32)
    qspec = pl.BlockSpec((tile, ATT_WIDTH), lambda b, t, qi, ki: (b * nq + qi[t], 0))
    kspec = pl.BlockSpec((tile, ATT_WIDTH), lambda b, t, qi, ki: (b * nq + ki[t], 0))
    cast_specs, cast_blocks = [], []
    for w in casts:
        br = _cast_block_rows(w.shape[0], n_steps)
        nblk = w.shape[0] // br
        cast_blocks.append(nblk)
        cast_specs.append(pl.BlockSpec(
            (br, w.shape[1]),
            lambda b, t, qi, ki, nblk=nblk: (jnp.minimum(b * len(pairs) + t, nblk - 1), 0)))
    grid_spec = pltpu.PrefetchScalarGridSpec(
        num_scalar_prefetch=2,
        grid=(n_p, len(pairs)),
        in_specs=[qspec, kspec, kspec,
                  pl.BlockSpec((4, ATT_HEAD_DIM), lambda b, t, qi, ki: (0, 0)),
                  pl.BlockSpec((1, ATT_V_DIM), lambda b, t, qi, ki: (0, 0))] + cast_specs,
        out_specs=[qspec] + cast_specs,
        scratch_shapes=[pltpu.VMEM((2 * ATT_HEADS, tile, ATT_V_DIM), F32),
                        pltpu.VMEM((2 * ATT_HEADS, tile, 2 * ATT_V_DIM), F32)],
    )
    return pl.pallas_call(
        functools.partial(_attn_kernel, tile=tile, lam_init=lam_init, cast_blocks=tuple(cast_blocks)),
        out_shape=[jax.ShapeDtypeStruct((n_p * seq, ATT_WIDTH), BF16)]
        + [jax.ShapeDtypeStruct(w.shape, BF16) for w in casts],
        grid_spec=grid_spec,
        compiler_params=_cparams(("arbitrary", "arbitrary")),
        name="attn_prompt",
    )(qi_tab, ki_tab, q, k, v, lp, hg, *casts)


def _attn_sample_kernel(pt_ref, q_ref, kn_ref, vn_ref, lp_ref, hg_ref, *rest, n_pages, dec, lam_init):
    del pt_ref
    kp = rest[:n_pages]
    vp = rest[n_pages:2 * n_pages]
    o_ref = rest[2 * n_pages]
    page = kp[0].shape[1]
    grp = 2 * dec
    n_rows = ATT_HEADS * grp
    q = q_ref[...].astype(F32)
    qt = jnp.concatenate([q] * (ATT_HEADS * 2), axis=0)
    r = lax.broadcasted_iota(jnp.int32, (n_rows, QK_WIDTH), 0)
    c = lax.broadcasted_iota(jnp.int32, (n_rows, QK_WIDTH), 1)
    qb = jnp.where(r // dec == c // ATT_HEAD_DIM, qt, 0.0).astype(BF16)
    nt = (((1,), (1,)), ((), ()))
    s_pages = [_bdot(qb, kp[j][...].astype(BF16)) for j in range(n_pages)]
    s_new = lax.dot_general(qb, kn_ref[...], nt, preferred_element_type=F32)
    rn = lax.broadcasted_iota(jnp.int32, (n_rows, dec), 0) % dec
    cn = lax.broadcasted_iota(jnp.int32, (n_rows, dec), 1)
    s_new = jnp.where(cn <= rn, s_new, NEG_INF)
    m = jnp.max(s_new, axis=-1, keepdims=True)
    for s in s_pages:
        m = jnp.maximum(m, jnp.max(s, axis=-1, keepdims=True))
    p_new = jnp.exp(s_new - m)
    l = jnp.sum(p_new, axis=-1, keepdims=True)
    acc_new = _bdot(p_new.astype(BF16), vn_ref[...])
    ps = []
    for j in range(n_pages):
        p = jnp.exp(s_pages[j] - m)
        l = l + jnp.sum(p, axis=-1, keepdims=True)
        ps.append(p)
    lam = _lambda(lp_ref, lam_init)
    hg = hg_ref[...]
    for h in range(ATT_HEADS):
        rows = slice(h * grp, (h + 1) * grp)
        cols = slice(h * ATT_V_DIM, (h + 1) * ATT_V_DIM)
        acc = acc_new[rows, cols]
        for j in range(n_pages):
            vh = vp[j][pl.ds(h, page, stride=ATT_HEADS), :].astype(BF16)
            acc = acc + _bdot(ps[j][rows, :].astype(BF16), vh)
        acc = acc / l[rows, :]
        o = acc[:dec] - lam * acc[dec:]
        o_ref[:, cols] = (_rms_rows(o, hg) * (1.0 - lam_init)).astype(o_ref.dtype)


def _attn_sample(q, kn, vn, cache_kt, cache_vr, layer, page_table, lp, hg, lam_init):
    n_s, dec, _ = q.shape
    n_pages = page_table.shape[1]
    page = cache_kt.shape[3]
    tok = pl.BlockSpec((None, dec, QK_WIDTH), lambda n, pt: (n, 0, 0))
    kpages = [pl.BlockSpec((None, None, QK_WIDTH, page), lambda n, pt, j=j: (layer, pt[n, j], 0, 0))
              for j in range(n_pages)]
    vpages = [pl.BlockSpec((None, None, page * ATT_HEADS, ATT_V_DIM),
                           lambda n, pt, j=j: (layer, pt[n, j], 0, 0)) for j in range(n_pages)]
    grid_spec = pltpu.PrefetchScalarGridSpec(
        num_scalar_prefetch=1,
        grid=(n_s,),
        in_specs=[tok, tok, tok,
                  pl.BlockSpec((4, ATT_HEAD_DIM), lambda n, pt: (0, 0)),
                  pl.BlockSpec((1, ATT_V_DIM), lambda n, pt: (0, 0))] + kpages + vpages,
        out_specs=tok,
    )
    return pl.pallas_call(
        functools.partial(_attn_sample_kernel, n_pages=n_pages, dec=dec, lam_init=lam_init),
        out_shape=jax.ShapeDtypeStruct((n_s, dec, ATT_WIDTH), BF16),
        grid_spec=grid_spec,
        compiler_params=_cparams(("parallel",)),
        name="attn_sample",
    )(page_table, q, kn, vn, lp, hg, *([cache_kt] * n_pages), *([cache_vr] * n_pages))


def _merge_kernel(xp_ref, xs_ref, u_ref, cb_ref, vin_ref, halo_ref, gate_ref,
                  yrp_ref, yrs_ref, ybp_ref, ybs_ref, vm1s_ref, vm2s_ref,
                  d_ref, wglu_ref, bglu_ref, cw_ref, wssm_ref, watt_ref, wconv_ref, wout_ref,
                  op_ref, os_ref, *, n_prompt_tiles, tiles_per_seq):
    i = pl.program_id(0)
    is_prompt = i < n_prompt_tiles
    yraw = jnp.where(is_prompt, yrp_ref[...], yrs_ref[...])
    yb = jnp.where(is_prompt, ybp_ref[...], ybs_ref[...])
    y = jax.nn.gelu(yraw + d_ref[...] * u_ref[...])
    ya = y * jax.nn.sigmoid(_bdot(y.astype(BF16), wglu_ref[...]) + bglu_ref[...])

    vin = vin_ref[...]
    row = lax.broadcasted_iota(jnp.int32, vin.shape, 0)
    halo = jnp.where(i % tiles_per_seq == 0, 0.0, halo_ref[...])
    h1 = jnp.broadcast_to(halo[SUBLANES - 1:SUBLANES, :], vin.shape)
    h2 = jnp.broadcast_to(halo[SUBLANES - 2:SUBLANES - 1, :], vin.shape)
    vm1 = jnp.where(row == 0, h1, pltpu.roll(vin, 1, axis=0))
    vm2 = jnp.where(row == 0, h2, jnp.where(row == 1, h1, pltpu.roll(vin, 2, axis=0)))
    vm1 = jnp.where(is_prompt, vm1, vm1s_ref[...])
    vm2 = jnp.where(is_prompt, vm2, vm2s_ref[...])
    conv = vm2 * cw_ref[0:1, :] + vm1 * cw_ref[1:2, :] + vin * cw_ref[2:3, :]
    yc = cb_ref[...] * conv
    merged = (gate_ref[:, 0:D_MODEL] * _bdot(ya.astype(BF16), wssm_ref[...])
              + gate_ref[:, D_MODEL:2 * D_MODEL] * _bdot(yb, watt_ref[...])
              + gate_ref[:, 2 * D_MODEL:3 * D_MODEL] * _bdot(yc.astype(BF16), wconv_ref[...]))
    x_new = _read_group(xp_ref, xs_ref, is_prompt) + _bdot(merged.astype(BF16), wout_ref[...])
    _write_group(op_ref, os_ref, is_prompt, x_new)


def _merge(xp, xs, u, cb, vin, gates, yr_p, yr_s, yb_p, yb_s, vm1_s, vm2_s,
           d, wglu, bglu, cw, wssm, watt, wconv, wout, tiles_per_seq):
    tm = ROW_TILE
    tp, ts = xp.shape[0], xs.shape[0]
    npt = tp // tm
    row = lambda w: pl.BlockSpec((tm, w), lambda i: (i, 0))
    prow = lambda w: _group_specs(tm, w, npt)[0]
    srow = lambda w: _group_specs(tm, w, npt)[1]
    halo = pl.BlockSpec((SUBLANES, CONV_WIDTH), lambda i: (jnp.maximum(i * (tm // SUBLANES) - 1, 0), 0))
    return pl.pallas_call(
        functools.partial(_merge_kernel, n_prompt_tiles=npt, tiles_per_seq=tiles_per_seq),
        out_shape=[jax.ShapeDtypeStruct((tp, D_MODEL), F32), jax.ShapeDtypeStruct((ts, D_MODEL), F32)],
        grid=((tp + ts) // tm,),
        in_specs=[prow(D_MODEL), srow(D_MODEL), row(SSM_WIDTH), row(CONV_WIDTH), row(CONV_WIDTH), halo,
                  row(N_BRANCH * D_MODEL),
                  prow(SSM_WIDTH), srow(SSM_WIDTH), prow(ATT_WIDTH), srow(ATT_WIDTH),
                  srow(CONV_WIDTH), srow(CONV_WIDTH),
                  _const_spec((1, SSM_WIDTH)), _const_spec((SSM_WIDTH, SSM_WIDTH)),
                  _const_spec((1, SSM_WIDTH)), _const_spec((CONV_K, CONV_WIDTH)),
                  _const_spec((SSM_WIDTH, D_MODEL)), _const_spec((ATT_WIDTH, D_MODEL)),
                  _const_spec((CONV_WIDTH, D_MODEL)), _const_spec((D_MODEL, D_MODEL))],
        out_specs=[prow(D_MODEL), srow(D_MODEL)],
        compiler_params=_cparams(("arbitrary",)),
        name="merge",
    )(xp, xs, u, cb, vin, vin, gates, yr_p, yr_s, yb_p, yb_s, vm1_s, vm2_s,
      d, wglu, bglu, cw, wssm, watt, wconv, wout)


def _swiglu(h, w1, w3, w2):
    a = _bdot(h, w1)
    b = _bdot(h, w3)
    return _bdot((jax.nn.silu(a) * b).astype(BF16), w2)


def _ffn_kernel(xp_ref, xs_ref, g_ref, w1_ref, w3_ref, w2_ref, op_ref, os_ref, *, n_prompt_tiles):
    is_prompt = pl.program_id(0) < n_prompt_tiles
    x = _read_group(xp_ref, xs_ref, is_prompt)
    h = _rms_rows(x, g_ref[...]).astype(BF16)
    _write_group(op_ref, os_ref, is_prompt, x + _swiglu(h, w1_ref[...], w3_ref[...], w2_ref[...]))


def _ffn(xp, xs, g, w1, w3, w2):
    tm = ROW_TILE
    tp, ts = xp.shape[0], xs.shape[0]
    npt = tp // tm
    d_ff = w1.shape[1]
    rows = _group_specs(tm, D_MODEL, npt)
    return pl.pallas_call(
        functools.partial(_ffn_kernel, n_prompt_tiles=npt),
        out_shape=[jax.ShapeDtypeStruct((tp, D_MODEL), F32), jax.ShapeDtypeStruct((ts, D_MODEL), F32)],
        grid=((tp + ts) // tm,),
        in_specs=rows + [_const_spec((1, D_MODEL)), _const_spec((D_MODEL, d_ff)),
                         _const_spec((D_MODEL, d_ff)), _const_spec((d_ff, D_MODEL))],
        out_specs=rows,
        compiler_params=_cparams(("arbitrary",)),
        name="ffn",
    )(xp, xs, g, w1, w3, w2)


def _router_kernel(xp_ref, xs_ref, g_ref, wr_ref, tri_ref, h_ref, meta_ref, gate_ref, cnt_ref, carry,
                   *, n_prompt_tiles):
    @pl.when(pl.program_id(0) == 0)
    def _():
        carry[...] = jnp.zeros(carry.shape, F32)

    x = _read_group(xp_ref, xs_ref, pl.program_id(0) < n_prompt_tiles)
    h = _rms_rows(x, g_ref[...])
    h_ref[...] = h
    logits = jnp.dot(h, wr_ref[...], preferred_element_type=F32, precision=lax.Precision.HIGHEST)
    lane = lax.broadcasted_iota(jnp.int32, logits.shape, 1)
    logits = jnp.where(lane < N_EXPERTS, logits, -jnp.inf)
    big = jnp.int32(logits.shape[1])
    m1 = jnp.max(logits, axis=-1, keepdims=True)
    i1 = jnp.min(jnp.where(logits == m1, lane, big), axis=-1, keepdims=True)
    rest = jnp.where(lane == i1, -jnp.inf, logits)
    m2 = jnp.max(rest, axis=-1, keepdims=True)
    i2 = jnp.min(jnp.where(rest == m2, lane, big), axis=-1, keepdims=True)
    e = jnp.exp(m2 - m1)
    g1 = 1.0 / (1.0 + e)
    g2 = e / (1.0 + e)
    o1 = lane == i1
    o2 = lane == i2
    chosen = jnp.where(o1 | o2, 1.0, 0.0)
    base = _bdot(tri_ref[...], chosen.astype(BF16)) + carry[...]
    r1 = jnp.sum(jnp.where(o1, base, 0.0), axis=-1, keepdims=True).astype(jnp.int32)
    r2 = jnp.sum(jnp.where(o2, base, 0.0), axis=-1, keepdims=True).astype(jnp.int32)
    carry[...] = carry[...] + jnp.sum(chosen, axis=0, keepdims=True)
    cnt_ref[...] = carry[...]
    meta_ref[...] = jnp.where(lane == 0, i1, jnp.where(lane == 1, i2,
                              jnp.where(lane == 2, r1, jnp.where(lane == 3, r2, 0))))
    gate_ref[...] = jnp.where(lane == 0, g1, jnp.where(lane == 1, g2, 0.0))


def _router(xp, xs, g, wr_pad):
    tm = ROW_TILE
    t = xp.shape[0] + xs.shape[0]
    npt = xp.shape[0] // tm
    lanes = wr_pad.shape[1]
    tri = jnp.tri(tm, k=-1, dtype=BF16)
    row = lambda w: pl.BlockSpec((tm, w), lambda i: (i, 0))
    return pl.pallas_call(
        functools.partial(_router_kernel, n_prompt_tiles=npt),
        out_shape=[jax.ShapeDtypeStruct((t, D_MODEL), F32),
                   jax.ShapeDtypeStruct((t, lanes), jnp.int32),
                   jax.ShapeDtypeStruct((t, lanes), F32),
                   jax.ShapeDtypeStruct((1, lanes), F32)],
        grid=(t // tm,),
        in_specs=_group_specs(tm, D_MODEL, npt) + [_const_spec((1, D_MODEL)), _const_spec((D_MODEL, lanes)),
                                                   _const_spec((tm, tm))],
        out_specs=[row(D_MODEL), row(lanes), row(lanes), _const_spec((1, lanes))],
        scratch_shapes=[pltpu.VMEM((1, lanes), F32)],
        compiler_params=_cparams(("arbitrary",)),
        name="router",
    )(xp, xs, g, wr_pad, tri)


def _sc_mesh():
    return plsc.VectorSubcoreMesh(core_axis_name="c", subcore_axis_name="s")


def _sc_worker_base(per_worker):
    return (lax.axis_index("s") * SC_CORES + lax.axis_index("c")) * per_worker


def _sc_scatter_rows(x, idx0, idx1, n_out):
    n, w = x.shape
    per_worker = n // SC_WORKERS
    assert n % (SC_WORKERS * SC_ROWS) == 0

    @functools.partial(
        pl.kernel, mesh=_sc_mesh(), out_type=jax.ShapeDtypeStruct((n_out, w), x.dtype),
        scratch_types=[pltpu.VMEM((SC_ROWS,), jnp.int32), pltpu.VMEM((SC_ROWS,), jnp.int32),
                       pltpu.VMEM((SC_ROWS, w), x.dtype), pltpu.SemaphoreType.DMA])
    def scatter(x_hbm, i0_hbm, i1_hbm, out_hbm, i0_v, i1_v, rows_v, sem):
        start = _sc_worker_base(per_worker)

        @pl.loop(0, per_worker // SC_ROWS)
        def _(c):
            rows = pl.ds(pl.multiple_of(start + c * SC_ROWS, SUBLANES), SC_ROWS)
            pltpu.sync_copy(x_hbm.at[rows], rows_v)
            pltpu.sync_copy(i0_hbm.at[rows], i0_v)
            pltpu.sync_copy(i1_hbm.at[rows], i1_v)
            pltpu.async_copy(rows_v, out_hbm.at[i0_v], sem).wait()
            pltpu.async_copy(rows_v, out_hbm.at[i1_v], sem).wait()

    return scatter(x, idx0, idx1)


def _sc_gather_rows(table, idx):
    n, w = idx.shape[0], table.shape[1]
    per_worker = n // SC_WORKERS
    assert n % (SC_WORKERS * SC_ROWS) == 0

    @functools.partial(
        pl.kernel, mesh=_sc_mesh(), out_type=jax.ShapeDtypeStruct((n, w), table.dtype),
        scratch_types=[pltpu.VMEM((SC_ROWS,), jnp.int32), pltpu.VMEM((SC_ROWS, w), table.dtype),
                       pltpu.SemaphoreType.DMA])
    def gather(table_hbm, idx_hbm, out_hbm, idx_v, rows_v, sem):
        start = _sc_worker_base(per_worker)

        @pl.loop(0, per_worker // SC_ROWS)
        def _(c):
            rows = pl.ds(pl.multiple_of(start + c * SC_ROWS, SUBLANES), SC_ROWS)
            pltpu.sync_copy(idx_hbm.at[rows], idx_v)
            pltpu.async_copy(table_hbm.at[idx_v], rows_v, sem).wait()
            pltpu.sync_copy(rows_v, out_hbm.at[rows])

    return gather(table, idx)


def _moe_kernel(be_ref, nv_ref, x_ref, w1_ref, w3_ref, w2_ref, o_ref):
    del be_ref

    @pl.when(pl.program_id(0) < nv_ref[0])
    def _():
        o_ref[...] = _swiglu(x_ref[...].astype(BF16), w1_ref[...], w3_ref[...], w2_ref[...])


def _moe_blocks(xs, block_expert, n_valid, w1, w3, w2):
    cap = xs.shape[0]
    tb = MOE_TILE
    d_ff = w1.shape[2]
    grid_spec = pltpu.PrefetchScalarGridSpec(
        num_scalar_prefetch=2,
        grid=(cap // tb,),
        in_specs=[pl.BlockSpec((tb, D_MODEL), lambda i, be, nv: (jnp.minimum(i, nv[0] - 1), 0)),
                  pl.BlockSpec((None, D_MODEL, d_ff), lambda i, be, nv: (be[i], 0, 0)),
                  pl.BlockSpec((None, D_MODEL, d_ff), lambda i, be, nv: (be[i], 0, 0)),
                  pl.BlockSpec((None, d_ff, D_MODEL), lambda i, be, nv: (be[i], 0, 0))],
        out_specs=pl.BlockSpec((tb, D_MODEL), lambda i, be, nv: (i, 0)),
    )
    return pl.pallas_call(
        _moe_kernel,
        out_shape=jax.ShapeDtypeStruct((cap, D_MODEL), F32),
        grid_spec=grid_spec,
        compiler_params=_cparams(("arbitrary",)),
        name="moe",
    )(block_expert, n_valid, xs, w1, w3, w2)


def _combine_kernel(xp_ref, xs_ref, a0_ref, a1_ref, gate_ref, op_ref, os_ref, *, n_prompt_tiles):
    is_prompt = pl.program_id(0) < n_prompt_tiles
    gate = gate_ref[...]
    y = (_read_group(xp_ref, xs_ref, is_prompt)
         + gate[:, 0:1] * a0_ref[...] + gate[:, 1:2] * a1_ref[...])
    _write_group(op_ref, os_ref, is_prompt, y)


def _combine(xp, xs, picked, gate):
    tm = ROW_TILE
    tp, ts = xp.shape[0], xs.shape[0]
    nt = (tp + ts) // tm
    rows = _group_specs(tm, D_MODEL, tp // tm)
    return pl.pallas_call(
        functools.partial(_combine_kernel, n_prompt_tiles=tp // tm),
        out_shape=[jax.ShapeDtypeStruct((tp, D_MODEL), F32), jax.ShapeDtypeStruct((ts, D_MODEL), F32)],
        grid=(nt,),
        in_specs=rows + [pl.BlockSpec((tm, D_MODEL), lambda i: (i, 0)),
                         pl.BlockSpec((tm, D_MODEL), lambda i: (i + nt, 0)),
                         pl.BlockSpec((tm, gate.shape[1]), lambda i: (i, 0))],
        out_specs=rows,
        compiler_params=_cparams(("arbitrary",)),
        name="moe_combine",
    )(xp, xs, picked, picked, gate)


def _moe(xp, xs, g, wr, w1, w3, w2):
    t = xp.shape[0] + xs.shape[0]
    tb = MOE_TILE
    lanes = 128
    wr_pad = jnp.pad(wr, ((0, 0), (0, lanes - N_EXPERTS)))
    h, meta, gate, cnt = _router(xp, xs, g, wr_pad)
    counts = cnt[0, :N_EXPERTS].astype(jnp.int32)
    padded = (counts + tb - 1) // tb * tb
    pad_ends = jnp.cumsum(padded)
    pad_starts = pad_ends - padded
    experts = jnp.arange(N_EXPERTS, dtype=jnp.int32)
    slot = lambda e, r: jnp.sum(jnp.where(e[:, None] == experts, pad_starts, 0), axis=1) + r
    dest0 = slot(meta[:, 0], meta[:, 2])
    dest1 = slot(meta[:, 1], meta[:, 3])
    nb = -(-(t * TOP_K + N_EXPERTS * (tb - 1)) // tb)
    block_expert = jnp.minimum(
        jnp.sum(pad_ends[None, :] <= (jnp.arange(nb, dtype=jnp.int32) * tb)[:, None], axis=1),
        N_EXPERTS - 1).astype(jnp.int32)
    n_valid = (pad_ends[-1:] // tb).astype(jnp.int32)
    slots = _sc_scatter_rows(h, dest0, dest1, nb * tb)
    out = _moe_blocks(slots, block_expert, n_valid, w1, w3, w2)
    picked = _sc_gather_rows(out, jnp.concatenate([dest0, dest1]))
    return _combine(xp, xs, picked, gate)


def kernel(x_prompt, x_sample, cache_k, cache_v, page_table, state_ssm_re, state_ssm_im, state_conv,
           norm_mix_g, norm_ffn_g, w_in, ssm_a_re, ssm_a_im, ssm_log_dt, ssm_b_re, ssm_b_im,
           ssm_c_re, ssm_c_im, ssm_d, ssm_w_glu, ssm_b_glu, q_norm_g, k_norm_g,
           lambda_q1, lambda_k1, lambda_q2, lambda_k2, head_norm_g, conv_w,
           w_br_ssm, w_br_att, w_br_conv, w_out, ffn_w1, ffn_w3, ffn_w2,
           router_w, moe_w1, moe_w3, moe_w2):
    n_p, seq, _ = x_prompt.shape
    n_s, dec, _ = x_sample.shape
    depth = w_in.shape[0]
    tp = n_p * seq
    ts = n_s * dec
    assert seq % ROW_TILE == 0 and ts % ROW_TILE == 0 and seq >= CONV_K - 1
    pool, page = cache_k.shape[1], cache_k.shape[2]
    cache_kt = cache_k.reshape(depth, pool, page, QK_WIDTH).transpose(0, 1, 3, 2)
    cache_vr = cache_v.reshape(depth, pool, page * ATT_HEADS, ATT_V_DIM)
    seg = jnp.kron(jnp.eye(QK_WIDTH // ATT_HEAD_DIM, dtype=F32),
                   jnp.full((ATT_HEAD_DIM, ATT_HEAD_DIM), 1.0 / ATT_HEAD_DIM, F32)).astype(BF16)
    n_rep = QK_WIDTH // ATT_HEAD_DIM

    xp = x_prompt.reshape(tp, D_MODEL)
    xs = x_sample.reshape(ts, D_MODEL)
    all_tables = _s5_tables(ssm_a_re, ssm_a_im, ssm_log_dt, ssm_b_re, ssm_b_im, ssm_c_re, ssm_c_im, dec)
    kv_prompt = None
    srp, sip, cvp = [], [], []
    ks, vs, srs, sis, cvs = [], [], [], [], []
    for l in range(depth):
        lam_init = 0.8 - 0.6 * math.exp(-0.3 * l)
        u, q, kb, vb, cb, vin, gates, kt, vr, k_s, v_s = _inproj(
            xp, xs, norm_mix_g[l][None], w_in[l].astype(BF16),
            jnp.tile(q_norm_g[l], n_rep)[None], jnp.tile(k_norm_g[l], n_rep)[None], seg,
            l, depth, n_p, seq, kv_prompt)
        kv_prompt = (kt, vr)

        tables = [tab[l] for tab in all_tables]
        yr_p, yr_s, p_re, p_im, s_re, s_im = _s5(u, state_ssm_re[l], state_ssm_im[l], tables,
                                                  n_p, seq, n_s, dec)

        lp = jnp.stack([lambda_q1[l], lambda_k1[l], lambda_q2[l], lambda_k2[l]])
        hg = head_norm_g[l][None]
        casts = ()
        if l + 1 < depth and (l + 1) % 2 == 1:
            e = (l + 1) // 2
            casts = (moe_w1[e].reshape(-1, moe_w1.shape[-1]), moe_w3[e].reshape(-1, moe_w3.shape[-1]),
                     moe_w2[e].reshape(-1, moe_w2.shape[-1]))
        yb_p, *cast_out = _attn_prompt(q, kb, vb, lp, hg, n_p, seq, lam_init, casts)
        if casts:
            moe_bf = [c.reshape(w.shape[1:]) for c, w in zip(cast_out, (moe_w1, moe_w3, moe_w2))]
        yb_s = _attn_sample(q[tp:].reshape(n_s, dec, QK_WIDTH), kb[tp:].reshape(n_s, dec, QK_WIDTH),
                            vb[tp:].reshape(n_s, dec, ATT_WIDTH), cache_kt, cache_vr, l, page_table,
                            lp, hg, lam_init).reshape(ts, ATT_WIDTH)

        ext_s = jnp.concatenate([state_conv[l], vin[tp:].reshape(n_s, dec, CONV_WIDTH)], axis=1)
        vm1_s = ext_s[:, 1:1 + dec].reshape(ts, CONV_WIDTH)
        vm2_s = ext_s[:, 0:dec].reshape(ts, CONV_WIDTH)
        xp, xs = _merge(xp, xs, u, cb, vin, gates, yr_p, yr_s, yb_p, yb_s, vm1_s, vm2_s,
                        ssm_d[l][None], ssm_w_glu[l].astype(BF16), ssm_b_glu[l][None], conv_w[l],
                        w_br_ssm[l].astype(BF16), w_br_att[l].astype(BF16), w_br_conv[l].astype(BF16),
                        w_out[l].astype(BF16), seq // ROW_TILE)

        i = l // 2
        if l % 2 == 0:
            xp, xs = _ffn(xp, xs, norm_ffn_g[l][None], ffn_w1[i].astype(BF16), ffn_w3[i].astype(BF16),
                          ffn_w2[i].astype(BF16))
        else:
            xp, xs = _moe(xp, xs, norm_ffn_g[l][None], router_w[i], *moe_bf)

        srp.append(p_re); sip.append(p_im)
        cvp.append(vin[:tp].reshape(n_p, seq, CONV_WIDTH)[:, seq - (CONV_K - 1):])
        ks.append(k_s.reshape(n_s, dec, ATT_HEADS, 2, ATT_HEAD_DIM))
        vs.append(v_s.reshape(n_s, dec, ATT_HEADS, ATT_V_DIM))
        srs.append(s_re); sis.append(s_im); cvs.append(ext_s[:, dec:])

    kt, vr = kv_prompt
    k_prompt = kt.reshape(depth, n_p, ATT_HEADS, 2, ATT_HEAD_DIM, seq).transpose(0, 1, 5, 2, 3, 4)
    v_prompt = vr.reshape(depth, n_p, seq, ATT_HEADS, ATT_V_DIM)
    return (xp.reshape(n_p, seq, D_MODEL), xs.reshape(n_s, dec, D_MODEL),
            k_prompt, v_prompt, jnp.stack(srp), jnp.stack(sip), jnp.stack(cvp),
            jnp.stack(ks), jnp.stack(vs), jnp.stack(srs), jnp.stack(sis), jnp.stack(cvs))
```

```python
import functools
import math

import jax
import jax.numpy as jnp
from jax import lax
from jax.experimental import pallas as pl
from jax.experimental.pallas import tpu as pltpu
from jax.experimental.pallas import tpu_sc as plsc

F32 = jnp.float32
BF16 = jnp.bfloat16

D_MODEL = 1024
SSM_WIDTH = 256
SSM_GROUP = 16
SSM_GROUPS = SSM_WIDTH // SSM_GROUP
SSM_STATE = 64
ATT_HEADS = 4
ATT_HEAD_DIM = 64
ATT_V_DIM = 2 * ATT_HEAD_DIM
QK_WIDTH = ATT_HEADS * 2 * ATT_HEAD_DIM
ATT_WIDTH = ATT_HEADS * ATT_V_DIM
CONV_WIDTH = 256
CONV_K = 3
N_BRANCH = 3
N_EXPERTS = 8
TOP_K = 2
EPS = 1e-6
NEG_INF = -1e30

C_U = 0
C_Q = C_U + SSM_WIDTH
C_K = C_Q + QK_WIDTH
C_V = C_K + QK_WIDTH
C_CB = C_V + ATT_WIDTH
C_CC = C_CB + CONV_WIDTH
C_CH = C_CC + CONV_WIDTH
C_G = C_CH + CONV_WIDTH
IN_COLS = C_G + N_BRANCH * D_MODEL

SSM_CHUNK = 8
SUBLANES = 8
BF16_SUBLANES = 16
ROW_TILE = 512
ATT_TILE = 512
MOE_TILE = 256
SC_CORES = 2
SC_WORKERS = SC_CORES * 16
SC_ROWS = 48
VMEM_LIMIT = 56 * 1024 * 1024


def _cparams(sem):
    return pltpu.CompilerParams(dimension_semantics=sem, vmem_limit_bytes=VMEM_LIMIT)


def _const_spec(shape):
    nd = len(shape)
    return pl.BlockSpec(shape, lambda *_: (0,) * nd)


def _bdot(a, b):
    return jnp.dot(a, b, preferred_element_type=F32)


def _rms_rows(x, g):
    ms = jnp.mean(x * x, axis=-1, keepdims=True)
    return x * lax.rsqrt(ms + EPS) * g


def _group_specs(tm, w, n_prompt_tiles):
    return [pl.BlockSpec((tm, w), lambda i: (jnp.minimum(i, n_prompt_tiles - 1), 0)),
            pl.BlockSpec((tm, w), lambda i: (jnp.maximum(i - n_prompt_tiles, 0), 0))]


def _read_group(p_ref, s_ref, is_prompt):
    return jnp.where(is_prompt, p_ref[...], s_ref[...])


def _write_group(p_ref, s_ref, is_prompt, val):
    @pl.when(is_prompt)
    def _():
        p_ref[...] = val

    @pl.when(jnp.logical_not(is_prompt))
    def _():
        s_ref[...] = val


def _segment_rms(z, g, seg):
    ms = _bdot((z * z).astype(BF16), seg)
    return z * lax.rsqrt(ms + EPS) * g


def _inproj_kernel(xp_ref, xs_ref, g_ref, w_ref, qg_ref, kg_ref, seg_ref, *rest,
                   n_prompt_tiles, n_prev, layer):
    (q_ref, kb_ref, vb_ref, cb_ref, vin_ref, gate_ref,
     kt_ref, vr_ref, ks_ref, vs_ref, up_ref, us_ref) = rest[n_prev:]
    tm = q_ref.shape[0]
    is_prompt = pl.program_id(0) < n_prompt_tiles
    h = _rms_rows(_read_group(xp_ref, xs_ref, is_prompt), g_ref[...]).astype(BF16)

    def proj(a, b):
        return _bdot(h, w_ref[:, a:b])

    seg = seg_ref[...]
    _write_group(up_ref, us_ref, is_prompt, proj(C_U, C_Q))
    qn = _segment_rms(proj(C_Q, C_K), qg_ref[...], seg)
    q_ref[...] = (qn * (ATT_HEAD_DIM ** -0.5)).astype(BF16)
    kn = _segment_rms(proj(C_K, C_V), kg_ref[...], seg)
    kb_ref[...] = kn.astype(BF16)
    v = proj(C_V, C_CB)
    vb_ref[...] = v.astype(BF16)
    cb_ref[...] = proj(C_CB, C_CC)
    vin_ref[...] = proj(C_CC, C_CH) * proj(C_CH, C_G)
    for j in range(N_BRANCH):
        a = C_G + j * D_MODEL
        gate_ref[:, j * D_MODEL:(j + 1) * D_MODEL] = jax.nn.sigmoid(proj(a, a + D_MODEL)).astype(gate_ref.dtype)

    @pl.when(is_prompt)
    def _():
        if n_prev:
            kt_l, vr_l = kt_ref, vr_ref
        else:
            kt_l, vr_l = kt_ref.at[layer], vr_ref.at[layer]
            for other in range(kt_ref.shape[0]):
                if other != layer:
                    kt_ref[other] = jnp.zeros(kt_ref.shape[1:], F32)
                    vr_ref[other] = jnp.zeros(vr_ref.shape[1:], F32)
        kt_l[...] = kn.T
        for hd in range(ATT_HEADS):
            vr_l[pl.ds(hd, tm, stride=ATT_HEADS), :] = v[:, hd * ATT_V_DIM:(hd + 1) * ATT_V_DIM]

    @pl.when(jnp.logical_not(is_prompt))
    def _():
        ks_ref[...] = kn
        vs_ref[...] = v


def _inproj(xp, xs, g, w_bf, qg, kg, seg, layer, depth, n_p, seq, prev):
    tm = ROW_TILE
    tp, ts = xp.shape[0], xs.shape[0]
    t = tp + ts
    npt = tp // tm
    tps = seq // tm
    row = lambda w: pl.BlockSpec((tm, w), lambda i: (i, 0))
    pc = lambda i: jnp.minimum(i, npt - 1)
    outs = [(QK_WIDTH, BF16), (QK_WIDTH, BF16), (ATT_WIDTH, BF16),
            (CONV_WIDTH, F32), (CONV_WIDTH, F32), (N_BRANCH * D_MODEL, BF16)]
    out_shape = ([jax.ShapeDtypeStruct((t, w), d) for w, d in outs]
                 + [jax.ShapeDtypeStruct((depth, n_p, QK_WIDTH, seq), F32),
                    jax.ShapeDtypeStruct((depth, tp * ATT_HEADS, ATT_V_DIM), F32),
                    jax.ShapeDtypeStruct((ts, QK_WIDTH), F32), jax.ShapeDtypeStruct((ts, ATT_WIDTH), F32),
                    jax.ShapeDtypeStruct((tp, SSM_WIDTH), F32), jax.ShapeDtypeStruct((ts, SSM_WIDTH), F32)])
    srow = _group_specs(tm, QK_WIDTH, npt)[1]
    prev = () if prev is None else tuple(prev)
    lead, at = (None, layer) if prev else (depth, 0)
    out_specs = ([row(w) for w, _ in outs]
                 + [pl.BlockSpec((lead, None, QK_WIDTH, tm), lambda i: (at, pc(i) // tps, 0, pc(i) % tps)),
                    pl.BlockSpec((lead, tm * ATT_HEADS, ATT_V_DIM), lambda i: (at, pc(i), 0)),
                    srow, srow] + _group_specs(tm, SSM_WIDTH, npt))
    n_in = 7
    return pl.pallas_call(
        functools.partial(_inproj_kernel, n_prompt_tiles=npt, n_prev=len(prev), layer=layer),
        out_shape=out_shape,
        grid=(t // tm,),
        in_specs=(_group_specs(tm, D_MODEL, npt)
                  + [_const_spec((1, D_MODEL)), _const_spec((D_MODEL, IN_COLS)),
                     _const_spec((1, QK_WIDTH)), _const_spec((1, QK_WIDTH)),
                     _const_spec((QK_WIDTH, QK_WIDTH))]
                  + [pl.BlockSpec(memory_space=pl.ANY)] * len(prev)),
        out_specs=out_specs,
        input_output_aliases={n_in + j: len(outs) + j for j in range(len(prev))},
        compiler_params=_cparams(("arbitrary",)),
        name="inproj",
    )(xp, xs, g, w_bf, qg, kg, seg, *prev)


def _s5_intra(u, kb_ref, tc):
    rowmod = lax.broadcasted_iota(jnp.int32, u.shape, 0) % tc
    y = _bdot(u.astype(BF16), kb_ref[0])
    for m in range(1, tc):
        um = jnp.where(rowmod >= m, pltpu.roll(u, m, axis=0), 0.0)
        y = y + _bdot(um.astype(BF16), kb_ref[m])
    return y


def _s5_inject(u_refs, pb_ref, k0, n_chunks, tc):
    w = None
    for k in range(tc):
        rows = pl.ds(k, n_chunks, stride=tc)
        uk = jnp.concatenate([r[rows, :] for r in u_refs], axis=1).astype(BF16)
        d = _bdot(uk, pb_ref[k0 + k])
        w = d if w is None else w + d
    return w


def _s5_readout(y_ref, y_scrs, y, s_in, qb_ref, n_chunks, tc):
    lanes = y_scrs[0].shape[1]
    for h, scr in enumerate(y_scrs):
        scr[...] = y[:, h * lanes:(h + 1) * lanes]
    sb = s_in.astype(BF16)
    for k in range(tc):
        rows = pl.ds(k, n_chunks, stride=tc)
        yk = _bdot(sb, qb_ref[k])
        for h, scr in enumerate(y_scrs):
            scr[rows, :] = scr[rows, :] + yk[:, h * lanes:(h + 1) * lanes]
    y_ref[...] = jnp.concatenate([scr[...] for scr in y_scrs], axis=1)


def _cmul_add(a_re, a_im, s, w, half):
    s_re, s_im = s[:, :half], s[:, half:]
    return jnp.concatenate([a_re * s_re - a_im * s_im, a_re * s_im + a_im * s_re], axis=1) + w


def _s5_prompt_kernel(ua_ref, ub_ref, kb_ref, pb_ref, qb_ref, a_ref, y_ref, fs_ref,
                      carry, w_fold, s_fold, ya_scr, yb_scr, *, tc):
    n_b, tm, hw = ua_ref.shape
    n_chunks = tm // tc
    rows = n_b * n_chunks
    fold = carry.shape[0] // n_b
    hf = fold // 2

    @pl.when(pl.program_id(0) == 0)
    def _():
        carry[...] = jnp.zeros(carry.shape, F32)

    u = jnp.concatenate([ua_ref[...].reshape(n_b * tm, hw), ub_ref[...].reshape(n_b * tm, hw)], axis=1)
    y = _s5_intra(u, kb_ref, tc)
    w = None
    for k in range(tc):
        at_k = pl.ds(k, n_chunks, stride=tc)
        uk = jnp.concatenate([jnp.concatenate([ua_ref[b, at_k, :], ub_ref[b, at_k, :]], axis=1)
                              for b in range(n_b)], axis=0).astype(BF16)
        d = _bdot(uk, pb_ref[k])
        w = d if w is None else w + d
    for r in range(fold):
        w_fold[pl.ds(r, rows, stride=fold), :] = w[:, r * hw:(r + 1) * hw]
    a_re = a_ref[0]
    a_im = a_ref[1]
    for b in range(n_b):
        s = carry[b * fold:(b + 1) * fold, :]
        for c in range(n_chunks):
            at = slice((b * n_chunks + c) * fold, (b * n_chunks + c + 1) * fold)
            s_fold[at, :] = s
            s_re, s_im = s[:hf], s[hf:]
            s = jnp.concatenate([a_re * s_re - a_im * s_im, a_re * s_im + a_im * s_re], axis=0) + w_fold[at, :]
        carry[b * fold:(b + 1) * fold, :] = s
    fs_ref[...] = carry[...]
    s_in = jnp.concatenate([s_fold[pl.ds(r, rows, stride=fold), :] for r in range(fold)], axis=1)
    for h, scr in enumerate((ya_scr, yb_scr)):
        scr[...] = y[:, h * hw:(h + 1) * hw]
    sb = s_in.astype(BF16)
    for k in range(tc):
        at_k = pl.ds(k, rows, stride=tc)
        yk = _bdot(sb, qb_ref[k])
        for h, scr in enumerate((ya_scr, yb_scr)):
            scr[at_k, :] = scr[at_k, :] + yk[:, h * hw:(h + 1) * hw]
    for b in range(n_b):
        y_ref[b] = jnp.concatenate([ya_scr[b * tm:(b + 1) * tm, :], yb_scr[b * tm:(b + 1) * tm, :]], axis=1)


def _s5_sample_kernel(ua_ref, ub_ref, s0_ref, kb_ref, pb_ref, qb_ref, a_ref, y_ref, fs_ref,
                      ya_scr, yb_scr, *, tc, k0):
    n_chunks = ua_ref.shape[0] // tc
    half = a_ref.shape[1]
    y = _s5_intra(jnp.concatenate([ua_ref[...], ub_ref[...]], axis=1), kb_ref, tc)
    w = _s5_inject((ua_ref, ub_ref), pb_ref, k0, n_chunks, tc)
    s0 = s0_ref[...]
    fs_ref[...] = _cmul_add(a_ref[2:3, :], a_ref[3:4, :], s0, w, half)
    _s5_readout(y_ref, (ya_scr, yb_scr), y, s0, qb_ref, n_chunks, tc)


def _s5_tables_kernel(are_ref, aim_ref, ldt_ref, bre_ref, bim_ref, cre_ref, cim_ref,
                      kb_ref, pb_ref, qb_ref, a_ref, *, tc, dec):
    are = are_ref[...]
    aim = aim_ref[...]
    dt = jnp.exp(ldt_ref[...])

    def power(m):
        mag = jnp.exp(are * dt * m)
        ang = aim * dt * m
        return mag * jnp.cos(ang), mag * jnp.sin(ang)

    ab_re, ab_im = power(1.0)
    den = are * are + aim * aim
    nr = ab_re - 1.0
    cr = (nr * are + ab_im * aim) / den
    ci = (ab_im * are - nr * aim) / den
    gj, gp = bre_ref.shape
    row_g = lax.broadcasted_iota(jnp.int32, (gj, gp), 0) // SSM_GROUP
    col_g = lax.broadcasted_iota(jnp.int32, (gj, gp), 1) // SSM_STATE
    diag = row_g == col_g
    bre = bre_ref[...]
    bim = bim_ref[...]
    bb_re = jnp.where(diag, cr * bre - ci * bim, 0.0)
    bb_im = jnp.where(diag, cr * bim + ci * bre, 0.0)
    cc_re = jnp.where(diag, cre_ref[...], 0.0)
    cc_im = jnp.where(diag, cim_ref[...], 0.0)
    c_blk = jnp.concatenate([cc_re, -cc_im], axis=1).T
    for m in range(tc):
        pr, pi = power(float(m))
        pm = jnp.concatenate([pr * bb_re - pi * bb_im, pr * bb_im + pi * bb_re], axis=1)
        pb_ref[tc - 1 - m] = pm.astype(pb_ref.dtype)
        kb_ref[m] = jnp.dot(pm, c_blk, preferred_element_type=F32,
                            precision=lax.Precision.HIGHEST).astype(kb_ref.dtype)
        qr, qi = power(float(m + 1))
        qm = jnp.concatenate([qr * cc_re - qi * cc_im, -(qr * cc_im + qi * cc_re)], axis=1)
        qb_ref[m] = qm.T.astype(qb_ref.dtype)
    a_ref[...] = jnp.concatenate(list(power(float(tc)) + power(float(dec))), axis=0)


def _s5_tables(a_re, a_im, log_dt, b_re, b_im, c_re, c_im, dec):
    depth = a_re.shape[0]
    tc = SSM_CHUNK
    g, p, j = SSM_GROUPS, SSM_STATE, SSM_GROUP
    gp, gj = g * p, g * j
    rowv = lambda t: t.reshape(depth, 1, gp)
    ldt = jnp.repeat(log_dt, p, axis=1).reshape(depth, 1, gp)
    bt = lambda t: jnp.tile(t.transpose(0, 1, 3, 2).reshape(depth, gj, p), (1, 1, g))
    ct = lambda t: jnp.tile(t.reshape(depth, gj, p), (1, 1, g))
    lay = lambda *shape: pl.BlockSpec((None,) + shape, lambda l: (l,) + (0,) * len(shape))
    return pl.pallas_call(
        functools.partial(_s5_tables_kernel, tc=tc, dec=dec),
        out_shape=[jax.ShapeDtypeStruct((depth, tc, gj, gj), BF16),
                   jax.ShapeDtypeStruct((depth, tc, gj, 2 * gp), BF16),
                   jax.ShapeDtypeStruct((depth, tc, 2 * gp, gj), BF16),
                   jax.ShapeDtypeStruct((depth, 4, gp), F32)],
        grid=(depth,),
        in_specs=[lay(1, gp)] * 3 + [lay(gj, gp)] * 4,
        out_specs=[lay(tc, gj, gj), lay(tc, gj, 2 * gp), lay(tc, 2 * gp, gj), lay(4, gp)],
        compiler_params=_cparams(("arbitrary",)),
        name="s5_tables",
    )(rowv(a_re), rowv(a_im), ldt, bt(b_re), bt(b_im), ct(c_re), ct(c_im))


def _s5(up, us, s0_re, s0_im, tables, n_p, seq, n_s, dec):
    kb, pb, qb, adec = tables
    tc = SSM_CHUNK
    g, p = SSM_GROUPS, SSM_STATE
    sw = 2 * g * p
    tm = ROW_TILE
    tp = n_p * seq
    ts = n_s * dec
    hw = SSM_WIDTH // 2
    fold = sw // hw
    assert dec <= tc and seq % tm == 0 and tm % tc == 0
    up3 = up.reshape(n_p, seq, SSM_WIDTH)
    chunk_rows = n_p * (tm // tc)
    yp, fsp = pl.pallas_call(
        functools.partial(_s5_prompt_kernel, tc=tc),
        out_shape=[jax.ShapeDtypeStruct((n_p, seq, SSM_WIDTH), F32),
                   jax.ShapeDtypeStruct((n_p * fold, hw), F32)],
        grid=(seq // tm,),
        in_specs=[pl.BlockSpec((n_p, tm, hw), lambda i: (0, i, 0)),
                  pl.BlockSpec((n_p, tm, hw), lambda i: (0, i, 1)),
                  _const_spec(kb.shape), _const_spec(pb.shape), _const_spec(qb.shape),
                  _const_spec((4, fold // 2, hw))],
        out_specs=[pl.BlockSpec((n_p, tm, SSM_WIDTH), lambda i: (0, i, 0)),
                   _const_spec((n_p * fold, hw))],
        scratch_shapes=[pltpu.VMEM((n_p * fold, hw), F32),
                        pltpu.VMEM((chunk_rows * fold, hw), F32), pltpu.VMEM((chunk_rows * fold, hw), F32),
                        pltpu.VMEM((n_p * tm, hw), F32), pltpu.VMEM((n_p * tm, hw), F32)],
        compiler_params=_cparams(("arbitrary",)),
        name="s5_prompt",
    )(up3, up3, kb, pb, qb, adec.reshape(4, fold // 2, hw))
    yp = yp.reshape(tp, SSM_WIDTH)
    s0 = jnp.concatenate([s0_re.reshape(n_s, g * p), s0_im.reshape(n_s, g * p)], axis=1)
    ys, fss = pl.pallas_call(
        functools.partial(_s5_sample_kernel, tc=dec, k0=tc - dec),
        out_shape=[jax.ShapeDtypeStruct((ts, SSM_WIDTH), F32), jax.ShapeDtypeStruct((n_s, sw), F32)],
        grid=(1,),
        in_specs=[pl.BlockSpec((ts, hw), lambda i: (0, 0)), pl.BlockSpec((ts, hw), lambda i: (0, 1)),
                  _const_spec((n_s, sw)),
                  _const_spec(kb.shape), _const_spec(pb.shape), _const_spec(qb.shape),
                  _const_spec(adec.shape)],
        out_specs=[_const_spec((ts, SSM_WIDTH)), _const_spec((n_s, sw))],
        scratch_shapes=[pltpu.VMEM((ts, hw), F32), pltpu.VMEM((ts, hw), F32)],
        compiler_params=_cparams(("arbitrary",)),
        name="s5_sample",
    )(us, us, s0, kb, pb, qb, adec)
    half = g * p
    fsp = fsp.reshape(n_p, sw)
    st = lambda a, n: a.reshape(n, g, p)
    return (yp, ys, st(fsp[:, :half], n_p), st(fsp[:, half:], n_p),
            st(fss[:, :half], n_s), st(fss[:, half:], n_s))


def _lambda(lp_ref, lam_init):
    lp = lp_ref[...]
    s1 = jnp.sum(lp[0:1, :] * lp[1:2, :], axis=-1, keepdims=True)
    s2 = jnp.sum(lp[2:3, :] * lp[3:4, :], axis=-1, keepdims=True)
    return jnp.exp(s1) - jnp.exp(s2) + lam_init


def _attn_kernel(qi_ref, ki_ref, q_ref, k_ref, v_ref, lp_ref, hg_ref, *rest, tile, lam_init, cast_blocks):
    n_cast = len(cast_blocks)
    cast_in = rest[:n_cast]
    o_ref = rest[n_cast]
    cast_out = rest[n_cast + 1:2 * n_cast + 1]
    m_scr, acc_scr = rest[2 * n_cast + 1:]
    t = pl.program_id(1)
    qi = qi_ref[t]
    ki = ki_ref[t]
    vd = ATT_V_DIM

    step = pl.program_id(0) * pl.num_programs(1) + t
    for src, dst, n_blocks in zip(cast_in, cast_out, cast_blocks):
        @pl.when(step < n_blocks)
        def _(src=src, dst=dst):
            dst[...] = src[...].astype(dst.dtype)

    @pl.when(ki == 0)
    def _():
        m_scr[...] = jnp.full(m_scr.shape, NEG_INF, F32)
        acc_scr[...] = jnp.zeros(acc_scr.shape, F32)

    def accumulate(masked):
        lane = lax.broadcasted_iota(jnp.int32, (tile, vd), 1)
        ones = jnp.ones((tile, vd), BF16)
        if masked:
            mask = (lax.broadcasted_iota(jnp.int32, (tile, tile), 1)
                    <= lax.broadcasted_iota(jnp.int32, (tile, tile), 0))
        nt = (((1,), (1,)), ((), ()))
        for h in range(ATT_HEADS):
            cols = slice(h * vd, (h + 1) * vd)
            q = q_ref[:, cols]
            k = k_ref[:, cols]
            v1 = jnp.concatenate([v_ref[:, cols], ones], axis=1)
            zero = jnp.zeros_like(q)
            for c in range(2):
                qm = jnp.where((lane >= ATT_HEAD_DIM) == bool(c), q, zero)
                s = lax.dot_general(qm, k, nt, preferred_element_type=F32)
                if masked:
                    s = jnp.where(mask, s, NEG_INF)
                idx = 2 * h + c
                m_old = m_scr[idx]
                m_row = jnp.max(s, axis=-1, keepdims=True)
                m_new = jnp.maximum(m_old, jnp.broadcast_to(m_row, m_old.shape))
                alpha = jnp.exp(m_old - m_new)
                p = jnp.exp(s - jnp.concatenate([m_new] * (tile // vd), axis=1)).astype(BF16)
                acc_scr[idx] = jnp.concatenate([alpha, alpha], axis=1) * acc_scr[idx] + _bdot(p, v1)
                m_scr[idx] = m_new

    @pl.when(ki < qi)
    def _():
        accumulate(False)

    @pl.when(ki == qi)
    def _():
        accumulate(True)
        lam = _lambda(lp_ref, lam_init)
        hg = hg_ref[...]
        for h in range(ATT_HEADS):
            a1 = acc_scr[2 * h]
            a2 = acc_scr[2 * h + 1]
            o = a1[:, :vd] / a1[:, vd:] - lam * (a2[:, :vd] / a2[:, vd:])
            o_ref[:, h * vd:(h + 1) * vd] = (_rms_rows(o, hg) * (1.0 - lam_init)).astype(o_ref.dtype)


def _cast_block_rows(rows, n_steps):
    for br in range(BF16_SUBLANES, rows + 1, BF16_SUBLANES):
        if rows % br == 0 and rows // br <= n_steps:
            return br
    raise ValueError((rows, n_steps))


def _attn_prompt(q, k, v, lp, hg, n_p, seq, lam_init, casts=()):
    tile = min(ATT_TILE, seq)
    assert seq % tile == 0 and tile % ATT_V_DIM == 0
    nq = seq // tile
    pairs = [(i, j) for i in range(nq) for j in range(i + 1)]
    n_steps = n_p * len(pairs)
    qi_tab = jnp.asarray([a for a, _ in pairs], jnp.int32)
    ki_tab = jnp.asarray([b for _, b in pairs], jnp.int32)
    qspec = pl.BlockSpec((tile, ATT_WIDTH), lambda b, t, qi, ki: (b * nq + qi[t], 0))
    kspec = pl.BlockSpec((tile, ATT_WIDTH), lambda b, t, qi, ki: (b * nq + ki[t], 0))
    cast_specs, cast_blocks = [], []
    for w in casts:
        br = _cast_block_rows(w.shape[0], n_steps)
        nblk = w.shape[0] // br
        cast_blocks.append(nblk)
        cast_specs.append(pl.BlockSpec(
            (br, w.shape[1]),
            lambda b, t, qi, ki, nblk=nblk: (jnp.minimum(b * len(pairs) + t, nblk - 1), 0)))
    grid_spec = pltpu.PrefetchScalarGridSpec(
        num_scalar_prefetch=2,
        grid=(n_p, len(pairs)),
        in_specs=[qspec, kspec, kspec,
                  pl.BlockSpec((4, ATT_HEAD_DIM), lambda b, t, qi, ki: (0, 0)),
                  pl.BlockSpec((1, ATT_V_DIM), lambda b, t, qi, ki: (0, 0))] + cast_specs,
        out_specs=[qspec] + cast_specs,
        scratch_shapes=[pltpu.VMEM((2 * ATT_HEADS, tile, ATT_V_DIM), F32),
                        pltpu.VMEM((2 * ATT_HEADS, tile, 2 * ATT_V_DIM), F32)],
    )
    return pl.pallas_call(
        functools.partial(_attn_kernel, tile=tile, lam_init=lam_init, cast_blocks=tuple(cast_blocks)),
        out_shape=[jax.ShapeDtypeStruct((n_p * seq, ATT_WIDTH), BF16)]
        + [jax.ShapeDtypeStruct(w.shape, BF16) for w in casts],
        grid_spec=grid_spec,
        compiler_params=_cparams(("arbitrary", "arbitrary")),
        name="attn_prompt",
    )(qi_tab, ki_tab, q, k, v, lp, hg, *casts)


def _attn_sample_kernel(pt_ref, q_ref, kn_ref, vn_ref, lp_ref, hg_ref, *rest, n_pages, dec, lam_init):
    del pt_ref
    kp = rest[:n_pages]
    vp = rest[n_pages:2 * n_pages]
    o_ref = rest[2 * n_pages]
    page = kp[0].shape[1]
    grp = 2 * dec
    n_rows = ATT_HEADS * grp
    q = q_ref[...].astype(F32)
    qt = jnp.concatenate([q] * (ATT_HEADS * 2), axis=0)
    r = lax.broadcasted_iota(jnp.int32, (n_rows, QK_WIDTH), 0)
    c = lax.broadcasted_iota(jnp.int32, (n_rows, QK_WIDTH), 1)
    qb = jnp.where(r // dec == c // ATT_HEAD_DIM, qt, 0.0).astype(BF16)
    nt = (((1,), (1,)), ((), ()))
    kt_past = jnp.concatenate([kp[j][...].astype(BF16) for j in range(n_pages)], axis=1)
    s_past = _bdot(qb, kt_past)
    s_new = lax.dot_general(qb, kn_ref[...], nt, preferred_element_type=F32)
    rn = lax.broadcasted_iota(jnp.int32, (n_rows, dec), 0) % dec
    cn = lax.broadcasted_iota(jnp.int32, (n_rows, dec), 1)
    s_new = jnp.where(cn <= rn, s_new, NEG_INF)
    m = jnp.maximum(jnp.max(s_new, axis=-1, keepdims=True), jnp.max(s_past, axis=-1, keepdims=True))
    p_new = jnp.exp(s_new - m)
    p_past = jnp.exp(s_past - m)
    l = jnp.sum(p_new, axis=-1, keepdims=True) + jnp.sum(p_past, axis=-1, keepdims=True)
    acc_new = _bdot(p_new.astype(BF16), vn_ref[...])
    lam = _lambda(lp_ref, lam_init)
    hg = hg_ref[...]
    for h in range(ATT_HEADS):
        rows = slice(h * grp, (h + 1) * grp)
        cols = slice(h * ATT_V_DIM, (h + 1) * ATT_V_DIM)
        v_past = jnp.concatenate([vp[j][pl.ds(h, page, stride=ATT_HEADS), :].astype(BF16)
                                  for j in range(n_pages)], axis=0)
        acc = acc_new[rows, cols] + _bdot(p_past[rows, :].astype(BF16), v_past)
        acc = acc / l[rows, :]
        o = acc[:dec] - lam * acc[dec:]
        o_ref[:, cols] = (_rms_rows(o, hg) * (1.0 - lam_init)).astype(o_ref.dtype)


def _attn_sample(q, kn, vn, cache_kt, cache_vr, layer, page_table, lp, hg, lam_init):
    n_s, dec, _ = q.shape
    n_pages = page_table.shape[1]
    page = cache_kt.shape[3]
    tok = pl.BlockSpec((None, dec, QK_WIDTH), lambda n, pt: (n, 0, 0))
    kpages = [pl.BlockSpec((None, None, QK_WIDTH, page), lambda n, pt, j=j: (layer, pt[n, j], 0, 0))
              for j in range(n_pages)]
    vpages = [pl.BlockSpec((None, None, page * ATT_HEADS, ATT_V_DIM),
                           lambda n, pt, j=j: (layer, pt[n, j], 0, 0)) for j in range(n_pages)]
    grid_spec = pltpu.PrefetchScalarGridSpec(
        num_scalar_prefetch=1,
        grid=(n_s,),
        in_specs=[tok, tok, tok,
                  pl.BlockSpec((4, ATT_HEAD_DIM), lambda n, pt: (0, 0)),
                  pl.BlockSpec((1, ATT_V_DIM), lambda n, pt: (0, 0))] + kpages + vpages,
        out_specs=tok,
    )
    return pl.pallas_call(
        functools.partial(_attn_sample_kernel, n_pages=n_pages, dec=dec, lam_init=lam_init),
        out_shape=jax.ShapeDtypeStruct((n_s, dec, ATT_WIDTH), BF16),
        grid_spec=grid_spec,
        compiler_params=_cparams(("parallel",)),
        name="attn_sample",
    )(page_table, q, kn, vn, lp, hg, *([cache_kt] * n_pages), *([cache_vr] * n_pages))


def _merge_kernel(xp_ref, xs_ref, up_ref, us_ref, cb_ref, vin_ref, halo_ref, gate_ref,
                  yrp_ref, yrs_ref, ybp_ref, ybs_ref, vm1s_ref, vm2s_ref,
                  d_ref, wglu_ref, bglu_ref, cw_ref, wssm_ref, watt_ref, wconv_ref, wout_ref,
                  op_ref, os_ref, *, n_prompt_tiles, tiles_per_seq):
    i = pl.program_id(0)
    is_prompt = i < n_prompt_tiles
    yraw = jnp.where(is_prompt, yrp_ref[...], yrs_ref[...])
    yb = jnp.where(is_prompt, ybp_ref[...], ybs_ref[...])
    y = jax.nn.gelu(yraw + d_ref[...] * _read_group(up_ref, us_ref, is_prompt))
    ya = y * jax.nn.sigmoid(_bdot(y.astype(BF16), wglu_ref[...]) + bglu_ref[...])

    vin = vin_ref[...]
    row = lax.broadcasted_iota(jnp.int32, vin.shape, 0)
    halo = jnp.where(i % tiles_per_seq == 0, 0.0, halo_ref[...])
    h1 = jnp.broadcast_to(halo[SUBLANES - 1:SUBLANES, :], vin.shape)
    h2 = jnp.broadcast_to(halo[SUBLANES - 2:SUBLANES - 1, :], vin.shape)
    vm1 = jnp.where(row == 0, h1, pltpu.roll(vin, 1, axis=0))
    vm2 = jnp.where(row == 0, h2, jnp.where(row == 1, h1, pltpu.roll(vin, 2, axis=0)))
    vm1 = jnp.where(is_prompt, vm1, vm1s_ref[...])
    vm2 = jnp.where(is_prompt, vm2, vm2s_ref[...])
    conv = vm2 * cw_ref[0:1, :] + vm1 * cw_ref[1:2, :] + vin * cw_ref[2:3, :]
    yc = cb_ref[...] * conv
    merged = (gate_ref[:, 0:D_MODEL] * _bdot(ya.astype(BF16), wssm_ref[...])
              + gate_ref[:, D_MODEL:2 * D_MODEL] * _bdot(yb, watt_ref[...])
              + gate_ref[:, 2 * D_MODEL:3 * D_MODEL] * _bdot(yc.astype(BF16), wconv_ref[...]))
    x_new = _read_group(xp_ref, xs_ref, is_prompt) + _bdot(merged.astype(BF16), wout_ref[...])
    _write_group(op_ref, os_ref, is_prompt, x_new)


def _merge(xp, xs, up, us, cb, vin, gates, yr_p, yr_s, yb_p, yb_s, vm1_s, vm2_s,
           d, wglu, bglu, cw, wssm, watt, wconv, wout, tiles_per_seq):
    tm = ROW_TILE
    tp, ts = xp.shape[0], xs.shape[0]
    npt = tp // tm
    row = lambda w: pl.BlockSpec((tm, w), lambda i: (i, 0))
    prow = lambda w: _group_specs(tm, w, npt)[0]
    srow = lambda w: _group_specs(tm, w, npt)[1]
    halo = pl.BlockSpec((SUBLANES, CONV_WIDTH), lambda i: (jnp.maximum(i * (tm // SUBLANES) - 1, 0), 0))
    return pl.pallas_call(
        functools.partial(_merge_kernel, n_prompt_tiles=npt, tiles_per_seq=tiles_per_seq),
        out_shape=[jax.ShapeDtypeStruct((tp, D_MODEL), F32), jax.ShapeDtypeStruct((ts, D_MODEL), F32)],
        grid=((tp + ts) // tm,),
        in_specs=[prow(D_MODEL), srow(D_MODEL), prow(SSM_WIDTH), srow(SSM_WIDTH),
                  row(CONV_WIDTH), row(CONV_WIDTH), halo,
                  row(N_BRANCH * D_MODEL),
                  prow(SSM_WIDTH), srow(SSM_WIDTH), prow(ATT_WIDTH), srow(ATT_WIDTH),
                  srow(CONV_WIDTH), srow(CONV_WIDTH),
                  _const_spec((1, SSM_WIDTH)), _const_spec((SSM_WIDTH, SSM_WIDTH)),
                  _const_spec((1, SSM_WIDTH)), _const_spec((CONV_K, CONV_WIDTH)),
                  _const_spec((SSM_WIDTH, D_MODEL)), _const_spec((ATT_WIDTH, D_MODEL)),
                  _const_spec((CONV_WIDTH, D_MODEL)), _const_spec((D_MODEL, D_MODEL))],
        out_specs=[prow(D_MODEL), srow(D_MODEL)],
        compiler_params=_cparams(("arbitrary",)),
        name="merge",
    )(xp, xs, up, us, cb, vin, vin, gates, yr_p, yr_s, yb_p, yb_s, vm1_s, vm2_s,
      d, wglu, bglu, cw, wssm, watt, wconv, wout)


def _swiglu(h, w1, w3, w2):
    a = _bdot(h, w1)
    b = _bdot(h, w3)
    return _bdot((jax.nn.silu(a) * b).astype(BF16), w2)


def _ffn_kernel(xp_ref, xs_ref, g_ref, w1_ref, w3_ref, w2_ref, op_ref, os_ref, *, n_prompt_tiles):
    is_prompt = pl.program_id(0) < n_prompt_tiles
    x = _read_group(xp_ref, xs_ref, is_prompt)
    h = _rms_rows(x, g_ref[...]).astype(BF16)
    _write_group(op_ref, os_ref, is_prompt, x + _swiglu(h, w1_ref[...], w3_ref[...], w2_ref[...]))


def _ffn(xp, xs, g, w1, w3, w2):
    tm = ROW_TILE
    tp, ts = xp.shape[0], xs.shape[0]
    npt = tp // tm
    d_ff = w1.shape[1]
    rows = _group_specs(tm, D_MODEL, npt)
    return pl.pallas_call(
        functools.partial(_ffn_kernel, n_prompt_tiles=npt),
        out_shape=[jax.ShapeDtypeStruct((tp, D_MODEL), F32), jax.ShapeDtypeStruct((ts, D_MODEL), F32)],
        grid=((tp + ts) // tm,),
        in_specs=rows + [_const_spec((1, D_MODEL)), _const_spec((D_MODEL, d_ff)),
                         _const_spec((D_MODEL, d_ff)), _const_spec((d_ff, D_MODEL))],
        out_specs=rows,
        compiler_params=_cparams(("arbitrary",)),
        name="ffn",
    )(xp, xs, g, w1, w3, w2)


def _router_kernel(xp_ref, xs_ref, g_ref, wr_ref, tri_ref, h_ref, meta_ref, gate_ref, cnt_ref, carry,
                   *, n_prompt_tiles):
    @pl.when(pl.program_id(0) == 0)
    def _():
        carry[...] = jnp.zeros(carry.shape, F32)

    x = _read_group(xp_ref, xs_ref, pl.program_id(0) < n_prompt_tiles)
    h = _rms_rows(x, g_ref[...])
    h_ref[...] = h
    logits = jnp.dot(h, wr_ref[...], preferred_element_type=F32, precision=lax.Precision.HIGHEST)
    lane = lax.broadcasted_iota(jnp.int32, logits.shape, 1)
    logits = jnp.where(lane < N_EXPERTS, logits, -jnp.inf)
    big = jnp.int32(logits.shape[1])
    m1 = jnp.max(logits, axis=-1, keepdims=True)
    i1 = jnp.min(jnp.where(logits == m1, lane, big), axis=-1, keepdims=True)
    rest = jnp.where(lane == i1, -jnp.inf, logits)
    m2 = jnp.max(rest, axis=-1, keepdims=True)
    i2 = jnp.min(jnp.where(rest == m2, lane, big), axis=-1, keepdims=True)
    e = jnp.exp(m2 - m1)
    g1 = 1.0 / (1.0 + e)
    g2 = e / (1.0 + e)
    o1 = lane == i1
    o2 = lane == i2
    chosen = jnp.where(o1 | o2, 1.0, 0.0)
    base = _bdot(tri_ref[...], chosen.astype(BF16)) + carry[...]
    r1 = jnp.sum(jnp.where(o1, base, 0.0), axis=-1, keepdims=True).astype(jnp.int32)
    r2 = jnp.sum(jnp.where(o2, base, 0.0), axis=-1, keepdims=True).astype(jnp.int32)
    carry[...] = carry[...] + jnp.sum(chosen, axis=0, keepdims=True)
    cnt_ref[...] = carry[...]
    meta_ref[...] = jnp.where(lane == 0, i1, jnp.where(lane == 1, i2,
                              jnp.where(lane == 2, r1, jnp.where(lane == 3, r2, 0))))
    gate_ref[...] = jnp.where(lane == 0, g1, jnp.where(lane == 1, g2, 0.0))


def _router(xp, xs, g, wr_pad):
    tm = ROW_TILE
    t = xp.shape[0] + xs.shape[0]
    npt = xp.shape[0] // tm
    lanes = wr_pad.shape[1]
    tri = jnp.tri(tm, k=-1, dtype=BF16)
    row = lambda w: pl.BlockSpec((tm, w), lambda i: (i, 0))
    return pl.pallas_call(
        functools.partial(_router_kernel, n_prompt_tiles=npt),
        out_shape=[jax.ShapeDtypeStruct((t, D_MODEL), F32),
                   jax.ShapeDtypeStruct((t, lanes), jnp.int32),
                   jax.ShapeDtypeStruct((t, lanes), F32),
                   jax.ShapeDtypeStruct((1, lanes), F32)],
        grid=(t // tm,),
        in_specs=_group_specs(tm, D_MODEL, npt) + [_const_spec((1, D_MODEL)), _const_spec((D_MODEL, lanes)),
                                                   _const_spec((tm, tm))],
        out_specs=[row(D_MODEL), row(lanes), row(lanes), _const_spec((1, lanes))],
        scratch_shapes=[pltpu.VMEM((1, lanes), F32)],
        compiler_params=_cparams(("arbitrary",)),
        name="router",
    )(xp, xs, g, wr_pad, tri)


def _sc_mesh():
    return plsc.VectorSubcoreMesh(core_axis_name="c", subcore_axis_name="s")


def _sc_worker_base(per_worker):
    return (lax.axis_index("s") * SC_CORES + lax.axis_index("c")) * per_worker


def _sc_scatter_rows(x, idx0, idx1, n_out):
    n, w = x.shape
    per_worker = n // SC_WORKERS
    assert n % (SC_WORKERS * SC_ROWS) == 0

    @functools.partial(
        pl.kernel, mesh=_sc_mesh(), out_type=jax.ShapeDtypeStruct((n_out, w), x.dtype),
        scratch_types=[pltpu.VMEM((SC_ROWS,), jnp.int32), pltpu.VMEM((SC_ROWS,), jnp.int32),
                       pltpu.VMEM((SC_ROWS, w), x.dtype), pltpu.SemaphoreType.DMA])
    def scatter(x_hbm, i0_hbm, i1_hbm, out_hbm, i0_v, i1_v, rows_v, sem):
        start = _sc_worker_base(per_worker)

        @pl.loop(0, per_worker // SC_ROWS)
        def _(c):
            rows = pl.ds(pl.multiple_of(start + c * SC_ROWS, SUBLANES), SC_ROWS)
            pltpu.sync_copy(x_hbm.at[rows], rows_v)
            pltpu.sync_copy(i0_hbm.at[rows], i0_v)
            pltpu.sync_copy(i1_hbm.at[rows], i1_v)
            pltpu.async_copy(rows_v, out_hbm.at[i0_v], sem).wait()
            pltpu.async_copy(rows_v, out_hbm.at[i1_v], sem).wait()

    return scatter(x, idx0, idx1)


def _sc_gather_rows(table, idx):
    n, w = idx.shape[0], table.shape[1]
    per_worker = n // SC_WORKERS
    assert n % (SC_WORKERS * SC_ROWS) == 0

    @functools.partial(
        pl.kernel, mesh=_sc_mesh(), out_type=jax.ShapeDtypeStruct((n, w), table.dtype),
        scratch_types=[pltpu.VMEM((SC_ROWS,), jnp.int32), pltpu.VMEM((SC_ROWS, w), table.dtype),
                       pltpu.SemaphoreType.DMA])
    def gather(table_hbm, idx_hbm, out_hbm, idx_v, rows_v, sem):
        start = _sc_worker_base(per_worker)

        @pl.loop(0, per_worker // SC_ROWS)
        def _(c):
            rows = pl.ds(pl.multiple_of(start + c * SC_ROWS, SUBLANES), SC_ROWS)
            pltpu.sync_copy(idx_hbm.at[rows], idx_v)
            pltpu.async_copy(table_hbm.at[idx_v], rows_v, sem).wait()
            pltpu.sync_copy(rows_v, out_hbm.at[rows])

    return gather(table, idx)


def _moe_kernel(be_ref, nv_ref, x_ref, w1_ref, w3_ref, w2_ref, o_ref):
    del be_ref

    @pl.when(pl.program_id(0) < nv_ref[0])
    def _():
        o_ref[...] = _swiglu(x_ref[...].astype(BF16), w1_ref[...], w3_ref[...], w2_ref[...])


def _moe_blocks(xs, block_expert, n_valid, w1, w3, w2):
    cap = xs.shape[0]
    tb = MOE_TILE
    d_ff = w1.shape[2]
    grid_spec = pltpu.PrefetchScalarGridSpec(
        num_scalar_prefetch=2,
        grid=(cap // tb,),
        in_specs=[pl.BlockSpec((tb, D_MODEL), lambda i, be, nv: (jnp.minimum(i, nv[0] - 1), 0)),
                  pl.BlockSpec((None, D_MODEL, d_ff), lambda i, be, nv: (be[i], 0, 0)),
                  pl.BlockSpec((None, D_MODEL, d_ff), lambda i, be, nv: (be[i], 0, 0)),
                  pl.BlockSpec((None, d_ff, D_MODEL), lambda i, be, nv: (be[i], 0, 0))],
        out_specs=pl.BlockSpec((tb, D_MODEL), lambda i, be, nv: (i, 0)),
    )
    return pl.pallas_call(
        _moe_kernel,
        out_shape=jax.ShapeDtypeStruct((cap, D_MODEL), F32),
        grid_spec=grid_spec,
        compiler_params=_cparams(("arbitrary",)),
        name="moe",
    )(block_expert, n_valid, xs, w1, w3, w2)


def _combine_kernel(xp_ref, xs_ref, a0_ref, a1_ref, gate_ref, op_ref, os_ref, *, n_prompt_tiles):
    is_prompt = pl.program_id(0) < n_prompt_tiles
    gate = gate_ref[...]
    y = (_read_group(xp_ref, xs_ref, is_prompt)
         + gate[:, 0:1] * a0_ref[...] + gate[:, 1:2] * a1_ref[...])
    _write_group(op_ref, os_ref, is_prompt, y)


def _combine(xp, xs, picked, gate):
    tm = ROW_TILE
    tp, ts = xp.shape[0], xs.shape[0]
    nt = (tp + ts) // tm
    rows = _group_specs(tm, D_MODEL, tp // tm)
    return pl.pallas_call(
        functools.partial(_combine_kernel, n_prompt_tiles=tp // tm),
        out_shape=[jax.ShapeDtypeStruct((tp, D_MODEL), F32), jax.ShapeDtypeStruct((ts, D_MODEL), F32)],
        grid=(nt,),
        in_specs=rows + [pl.BlockSpec((tm, D_MODEL), lambda i: (i, 0)),
                         pl.BlockSpec((tm, D_MODEL), lambda i: (i + nt, 0)),
                         pl.BlockSpec((tm, gate.shape[1]), lambda i: (i, 0))],
        out_specs=rows,
        compiler_params=_cparams(("arbitrary",)),
        name="moe_combine",
    )(xp, xs, picked, picked, gate)


def _moe(xp, xs, g, wr, w1, w3, w2):
    t = xp.shape[0] + xs.shape[0]
    tb = MOE_TILE
    lanes = 128
    wr_pad = jnp.pad(wr, ((0, 0), (0, lanes - N_EXPERTS)))
    h, meta, gate, cnt = _router(xp, xs, g, wr_pad)
    counts = cnt[0, :N_EXPERTS].astype(jnp.int32)
    padded = (counts + tb - 1) // tb * tb
    pad_ends = jnp.cumsum(padded)
    pad_starts = pad_ends - padded
    experts = jnp.arange(N_EXPERTS, dtype=jnp.int32)
    slot = lambda e, r: jnp.sum(jnp.where(e[:, None] == experts, pad_starts, 0), axis=1) + r
    dest0 = slot(meta[:, 0], meta[:, 2])
    dest1 = slot(meta[:, 1], meta[:, 3])
    nb = -(-(t * TOP_K + N_EXPERTS * (tb - 1)) // tb)
    block_expert = jnp.minimum(
        jnp.sum(pad_ends[None, :] <= (jnp.arange(nb, dtype=jnp.int32) * tb)[:, None], axis=1),
        N_EXPERTS - 1).astype(jnp.int32)
    n_valid = (pad_ends[-1:] // tb).astype(jnp.int32)
    slots = _sc_scatter_rows(h, dest0, dest1, nb * tb)
    out = _moe_blocks(slots, block_expert, n_valid, w1, w3, w2)
    picked = _sc_gather_rows(out, jnp.concatenate([dest0, dest1]))
    return _combine(xp, xs, picked, gate)


def kernel(x_prompt, x_sample, cache_k, cache_v, page_table, state_ssm_re, state_ssm_im, state_conv,
           norm_mix_g, norm_ffn_g, w_in, ssm_a_re, ssm_a_im, ssm_log_dt, ssm_b_re, ssm_b_im,
           ssm_c_re, ssm_c_im, ssm_d, ssm_w_glu, ssm_b_glu, q_norm_g, k_norm_g,
           lambda_q1, lambda_k1, lambda_q2, lambda_k2, head_norm_g, conv_w,
           w_br_ssm, w_br_att, w_br_conv, w_out, ffn_w1, ffn_w3, ffn_w2,
           router_w, moe_w1, moe_w3, moe_w2):
    n_p, seq, _ = x_prompt.shape
    n_s, dec, _ = x_sample.shape
    depth = w_in.shape[0]
    tp = n_p * seq
    ts = n_s * dec
    assert seq % ROW_TILE == 0 and ts % ROW_TILE == 0 and seq >= CONV_K - 1
    pool, page = cache_k.shape[1], cache_k.shape[2]
    cache_kt = cache_k.reshape(depth, pool, page, QK_WIDTH).transpose(0, 1, 3, 2)
    cache_vr = cache_v.reshape(depth, pool, page * ATT_HEADS, ATT_V_DIM)
    seg = jnp.kron(jnp.eye(QK_WIDTH // ATT_HEAD_DIM, dtype=F32),
                   jnp.full((ATT_HEAD_DIM, ATT_HEAD_DIM), 1.0 / ATT_HEAD_DIM, F32)).astype(BF16)
    n_rep = QK_WIDTH // ATT_HEAD_DIM

    xp = x_prompt.reshape(tp, D_MODEL)
    xs = x_sample.reshape(ts, D_MODEL)
    all_tables = _s5_tables(ssm_a_re, ssm_a_im, ssm_log_dt, ssm_b_re, ssm_b_im, ssm_c_re, ssm_c_im, dec)
    kv_prompt = None
    srp, sip, cvp = [], [], []
    ks, vs, srs, sis, cvs = [], [], [], [], []
    for l in range(depth):
        lam_init = 0.8 - 0.6 * math.exp(-0.3 * l)
        q, kb, vb, cb, vin, gates, kt, vr, k_s, v_s, u_p, u_s = _inproj(
            xp, xs, norm_mix_g[l][None], w_in[l].astype(BF16),
            jnp.tile(q_norm_g[l], n_rep)[None], jnp.tile(k_norm_g[l], n_rep)[None], seg,
            l, depth, n_p, seq, kv_prompt)
        kv_prompt = (kt, vr)

        tables = [tab[l] for tab in all_tables]
        yr_p, yr_s, p_re, p_im, s_re, s_im = _s5(u_p, u_s, state_ssm_re[l], state_ssm_im[l], tables,
                                                  n_p, seq, n_s, dec)

        lp = jnp.stack([lambda_q1[l], lambda_k1[l], lambda_q2[l], lambda_k2[l]])
        hg = head_norm_g[l][None]
        casts = ()
        if l + 1 < depth and (l + 1) % 2 == 1:
            e = (l + 1) // 2
            casts = (moe_w1[e].reshape(-1, moe_w1.shape[-1]), moe_w3[e].reshape(-1, moe_w3.shape[-1]),
                     moe_w2[e].reshape(-1, moe_w2.shape[-1]))
        yb_p, *cast_out = _attn_prompt(q, kb, vb, lp, hg, n_p, seq, lam_init, casts)
        if casts:
            moe_bf = [c.reshape(w.shape[1:]) for c, w in zip(cast_out, (moe_w1, moe_w3, moe_w2))]
        yb_s = _attn_sample(q[tp:].reshape(n_s, dec, QK_WIDTH), kb[tp:].reshape(n_s, dec, QK_WIDTH),
                            vb[tp:].reshape(n_s, dec, ATT_WIDTH), cache_kt, cache_vr, l, page_table,
                            lp, hg, lam_init).reshape(ts, ATT_WIDTH)

        ext_s = jnp.concatenate([state_conv[l], vin[tp:].reshape(n_s, dec, CONV_WIDTH)], axis=1)
        vm1_s = ext_s[:, 1:1 + dec].reshape(ts, CONV_WIDTH)
        vm2_s = ext_s[:, 0:dec].reshape(ts, CONV_WIDTH)
        xp, xs = _merge(xp, xs, u_p, u_s, cb, vin, gates, yr_p, yr_s, yb_p, yb_s, vm1_s, vm2_s,
                        ssm_d[l][None], ssm_w_glu[l].astype(BF16), ssm_b_glu[l][None], conv_w[l],
                        w_br_ssm[l].astype(BF16), w_br_att[l].astype(BF16), w_br_conv[l].astype(BF16),
                        w_out[l].astype(BF16), seq // ROW_TILE)

        i = l // 2
        if l % 2 == 0:
            xp, xs = _ffn(xp, xs, norm_ffn_g[l][None], ffn_w1[i].astype(BF16), ffn_w3[i].astype(BF16),
                          ffn_w2[i].astype(BF16))
        else:
            xp, xs = _moe(xp, xs, norm_ffn_g[l][None], router_w[i], *moe_bf)

        srp.append(p_re); sip.append(p_im)
        cvp.append(vin[:tp].reshape(n_p, seq, CONV_WIDTH)[:, seq - (CONV_K - 1):])
        ks.append(k_s.reshape(n_s, dec, ATT_HEADS, 2, ATT_HEAD_DIM))
        vs.append(v_s.reshape(n_s, dec, ATT_HEADS, ATT_V_DIM))
        srs.append(s_re); sis.append(s_im); cvs.append(ext_s[:, dec:])

    kt, vr = kv_prompt
    k_prompt = kt.reshape(depth, n_p, ATT_HEADS, 2, ATT_HEAD_DIM, seq).transpose(0, 1, 5, 2, 3, 4)
    v_prompt = vr.reshape(depth, n_p, seq, ATT_HEADS, ATT_V_DIM)
    return (xp.reshape(n_p, seq, D_MODEL), xs.reshape(n_s, dec, D_MODEL),
            k_prompt, v_prompt, jnp.stack(srp), jnp.stack(sip), jnp.stack(cvp),
            jnp.stack(ks), jnp.stack(vs), jnp.stack(srs), jnp.stack(sis), jnp.stack(cvs))
```

```python
import functools
import math

import jax
import jax.numpy as jnp
from jax import lax
from jax.experimental import pallas as pl
from jax.experimental.pallas import tpu as pltpu
from jax.experimental.pallas import tpu_sc as plsc

F32 = jnp.float32
BF16 = jnp.bfloat16

D_MODEL = 1024
SSM_WIDTH = 256
SSM_GROUP = 16
SSM_GROUPS = SSM_WIDTH // SSM_GROUP
SSM_STATE = 64
ATT_HEADS = 4
ATT_HEAD_DIM = 64
ATT_V_DIM = 2 * ATT_HEAD_DIM
QK_WIDTH = ATT_HEADS * 2 * ATT_HEAD_DIM
ATT_WIDTH = ATT_HEADS * ATT_V_DIM
CONV_WIDTH = 256
CONV_K = 3
N_BRANCH = 3
N_EXPERTS = 8
TOP_K = 2
EPS = 1e-6
NEG_INF = -1e30

C_U = 0
C_Q = C_U + SSM_WIDTH
C_K = C_Q + QK_WIDTH
C_V = C_K + QK_WIDTH
C_CB = C_V + ATT_WIDTH
C_CC = C_CB + CONV_WIDTH
C_CH = C_CC + CONV_WIDTH
C_G = C_CH + CONV_WIDTH
IN_COLS = C_G + N_BRANCH * D_MODEL

SSM_CHUNK = 8
SUBLANES = 8
BF16_SUBLANES = 16
ROW_TILE = 512
ATT_TILE = 512
MOE_TILE = 256
SC_CORES = 2
SC_WORKERS = SC_CORES * 16
SC_CHUNK_BYTES = 192 * 1024
VMEM_LIMIT = 56 * 1024 * 1024


def _cparams(sem):
    return pltpu.CompilerParams(dimension_semantics=sem, vmem_limit_bytes=VMEM_LIMIT)


def _const_spec(shape):
    nd = len(shape)
    return pl.BlockSpec(shape, lambda *_: (0,) * nd)


def _bdot(a, b):
    return jnp.dot(a, b, preferred_element_type=F32)


def _rms_rows(x, g):
    ms = jnp.mean(x * x, axis=-1, keepdims=True)
    return x * lax.rsqrt(ms + EPS) * g


def _group_specs(tm, w, n_prompt_tiles):
    return [pl.BlockSpec((tm, w), lambda i: (jnp.minimum(i, n_prompt_tiles - 1), 0)),
            pl.BlockSpec((tm, w), lambda i: (jnp.maximum(i - n_prompt_tiles, 0), 0))]


def _read_group(p_ref, s_ref, is_prompt):
    return jnp.where(is_prompt, p_ref[...], s_ref[...])


def _write_group(p_ref, s_ref, is_prompt, val):
    @pl.when(is_prompt)
    def _():
        p_ref[...] = val

    @pl.when(jnp.logical_not(is_prompt))
    def _():
        s_ref[...] = val


def _pack_halves(x):
    half = x.shape[1] // 2
    bits = lambda t: lax.bitcast_convert_type(t.astype(BF16).astype(F32), jnp.uint32)
    return (bits(x[:, :half]) >> 16) | (bits(x[:, half:]) & jnp.uint32(0xFFFF0000))


def _unpack_halves(p):
    lo = lax.bitcast_convert_type(p << 16, F32)
    hi = lax.bitcast_convert_type(p & jnp.uint32(0xFFFF0000), F32)
    return jnp.concatenate([lo, hi], axis=1)


def _segment_rms(z, g, seg):
    ms = _bdot((z * z).astype(BF16), seg)
    return z * lax.rsqrt(ms + EPS) * g


def _inproj_kernel(xp_ref, xs_ref, g_ref, w_ref, qg_ref, kg_ref, seg_ref, *rest,
                   n_prompt_tiles, n_prev, layer):
    (q_ref, kb_ref, vb_ref, cb_ref, vin_ref, gate_ref,
     kt_ref, vr_ref, ks_ref, vs_ref, up_ref, us_ref) = rest[n_prev:]
    tm = q_ref.shape[0]
    is_prompt = pl.program_id(0) < n_prompt_tiles
    h = _rms_rows(_read_group(xp_ref, xs_ref, is_prompt), g_ref[...]).astype(BF16)

    def proj(a, b):
        return _bdot(h, w_ref[:, a:b])

    seg = seg_ref[...]
    _write_group(up_ref, us_ref, is_prompt, proj(C_U, C_Q))
    qn = _segment_rms(proj(C_Q, C_K), qg_ref[...], seg)
    q_ref[...] = (qn * (ATT_HEAD_DIM ** -0.5)).astype(BF16)
    kn = _segment_rms(proj(C_K, C_V), kg_ref[...], seg)
    kb_ref[...] = kn.astype(BF16)
    v = proj(C_V, C_CB)
    vb_ref[...] = v.astype(BF16)
    cb_ref[...] = proj(C_CB, C_CC)
    vin_ref[...] = proj(C_CC, C_CH) * proj(C_CH, C_G)
    for j in range(N_BRANCH):
        a = C_G + j * D_MODEL
        gate_ref[:, j * D_MODEL:(j + 1) * D_MODEL] = jax.nn.sigmoid(proj(a, a + D_MODEL)).astype(gate_ref.dtype)

    @pl.when(is_prompt)
    def _():
        if n_prev:
            kt_l, vr_l = kt_ref, vr_ref
        else:
            kt_l, vr_l = kt_ref.at[layer], vr_ref.at[layer]
            for other in range(kt_ref.shape[0]):
                if other != layer:
                    kt_ref[other] = jnp.zeros(kt_ref.shape[1:], F32)
                    vr_ref[other] = jnp.zeros(vr_ref.shape[1:], F32)
        kt_l[...] = kn.T
        for hd in range(ATT_HEADS):
            vr_l[pl.ds(hd, tm, stride=ATT_HEADS), :] = v[:, hd * ATT_V_DIM:(hd + 1) * ATT_V_DIM]

    @pl.when(jnp.logical_not(is_prompt))
    def _():
        ks_ref[...] = kn
        vs_ref[...] = v


def _inproj(xp, xs, g, w_bf, qg, kg, seg, layer, depth, n_p, seq, prev):
    tm = ROW_TILE
    tp, ts = xp.shape[0], xs.shape[0]
    t = tp + ts
    npt = tp // tm
    tps = seq // tm
    row = lambda w: pl.BlockSpec((tm, w), lambda i: (i, 0))
    pc = lambda i: jnp.minimum(i, npt - 1)
    outs = [(QK_WIDTH, BF16), (QK_WIDTH, BF16), (ATT_WIDTH, BF16),
            (CONV_WIDTH, F32), (CONV_WIDTH, F32), (N_BRANCH * D_MODEL, BF16)]
    out_shape = ([jax.ShapeDtypeStruct((t, w), d) for w, d in outs]
                 + [jax.ShapeDtypeStruct((depth, n_p, QK_WIDTH, seq), F32),
                    jax.ShapeDtypeStruct((depth, tp * ATT_HEADS, ATT_V_DIM), F32),
                    jax.ShapeDtypeStruct((ts, QK_WIDTH), F32), jax.ShapeDtypeStruct((ts, ATT_WIDTH), F32),
                    jax.ShapeDtypeStruct((tp, SSM_WIDTH), F32), jax.ShapeDtypeStruct((ts, SSM_WIDTH), F32)])
    srow = _group_specs(tm, QK_WIDTH, npt)[1]
    prev = () if prev is None else tuple(prev)
    lead, at = (None, layer) if prev else (depth, 0)
    out_specs = ([row(w) for w, _ in outs]
                 + [pl.BlockSpec((lead, None, QK_WIDTH, tm), lambda i: (at, pc(i) // tps, 0, pc(i) % tps)),
                    pl.BlockSpec((lead, tm * ATT_HEADS, ATT_V_DIM), lambda i: (at, pc(i), 0)),
                    srow, srow] + _group_specs(tm, SSM_WIDTH, npt))
    n_in = 7
    return pl.pallas_call(
        functools.partial(_inproj_kernel, n_prompt_tiles=npt, n_prev=len(prev), layer=layer),
        out_shape=out_shape,
        grid=(t // tm,),
        in_specs=(_group_specs(tm, D_MODEL, npt)
                  + [_const_spec((1, D_MODEL)), _const_spec((D_MODEL, IN_COLS)),
                     _const_spec((1, QK_WIDTH)), _const_spec((1, QK_WIDTH)),
                     _const_spec((QK_WIDTH, QK_WIDTH))]
                  + [pl.BlockSpec(memory_space=pl.ANY)] * len(prev)),
        out_specs=out_specs,
        input_output_aliases={n_in + j: len(outs) + j for j in range(len(prev))},
        compiler_params=_cparams(("arbitrary",)),
        name="inproj",
    )(xp, xs, g, w_bf, qg, kg, seg, *prev)


def _s5_intra(u, kb_ref, tc):
    rowmod = lax.broadcasted_iota(jnp.int32, u.shape, 0) % tc
    y = _bdot(u.astype(BF16), kb_ref[0])
    for m in range(1, tc):
        um = jnp.where(rowmod >= m, pltpu.roll(u, m, axis=0), 0.0)
        y = y + _bdot(um.astype(BF16), kb_ref[m])
    return y


def _s5_inject(u_refs, pb_ref, k0, n_chunks, tc):
    w = None
    for k in range(tc):
        rows = pl.ds(k, n_chunks, stride=tc)
        uk = jnp.concatenate([r[rows, :] for r in u_refs], axis=1).astype(BF16)
        d = _bdot(uk, pb_ref[k0 + k])
        w = d if w is None else w + d
    return w


def _s5_readout(y_ref, y_scrs, y, s_in, qb_ref, n_chunks, tc):
    lanes = y_scrs[0].shape[1]
    for h, scr in enumerate(y_scrs):
        scr[...] = y[:, h * lanes:(h + 1) * lanes]
    sb = s_in.astype(BF16)
    for k in range(tc):
        rows = pl.ds(k, n_chunks, stride=tc)
        yk = _bdot(sb, qb_ref[k])
        for h, scr in enumerate(y_scrs):
            scr[rows, :] = scr[rows, :] + yk[:, h * lanes:(h + 1) * lanes]
    y_ref[...] = jnp.concatenate([scr[...] for scr in y_scrs], axis=1)


def _cmul_add(a_re, a_im, s, w, half):
    s_re, s_im = s[:, :half], s[:, half:]
    return jnp.concatenate([a_re * s_re - a_im * s_im, a_re * s_im + a_im * s_re], axis=1) + w


def _s5_prompt_kernel(ua_ref, ub_ref, kb_ref, pb_ref, qb_ref, a_ref, y_ref, fs_ref,
                      carry, w_fold, s_fold, ya_scr, yb_scr, *, tc):
    n_b, tm, hw = ua_ref.shape
    n_chunks = tm // tc
    rows = n_b * n_chunks
    fold = carry.shape[0] // n_b
    hf = fold // 2

    @pl.when(pl.program_id(0) == 0)
    def _():
        carry[...] = jnp.zeros(carry.shape, F32)

    u = jnp.concatenate([ua_ref[...].reshape(n_b * tm, hw), ub_ref[...].reshape(n_b * tm, hw)], axis=1)
    y = _s5_intra(u, kb_ref, tc)
    w = None
    for k in range(tc):
        at_k = pl.ds(k, n_chunks, stride=tc)
        uk = jnp.concatenate([jnp.concatenate([ua_ref[b, at_k, :], ub_ref[b, at_k, :]], axis=1)
                              for b in range(n_b)], axis=0).astype(BF16)
        d = _bdot(uk, pb_ref[k])
        w = d if w is None else w + d
    for r in range(fold):
        w_fold[pl.ds(r, rows, stride=fold), :] = w[:, r * hw:(r + 1) * hw]
    a_re = a_ref[0]
    a_im = a_ref[1]
    for b in range(n_b):
        s = carry[b * fold:(b + 1) * fold, :]
        for c in range(n_chunks):
            at = slice((b * n_chunks + c) * fold, (b * n_chunks + c + 1) * fold)
            s_fold[at, :] = s
            s_re, s_im = s[:hf], s[hf:]
            s = jnp.concatenate([a_re * s_re - a_im * s_im, a_re * s_im + a_im * s_re], axis=0) + w_fold[at, :]
        carry[b * fold:(b + 1) * fold, :] = s
    fs_ref[...] = carry[...]
    s_in = jnp.concatenate([s_fold[pl.ds(r, rows, stride=fold), :] for r in range(fold)], axis=1)
    for h, scr in enumerate((ya_scr, yb_scr)):
        scr[...] = y[:, h * hw:(h + 1) * hw]
    sb = s_in.astype(BF16)
    for k in range(tc):
        at_k = pl.ds(k, rows, stride=tc)
        yk = _bdot(sb, qb_ref[k])
        for h, scr in enumerate((ya_scr, yb_scr)):
            scr[at_k, :] = scr[at_k, :] + yk[:, h * hw:(h + 1) * hw]
    for b in range(n_b):
        y_ref[b] = jnp.concatenate([ya_scr[b * tm:(b + 1) * tm, :], yb_scr[b * tm:(b + 1) * tm, :]], axis=1)


def _s5_sample_kernel(ua_ref, ub_ref, s0_ref, kb_ref, pb_ref, qb_ref, a_ref, y_ref, fs_ref,
                      ya_scr, yb_scr, *, tc, k0):
    n_chunks = ua_ref.shape[0] // tc
    half = a_ref.shape[1]
    y = _s5_intra(jnp.concatenate([ua_ref[...], ub_ref[...]], axis=1), kb_ref, tc)
    w = _s5_inject((ua_ref, ub_ref), pb_ref, k0, n_chunks, tc)
    s0 = s0_ref[...]
    fs_ref[...] = _cmul_add(a_ref[2:3, :], a_ref[3:4, :], s0, w, half)
    _s5_readout(y_ref, (ya_scr, yb_scr), y, s0, qb_ref, n_chunks, tc)


def _s5_tables_kernel(are_ref, aim_ref, ldt_ref, bre_ref, bim_ref, cre_ref, cim_ref,
                      kb_ref, pb_ref, qb_ref, a_ref, *, tc, dec):
    are = are_ref[...]
    aim = aim_ref[...]
    dt = jnp.exp(ldt_ref[...])

    def power(m):
        mag = jnp.exp(are * dt * m)
        ang = aim * dt * m
        return mag * jnp.cos(ang), mag * jnp.sin(ang)

    ab_re, ab_im = power(1.0)
    den = are * are + aim * aim
    nr = ab_re - 1.0
    cr = (nr * are + ab_im * aim) / den
    ci = (ab_im * are - nr * aim) / den
    gj, gp = bre_ref.shape
    row_g = lax.broadcasted_iota(jnp.int32, (gj, gp), 0) // SSM_GROUP
    col_g = lax.broadcasted_iota(jnp.int32, (gj, gp), 1) // SSM_STATE
    diag = row_g == col_g
    bre = bre_ref[...]
    bim = bim_ref[...]
    bb_re = jnp.where(diag, cr * bre - ci * bim, 0.0)
    bb_im = jnp.where(diag, cr * bim + ci * bre, 0.0)
    cc_re = jnp.where(diag, cre_ref[...], 0.0)
    cc_im = jnp.where(diag, cim_ref[...], 0.0)
    c_blk = jnp.concatenate([cc_re, -cc_im], axis=1).T
    for m in range(tc):
        pr, pi = power(float(m))
        pm = jnp.concatenate([pr * bb_re - pi * bb_im, pr * bb_im + pi * bb_re], axis=1)
        pb_ref[tc - 1 - m] = pm.astype(pb_ref.dtype)
        kb_ref[m] = jnp.dot(pm, c_blk, preferred_element_type=F32,
                            precision=lax.Precision.HIGHEST).astype(kb_ref.dtype)
        qr, qi = power(float(m + 1))
        qm = jnp.concatenate([qr * cc_re - qi * cc_im, -(qr * cc_im + qi * cc_re)], axis=1)
        qb_ref[m] = qm.T.astype(qb_ref.dtype)
    a_ref[...] = jnp.concatenate(list(power(float(tc)) + power(float(dec))), axis=0)


def _s5_tables(a_re, a_im, log_dt, b_re, b_im, c_re, c_im, dec):
    depth = a_re.shape[0]
    tc = SSM_CHUNK
    g, p, j = SSM_GROUPS, SSM_STATE, SSM_GROUP
    gp, gj = g * p, g * j
    rowv = lambda t: t.reshape(depth, 1, gp)
    ldt = jnp.repeat(log_dt, p, axis=1).reshape(depth, 1, gp)
    bt = lambda t: jnp.tile(t.transpose(0, 1, 3, 2).reshape(depth, gj, p), (1, 1, g))
    ct = lambda t: jnp.tile(t.reshape(depth, gj, p), (1, 1, g))
    lay = lambda *shape: pl.BlockSpec((None,) + shape, lambda l: (l,) + (0,) * len(shape))
    return pl.pallas_call(
        functools.partial(_s5_tables_kernel, tc=tc, dec=dec),
        out_shape=[jax.ShapeDtypeStruct((depth, tc, gj, gj), BF16),
                   jax.ShapeDtypeStruct((depth, tc, gj, 2 * gp), BF16),
                   jax.ShapeDtypeStruct((depth, tc, 2 * gp, gj), BF16),
                   jax.ShapeDtypeStruct((depth, 4, gp), F32)],
        grid=(depth,),
        in_specs=[lay(1, gp)] * 3 + [lay(gj, gp)] * 4,
        out_specs=[lay(tc, gj, gj), lay(tc, gj, 2 * gp), lay(tc, 2 * gp, gj), lay(4, gp)],
        compiler_params=_cparams(("arbitrary",)),
        name="s5_tables",
    )(rowv(a_re), rowv(a_im), ldt, bt(b_re), bt(b_im), ct(c_re), ct(c_im))


def _s5(up, us, s0_re, s0_im, tables, n_p, seq, n_s, dec):
    kb, pb, qb, adec = tables
    tc = SSM_CHUNK
    g, p = SSM_GROUPS, SSM_STATE
    sw = 2 * g * p
    tm = ROW_TILE
    tp = n_p * seq
    ts = n_s * dec
    hw = SSM_WIDTH // 2
    fold = sw // hw
    assert dec <= tc and seq % tm == 0 and tm % tc == 0
    up3 = up.reshape(n_p, seq, SSM_WIDTH)
    chunk_rows = n_p * (tm // tc)
    yp, fsp = pl.pallas_call(
        functools.partial(_s5_prompt_kernel, tc=tc),
        out_shape=[jax.ShapeDtypeStruct((n_p, seq, SSM_WIDTH), F32),
                   jax.ShapeDtypeStruct((n_p * fold, hw), F32)],
        grid=(seq // tm,),
        in_specs=[pl.BlockSpec((n_p, tm, hw), lambda i: (0, i, 0)),
                  pl.BlockSpec((n_p, tm, hw), lambda i: (0, i, 1)),
                  _const_spec(kb.shape), _const_spec(pb.shape), _const_spec(qb.shape),
                  _const_spec((4, fold // 2, hw))],
        out_specs=[pl.BlockSpec((n_p, tm, SSM_WIDTH), lambda i: (0, i, 0)),
                   _const_spec((n_p * fold, hw))],
        scratch_shapes=[pltpu.VMEM((n_p * fold, hw), F32),
                        pltpu.VMEM((chunk_rows * fold, hw), F32), pltpu.VMEM((chunk_rows * fold, hw), F32),
                        pltpu.VMEM((n_p * tm, hw), F32), pltpu.VMEM((n_p * tm, hw), F32)],
        compiler_params=_cparams(("arbitrary",)),
        name="s5_prompt",
    )(up3, up3, kb, pb, qb, adec.reshape(4, fold // 2, hw))
    yp = yp.reshape(tp, SSM_WIDTH)
    s0 = jnp.concatenate([s0_re.reshape(n_s, g * p), s0_im.reshape(n_s, g * p)], axis=1)
    ys, fss = pl.pallas_call(
        functools.partial(_s5_sample_kernel, tc=dec, k0=tc - dec),
        out_shape=[jax.ShapeDtypeStruct((ts, SSM_WIDTH), F32), jax.ShapeDtypeStruct((n_s, sw), F32)],
        grid=(1,),
        in_specs=[pl.BlockSpec((ts, hw), lambda i: (0, 0)), pl.BlockSpec((ts, hw), lambda i: (0, 1)),
                  _const_spec((n_s, sw)),
                  _const_spec(kb.shape), _const_spec(pb.shape), _const_spec(qb.shape),
                  _const_spec(adec.shape)],
        out_specs=[_const_spec((ts, SSM_WIDTH)), _const_spec((n_s, sw))],
        scratch_shapes=[pltpu.VMEM((ts, hw), F32), pltpu.VMEM((ts, hw), F32)],
        compiler_params=_cparams(("arbitrary",)),
        name="s5_sample",
    )(us, us, s0, kb, pb, qb, adec)
    half = g * p
    fsp = fsp.reshape(n_p, sw)
    st = lambda a, n: a.reshape(n, g, p)
    return (yp, ys, st(fsp[:, :half], n_p), st(fsp[:, half:], n_p),
            st(fss[:, :half], n_s), st(fss[:, half:], n_s))


def _lambda(lp_ref, lam_init):
    lp = lp_ref[...]
    s1 = jnp.sum(lp[0:1, :] * lp[1:2, :], axis=-1, keepdims=True)
    s2 = jnp.sum(lp[2:3, :] * lp[3:4, :], axis=-1, keepdims=True)
    return jnp.exp(s1) - jnp.exp(s2) + lam_init


def _attn_kernel(qi_ref, ki_ref, q_ref, k_ref, v_ref, lp_ref, hg_ref, *rest, tile, lam_init, cast_blocks):
    n_cast = len(cast_blocks)
    cast_in = rest[:n_cast]
    o_ref = rest[n_cast]
    cast_out = rest[n_cast + 1:2 * n_cast + 1]
    m_scr, acc_scr = rest[2 * n_cast + 1:]
    t = pl.program_id(1)
    qi = qi_ref[t]
    ki = ki_ref[t]
    vd = ATT_V_DIM

    step = pl.program_id(0) * pl.num_programs(1) + t
    for src, dst, n_blocks in zip(cast_in, cast_out, cast_blocks):
        @pl.when(step < n_blocks)
        def _(src=src, dst=dst):
            dst[...] = src[...].astype(dst.dtype)

    @pl.when(ki == 0)
    def _():
        m_scr[...] = jnp.full(m_scr.shape, NEG_INF, F32)
        acc_scr[...] = jnp.zeros(acc_scr.shape, F32)

    def accumulate(masked):
        lane = lax.broadcasted_iota(jnp.int32, (tile, vd), 1)
        ones = jnp.ones((tile, vd), BF16)
        if masked:
            mask = (lax.broadcasted_iota(jnp.int32, (tile, tile), 1)
                    <= lax.broadcasted_iota(jnp.int32, (tile, tile), 0))
        nt = (((1,), (1,)), ((), ()))
        for h in range(ATT_HEADS):
            cols = slice(h * vd, (h + 1) * vd)
            q = q_ref[:, cols]
            k = k_ref[:, cols]
            v1 = jnp.concatenate([v_ref[:, cols], ones], axis=1)
            zero = jnp.zeros_like(q)
            for c in range(2):
                qm = jnp.where((lane >= ATT_HEAD_DIM) == bool(c), q, zero)
                s = lax.dot_general(qm, k, nt, preferred_element_type=F32)
                if masked:
                    s = jnp.where(mask, s, NEG_INF)
                idx = 2 * h + c
                m_old = m_scr[idx]
                m_row = jnp.max(s, axis=-1, keepdims=True)
                m_new = jnp.maximum(m_old, jnp.broadcast_to(m_row, m_old.shape))
                alpha = jnp.exp(m_old - m_new)
                p = jnp.exp(s - jnp.concatenate([m_new] * (tile // vd), axis=1)).astype(BF16)
                acc_scr[idx] = jnp.concatenate([alpha, alpha], axis=1) * acc_scr[idx] + _bdot(p, v1)
                m_scr[idx] = m_new

    @pl.when(ki < qi)
    def _():
        accumulate(False)

    @pl.when(ki == qi)
    def _():
        accumulate(True)
        lam = _lambda(lp_ref, lam_init)
        hg = hg_ref[...]
        for h in range(ATT_HEADS):
            a1 = acc_scr[2 * h]
            a2 = acc_scr[2 * h + 1]
            o = a1[:, :vd] / a1[:, vd:] - lam * (a2[:, :vd] / a2[:, vd:])
            o_ref[:, h * vd:(h + 1) * vd] = (_rms_rows(o, hg) * (1.0 - lam_init)).astype(o_ref.dtype)


def _cast_block_rows(rows, n_steps):
    for br in range(BF16_SUBLANES, rows + 1, BF16_SUBLANES):
        if rows % br == 0 and rows // br <= n_steps:
            return br
    raise ValueError((rows, n_steps))


def _attn_prompt(q, k, v, lp, hg, n_p, seq, lam_init, casts=()):
    tile = min(ATT_TILE, seq)
    assert seq % tile == 0 and tile % ATT_V_DIM == 0
    nq = seq // tile
    pairs = [(i, j) for i in range(nq) for j in range(i + 1)]
    n_steps = n_p * len(pairs)
    qi_tab = jnp.asarray([a for a, _ in pairs], jnp.int32)
    ki_tab = jnp.asarray([b for _, b in pairs], jnp.int32)
    qspec = pl.BlockSpec((tile, ATT_WIDTH), lambda b, t, qi, ki: (b * nq + qi[t], 0))
    kspec = pl.BlockSpec((tile, ATT_WIDTH), lambda b, t, qi, ki: (b * nq + ki[t], 0))
    cast_specs, cast_blocks = [], []
    for w in casts:
        br = _cast_block_rows(w.shape[0], n_steps)
        nblk = w.shape[0] // br
        cast_blocks.append(nblk)
        cast_specs.append(pl.BlockSpec(
            (br, w.shape[1]),
            lambda b, t, qi, ki, nblk=nblk: (jnp.minimum(b * len(pairs) + t, nblk - 1), 0)))
    grid_spec = pltpu.PrefetchScalarGridSpec(
        num_scalar_prefetch=2,
        grid=(n_p, len(pairs)),
        in_specs=[qspec, kspec, kspec,
                  pl.BlockSpec((4, ATT_HEAD_DIM), lambda b, t, qi, ki: (0, 0)),
                  pl.BlockSpec((1, ATT_V_DIM), lambda b, t, qi, ki: (0, 0))] + cast_specs,
        out_specs=[qspec] + cast_specs,
        scratch_shapes=[pltpu.VMEM((2 * ATT_HEADS, tile, ATT_V_DIM), F32),
                        pltpu.VMEM((2 * ATT_HEADS, tile, 2 * ATT_V_DIM), F32)],
    )
    return pl.pallas_call(
        functools.partial(_attn_kernel, tile=tile, lam_init=lam_init, cast_blocks=tuple(cast_blocks)),
        out_shape=[jax.ShapeDtypeStruct((n_p * seq, ATT_WIDTH), BF16)]
        + [jax.ShapeDtypeStruct(w.shape, BF16) for w in casts],
        grid_spec=grid_spec,
        compiler_params=_cparams(("arbitrary", "arbitrary")),
        name="attn_prompt",
    )(qi_tab, ki_tab, q, k, v, lp, hg, *casts)


def _attn_sample_kernel(pt_ref, q_ref, kn_ref, vn_ref, lp_ref, hg_ref, *rest, n_pages, dec, lam_init):
    del pt_ref
    kp = rest[:n_pages]
    vp = rest[n_pages:2 * n_pages]
    o_ref = rest[2 * n_pages]
    page = kp[0].shape[1]
    grp = 2 * dec
    n_rows = ATT_HEADS * grp
    q = q_ref[...].astype(F32)
    qt = jnp.concatenate([q] * (ATT_HEADS * 2), axis=0)
    r = lax.broadcasted_iota(jnp.int32, (n_rows, QK_WIDTH), 0)
    c = lax.broadcasted_iota(jnp.int32, (n_rows, QK_WIDTH), 1)
    qb = jnp.where(r // dec == c // ATT_HEAD_DIM, qt, 0.0).astype(BF16)
    nt = (((1,), (1,)), ((), ()))
    kt_past = jnp.concatenate([kp[j][...].astype(BF16) for j in range(n_pages)], axis=1)
    s_past = _bdot(qb, kt_past)
    s_new = lax.dot_general(qb, kn_ref[...], nt, preferred_element_type=F32)
    rn = lax.broadcasted_iota(jnp.int32, (n_rows, dec), 0) % dec
    cn = lax.broadcasted_iota(jnp.int32, (n_rows, dec), 1)
    s_new = jnp.where(cn <= rn, s_new, NEG_INF)
    m = jnp.maximum(jnp.max(s_new, axis=-1, keepdims=True), jnp.max(s_past, axis=-1, keepdims=True))
    p_new = jnp.exp(s_new - m)
    p_past = jnp.exp(s_past - m)
    l = jnp.sum(p_new, axis=-1, keepdims=True) + jnp.sum(p_past, axis=-1, keepdims=True)
    acc_new = _bdot(p_new.astype(BF16), vn_ref[...])
    lam = _lambda(lp_ref, lam_init)
    hg = hg_ref[...]
    for h in range(ATT_HEADS):
        rows = slice(h * grp, (h + 1) * grp)
        cols = slice(h * ATT_V_DIM, (h + 1) * ATT_V_DIM)
        v_past = jnp.concatenate([vp[j][pl.ds(h, page, stride=ATT_HEADS), :].astype(BF16)
                                  for j in range(n_pages)], axis=0)
        acc = acc_new[rows, cols] + _bdot(p_past[rows, :].astype(BF16), v_past)
        acc = acc / l[rows, :]
        o = acc[:dec] - lam * acc[dec:]
        o_ref[:, cols] = (_rms_rows(o, hg) * (1.0 - lam_init)).astype(o_ref.dtype)


def _attn_sample(q, kn, vn, cache_kt, cache_vr, layer, page_table, lp, hg, lam_init):
    n_s, dec, _ = q.shape
    n_pages = page_table.shape[1]
    page = cache_kt.shape[3]
    tok = pl.BlockSpec((None, dec, QK_WIDTH), lambda n, pt: (n, 0, 0))
    kpages = [pl.BlockSpec((None, None, QK_WIDTH, page), lambda n, pt, j=j: (layer, pt[n, j], 0, 0))
              for j in range(n_pages)]
    vpages = [pl.BlockSpec((None, None, page * ATT_HEADS, ATT_V_DIM),
                           lambda n, pt, j=j: (layer, pt[n, j], 0, 0)) for j in range(n_pages)]
    grid_spec = pltpu.PrefetchScalarGridSpec(
        num_scalar_prefetch=1,
        grid=(n_s,),
        in_specs=[tok, tok, tok,
                  pl.BlockSpec((4, ATT_HEAD_DIM), lambda n, pt: (0, 0)),
                  pl.BlockSpec((1, ATT_V_DIM), lambda n, pt: (0, 0))] + kpages + vpages,
        out_specs=tok,
    )
    return pl.pallas_call(
        functools.partial(_attn_sample_kernel, n_pages=n_pages, dec=dec, lam_init=lam_init),
        out_shape=jax.ShapeDtypeStruct((n_s, dec, ATT_WIDTH), BF16),
        grid_spec=grid_spec,
        compiler_params=_cparams(("parallel",)),
        name="attn_sample",
    )(page_table, q, kn, vn, lp, hg, *([cache_kt] * n_pages), *([cache_vr] * n_pages))


def _merge_kernel(xp_ref, xs_ref, up_ref, us_ref, cb_ref, vin_ref, halo_ref, gate_ref,
                  yrp_ref, yrs_ref, ybp_ref, ybs_ref, vm1s_ref, vm2s_ref,
                  d_ref, wglu_ref, bglu_ref, cw_ref, wssm_ref, watt_ref, wconv_ref, wout_ref,
                  op_ref, os_ref, *, n_prompt_tiles, tiles_per_seq):
    i = pl.program_id(0)
    is_prompt = i < n_prompt_tiles
    yraw = jnp.where(is_prompt, yrp_ref[...], yrs_ref[...])
    yb = jnp.where(is_prompt, ybp_ref[...], ybs_ref[...])
    y = jax.nn.gelu(yraw + d_ref[...] * _read_group(up_ref, us_ref, is_prompt))
    ya = y * jax.nn.sigmoid(_bdot(y.astype(BF16), wglu_ref[...]) + bglu_ref[...])

    vin = vin_ref[...]
    row = lax.broadcasted_iota(jnp.int32, vin.shape, 0)
    halo = jnp.where(i % tiles_per_seq == 0, 0.0, halo_ref[...])
    h1 = jnp.broadcast_to(halo[SUBLANES - 1:SUBLANES, :], vin.shape)
    h2 = jnp.broadcast_to(halo[SUBLANES - 2:SUBLANES - 1, :], vin.shape)
    vm1 = jnp.where(row == 0, h1, pltpu.roll(vin, 1, axis=0))
    vm2 = jnp.where(row == 0, h2, jnp.where(row == 1, h1, pltpu.roll(vin, 2, axis=0)))
    vm1 = jnp.where(is_prompt, vm1, vm1s_ref[...])
    vm2 = jnp.where(is_prompt, vm2, vm2s_ref[...])
    conv = vm2 * cw_ref[0:1, :] + vm1 * cw_ref[1:2, :] + vin * cw_ref[2:3, :]
    yc = cb_ref[...] * conv
    merged = (gate_ref[:, 0:D_MODEL] * _bdot(ya.astype(BF16), wssm_ref[...])
              + gate_ref[:, D_MODEL:2 * D_MODEL] * _bdot(yb, watt_ref[...])
              + gate_ref[:, 2 * D_MODEL:3 * D_MODEL] * _bdot(yc.astype(BF16), wconv_ref[...]))
    x_new = _read_group(xp_ref, xs_ref, is_prompt) + _bdot(merged.astype(BF16), wout_ref[...])
    _write_group(op_ref, os_ref, is_prompt, x_new)


def _merge(xp, xs, up, us, cb, vin, gates, yr_p, yr_s, yb_p, yb_s, vm1_s, vm2_s,
           d, wglu, bglu, cw, wssm, watt, wconv, wout, tiles_per_seq):
    tm = ROW_TILE
    tp, ts = xp.shape[0], xs.shape[0]
    npt = tp // tm
    row = lambda w: pl.BlockSpec((tm, w), lambda i: (i, 0))
    prow = lambda w: _group_specs(tm, w, npt)[0]
    srow = lambda w: _group_specs(tm, w, npt)[1]
    halo = pl.BlockSpec((SUBLANES, CONV_WIDTH), lambda i: (jnp.maximum(i * (tm // SUBLANES) - 1, 0), 0))
    return pl.pallas_call(
        functools.partial(_merge_kernel, n_prompt_tiles=npt, tiles_per_seq=tiles_per_seq),
        out_shape=[jax.ShapeDtypeStruct((tp, D_MODEL), F32), jax.ShapeDtypeStruct((ts, D_MODEL), F32)],
        grid=((tp + ts) // tm,),
        in_specs=[prow(D_MODEL), srow(D_MODEL), prow(SSM_WIDTH), srow(SSM_WIDTH),
                  row(CONV_WIDTH), row(CONV_WIDTH), halo,
                  row(N_BRANCH * D_MODEL),
                  prow(SSM_WIDTH), srow(SSM_WIDTH), prow(ATT_WIDTH), srow(ATT_WIDTH),
                  srow(CONV_WIDTH), srow(CONV_WIDTH),
                  _const_spec((1, SSM_WIDTH)), _const_spec((SSM_WIDTH, SSM_WIDTH)),
                  _const_spec((1, SSM_WIDTH)), _const_spec((CONV_K, CONV_WIDTH)),
                  _const_spec((SSM_WIDTH, D_MODEL)), _const_spec((ATT_WIDTH, D_MODEL)),
                  _const_spec((CONV_WIDTH, D_MODEL)), _const_spec((D_MODEL, D_MODEL))],
        out_specs=[prow(D_MODEL), srow(D_MODEL)],
        compiler_params=_cparams(("arbitrary",)),
        name="merge",
    )(xp, xs, up, us, cb, vin, vin, gates, yr_p, yr_s, yb_p, yb_s, vm1_s, vm2_s,
      d, wglu, bglu, cw, wssm, watt, wconv, wout)


def _swiglu(h, w1, w3, w2):
    a = _bdot(h, w1)
    b = _bdot(h, w3)
    return _bdot((jax.nn.silu(a) * b).astype(BF16), w2)


def _ffn_kernel(xp_ref, xs_ref, g_ref, w1_ref, w3_ref, w2_ref, op_ref, os_ref, *, n_prompt_tiles):
    is_prompt = pl.program_id(0) < n_prompt_tiles
    x = _read_group(xp_ref, xs_ref, is_prompt)
    h = _rms_rows(x, g_ref[...]).astype(BF16)
    _write_group(op_ref, os_ref, is_prompt, x + _swiglu(h, w1_ref[...], w3_ref[...], w2_ref[...]))


def _ffn(xp, xs, g, w1, w3, w2):
    tm = ROW_TILE
    tp, ts = xp.shape[0], xs.shape[0]
    npt = tp // tm
    d_ff = w1.shape[1]
    rows = _group_specs(tm, D_MODEL, npt)
    return pl.pallas_call(
        functools.partial(_ffn_kernel, n_prompt_tiles=npt),
        out_shape=[jax.ShapeDtypeStruct((tp, D_MODEL), F32), jax.ShapeDtypeStruct((ts, D_MODEL), F32)],
        grid=((tp + ts) // tm,),
        in_specs=rows + [_const_spec((1, D_MODEL)), _const_spec((D_MODEL, d_ff)),
                         _const_spec((D_MODEL, d_ff)), _const_spec((d_ff, D_MODEL))],
        out_specs=rows,
        compiler_params=_cparams(("arbitrary",)),
        name="ffn",
    )(xp, xs, g, w1, w3, w2)


def _router_kernel(xp_ref, xs_ref, g_ref, wr_ref, tri_ref, h_ref, meta_ref, gate_ref, cnt_ref, carry,
                   *, n_prompt_tiles):
    @pl.when(pl.program_id(0) == 0)
    def _():
        carry[...] = jnp.zeros(carry.shape, F32)

    x = _read_group(xp_ref, xs_ref, pl.program_id(0) < n_prompt_tiles)
    h = _rms_rows(x, g_ref[...])
    h_ref[...] = _pack_halves(h)
    logits = jnp.dot(h, wr_ref[...], preferred_element_type=F32, precision=lax.Precision.HIGHEST)
    lane = lax.broadcasted_iota(jnp.int32, logits.shape, 1)
    logits = jnp.where(lane < N_EXPERTS, logits, -jnp.inf)
    big = jnp.int32(logits.shape[1])
    m1 = jnp.max(logits, axis=-1, keepdims=True)
    i1 = jnp.min(jnp.where(logits == m1, lane, big), axis=-1, keepdims=True)
    rest = jnp.where(lane == i1, -jnp.inf, logits)
    m2 = jnp.max(rest, axis=-1, keepdims=True)
    i2 = jnp.min(jnp.where(rest == m2, lane, big), axis=-1, keepdims=True)
    e = jnp.exp(m2 - m1)
    g1 = 1.0 / (1.0 + e)
    g2 = e / (1.0 + e)
    o1 = lane == i1
    o2 = lane == i2
    chosen = jnp.where(o1 | o2, 1.0, 0.0)
    base = _bdot(tri_ref[...], chosen.astype(BF16)) + carry[...]
    r1 = jnp.sum(jnp.where(o1, base, 0.0), axis=-1, keepdims=True).astype(jnp.int32)
    r2 = jnp.sum(jnp.where(o2, base, 0.0), axis=-1, keepdims=True).astype(jnp.int32)
    carry[...] = carry[...] + jnp.sum(chosen, axis=0, keepdims=True)
    cnt_ref[...] = carry[...]
    meta_ref[...] = jnp.where(lane == 0, i1, jnp.where(lane == 1, i2,
                              jnp.where(lane == 2, r1, jnp.where(lane == 3, r2, 0))))
    gate_ref[...] = jnp.where(lane == 0, g1, jnp.where(lane == 1, g2, 0.0))


def _router(xp, xs, g, wr_pad):
    tm = ROW_TILE
    t = xp.shape[0] + xs.shape[0]
    npt = xp.shape[0] // tm
    lanes = wr_pad.shape[1]
    tri = jnp.tri(tm, k=-1, dtype=BF16)
    row = lambda w: pl.BlockSpec((tm, w), lambda i: (i, 0))
    return pl.pallas_call(
        functools.partial(_router_kernel, n_prompt_tiles=npt),
        out_shape=[jax.ShapeDtypeStruct((t, D_MODEL // 2), jnp.uint32),
                   jax.ShapeDtypeStruct((t, lanes), jnp.int32),
                   jax.ShapeDtypeStruct((t, lanes), F32),
                   jax.ShapeDtypeStruct((1, lanes), F32)],
        grid=(t // tm,),
        in_specs=_group_specs(tm, D_MODEL, npt) + [_const_spec((1, D_MODEL)), _const_spec((D_MODEL, lanes)),
                                                   _const_spec((tm, tm))],
        out_specs=[row(D_MODEL // 2), row(lanes), row(lanes), _const_spec((1, lanes))],
        scratch_shapes=[pltpu.VMEM((1, lanes), F32)],
        compiler_params=_cparams(("arbitrary",)),
        name="router",
    )(xp, xs, g, wr_pad, tri)


def _sc_mesh():
    return plsc.VectorSubcoreMesh(core_axis_name="c", subcore_axis_name="s")


def _sc_worker_base(per_worker):
    return (lax.axis_index("s") * SC_CORES + lax.axis_index("c")) * per_worker


def _sc_chunk_rows(n, row_bytes):
    assert n % (SC_WORKERS * SUBLANES) == 0
    per_worker = n // SC_WORKERS
    fits = [r for r in range(SUBLANES, per_worker + 1, SUBLANES)
            if per_worker % r == 0 and r * row_bytes <= SC_CHUNK_BYTES]
    return per_worker, fits[-1]


def _sc_scatter_rows(x, idx0, idx1, n_out):
    n, w = x.shape
    per_worker, chunk = _sc_chunk_rows(n, w * x.dtype.itemsize)

    @functools.partial(
        pl.kernel, mesh=_sc_mesh(), out_type=jax.ShapeDtypeStruct((n_out, w), x.dtype),
        scratch_types=[pltpu.VMEM((chunk,), jnp.int32), pltpu.VMEM((chunk,), jnp.int32),
                       pltpu.VMEM((chunk, w), x.dtype), pltpu.SemaphoreType.DMA])
    def scatter(x_hbm, i0_hbm, i1_hbm, out_hbm, i0_v, i1_v, rows_v, sem):
        start = _sc_worker_base(per_worker)

        @pl.loop(0, per_worker // chunk)
        def _(c):
            rows = pl.ds(pl.multiple_of(start + c * chunk, SUBLANES), chunk)
            pltpu.sync_copy(x_hbm.at[rows], rows_v)
            pltpu.sync_copy(i0_hbm.at[rows], i0_v)
            pltpu.sync_copy(i1_hbm.at[rows], i1_v)
            pltpu.async_copy(rows_v, out_hbm.at[i0_v], sem).wait()
            pltpu.async_copy(rows_v, out_hbm.at[i1_v], sem).wait()

    return scatter(x, idx0, idx1)


def _sc_gather_rows(table, idx):
    n, w = idx.shape[0], table.shape[1]
    per_worker, chunk = _sc_chunk_rows(n, w * table.dtype.itemsize)

    @functools.partial(
        pl.kernel, mesh=_sc_mesh(), out_type=jax.ShapeDtypeStruct((n, w), table.dtype),
        scratch_types=[pltpu.VMEM((chunk,), jnp.int32), pltpu.VMEM((chunk, w), table.dtype),
                       pltpu.SemaphoreType.DMA])
    def gather(table_hbm, idx_hbm, out_hbm, idx_v, rows_v, sem):
        start = _sc_worker_base(per_worker)

        @pl.loop(0, per_worker // chunk)
        def _(c):
            rows = pl.ds(pl.multiple_of(start + c * chunk, SUBLANES), chunk)
            pltpu.sync_copy(idx_hbm.at[rows], idx_v)
            pltpu.async_copy(table_hbm.at[idx_v], rows_v, sem).wait()
            pltpu.sync_copy(rows_v, out_hbm.at[rows])

    return gather(table, idx)


def _moe_kernel(be_ref, nv_ref, x_ref, w1_ref, w3_ref, w2_ref, o_ref):
    del be_ref

    @pl.when(pl.program_id(0) < nv_ref[0])
    def _():
        x = _unpack_halves(x_ref[...]).astype(BF16)
        o_ref[...] = _pack_halves(_swiglu(x, w1_ref[...], w3_ref[...], w2_ref[...]))


def _moe_blocks(xs, block_expert, n_valid, w1, w3, w2):
    cap = xs.shape[0]
    tb = MOE_TILE
    d_ff = w1.shape[2]
    grid_spec = pltpu.PrefetchScalarGridSpec(
        num_scalar_prefetch=2,
        grid=(cap // tb,),
        in_specs=[pl.BlockSpec((tb, D_MODEL // 2), lambda i, be, nv: (jnp.minimum(i, nv[0] - 1), 0)),
                  pl.BlockSpec((None, D_MODEL, d_ff), lambda i, be, nv: (be[i], 0, 0)),
                  pl.BlockSpec((None, D_MODEL, d_ff), lambda i, be, nv: (be[i], 0, 0)),
                  pl.BlockSpec((None, d_ff, D_MODEL), lambda i, be, nv: (be[i], 0, 0))],
        out_specs=pl.BlockSpec((tb, D_MODEL // 2), lambda i, be, nv: (i, 0)),
    )
    return pl.pallas_call(
        _moe_kernel,
        out_shape=jax.ShapeDtypeStruct((cap, D_MODEL // 2), jnp.uint32),
        grid_spec=grid_spec,
        compiler_params=_cparams(("arbitrary",)),
        name="moe",
    )(block_expert, n_valid, xs, w1, w3, w2)


def _combine_kernel(xp_ref, xs_ref, a0_ref, a1_ref, gate_ref, op_ref, os_ref, *, n_prompt_tiles):
    is_prompt = pl.program_id(0) < n_prompt_tiles
    gate = gate_ref[...]
    y = (_read_group(xp_ref, xs_ref, is_prompt)
         + gate[:, 0:1] * _unpack_halves(a0_ref[...]) + gate[:, 1:2] * _unpack_halves(a1_ref[...]))
    _write_group(op_ref, os_ref, is_prompt, y)


def _combine(xp, xs, picked, gate):
    tm = ROW_TILE
    tp, ts = xp.shape[0], xs.shape[0]
    nt = (tp + ts) // tm
    rows = _group_specs(tm, D_MODEL, tp // tm)
    return pl.pallas_call(
        functools.partial(_combine_kernel, n_prompt_tiles=tp // tm),
        out_shape=[jax.ShapeDtypeStruct((tp, D_MODEL), F32), jax.ShapeDtypeStruct((ts, D_MODEL), F32)],
        grid=(nt,),
        in_specs=rows + [pl.BlockSpec((tm, D_MODEL // 2), lambda i: (i, 0)),
                         pl.BlockSpec((tm, D_MODEL // 2), lambda i: (i + nt, 0)),
                         pl.BlockSpec((tm, gate.shape[1]), lambda i: (i, 0))],
        out_specs=rows,
        compiler_params=_cparams(("arbitrary",)),
        name="moe_combine",
    )(xp, xs, picked, picked, gate)


def _moe(xp, xs, g, wr, w1, w3, w2):
    t = xp.shape[0] + xs.shape[0]
    tb = MOE_TILE
    lanes = 128
    wr_pad = jnp.pad(wr, ((0, 0), (0, lanes - N_EXPERTS)))
    h, meta, gate, cnt = _router(xp, xs, g, wr_pad)
    counts = cnt[0, :N_EXPERTS].astype(jnp.int32)
    padded = (counts + tb - 1) // tb * tb
    pad_ends = jnp.cumsum(padded)
    pad_starts = pad_ends - padded
    experts = jnp.arange(N_EXPERTS, dtype=jnp.int32)
    slot = lambda e, r: jnp.sum(jnp.where(e[:, None] == experts, pad_starts, 0), axis=1) + r
    dest0 = slot(meta[:, 0], meta[:, 2])
    dest1 = slot(meta[:, 1], meta[:, 3])
    nb = -(-(t * TOP_K + N_EXPERTS * (tb - 1)) // tb)
    block_expert = jnp.minimum(
        jnp.sum(pad_ends[None, :] <= (jnp.arange(nb, dtype=jnp.int32) * tb)[:, None], axis=1),
        N_EXPERTS - 1).astype(jnp.int32)
    n_valid = (pad_ends[-1:] // tb).astype(jnp.int32)
    slots = _sc_scatter_rows(h, dest0, dest1, nb * tb)
    out = _moe_blocks(slots, block_expert, n_valid, w1, w3, w2)
    picked = _sc_gather_rows(out, jnp.concatenate([dest0, dest1]))
    return _combine(xp, xs, picked, gate)


def kernel(x_prompt, x_sample, cache_k, cache_v, page_table, state_ssm_re, state_ssm_im, state_conv,
           norm_mix_g, norm_ffn_g, w_in, ssm_a_re, ssm_a_im, ssm_log_dt, ssm_b_re, ssm_b_im,
           ssm_c_re, ssm_c_im, ssm_d, ssm_w_glu, ssm_b_glu, q_norm_g, k_norm_g,
           lambda_q1, lambda_k1, lambda_q2, lambda_k2, head_norm_g, conv_w,
           w_br_ssm, w_br_att, w_br_conv, w_out, ffn_w1, ffn_w3, ffn_w2,
           router_w, moe_w1, moe_w3, moe_w2):
    n_p, seq, _ = x_prompt.shape
    n_s, dec, _ = x_sample.shape
    depth = w_in.shape[0]
    tp = n_p * seq
    ts = n_s * dec
    assert seq % ROW_TILE == 0 and ts % ROW_TILE == 0 and seq >= CONV_K - 1
    pool, page = cache_k.shape[1], cache_k.shape[2]
    cache_kt = cache_k.reshape(depth, pool, page, QK_WIDTH).transpose(0, 1, 3, 2)
    cache_vr = cache_v.reshape(depth, pool, page * ATT_HEADS, ATT_V_DIM)
    seg = jnp.kron(jnp.eye(QK_WIDTH // ATT_HEAD_DIM, dtype=F32),
                   jnp.full((ATT_HEAD_DIM, ATT_HEAD_DIM), 1.0 / ATT_HEAD_DIM, F32)).astype(BF16)
    n_rep = QK_WIDTH // ATT_HEAD_DIM

    xp = x_prompt.reshape(tp, D_MODEL)
    xs = x_sample.reshape(ts, D_MODEL)
    all_tables = _s5_tables(ssm_a_re, ssm_a_im, ssm_log_dt, ssm_b_re, ssm_b_im, ssm_c_re, ssm_c_im, dec)
    kv_prompt = None
    srp, sip, cvp = [], [], []
    ks, vs, srs, sis, cvs = [], [], [], [], []
    for l in range(depth):
        lam_init = 0.8 - 0.6 * math.exp(-0.3 * l)
        q, kb, vb, cb, vin, gates, kt, vr, k_s, v_s, u_p, u_s = _inproj(
            xp, xs, norm_mix_g[l][None], w_in[l].astype(BF16),
            jnp.tile(q_norm_g[l], n_rep)[None], jnp.tile(k_norm_g[l], n_rep)[None], seg,
            l, depth, n_p, seq, kv_prompt)
        kv_prompt = (kt, vr)

        tables = [tab[l] for tab in all_tables]
        yr_p, yr_s, p_re, p_im, s_re, s_im = _s5(u_p, u_s, state_ssm_re[l], state_ssm_im[l], tables,
                                                  n_p, seq, n_s, dec)

        lp = jnp.stack([lambda_q1[l], lambda_k1[l], lambda_q2[l], lambda_k2[l]])
        hg = head_norm_g[l][None]
        casts = ()
        if l + 1 < depth and (l + 1) % 2 == 1:
            e = (l + 1) // 2
            casts = (moe_w1[e].reshape(-1, moe_w1.shape[-1]), moe_w3[e].reshape(-1, moe_w3.shape[-1]),
                     moe_w2[e].reshape(-1, moe_w2.shape[-1]))
        yb_p, *cast_out = _attn_prompt(q, kb, vb, lp, hg, n_p, seq, lam_init, casts)
        if casts:
            moe_bf = [c.reshape(w.shape[1:]) for c, w in zip(cast_out, (moe_w1, moe_w3, moe_w2))]
        yb_s = _attn_sample(q[tp:].reshape(n_s, dec, QK_WIDTH), kb[tp:].reshape(n_s, dec, QK_WIDTH),
                            vb[tp:].reshape(n_s, dec, ATT_WIDTH), cache_kt, cache_vr, l, page_table,
                            lp, hg, lam_init).reshape(ts, ATT_WIDTH)

        ext_s = jnp.concatenate([state_conv[l], vin[tp:].reshape(n_s, dec, CONV_WIDTH)], axis=1)
        vm1_s = ext_s[:, 1:1 + dec].reshape(ts, CONV_WIDTH)
        vm2_s = ext_s[:, 0:dec].reshape(ts, CONV_WIDTH)
        xp, xs = _merge(xp, xs, u_p, u_s, cb, vin, gates, yr_p, yr_s, yb_p, yb_s, vm1_s, vm2_s,
                        ssm_d[l][None], ssm_w_glu[l].astype(BF16), ssm_b_glu[l][None], conv_w[l],
                        w_br_ssm[l].astype(BF16), w_br_att[l].astype(BF16), w_br_conv[l].astype(BF16),
                        w_out[l].astype(BF16), seq // ROW_TILE)

        i = l // 2
        if l % 2 == 0:
            xp, xs = _ffn(xp, xs, norm_ffn_g[l][None], ffn_w1[i].astype(BF16), ffn_w3[i].astype(BF16),
                          ffn_w2[i].astype(BF16))
        else:
            xp, xs = _moe(xp, xs, norm_ffn_g[l][None], router_w[i], *moe_bf)

        srp.append(p_re); sip.append(p_im)
        cvp.append(vin[:tp].reshape(n_p, seq, CONV_WIDTH)[:, seq - (CONV_K - 1):])
        ks.append(k_s.reshape(n_s, dec, ATT_HEADS, 2, ATT_HEAD_DIM))
        vs.append(v_s.reshape(n_s, dec, ATT_HEADS, ATT_V_DIM))
        srs.append(s_re); sis.append(s_im); cvs.append(ext_s[:, dec:])

    kt, vr = kv_prompt
    k_prompt = kt.reshape(depth, n_p, ATT_HEADS, 2, ATT_HEAD_DIM, seq).transpose(0, 1, 5, 2, 3, 4)
    v_prompt = vr.reshape(depth, n_p, seq, ATT_HEADS, ATT_V_DIM)
    return (xp.reshape(n_p, seq, D_MODEL), xs.reshape(n_s, dec, D_MODEL),
            k_prompt, v_prompt, jnp.stack(srp), jnp.stack(sip), jnp.stack(cvp),
            jnp.stack(ks), jnp.stack(vs), jnp.stack(srs), jnp.stack(sis), jnp.stack(cvs))
```

```python
import functools
import math

import jax
import jax.numpy as jnp
from jax import lax
from jax.experimental import pallas as pl
from jax.experimental.pallas import tpu as pltpu
from jax.experimental.pallas import tpu_sc as plsc

F32 = jnp.float32
BF16 = jnp.bfloat16

D_MODEL = 1024
SSM_WIDTH = 256
SSM_GROUP = 16
SSM_GROUPS = SSM_WIDTH // SSM_GROUP
SSM_STATE = 64
ATT_HEADS = 4
ATT_HEAD_DIM = 64
ATT_V_DIM = 2 * ATT_HEAD_DIM
QK_WIDTH = ATT_HEADS * 2 * ATT_HEAD_DIM
ATT_WIDTH = ATT_HEADS * ATT_V_DIM
CONV_WIDTH = 256
CONV_K = 3
N_BRANCH = 3
N_EXPERTS = 8
TOP_K = 2
EPS = 1e-6
NEG_INF = -1e30
LOG2_E = 1.4426950408889634

C_U = 0
C_Q = C_U + SSM_WIDTH
C_K = C_Q + QK_WIDTH
C_V = C_K + QK_WIDTH
C_CB = C_V + ATT_WIDTH
C_CC = C_CB + CONV_WIDTH
C_CH = C_CC + CONV_WIDTH
C_G = C_CH + CONV_WIDTH
IN_COLS = C_G + N_BRANCH * D_MODEL

SSM_CHUNK = 8
SUBLANES = 8
BF16_SUBLANES = 16
ROW_TILE = 512
ATT_TILE = 512
MOE_TILE = 256
SC_CORES = 2
SC_WORKERS = SC_CORES * 16
SC_CHUNK_BYTES = 192 * 1024
VMEM_LIMIT = 56 * 1024 * 1024


def _cparams(sem):
    return pltpu.CompilerParams(dimension_semantics=sem, vmem_limit_bytes=VMEM_LIMIT)


def _const_spec(shape):
    nd = len(shape)
    return pl.BlockSpec(shape, lambda *_: (0,) * nd)


def _bdot(a, b):
    return jnp.dot(a, b, preferred_element_type=F32)


def _rms_rows(x, g):
    ms = jnp.mean(x * x, axis=-1, keepdims=True)
    return x * lax.rsqrt(ms + EPS) * g


def _group_specs(tm, w, n_prompt_tiles):
    return [pl.BlockSpec((tm, w), lambda i: (jnp.minimum(i, n_prompt_tiles - 1), 0)),
            pl.BlockSpec((tm, w), lambda i: (jnp.maximum(i - n_prompt_tiles, 0), 0))]


def _read_group(p_ref, s_ref, is_prompt):
    return jnp.where(is_prompt, p_ref[...], s_ref[...])


def _write_group(p_ref, s_ref, is_prompt, val):
    @pl.when(is_prompt)
    def _():
        p_ref[...] = val

    @pl.when(jnp.logical_not(is_prompt))
    def _():
        s_ref[...] = val


def _pack_halves(x):
    half = x.shape[1] // 2
    bits = lambda t: lax.bitcast_convert_type(t.astype(BF16).astype(F32), jnp.uint32)
    return (bits(x[:, :half]) >> 16) | (bits(x[:, half:]) & jnp.uint32(0xFFFF0000))


def _unpack_halves(p):
    lo = lax.bitcast_convert_type(p << 16, F32)
    hi = lax.bitcast_convert_type(p & jnp.uint32(0xFFFF0000), F32)
    return jnp.concatenate([lo, hi], axis=1)


def _segment_rms(z, g, seg):
    ms = _bdot((z * z).astype(BF16), seg)
    return z * lax.rsqrt(ms + EPS) * g


def _inproj_kernel(xp_ref, xs_ref, g_ref, w_ref, qg_ref, kg_ref, seg_ref, *rest,
                   n_prompt_tiles, n_prev, layer):
    (q_ref, kb_ref, vb_ref, cb_ref, vin_ref, gate_ref,
     kt_ref, vr_ref, ks_ref, vs_ref, up_ref, us_ref) = rest[n_prev:]
    tm = q_ref.shape[0]
    is_prompt = pl.program_id(0) < n_prompt_tiles
    h = _rms_rows(_read_group(xp_ref, xs_ref, is_prompt), g_ref[...]).astype(BF16)

    def proj(a, b):
        return _bdot(h, w_ref[:, a:b])

    seg = seg_ref[...]
    _write_group(up_ref, us_ref, is_prompt, proj(C_U, C_Q))
    qn = _segment_rms(proj(C_Q, C_K), qg_ref[...], seg)
    q_ref[...] = (qn * (ATT_HEAD_DIM ** -0.5 * LOG2_E)).astype(BF16)
    kn = _segment_rms(proj(C_K, C_V), kg_ref[...], seg)
    kb_ref[...] = kn.astype(BF16)
    v = proj(C_V, C_CB)
    vb_ref[...] = v.astype(BF16)
    cb_ref[...] = proj(C_CB, C_CC)
    vin_ref[...] = proj(C_CC, C_CH) * proj(C_CH, C_G)
    for j in range(N_BRANCH):
        a = C_G + j * D_MODEL
        gate_ref[:, j * D_MODEL:(j + 1) * D_MODEL] = jax.nn.sigmoid(proj(a, a + D_MODEL)).astype(gate_ref.dtype)

    @pl.when(is_prompt)
    def _():
        if n_prev:
            kt_l, vr_l = kt_ref, vr_ref
        else:
            kt_l, vr_l = kt_ref.at[layer], vr_ref.at[layer]
            for other in range(kt_ref.shape[0]):
                if other != layer:
                    kt_ref[other] = jnp.zeros(kt_ref.shape[1:], F32)
                    vr_ref[other] = jnp.zeros(vr_ref.shape[1:], F32)
        kt_l[...] = kn.T
        for hd in range(ATT_HEADS):
            vr_l[pl.ds(hd, tm, stride=ATT_HEADS), :] = v[:, hd * ATT_V_DIM:(hd + 1) * ATT_V_DIM]

    @pl.when(jnp.logical_not(is_prompt))
    def _():
        ks_ref[...] = kn
        vs_ref[...] = v


def _inproj(xp, xs, g, w_bf, qg, kg, seg, layer, depth, n_p, seq, prev):
    tm = ROW_TILE
    tp, ts = xp.shape[0], xs.shape[0]
    t = tp + ts
    npt = tp // tm
    tps = seq // tm
    row = lambda w: pl.BlockSpec((tm, w), lambda i: (i, 0))
    pc = lambda i: jnp.minimum(i, npt - 1)
    outs = [(QK_WIDTH, BF16), (QK_WIDTH, BF16), (ATT_WIDTH, BF16),
            (CONV_WIDTH, F32), (CONV_WIDTH, F32), (N_BRANCH * D_MODEL, BF16)]
    out_shape = ([jax.ShapeDtypeStruct((t, w), d) for w, d in outs]
                 + [jax.ShapeDtypeStruct((depth, n_p, QK_WIDTH, seq), F32),
                    jax.ShapeDtypeStruct((depth, tp * ATT_HEADS, ATT_V_DIM), F32),
                    jax.ShapeDtypeStruct((ts, QK_WIDTH), F32), jax.ShapeDtypeStruct((ts, ATT_WIDTH), F32),
                    jax.ShapeDtypeStruct((tp, SSM_WIDTH), F32), jax.ShapeDtypeStruct((ts, SSM_WIDTH), F32)])
    srow = _group_specs(tm, QK_WIDTH, npt)[1]
    prev = () if prev is None else tuple(prev)
    lead, at = (None, layer) if prev else (depth, 0)
    out_specs = ([row(w) for w, _ in outs]
                 + [pl.BlockSpec((lead, None, QK_WIDTH, tm), lambda i: (at, pc(i) // tps, 0, pc(i) % tps)),
                    pl.BlockSpec((lead, tm * ATT_HEADS, ATT_V_DIM), lambda i: (at, pc(i), 0)),
                    srow, srow] + _group_specs(tm, SSM_WIDTH, npt))
    n_in = 7
    return pl.pallas_call(
        functools.partial(_inproj_kernel, n_prompt_tiles=npt, n_prev=len(prev), layer=layer),
        out_shape=out_shape,
        grid=(t // tm,),
        in_specs=(_group_specs(tm, D_MODEL, npt)
                  + [_const_spec((1, D_MODEL)),
                     pl.BlockSpec((None, D_MODEL, IN_COLS), lambda i: (layer, 0, 0)),
                     _const_spec((1, QK_WIDTH)), _const_spec((1, QK_WIDTH)),
                     _const_spec((QK_WIDTH, QK_WIDTH))]
                  + [pl.BlockSpec(memory_space=pl.ANY)] * len(prev)),
        out_specs=out_specs,
        input_output_aliases={n_in + j: len(outs) + j for j in range(len(prev))},
        compiler_params=_cparams(("arbitrary",)),
        name="inproj",
    )(xp, xs, g, w_bf, qg, kg, seg, *prev)


def _s5_intra(u, kb_ref, tc):
    rowmod = lax.broadcasted_iota(jnp.int32, u.shape, 0) % tc
    y = _bdot(u.astype(BF16), kb_ref[0])
    for m in range(1, tc):
        um = jnp.where(rowmod >= m, pltpu.roll(u, m, axis=0), 0.0)
        y = y + _bdot(um.astype(BF16), kb_ref[m])
    return y


def _s5_inject(u_refs, pb_ref, k0, n_chunks, tc):
    w = None
    for k in range(tc):
        rows = pl.ds(k, n_chunks, stride=tc)
        uk = jnp.concatenate([r[rows, :] for r in u_refs], axis=1).astype(BF16)
        d = _bdot(uk, pb_ref[k0 + k])
        w = d if w is None else w + d
    return w


def _s5_readout(y_ref, y_scrs, y, s_in, qb_ref, n_chunks, tc):
    lanes = y_scrs[0].shape[1]
    for h, scr in enumerate(y_scrs):
        scr[...] = y[:, h * lanes:(h + 1) * lanes]
    sb = s_in.astype(BF16)
    for k in range(tc):
        rows = pl.ds(k, n_chunks, stride=tc)
        yk = _bdot(sb, qb_ref[k])
        for h, scr in enumerate(y_scrs):
            scr[rows, :] = scr[rows, :] + yk[:, h * lanes:(h + 1) * lanes]
    y_ref[...] = jnp.concatenate([scr[...] for scr in y_scrs], axis=1)


def _cmul_add(a_re, a_im, s, w, half):
    s_re, s_im = s[:, :half], s[:, half:]
    return jnp.concatenate([a_re * s_re - a_im * s_im, a_re * s_im + a_im * s_re], axis=1) + w


def _s5_prompt_kernel(ua_ref, ub_ref, kb_ref, pb_ref, qb_ref, a_ref, y_ref, fs_ref,
                      carry, w_fold, s_fold, ya_scr, yb_scr, *, tc):
    n_b, tm, hw = ua_ref.shape
    n_chunks = tm // tc
    rows = n_b * n_chunks
    fold = carry.shape[0] // n_b
    hf = fold // 2

    @pl.when(pl.program_id(0) == 0)
    def _():
        carry[...] = jnp.zeros(carry.shape, F32)

    u = jnp.concatenate([ua_ref[...].reshape(n_b * tm, hw), ub_ref[...].reshape(n_b * tm, hw)], axis=1)
    y = _s5_intra(u, kb_ref, tc)
    w = None
    for k in range(tc):
        at_k = pl.ds(k, n_chunks, stride=tc)
        uk = jnp.concatenate([jnp.concatenate([ua_ref[b, at_k, :], ub_ref[b, at_k, :]], axis=1)
                              for b in range(n_b)], axis=0).astype(BF16)
        d = _bdot(uk, pb_ref[k])
        w = d if w is None else w + d
    for r in range(fold):
        w_fold[pl.ds(r, rows, stride=fold), :] = w[:, r * hw:(r + 1) * hw]
    a_re = a_ref[0]
    a_im = a_ref[1]
    for b in range(n_b):
        s = carry[b * fold:(b + 1) * fold, :]
        for c in range(n_chunks):
            at = slice((b * n_chunks + c) * fold, (b * n_chunks + c + 1) * fold)
            s_fold[at, :] = s
            s_re, s_im = s[:hf], s[hf:]
            s = jnp.concatenate([a_re * s_re - a_im * s_im, a_re * s_im + a_im * s_re], axis=0) + w_fold[at, :]
        carry[b * fold:(b + 1) * fold, :] = s
    fs_ref[...] = carry[...]
    s_in = jnp.concatenate([s_fold[pl.ds(r, rows, stride=fold), :] for r in range(fold)], axis=1)
    for h, scr in enumerate((ya_scr, yb_scr)):
        scr[...] = y[:, h * hw:(h + 1) * hw]
    sb = s_in.astype(BF16)
    for k in range(tc):
        at_k = pl.ds(k, rows, stride=tc)
        yk = _bdot(sb, qb_ref[k])
        for h, scr in enumerate((ya_scr, yb_scr)):
            scr[at_k, :] = scr[at_k, :] + yk[:, h * hw:(h + 1) * hw]
    for b in range(n_b):
        y_ref[b] = jnp.concatenate([ya_scr[b * tm:(b + 1) * tm, :], yb_scr[b * tm:(b + 1) * tm, :]], axis=1)


def _s5_sample_kernel(ua_ref, ub_ref, s0_ref, kb_ref, pb_ref, qb_ref, a_ref, y_ref, fs_ref,
                      ya_scr, yb_scr, *, tc, k0):
    n_chunks = ua_ref.shape[0] // tc
    half = a_ref.shape[1]
    y = _s5_intra(jnp.concatenate([ua_ref[...], ub_ref[...]], axis=1), kb_ref, tc)
    w = _s5_inject((ua_ref, ub_ref), pb_ref, k0, n_chunks, tc)
    s0 = s0_ref[...]
    fs_ref[...] = _cmul_add(a_ref[2:3, :], a_ref[3:4, :], s0, w, half)
    _s5_readout(y_ref, (ya_scr, yb_scr), y, s0, qb_ref, n_chunks, tc)


def _s5_tables_kernel(are_ref, aim_ref, ldt_ref, bre_ref, bim_ref, cre_ref, cim_ref,
                      kb_ref, pb_ref, qb_ref, a_ref, *, tc, dec):
    are = are_ref[...]
    aim = aim_ref[...]
    dt = jnp.exp(ldt_ref[...])

    def power(m):
        mag = jnp.exp(are * dt * m)
        ang = aim * dt * m
        return mag * jnp.cos(ang), mag * jnp.sin(ang)

    ab_re, ab_im = power(1.0)
    den = are * are + aim * aim
    nr = ab_re - 1.0
    cr = (nr * are + ab_im * aim) / den
    ci = (ab_im * are - nr * aim) / den
    gj, gp = bre_ref.shape
    row_g = lax.broadcasted_iota(jnp.int32, (gj, gp), 0) // SSM_GROUP
    col_g = lax.broadcasted_iota(jnp.int32, (gj, gp), 1) // SSM_STATE
    diag = row_g == col_g
    bre = bre_ref[...]
    bim = bim_ref[...]
    bb_re = jnp.where(diag, cr * bre - ci * bim, 0.0)
    bb_im = jnp.where(diag, cr * bim + ci * bre, 0.0)
    cc_re = jnp.where(diag, cre_ref[...], 0.0)
    cc_im = jnp.where(diag, cim_ref[...], 0.0)
    c_blk = jnp.concatenate([cc_re, -cc_im], axis=1).T
    for m in range(tc):
        pr, pi = power(float(m))
        pm = jnp.concatenate([pr * bb_re - pi * bb_im, pr * bb_im + pi * bb_re], axis=1)
        pb_ref[tc - 1 - m] = pm.astype(pb_ref.dtype)
        kb_ref[m] = jnp.dot(pm, c_blk, preferred_element_type=F32,
                            precision=lax.Precision.HIGHEST).astype(kb_ref.dtype)
        qr, qi = power(float(m + 1))
        qm = jnp.concatenate([qr * cc_re - qi * cc_im, -(qr * cc_im + qi * cc_re)], axis=1)
        qb_ref[m] = qm.T.astype(qb_ref.dtype)
    a_ref[...] = jnp.concatenate(list(power(float(tc)) + power(float(dec))), axis=0)


def _s5_tables(a_re, a_im, log_dt, b_re, b_im, c_re, c_im, dec):
    depth = a_re.shape[0]
    tc = SSM_CHUNK
    g, p, j = SSM_GROUPS, SSM_STATE, SSM_GROUP
    gp, gj = g * p, g * j
    rowv = lambda t: t.reshape(depth, 1, gp)
    ldt = jnp.repeat(log_dt, p, axis=1).reshape(depth, 1, gp)
    bt = lambda t: jnp.tile(t.transpose(0, 1, 3, 2).reshape(depth, gj, p), (1, 1, g))
    ct = lambda t: jnp.tile(t.reshape(depth, gj, p), (1, 1, g))
    lay = lambda *shape: pl.BlockSpec((None,) + shape, lambda l: (l,) + (0,) * len(shape))
    return pl.pallas_call(
        functools.partial(_s5_tables_kernel, tc=tc, dec=dec),
        out_shape=[jax.ShapeDtypeStruct((depth, tc, gj, gj), BF16),
                   jax.ShapeDtypeStruct((depth, tc, gj, 2 * gp), BF16),
                   jax.ShapeDtypeStruct((depth, tc, 2 * gp, gj), BF16),
                   jax.ShapeDtypeStruct((depth, 4, gp), F32)],
        grid=(depth,),
        in_specs=[lay(1, gp)] * 3 + [lay(gj, gp)] * 4,
        out_specs=[lay(tc, gj, gj), lay(tc, gj, 2 * gp), lay(tc, 2 * gp, gj), lay(4, gp)],
        compiler_params=_cparams(("arbitrary",)),
        name="s5_tables",
    )(rowv(a_re), rowv(a_im), ldt, bt(b_re), bt(b_im), ct(c_re), ct(c_im))


def _s5(up, us, s0_re, s0_im, tables, layer, n_p, seq, n_s, dec):
    kb, pb, qb, adec = tables
    lay = lambda t: pl.BlockSpec((None,) + t.shape[1:], lambda i: (layer,) + (0,) * (t.ndim - 1))
    tc = SSM_CHUNK
    g, p = SSM_GROUPS, SSM_STATE
    sw = 2 * g * p
    tm = ROW_TILE
    tp = n_p * seq
    ts = n_s * dec
    hw = SSM_WIDTH // 2
    fold = sw // hw
    assert dec <= tc and seq % tm == 0 and tm % tc == 0
    up3 = up.reshape(n_p, seq, SSM_WIDTH)
    chunk_rows = n_p * (tm // tc)
    yp, fsp = pl.pallas_call(
        functools.partial(_s5_prompt_kernel, tc=tc),
        out_shape=[jax.ShapeDtypeStruct((n_p, seq, SSM_WIDTH), F32),
                   jax.ShapeDtypeStruct((n_p * fold, hw), F32)],
        grid=(seq // tm,),
        in_specs=[pl.BlockSpec((n_p, tm, hw), lambda i: (0, i, 0)),
                  pl.BlockSpec((n_p, tm, hw), lambda i: (0, i, 1)),
                  lay(kb), lay(pb), lay(qb),
                  pl.BlockSpec((None, 4, fold // 2, hw), lambda i: (layer, 0, 0, 0))],
        out_specs=[pl.BlockSpec((n_p, tm, SSM_WIDTH), lambda i: (0, i, 0)),
                   _const_spec((n_p * fold, hw))],
        scratch_shapes=[pltpu.VMEM((n_p * fold, hw), F32),
                        pltpu.VMEM((chunk_rows * fold, hw), F32), pltpu.VMEM((chunk_rows * fold, hw), F32),
                        pltpu.VMEM((n_p * tm, hw), F32), pltpu.VMEM((n_p * tm, hw), F32)],
        compiler_params=_cparams(("arbitrary",)),
        name="s5_prompt",
    )(up3, up3, kb, pb, qb, adec.reshape(adec.shape[0], 4, fold // 2, hw))
    yp = yp.reshape(tp, SSM_WIDTH)
    s0 = jnp.concatenate([s0_re.reshape(n_s, g * p), s0_im.reshape(n_s, g * p)], axis=1)
    ys, fss = pl.pallas_call(
        functools.partial(_s5_sample_kernel, tc=dec, k0=tc - dec),
        out_shape=[jax.ShapeDtypeStruct((ts, SSM_WIDTH), F32), jax.ShapeDtypeStruct((n_s, sw), F32)],
        grid=(1,),
        in_specs=[pl.BlockSpec((ts, hw), lambda i: (0, 0)), pl.BlockSpec((ts, hw), lambda i: (0, 1)),
                  _const_spec((n_s, sw)),
                  lay(kb), lay(pb), lay(qb), lay(adec)],
        out_specs=[_const_spec((ts, SSM_WIDTH)), _const_spec((n_s, sw))],
        scratch_shapes=[pltpu.VMEM((ts, hw), F32), pltpu.VMEM((ts, hw), F32)],
        compiler_params=_cparams(("arbitrary",)),
        name="s5_sample",
    )(us, us, s0, kb, pb, qb, adec)
    half = g * p
    fsp = fsp.reshape(n_p, sw)
    st = lambda a, n: a.reshape(n, g, p)
    return (yp, ys, st(fsp[:, :half], n_p), st(fsp[:, half:], n_p),
            st(fss[:, :half], n_s), st(fss[:, half:], n_s))


def _lambda(lp_ref, lam_init):
    lp = lp_ref[...]
    s1 = jnp.sum(lp[0:1, :] * lp[1:2, :], axis=-1, keepdims=True)
    s2 = jnp.sum(lp[2:3, :] * lp[3:4, :], axis=-1, keepdims=True)
    return jnp.exp(s1) - jnp.exp(s2) + lam_init


def _attn_kernel(qi_ref, ki_ref, q_ref, k_ref, v_ref, lp_ref, hg_ref, *rest, tile, lam_init, cast_blocks):
    n_cast = len(cast_blocks)
    cast_in = rest[:n_cast]
    o_ref = rest[n_cast]
    cast_out = rest[n_cast + 1:2 * n_cast + 1]
    m_scr, acc_scr = rest[2 * n_cast + 1:]
    t = pl.program_id(1)
    qi = qi_ref[t]
    ki = ki_ref[t]
    vd = ATT_V_DIM

    step = pl.program_id(0) * pl.num_programs(1) + t
    for src, dst, n_blocks in zip(cast_in, cast_out, cast_blocks):
        @pl.when(step < n_blocks)
        def _(src=src, dst=dst):
            dst[...] = src[...].astype(dst.dtype)

    @pl.when(ki == 0)
    def _():
        m_scr[...] = jnp.full(m_scr.shape, NEG_INF, F32)
        acc_scr[...] = jnp.zeros(acc_scr.shape, F32)

    def accumulate(masked):
        lane = lax.broadcasted_iota(jnp.int32, (tile, vd), 1)
        ones = jnp.ones((tile, vd), BF16)
        if masked:
            mask = (lax.broadcasted_iota(jnp.int32, (tile, tile), 1)
                    <= lax.broadcasted_iota(jnp.int32, (tile, tile), 0))
        nt = (((1,), (1,)), ((), ()))
        for h in range(ATT_HEADS):
            cols = slice(h * vd, (h + 1) * vd)
            q = q_ref[:, cols]
            k = k_ref[:, cols]
            v1 = jnp.concatenate([v_ref[:, cols], ones], axis=1)
            zero = jnp.zeros_like(q)
            for c in range(2):
                qm = jnp.where((lane >= ATT_HEAD_DIM) == bool(c), q, zero)
                s = lax.dot_general(qm, k, nt, preferred_element_type=F32)
                if masked:
                    s = jnp.where(mask, s, NEG_INF)
                idx = 2 * h + c
                m_old = m_scr[idx]
                m_row = jnp.max(s, axis=-1, keepdims=True)
                m_new = jnp.maximum(m_old, jnp.broadcast_to(m_row, m_old.shape))
                alpha = jnp.exp2(m_old - m_new)
                p = jnp.exp2(s - jnp.concatenate([m_new] * (tile // vd), axis=1)).astype(BF16)
                acc_scr[idx] = jnp.concatenate([alpha, alpha], axis=1) * acc_scr[idx] + _bdot(p, v1)
                m_scr[idx] = m_new

    @pl.when(ki < qi)
    def _():
        accumulate(False)

    @pl.when(ki == qi)
    def _():
        accumulate(True)
        lam = _lambda(lp_ref, lam_init)
        hg = hg_ref[...]
        for h in range(ATT_HEADS):
            a1 = acc_scr[2 * h]
            a2 = acc_scr[2 * h + 1]
            o = a1[:, :vd] / a1[:, vd:] - lam * (a2[:, :vd] / a2[:, vd:])
            o_ref[:, h * vd:(h + 1) * vd] = (_rms_rows(o, hg) * (1.0 - lam_init)).astype(o_ref.dtype)


def _cast_block_rows(rows, n_steps):
    for br in range(BF16_SUBLANES, rows + 1, BF16_SUBLANES):
        if rows % br == 0 and rows // br <= n_steps:
            return br
    raise ValueError((rows, n_steps))


def _attn_prompt(q, k, v, lp, hg, n_p, seq, lam_init, casts=()):
    tile = min(ATT_TILE, seq)
    assert seq % tile == 0 and tile % ATT_V_DIM == 0
    nq = seq // tile
    pairs = [(i, j) for i in range(nq) for j in range(i + 1)]
    n_steps = n_p * len(pairs)
    qi_tab = jnp.asarray([a for a, _ in pairs], jnp.int32)
    ki_tab = jnp.asarray([b for _, b in pairs], jnp.int32)
    qspec = pl.BlockSpec((tile, ATT_WIDTH), lambda b, t, qi, ki: (b * nq + qi[t], 0))
    kspec = pl.BlockSpec((tile, ATT_WIDTH), lambda b, t, qi, ki: (b * nq + ki[t], 0))
    cast_specs, cast_blocks = [], []
    for w in casts:
        br = _cast_block_rows(w.shape[0], n_steps)
        nblk = w.shape[0] // br
        cast_blocks.append(nblk)
        cast_specs.append(pl.BlockSpec(
            (br, w.shape[1]),
            lambda b, t, qi, ki, nblk=nblk: (jnp.minimum(b * len(pairs) + t, nblk - 1), 0)))
    grid_spec = pltpu.PrefetchScalarGridSpec(
        num_scalar_prefetch=2,
        grid=(n_p, len(pairs)),
        in_specs=[qspec, kspec, kspec,
                  pl.BlockSpec((4, ATT_HEAD_DIM), lambda b, t, qi, ki: (0, 0)),
                  pl.BlockSpec((1, ATT_V_DIM), lambda b, t, qi, ki: (0, 0))] + cast_specs,
        out_specs=[qspec] + cast_specs,
        scratch_shapes=[pltpu.VMEM((2 * ATT_HEADS, tile, ATT_V_DIM), F32),
                        pltpu.VMEM((2 * ATT_HEADS, tile, 2 * ATT_V_DIM), F32)],
    )
    return pl.pallas_call(
        functools.partial(_attn_kernel, tile=tile, lam_init=lam_init, cast_blocks=tuple(cast_blocks)),
        out_shape=[jax.ShapeDtypeStruct((n_p * seq, ATT_WIDTH), BF16)]
        + [jax.ShapeDtypeStruct(w.shape, BF16) for w in casts],
        grid_spec=grid_spec,
        compiler_params=_cparams(("arbitrary", "arbitrary")),
        name="attn_prompt",
    )(qi_tab, ki_tab, q, k, v, lp, hg, *casts)


def _attn_sample_kernel(pt_ref, q_ref, kn_ref, vn_ref, lp_ref, hg_ref, *rest, n_pages, dec, lam_init):
    del pt_ref
    kp = rest[:n_pages]
    vp = rest[n_pages:2 * n_pages]
    o_ref = rest[2 * n_pages]
    page = kp[0].shape[1]
    grp = 2 * dec
    n_rows = ATT_HEADS * grp
    q = q_ref[...].astype(F32)
    qt = jnp.concatenate([q] * (ATT_HEADS * 2), axis=0)
    r = lax.broadcasted_iota(jnp.int32, (n_rows, QK_WIDTH), 0)
    c = lax.broadcasted_iota(jnp.int32, (n_rows, QK_WIDTH), 1)
    qb = jnp.where(r // dec == c // ATT_HEAD_DIM, qt, 0.0).astype(BF16)
    nt = (((1,), (1,)), ((), ()))
    kt_past = jnp.concatenate([kp[j][...].astype(BF16) for j in range(n_pages)], axis=1)
    s_past = _bdot(qb, kt_past)
    s_new = lax.dot_general(qb, kn_ref[...], nt, preferred_element_type=F32)
    rn = lax.broadcasted_iota(jnp.int32, (n_rows, dec), 0) % dec
    cn = lax.broadcasted_iota(jnp.int32, (n_rows, dec), 1)
    s_new = jnp.where(cn <= rn, s_new, NEG_INF)
    m = jnp.maximum(jnp.max(s_new, axis=-1, keepdims=True), jnp.max(s_past, axis=-1, keepdims=True))
    p_new = jnp.exp2(s_new - m)
    p_past = jnp.exp2(s_past - m)
    l = jnp.sum(p_new, axis=-1, keepdims=True) + jnp.sum(p_past, axis=-1, keepdims=True)
    acc_new = _bdot(p_new.astype(BF16), vn_ref[...])
    lam = _lambda(lp_ref, lam_init)
    hg = hg_ref[...]
    for h in range(ATT_HEADS):
        rows = slice(h * grp, (h + 1) * grp)
        cols = slice(h * ATT_V_DIM, (h + 1) * ATT_V_DIM)
        v_past = jnp.concatenate([vp[j][pl.ds(h, page, stride=ATT_HEADS), :].astype(BF16)
                                  for j in range(n_pages)], axis=0)
        acc = acc_new[rows, cols] + _bdot(p_past[rows, :].astype(BF16), v_past)
        acc = acc / l[rows, :]
        o = acc[:dec] - lam * acc[dec:]
        o_ref[:, cols] = (_rms_rows(o, hg) * (1.0 - lam_init)).astype(o_ref.dtype)


def _attn_sample(q, kn, vn, cache_kt, cache_vr, layer, page_table, lp, hg, lam_init):
    n_s, dec, _ = q.shape
    n_pages = page_table.shape[1]
    page = cache_kt.shape[3]
    tok = pl.BlockSpec((None, dec, QK_WIDTH), lambda n, pt: (n, 0, 0))
    kpages = [pl.BlockSpec((None, None, QK_WIDTH, page), lambda n, pt, j=j: (layer, pt[n, j], 0, 0))
              for j in range(n_pages)]
    vpages = [pl.BlockSpec((None, None, page * ATT_HEADS, ATT_V_DIM),
                           lambda n, pt, j=j: (layer, pt[n, j], 0, 0)) for j in range(n_pages)]
    grid_spec = pltpu.PrefetchScalarGridSpec(
        num_scalar_prefetch=1,
        grid=(n_s,),
        in_specs=[tok, tok, tok,
                  pl.BlockSpec((4, ATT_HEAD_DIM), lambda n, pt: (0, 0)),
                  pl.BlockSpec((1, ATT_V_DIM), lambda n, pt: (0, 0))] + kpages + vpages,
        out_specs=tok,
    )
    return pl.pallas_call(
        functools.partial(_attn_sample_kernel, n_pages=n_pages, dec=dec, lam_init=lam_init),
        out_shape=jax.ShapeDtypeStruct((n_s, dec, ATT_WIDTH), BF16),
        grid_spec=grid_spec,
        compiler_params=_cparams(("parallel",)),
        name="attn_sample",
    )(page_table, q, kn, vn, lp, hg, *([cache_kt] * n_pages), *([cache_vr] * n_pages))


def _merge_kernel(xp_ref, xs_ref, up_ref, us_ref, cb_ref, vin_ref, halo_ref, gate_ref,
                  yrp_ref, yrs_ref, ybp_ref, ybs_ref, vm1s_ref, vm2s_ref,
                  d_ref, wglu_ref, bglu_ref, cw_ref, wssm_ref, watt_ref, wconv_ref, wout_ref,
                  op_ref, os_ref, *, n_prompt_tiles, tiles_per_seq):
    i = pl.program_id(0)
    is_prompt = i < n_prompt_tiles
    yraw = jnp.where(is_prompt, yrp_ref[...], yrs_ref[...])
    yb = jnp.where(is_prompt, ybp_ref[...], ybs_ref[...])
    y = jax.nn.gelu(yraw + d_ref[...] * _read_group(up_ref, us_ref, is_prompt))
    ya = y * jax.nn.sigmoid(_bdot(y.astype(BF16), wglu_ref[...]) + bglu_ref[...])

    vin = vin_ref[...]
    row = lax.broadcasted_iota(jnp.int32, vin.shape, 0)
    halo = jnp.where(i % tiles_per_seq == 0, 0.0, halo_ref[...])
    h1 = jnp.broadcast_to(halo[SUBLANES - 1:SUBLANES, :], vin.shape)
    h2 = jnp.broadcast_to(halo[SUBLANES - 2:SUBLANES - 1, :], vin.shape)
    vm1 = jnp.where(row == 0, h1, pltpu.roll(vin, 1, axis=0))
    vm2 = jnp.where(row == 0, h2, jnp.where(row == 1, h1, pltpu.roll(vin, 2, axis=0)))
    vm1 = jnp.where(is_prompt, vm1, vm1s_ref[...])
    vm2 = jnp.where(is_prompt, vm2, vm2s_ref[...])
    conv = vm2 * cw_ref[0:1, :] + vm1 * cw_ref[1:2, :] + vin * cw_ref[2:3, :]
    yc = cb_ref[...] * conv
    merged = (gate_ref[:, 0:D_MODEL] * _bdot(ya.astype(BF16), wssm_ref[...])
              + gate_ref[:, D_MODEL:2 * D_MODEL] * _bdot(yb, watt_ref[...])
              + gate_ref[:, 2 * D_MODEL:3 * D_MODEL] * _bdot(yc.astype(BF16), wconv_ref[...]))
    x_new = _read_group(xp_ref, xs_ref, is_prompt) + _bdot(merged.astype(BF16), wout_ref[...])
    _write_group(op_ref, os_ref, is_prompt, x_new)


def _merge(xp, xs, up, us, cb, vin, gates, yr_p, yr_s, yb_p, yb_s, vm1_s, vm2_s,
           d, wglu, bglu, cw, wssm, watt, wconv, wout, tiles_per_seq):
    tm = ROW_TILE
    tp, ts = xp.shape[0], xs.shape[0]
    npt = tp // tm
    row = lambda w: pl.BlockSpec((tm, w), lambda i: (i, 0))
    prow = lambda w: _group_specs(tm, w, npt)[0]
    srow = lambda w: _group_specs(tm, w, npt)[1]
    halo = pl.BlockSpec((SUBLANES, CONV_WIDTH), lambda i: (jnp.maximum(i * (tm // SUBLANES) - 1, 0), 0))
    return pl.pallas_call(
        functools.partial(_merge_kernel, n_prompt_tiles=npt, tiles_per_seq=tiles_per_seq),
        out_shape=[jax.ShapeDtypeStruct((tp, D_MODEL), F32), jax.ShapeDtypeStruct((ts, D_MODEL), F32)],
        grid=((tp + ts) // tm,),
        in_specs=[prow(D_MODEL), srow(D_MODEL), prow(SSM_WIDTH), srow(SSM_WIDTH),
                  row(CONV_WIDTH), row(CONV_WIDTH), halo,
                  row(N_BRANCH * D_MODEL),
                  prow(SSM_WIDTH), srow(SSM_WIDTH), prow(ATT_WIDTH), srow(ATT_WIDTH),
                  srow(CONV_WIDTH), srow(CONV_WIDTH),
                  _const_spec((1, SSM_WIDTH)), _const_spec((SSM_WIDTH, SSM_WIDTH)),
                  _const_spec((1, SSM_WIDTH)), _const_spec((CONV_K, CONV_WIDTH)),
                  _const_spec((SSM_WIDTH, D_MODEL)), _const_spec((ATT_WIDTH, D_MODEL)),
                  _const_spec((CONV_WIDTH, D_MODEL)), _const_spec((D_MODEL, D_MODEL))],
        out_specs=[prow(D_MODEL), srow(D_MODEL)],
        compiler_params=_cparams(("arbitrary",)),
        name="merge",
    )(xp, xs, up, us, cb, vin, vin, gates, yr_p, yr_s, yb_p, yb_s, vm1_s, vm2_s,
      d, wglu, bglu, cw, wssm, watt, wconv, wout)


def _swiglu(h, w1, w3, w2):
    a = _bdot(h, w1)
    b = _bdot(h, w3)
    return _bdot((jax.nn.silu(a) * b).astype(BF16), w2)


def _ffn_kernel(xp_ref, xs_ref, g_ref, w1_ref, w3_ref, w2_ref, op_ref, os_ref, *, n_prompt_tiles):
    is_prompt = pl.program_id(0) < n_prompt_tiles
    x = _read_group(xp_ref, xs_ref, is_prompt)
    h = _rms_rows(x, g_ref[...]).astype(BF16)
    _write_group(op_ref, os_ref, is_prompt, x + _swiglu(h, w1_ref[...], w3_ref[...], w2_ref[...]))


def _ffn(xp, xs, g, w1, w3, w2):
    tm = ROW_TILE
    tp, ts = xp.shape[0], xs.shape[0]
    npt = tp // tm
    d_ff = w1.shape[1]
    rows = _group_specs(tm, D_MODEL, npt)
    return pl.pallas_call(
        functools.partial(_ffn_kernel, n_prompt_tiles=npt),
        out_shape=[jax.ShapeDtypeStruct((tp, D_MODEL), F32), jax.ShapeDtypeStruct((ts, D_MODEL), F32)],
        grid=((tp + ts) // tm,),
        in_specs=rows + [_const_spec((1, D_MODEL)), _const_spec((D_MODEL, d_ff)),
                         _const_spec((D_MODEL, d_ff)), _const_spec((d_ff, D_MODEL))],
        out_specs=rows,
        compiler_params=_cparams(("arbitrary",)),
        name="ffn",
    )(xp, xs, g, w1, w3, w2)


def _router_kernel(xp_ref, xs_ref, g_ref, wr_ref, tri_ref, h_ref, meta_ref, gate_ref, cnt_ref, carry,
                   *, n_prompt_tiles):
    @pl.when(pl.program_id(0) == 0)
    def _():
        carry[...] = jnp.zeros(carry.shape, F32)

    x = _read_group(xp_ref, xs_ref, pl.program_id(0) < n_prompt_tiles)
    h = _rms_rows(x, g_ref[...])
    h_ref[...] = _pack_halves(h)
    logits = jnp.dot(h, wr_ref[...], preferred_element_type=F32, precision=lax.Precision.HIGHEST)
    lane = lax.broadcasted_iota(jnp.int32, logits.shape, 1)
    logits = jnp.where(lane < N_EXPERTS, logits, -jnp.inf)
    big = jnp.int32(logits.shape[1])
    m1 = jnp.max(logits, axis=-1, keepdims=True)
    i1 = jnp.min(jnp.where(logits == m1, lane, big), axis=-1, keepdims=True)
    rest = jnp.where(lane == i1, -jnp.inf, logits)
    m2 = jnp.max(rest, axis=-1, keepdims=True)
    i2 = jnp.min(jnp.where(rest == m2, lane, big), axis=-1, keepdims=True)
    e = jnp.exp(m2 - m1)
    g1 = 1.0 / (1.0 + e)
    g2 = e / (1.0 + e)
    o1 = lane == i1
    o2 = lane == i2
    chosen = jnp.where(o1 | o2, 1.0, 0.0)
    base = _bdot(tri_ref[...], chosen.astype(BF16)) + carry[...]
    r1 = jnp.sum(jnp.where(o1, base, 0.0), axis=-1, keepdims=True).astype(jnp.int32)
    r2 = jnp.sum(jnp.where(o2, base, 0.0), axis=-1, keepdims=True).astype(jnp.int32)
    carry[...] = carry[...] + jnp.sum(chosen, axis=0, keepdims=True)
    cnt_ref[...] = carry[...]
    meta_ref[...] = jnp.where(lane == 0, i1, jnp.where(lane == 1, i2,
                              jnp.where(lane == 2, r1, jnp.where(lane == 3, r2, 0))))
    gate_ref[...] = jnp.where(lane == 0, g1, jnp.where(lane == 1, g2, 0.0))


def _router(xp, xs, g, wr_pad):
    tm = ROW_TILE
    t = xp.shape[0] + xs.shape[0]
    npt = xp.shape[0] // tm
    lanes = wr_pad.shape[1]
    tri = jnp.tri(tm, k=-1, dtype=BF16)
    row = lambda w: pl.BlockSpec((tm, w), lambda i: (i, 0))
    return pl.pallas_call(
        functools.partial(_router_kernel, n_prompt_tiles=npt),
        out_shape=[jax.ShapeDtypeStruct((t, D_MODEL // 2), jnp.uint32),
                   jax.ShapeDtypeStruct((t, lanes), jnp.int32),
                   jax.ShapeDtypeStruct((t, lanes), F32),
                   jax.ShapeDtypeStruct((1, lanes), F32)],
        grid=(t // tm,),
        in_specs=_group_specs(tm, D_MODEL, npt) + [_const_spec((1, D_MODEL)), _const_spec((D_MODEL, lanes)),
                                                   _const_spec((tm, tm))],
        out_specs=[row(D_MODEL // 2), row(lanes), row(lanes), _const_spec((1, lanes))],
        scratch_shapes=[pltpu.VMEM((1, lanes), F32)],
        compiler_params=_cparams(("arbitrary",)),
        name="router",
    )(xp, xs, g, wr_pad, tri)


def _sc_mesh():
    return plsc.VectorSubcoreMesh(core_axis_name="c", subcore_axis_name="s")


def _sc_worker_base(per_worker):
    return (lax.axis_index("s") * SC_CORES + lax.axis_index("c")) * per_worker


def _sc_chunk_rows(n, row_bytes):
    assert n % (SC_WORKERS * SUBLANES) == 0
    per_worker = n // SC_WORKERS
    fits = [r for r in range(SUBLANES, per_worker + 1, SUBLANES)
            if per_worker % r == 0 and r * row_bytes <= SC_CHUNK_BYTES]
    return per_worker, fits[-1]


def _sc_scatter_rows(x, idx0, idx1, n_out):
    n, w = x.shape
    per_worker, chunk = _sc_chunk_rows(n, w * x.dtype.itemsize)

    @functools.partial(
        pl.kernel, mesh=_sc_mesh(), out_type=jax.ShapeDtypeStruct((n_out, w), x.dtype),
        scratch_types=[pltpu.VMEM((chunk,), jnp.int32), pltpu.VMEM((chunk,), jnp.int32),
                       pltpu.VMEM((chunk, w), x.dtype), pltpu.SemaphoreType.DMA])
    def scatter(x_hbm, i0_hbm, i1_hbm, out_hbm, i0_v, i1_v, rows_v, sem):
        start = _sc_worker_base(per_worker)

        @pl.loop(0, per_worker // chunk)
        def _(c):
            rows = pl.ds(pl.multiple_of(start + c * chunk, SUBLANES), chunk)
            pltpu.sync_copy(x_hbm.at[rows], rows_v)
            pltpu.sync_copy(i0_hbm.at[rows], i0_v)
            pltpu.sync_copy(i1_hbm.at[rows], i1_v)
            pltpu.async_copy(rows_v, out_hbm.at[i0_v], sem).wait()
            pltpu.async_copy(rows_v, out_hbm.at[i1_v], sem).wait()

    return scatter(x, idx0, idx1)


def _sc_gather_rows(table, idx):
    n, w = idx.shape[0], table.shape[1]
    per_worker, chunk = _sc_chunk_rows(n, w * table.dtype.itemsize)

    @functools.partial(
        pl.kernel, mesh=_sc_mesh(), out_type=jax.ShapeDtypeStruct((n, w), table.dtype),
        scratch_types=[pltpu.VMEM((chunk,), jnp.int32), pltpu.VMEM((chunk, w), table.dtype),
                       pltpu.SemaphoreType.DMA])
    def gather(table_hbm, idx_hbm, out_hbm, idx_v, rows_v, sem):
        start = _sc_worker_base(per_worker)

        @pl.loop(0, per_worker // chunk)
        def _(c):
            rows = pl.ds(pl.multiple_of(start + c * chunk, SUBLANES), chunk)
            pltpu.sync_copy(idx_hbm.at[rows], idx_v)
            pltpu.async_copy(table_hbm.at[idx_v], rows_v, sem).wait()
            pltpu.sync_copy(rows_v, out_hbm.at[rows])

    return gather(table, idx)


def _moe_kernel(be_ref, nv_ref, x_ref, w1_ref, w3_ref, w2_ref, o_ref):
    del be_ref

    @pl.when(pl.program_id(0) < nv_ref[0])
    def _():
        x = _unpack_halves(x_ref[...]).astype(BF16)
        o_ref[...] = _pack_halves(_swiglu(x, w1_ref[...], w3_ref[...], w2_ref[...]))


def _moe_blocks(xs, block_expert, n_valid, w1, w3, w2):
    cap = xs.shape[0]
    tb = MOE_TILE
    d_ff = w1.shape[2]
    grid_spec = pltpu.PrefetchScalarGridSpec(
        num_scalar_prefetch=2,
        grid=(cap // tb,),
        in_specs=[pl.BlockSpec((tb, D_MODEL // 2), lambda i, be, nv: (jnp.minimum(i, nv[0] - 1), 0)),
                  pl.BlockSpec((None, D_MODEL, d_ff), lambda i, be, nv: (be[i], 0, 0)),
                  pl.BlockSpec((None, D_MODEL, d_ff), lambda i, be, nv: (be[i], 0, 0)),
                  pl.BlockSpec((None, d_ff, D_MODEL), lambda i, be, nv: (be[i], 0, 0))],
        out_specs=pl.BlockSpec((tb, D_MODEL // 2), lambda i, be, nv: (i, 0)),
    )
    return pl.pallas_call(
        _moe_kernel,
        out_shape=jax.ShapeDtypeStruct((cap, D_MODEL // 2), jnp.uint32),
        grid_spec=grid_spec,
        compiler_params=_cparams(("arbitrary",)),
        name="moe",
    )(block_expert, n_valid, xs, w1, w3, w2)


def _combine_kernel(xp_ref, xs_ref, a0_ref, a1_ref, gate_ref, op_ref, os_ref, *, n_prompt_tiles):
    is_prompt = pl.program_id(0) < n_prompt_tiles
    gate = gate_ref[...]
    y = (_read_group(xp_ref, xs_ref, is_prompt)
         + gate[:, 0:1] * _unpack_halves(a0_ref[...]) + gate[:, 1:2] * _unpack_halves(a1_ref[...]))
    _write_group(op_ref, os_ref, is_prompt, y)


def _combine(xp, xs, picked, gate):
    tm = ROW_TILE
    tp, ts = xp.shape[0], xs.shape[0]
    nt = (tp + ts) // tm
    rows = _group_specs(tm, D_MODEL, tp // tm)
    return pl.pallas_call(
        functools.partial(_combine_kernel, n_prompt_tiles=tp // tm),
        out_shape=[jax.ShapeDtypeStruct((tp, D_MODEL), F32), jax.ShapeDtypeStruct((ts, D_MODEL), F32)],
        grid=(nt,),
        in_specs=rows + [pl.BlockSpec((tm, D_MODEL // 2), lambda i: (i, 0)),
                         pl.BlockSpec((tm, D_MODEL // 2), lambda i: (i + nt, 0)),
                         pl.BlockSpec((tm, gate.shape[1]), lambda i: (i, 0))],
        out_specs=rows,
        compiler_params=_cparams(("arbitrary",)),
        name="moe_combine",
    )(xp, xs, picked, picked, gate)


def _moe(xp, xs, g, wr, w1, w3, w2):
    t = xp.shape[0] + xs.shape[0]
    tb = MOE_TILE
    lanes = 128
    wr_pad = jnp.pad(wr, ((0, 0), (0, lanes - N_EXPERTS)))
    h, meta, gate, cnt = _router(xp, xs, g, wr_pad)
    counts = cnt[0, :N_EXPERTS].astype(jnp.int32)
    padded = (counts + tb - 1) // tb * tb
    pad_ends = jnp.cumsum(padded)
    pad_starts = pad_ends - padded
    experts = jnp.arange(N_EXPERTS, dtype=jnp.int32)
    slot = lambda e, r: jnp.sum(jnp.where(e[:, None] == experts, pad_starts, 0), axis=1) + r
    dest0 = slot(meta[:, 0], meta[:, 2])
    dest1 = slot(meta[:, 1], meta[:, 3])
    nb = -(-(t * TOP_K + N_EXPERTS * (tb - 1)) // tb)
    block_expert = jnp.minimum(
        jnp.sum(pad_ends[None, :] <= (jnp.arange(nb, dtype=jnp.int32) * tb)[:, None], axis=1),
        N_EXPERTS - 1).astype(jnp.int32)
    n_valid = (pad_ends[-1:] // tb).astype(jnp.int32)
    slots = _sc_scatter_rows(h, dest0, dest1, nb * tb)
    out = _moe_blocks(slots, block_expert, n_valid, w1, w3, w2)
    picked = _sc_gather_rows(out, jnp.concatenate([dest0, dest1]))
    return _combine(xp, xs, picked, gate)


def kernel(x_prompt, x_sample, cache_k, cache_v, page_table, state_ssm_re, state_ssm_im, state_conv,
           norm_mix_g, norm_ffn_g, w_in, ssm_a_re, ssm_a_im, ssm_log_dt, ssm_b_re, ssm_b_im,
           ssm_c_re, ssm_c_im, ssm_d, ssm_w_glu, ssm_b_glu, q_norm_g, k_norm_g,
           lambda_q1, lambda_k1, lambda_q2, lambda_k2, head_norm_g, conv_w,
           w_br_ssm, w_br_att, w_br_conv, w_out, ffn_w1, ffn_w3, ffn_w2,
           router_w, moe_w1, moe_w3, moe_w2):
    n_p, seq, _ = x_prompt.shape
    n_s, dec, _ = x_sample.shape
    depth = w_in.shape[0]
    tp = n_p * seq
    ts = n_s * dec
    assert seq % ROW_TILE == 0 and ts % ROW_TILE == 0 and seq >= CONV_K - 1
    pool, page = cache_k.shape[1], cache_k.shape[2]
    cache_kt = cache_k.reshape(depth, pool, page, QK_WIDTH).transpose(0, 1, 3, 2)
    cache_vr = cache_v.reshape(depth, pool, page * ATT_HEADS, ATT_V_DIM)
    seg = jnp.kron(jnp.eye(QK_WIDTH // ATT_HEAD_DIM, dtype=F32),
                   jnp.full((ATT_HEAD_DIM, ATT_HEAD_DIM), 1.0 / ATT_HEAD_DIM, F32)).astype(BF16)
    n_rep = QK_WIDTH // ATT_HEAD_DIM

    xp = x_prompt.reshape(tp, D_MODEL)
    xs = x_sample.reshape(ts, D_MODEL)
    w_in_bf = w_in.astype(BF16)
    all_tables = _s5_tables(ssm_a_re, ssm_a_im, ssm_log_dt, ssm_b_re, ssm_b_im, ssm_c_re, ssm_c_im, dec)
    kv_prompt = None
    srp, sip, cvp = [], [], []
    ks, vs, srs, sis, cvs = [], [], [], [], []
    for l in range(depth):
        lam_init = 0.8 - 0.6 * math.exp(-0.3 * l)
        q, kb, vb, cb, vin, gates, kt, vr, k_s, v_s, u_p, u_s = _inproj(
            xp, xs, norm_mix_g[l][None], w_in_bf,
            jnp.tile(q_norm_g[l], n_rep)[None], jnp.tile(k_norm_g[l], n_rep)[None], seg,
            l, depth, n_p, seq, kv_prompt)
        kv_prompt = (kt, vr)

        yr_p, yr_s, p_re, p_im, s_re, s_im = _s5(u_p, u_s, state_ssm_re[l], state_ssm_im[l], all_tables, l,
                                                  n_p, seq, n_s, dec)

        lp = jnp.stack([lambda_q1[l], lambda_k1[l], lambda_q2[l], lambda_k2[l]])
        hg = head_norm_g[l][None]
        casts = ()
        if l + 1 < depth and (l + 1) % 2 == 1:
            e = (l + 1) // 2
            casts = (moe_w1[e].reshape(-1, moe_w1.shape[-1]), moe_w3[e].reshape(-1, moe_w3.shape[-1]),
                     moe_w2[e].reshape(-1, moe_w2.shape[-1]))
        yb_p, *cast_out = _attn_prompt(q, kb, vb, lp, hg, n_p, seq, lam_init, casts)
        if casts:
            moe_bf = [c.reshape(w.shape[1:]) for c, w in zip(cast_out, (moe_w1, moe_w3, moe_w2))]
        yb_s = _attn_sample(q[tp:].reshape(n_s, dec, QK_WIDTH), kb[tp:].reshape(n_s, dec, QK_WIDTH),
                            vb[tp:].reshape(n_s, dec, ATT_WIDTH), cache_kt, cache_vr, l, page_table,
                            lp, hg, lam_init).reshape(ts, ATT_WIDTH)

        ext_s = jnp.concatenate([state_conv[l], vin[tp:].reshape(n_s, dec, CONV_WIDTH)], axis=1)
        vm1_s = ext_s[:, 1:1 + dec].reshape(ts, CONV_WIDTH)
        vm2_s = ext_s[:, 0:dec].reshape(ts, CONV_WIDTH)
        xp, xs = _merge(xp, xs, u_p, u_s, cb, vin, gates, yr_p, yr_s, yb_p, yb_s, vm1_s, vm2_s,
                        ssm_d[l][None], ssm_w_glu[l].astype(BF16), ssm_b_glu[l][None], conv_w[l],
                        w_br_ssm[l].astype(BF16), w_br_att[l].astype(BF16), w_br_conv[l].astype(BF16),
                        w_out[l].astype(BF16), seq // ROW_TILE)

        i = l // 2
        if l % 2 == 0:
            xp, xs = _ffn(xp, xs, norm_ffn_g[l][None], ffn_w1[i].astype(BF16), ffn_w3[i].astype(BF16),
                          ffn_w2[i].astype(BF16))
        else:
            xp, xs = _moe(xp, xs, norm_ffn_g[l][None], router_w[i], *moe_bf)

        srp.append(p_re); sip.append(p_im)
        cvp.append(jnp.stack([vin[(b + 1) * seq - (CONV_K - 1):(b + 1) * seq] for b in range(n_p)]))
        ks.append(k_s.reshape(n_s, dec, ATT_HEADS, 2, ATT_HEAD_DIM))
        vs.append(v_s.reshape(n_s, dec, ATT_HEADS, ATT_V_DIM))
        srs.append(s_re); sis.append(s_im); cvs.append(ext_s[:, dec:])

    kt, vr = kv_prompt
    k_prompt = kt.reshape(depth, n_p, ATT_HEADS, 2, ATT_HEAD_DIM, seq).transpose(0, 1, 5, 2, 3, 4)
    v_prompt = vr.reshape(depth, n_p, seq, ATT_HEADS, ATT_V_DIM)
    return (xp.reshape(n_p, seq, D_MODEL), xs.reshape(n_s, dec, D_MODEL),
            k_prompt, v_prompt, jnp.stack(srp), jnp.stack(sip), jnp.stack(cvp),
            jnp.stack(ks), jnp.stack(vs), jnp.stack(srs), jnp.stack(sis), jnp.stack(cvs))
```

```python
import functools
import math

import jax
import jax.numpy as jnp
from jax import lax
from jax.experimental import pallas as pl
from jax.experimental.pallas import tpu as pltpu
from jax.experimental.pallas import tpu_sc as plsc

F32 = jnp.float32
BF16 = jnp.bfloat16

D_MODEL = 1024
SSM_WIDTH = 256
SSM_GROUP = 16
SSM_GROUPS = SSM_WIDTH // SSM_GROUP
SSM_STATE = 64
ATT_HEADS = 4
ATT_HEAD_DIM = 64
ATT_V_DIM = 2 * ATT_HEAD_DIM
QK_WIDTH = ATT_HEADS * 2 * ATT_HEAD_DIM
ATT_WIDTH = ATT_HEADS * ATT_V_DIM
CONV_WIDTH = 256
CONV_K = 3
N_BRANCH = 3
N_EXPERTS = 8
TOP_K = 2
EPS = 1e-6
NEG_INF = -1e30
LOG2_E = 1.4426950408889634

C_U = 0
C_Q = C_U + SSM_WIDTH
C_K = C_Q + QK_WIDTH
C_V = C_K + QK_WIDTH
C_CB = C_V + ATT_WIDTH
C_CC = C_CB + CONV_WIDTH
C_CH = C_CC + CONV_WIDTH
C_G = C_CH + CONV_WIDTH
IN_COLS = C_G + N_BRANCH * D_MODEL

SSM_CHUNK = 8
SUBLANES = 8
BF16_SUBLANES = 16
ROW_TILE = 512
ATT_TILE = 512
MOE_TILE = 256
SC_CORES = 2
SC_WORKERS = SC_CORES * 16
SC_CHUNK_BYTES = 192 * 1024
VMEM_LIMIT = 56 * 1024 * 1024


def _cparams(sem):
    return pltpu.CompilerParams(dimension_semantics=sem, vmem_limit_bytes=VMEM_LIMIT)


def _const_spec(shape):
    nd = len(shape)
    return pl.BlockSpec(shape, lambda *_: (0,) * nd)


def _bdot(a, b):
    return jnp.dot(a, b, preferred_element_type=F32)


def _rms_rows(x, g):
    ms = jnp.mean(x * x, axis=-1, keepdims=True)
    return x * lax.rsqrt(ms + EPS) * g


def _group_specs(tm, w, n_prompt_tiles):
    return [pl.BlockSpec((tm, w), lambda i: (jnp.minimum(i, n_prompt_tiles - 1), 0)),
            pl.BlockSpec((tm, w), lambda i: (jnp.maximum(i - n_prompt_tiles, 0), 0))]


def _read_group(p_ref, s_ref, is_prompt):
    return jnp.where(is_prompt, p_ref[...], s_ref[...])


def _write_group(p_ref, s_ref, is_prompt, val):
    @pl.when(is_prompt)
    def _():
        p_ref[...] = val

    @pl.when(jnp.logical_not(is_prompt))
    def _():
        s_ref[...] = val


def _pack_halves(x):
    half = x.shape[1] // 2
    bits = lambda t: lax.bitcast_convert_type(t.astype(BF16).astype(F32), jnp.uint32)
    return (bits(x[:, :half]) >> 16) | (bits(x[:, half:]) & jnp.uint32(0xFFFF0000))


def _unpack_halves(p):
    lo = lax.bitcast_convert_type(p << 16, F32)
    hi = lax.bitcast_convert_type(p & jnp.uint32(0xFFFF0000), F32)
    return jnp.concatenate([lo, hi], axis=1)


def _segment_rms(z, g, seg):
    ms = _bdot((z * z).astype(BF16), seg)
    return z * lax.rsqrt(ms + EPS) * g


def _inproj_kernel(xp_ref, xs_ref, g_ref, w_ref, qg_ref, kg_ref, seg_ref, *rest,
                   n_prompt_tiles, n_prev, layer):
    (q_ref, kb_ref, vb_ref, cb_ref, vin_ref, gate_ref,
     kt_ref, vr_ref, ks_ref, vs_ref, up_ref, us_ref) = rest[n_prev:]
    tm = q_ref.shape[0]
    is_prompt = pl.program_id(0) < n_prompt_tiles
    h = _rms_rows(_read_group(xp_ref, xs_ref, is_prompt), g_ref[...]).astype(BF16)

    def proj(a, b):
        return _bdot(h, w_ref[:, a:b])

    seg = seg_ref[...]
    _write_group(up_ref, us_ref, is_prompt, proj(C_U, C_Q))
    qn = _segment_rms(proj(C_Q, C_K), qg_ref[...], seg)
    q_ref[...] = (qn * (ATT_HEAD_DIM ** -0.5 * LOG2_E)).astype(BF16)
    kn = _segment_rms(proj(C_K, C_V), kg_ref[...], seg)
    kb_ref[...] = kn.astype(BF16)
    v = proj(C_V, C_CB)
    vb_ref[...] = v.astype(BF16)
    cb_ref[...] = proj(C_CB, C_CC)
    vin_ref[...] = proj(C_CC, C_CH) * proj(C_CH, C_G)
    for j in range(N_BRANCH):
        a = C_G + j * D_MODEL
        gate_ref[:, j * D_MODEL:(j + 1) * D_MODEL] = jax.nn.sigmoid(proj(a, a + D_MODEL)).astype(gate_ref.dtype)

    @pl.when(is_prompt)
    def _():
        if n_prev:
            kt_l, vr_l = kt_ref, vr_ref
        else:
            kt_l, vr_l = kt_ref.at[layer], vr_ref.at[layer]
            for other in range(kt_ref.shape[0]):
                if other != layer:
                    kt_ref[other] = jnp.zeros(kt_ref.shape[1:], F32)
                    vr_ref[other] = jnp.zeros(vr_ref.shape[1:], F32)
        kt_l[...] = kn.T
        for hd in range(ATT_HEADS):
            vr_l[pl.ds(hd, tm, stride=ATT_HEADS), :] = v[:, hd * ATT_V_DIM:(hd + 1) * ATT_V_DIM]

    @pl.when(jnp.logical_not(is_prompt))
    def _():
        ks_ref[...] = kn
        vs_ref[...] = v


def _inproj(xp, xs, g, w_bf, qg, kg, seg, layer, depth, n_p, seq, prev):
    tm = ROW_TILE
    tp, ts = xp.shape[0], xs.shape[0]
    t = tp + ts
    npt = tp // tm
    tps = seq // tm
    row = lambda w: pl.BlockSpec((tm, w), lambda i: (i, 0))
    pc = lambda i: jnp.minimum(i, npt - 1)
    outs = [(QK_WIDTH, BF16), (QK_WIDTH, BF16), (ATT_WIDTH, BF16),
            (CONV_WIDTH, F32), (CONV_WIDTH, F32), (N_BRANCH * D_MODEL, BF16)]
    out_shape = ([jax.ShapeDtypeStruct((t, w), d) for w, d in outs]
                 + [jax.ShapeDtypeStruct((depth, n_p, QK_WIDTH, seq), F32),
                    jax.ShapeDtypeStruct((depth, tp * ATT_HEADS, ATT_V_DIM), F32),
                    jax.ShapeDtypeStruct((ts, QK_WIDTH), F32), jax.ShapeDtypeStruct((ts, ATT_WIDTH), F32),
                    jax.ShapeDtypeStruct((tp, SSM_WIDTH), F32), jax.ShapeDtypeStruct((ts, SSM_WIDTH), F32)])
    srow = _group_specs(tm, QK_WIDTH, npt)[1]
    prev = () if prev is None else tuple(prev)
    lead, at = (None, layer) if prev else (depth, 0)
    out_specs = ([row(w) for w, _ in outs]
                 + [pl.BlockSpec((lead, None, QK_WIDTH, tm), lambda i: (at, pc(i) // tps, 0, pc(i) % tps)),
                    pl.BlockSpec((lead, tm * ATT_HEADS, ATT_V_DIM), lambda i: (at, pc(i), 0)),
                    srow, srow] + _group_specs(tm, SSM_WIDTH, npt))
    n_in = 7
    return pl.pallas_call(
        functools.partial(_inproj_kernel, n_prompt_tiles=npt, n_prev=len(prev), layer=layer),
        out_shape=out_shape,
        grid=(t // tm,),
        in_specs=(_group_specs(tm, D_MODEL, npt)
                  + [_const_spec((1, D_MODEL)),
                     pl.BlockSpec((None, D_MODEL, IN_COLS), lambda i: (layer, 0, 0)),
                     _const_spec((1, QK_WIDTH)), _const_spec((1, QK_WIDTH)),
                     _const_spec((QK_WIDTH, QK_WIDTH))]
                  + [pl.BlockSpec(memory_space=pl.ANY)] * len(prev)),
        out_specs=out_specs,
        input_output_aliases={n_in + j: len(outs) + j for j in range(len(prev))},
        compiler_params=_cparams(("arbitrary",)),
        name="inproj",
    )(xp, xs, g, w_bf, qg, kg, seg, *prev)


def _s5_intra(u, kb_ref, tc):
    rowmod = lax.broadcasted_iota(jnp.int32, u.shape, 0) % tc
    y = _bdot(u.astype(BF16), kb_ref[0])
    for m in range(1, tc):
        um = jnp.where(rowmod >= m, pltpu.roll(u, m, axis=0), 0.0)
        y = y + _bdot(um.astype(BF16), kb_ref[m])
    return y


def _s5_inject(u_refs, pb_ref, k0, n_chunks, tc):
    w = None
    for k in range(tc):
        rows = pl.ds(k, n_chunks, stride=tc)
        uk = jnp.concatenate([r[rows, :] for r in u_refs], axis=1).astype(BF16)
        d = _bdot(uk, pb_ref[k0 + k])
        w = d if w is None else w + d
    return w


def _s5_readout(y_ref, y_scrs, y, s_in, qb_ref, n_chunks, tc):
    lanes = y_scrs[0].shape[1]
    for h, scr in enumerate(y_scrs):
        scr[...] = y[:, h * lanes:(h + 1) * lanes]
    sb = s_in.astype(BF16)
    for k in range(tc):
        rows = pl.ds(k, n_chunks, stride=tc)
        yk = _bdot(sb, qb_ref[k])
        for h, scr in enumerate(y_scrs):
            scr[rows, :] = scr[rows, :] + yk[:, h * lanes:(h + 1) * lanes]
    y_ref[...] = jnp.concatenate([scr[...] for scr in y_scrs], axis=1)


def _cmul_add(a_re, a_im, s, w, half):
    s_re, s_im = s[:, :half], s[:, half:]
    return jnp.concatenate([a_re * s_re - a_im * s_im, a_re * s_im + a_im * s_re], axis=1) + w


def _s5_prompt_kernel(ua_ref, ub_ref, kb_ref, pb_ref, qb_ref, a_ref, y_ref, fs_ref,
                      carry, w_fold, s_fold, ya_scr, yb_scr, *, tc):
    n_b, tm, hw = ua_ref.shape
    n_chunks = tm // tc
    rows = n_b * n_chunks
    fold = carry.shape[0] // n_b
    hf = fold // 2

    @pl.when(pl.program_id(0) == 0)
    def _():
        carry[...] = jnp.zeros(carry.shape, F32)

    u = jnp.concatenate([ua_ref[...].reshape(n_b * tm, hw), ub_ref[...].reshape(n_b * tm, hw)], axis=1)
    y = _s5_intra(u, kb_ref, tc)
    w = None
    for k in range(tc):
        at_k = pl.ds(k, n_chunks, stride=tc)
        uk = jnp.concatenate([jnp.concatenate([ua_ref[b, at_k, :], ub_ref[b, at_k, :]], axis=1)
                              for b in range(n_b)], axis=0).astype(BF16)
        d = _bdot(uk, pb_ref[k])
        w = d if w is None else w + d
    for r in range(fold):
        w_fold[pl.ds(r, rows, stride=fold), :] = w[:, r * hw:(r + 1) * hw]
    a_re = a_ref[0]
    a_im = a_ref[1]
    for b in range(n_b):
        s = carry[b * fold:(b + 1) * fold, :]
        for c in range(n_chunks):
            at = slice((b * n_chunks + c) * fold, (b * n_chunks + c + 1) * fold)
            s_fold[at, :] = s
            s_re, s_im = s[:hf], s[hf:]
            s = jnp.concatenate([a_re * s_re - a_im * s_im, a_re * s_im + a_im * s_re], axis=0) + w_fold[at, :]
        carry[b * fold:(b + 1) * fold, :] = s
    fs_ref[...] = carry[...]
    s_in = jnp.concatenate([s_fold[pl.ds(r, rows, stride=fold), :] for r in range(fold)], axis=1)
    for h, scr in enumerate((ya_scr, yb_scr)):
        scr[...] = y[:, h * hw:(h + 1) * hw]
    sb = s_in.astype(BF16)
    for k in range(tc):
        at_k = pl.ds(k, rows, stride=tc)
        yk = _bdot(sb, qb_ref[k])
        for h, scr in enumerate((ya_scr, yb_scr)):
            scr[at_k, :] = scr[at_k, :] + yk[:, h * hw:(h + 1) * hw]
    for b in range(n_b):
        y_ref[b] = jnp.concatenate([ya_scr[b * tm:(b + 1) * tm, :], yb_scr[b * tm:(b + 1) * tm, :]], axis=1)


def _s5_sample_kernel(ua_ref, ub_ref, s0_ref, kb_ref, pb_ref, qb_ref, a_ref, y_ref, fs_ref,
                      ya_scr, yb_scr, *, tc, k0):
    n_chunks = ua_ref.shape[0] // tc
    half = a_ref.shape[1]
    y = _s5_intra(jnp.concatenate([ua_ref[...], ub_ref[...]], axis=1), kb_ref, tc)
    w = _s5_inject((ua_ref, ub_ref), pb_ref, k0, n_chunks, tc)
    s0 = s0_ref[...]
    fs_ref[...] = _cmul_add(a_ref[2:3, :], a_ref[3:4, :], s0, w, half)
    _s5_readout(y_ref, (ya_scr, yb_scr), y, s0, qb_ref, n_chunks, tc)


def _s5_tables_kernel(are_ref, aim_ref, ldt_ref, bre_ref, bim_ref, cre_ref, cim_ref,
                      kb_ref, pb_ref, qb_ref, a_ref, *, tc, dec):
    are = are_ref[...]
    aim = aim_ref[...]
    dt = jnp.exp(ldt_ref[...])

    def power(m):
        mag = jnp.exp(are * dt * m)
        ang = aim * dt * m
        return mag * jnp.cos(ang), mag * jnp.sin(ang)

    ab_re, ab_im = power(1.0)
    den = are * are + aim * aim
    nr = ab_re - 1.0
    cr = (nr * are + ab_im * aim) / den
    ci = (ab_im * are - nr * aim) / den
    gj, gp = bre_ref.shape
    row_g = lax.broadcasted_iota(jnp.int32, (gj, gp), 0) // SSM_GROUP
    col_g = lax.broadcasted_iota(jnp.int32, (gj, gp), 1) // SSM_STATE
    diag = row_g == col_g
    bre = bre_ref[...]
    bim = bim_ref[...]
    bb_re = jnp.where(diag, cr * bre - ci * bim, 0.0)
    bb_im = jnp.where(diag, cr * bim + ci * bre, 0.0)
    cc_re = jnp.where(diag, cre_ref[...], 0.0)
    cc_im = jnp.where(diag, cim_ref[...], 0.0)
    c_blk = jnp.concatenate([cc_re, -cc_im], axis=1).T
    for m in range(tc):
        pr, pi = power(float(m))
        pm = jnp.concatenate([pr * bb_re - pi * bb_im, pr * bb_im + pi * bb_re], axis=1)
        pb_ref[tc - 1 - m] = pm.astype(pb_ref.dtype)
        kb_ref[m] = jnp.dot(pm, c_blk, preferred_element_type=F32,
                            precision=lax.Precision.HIGHEST).astype(kb_ref.dtype)
        qr, qi = power(float(m + 1))
        qm = jnp.concatenate([qr * cc_re - qi * cc_im, -(qr * cc_im + qi * cc_re)], axis=1)
        qb_ref[m] = qm.T.astype(qb_ref.dtype)
    a_ref[...] = jnp.concatenate(list(power(float(tc)) + power(float(dec))), axis=0)


def _s5_tables(a_re, a_im, log_dt, b_re, b_im, c_re, c_im, dec):
    depth = a_re.shape[0]
    tc = SSM_CHUNK
    g, p, j = SSM_GROUPS, SSM_STATE, SSM_GROUP
    gp, gj = g * p, g * j
    rowv = lambda t: t.reshape(depth, 1, gp)
    ldt = jnp.repeat(log_dt, p, axis=1).reshape(depth, 1, gp)
    bt = lambda t: jnp.tile(t.transpose(0, 1, 3, 2).reshape(depth, gj, p), (1, 1, g))
    ct = lambda t: jnp.tile(t.reshape(depth, gj, p), (1, 1, g))
    lay = lambda *shape: pl.BlockSpec((None,) + shape, lambda l: (l,) + (0,) * len(shape))
    return pl.pallas_call(
        functools.partial(_s5_tables_kernel, tc=tc, dec=dec),
        out_shape=[jax.ShapeDtypeStruct((depth, tc, gj, gj), BF16),
                   jax.ShapeDtypeStruct((depth, tc, gj, 2 * gp), BF16),
                   jax.ShapeDtypeStruct((depth, tc, 2 * gp, gj), BF16),
                   jax.ShapeDtypeStruct((depth, 4, gp), F32)],
        grid=(depth,),
        in_specs=[lay(1, gp)] * 3 + [lay(gj, gp)] * 4,
        out_specs=[lay(tc, gj, gj), lay(tc, gj, 2 * gp), lay(tc, 2 * gp, gj), lay(4, gp)],
        compiler_params=_cparams(("arbitrary",)),
        name="s5_tables",
    )(rowv(a_re), rowv(a_im), ldt, bt(b_re), bt(b_im), ct(c_re), ct(c_im))


def _s5(up, us, s0_re, s0_im, tables, layer, n_p, seq, n_s, dec):
    kb, pb, qb, adec = tables
    lay = lambda t: pl.BlockSpec((None,) + t.shape[1:], lambda i: (layer,) + (0,) * (t.ndim - 1))
    tc = SSM_CHUNK
    g, p = SSM_GROUPS, SSM_STATE
    sw = 2 * g * p
    tm = ROW_TILE
    tp = n_p * seq
    ts = n_s * dec
    hw = SSM_WIDTH // 2
    fold = sw // hw
    assert dec <= tc and seq % tm == 0 and tm % tc == 0
    up3 = up.reshape(n_p, seq, SSM_WIDTH)
    chunk_rows = n_p * (tm // tc)
    yp, fsp = pl.pallas_call(
        functools.partial(_s5_prompt_kernel, tc=tc),
        out_shape=[jax.ShapeDtypeStruct((n_p, seq, SSM_WIDTH), F32),
                   jax.ShapeDtypeStruct((n_p * fold, hw), F32)],
        grid=(seq // tm,),
        in_specs=[pl.BlockSpec((n_p, tm, hw), lambda i: (0, i, 0)),
                  pl.BlockSpec((n_p, tm, hw), lambda i: (0, i, 1)),
                  lay(kb), lay(pb), lay(qb),
                  pl.BlockSpec((None, 4, fold // 2, hw), lambda i: (layer, 0, 0, 0))],
        out_specs=[pl.BlockSpec((n_p, tm, SSM_WIDTH), lambda i: (0, i, 0)),
                   _const_spec((n_p * fold, hw))],
        scratch_shapes=[pltpu.VMEM((n_p * fold, hw), F32),
                        pltpu.VMEM((chunk_rows * fold, hw), F32), pltpu.VMEM((chunk_rows * fold, hw), F32),
                        pltpu.VMEM((n_p * tm, hw), F32), pltpu.VMEM((n_p * tm, hw), F32)],
        compiler_params=_cparams(("arbitrary",)),
        name="s5_prompt",
    )(up3, up3, kb, pb, qb, adec.reshape(adec.shape[0], 4, fold // 2, hw))
    yp = yp.reshape(tp, SSM_WIDTH)
    s0 = jnp.concatenate([s0_re.reshape(n_s, g * p), s0_im.reshape(n_s, g * p)], axis=1)
    ys, fss = pl.pallas_call(
        functools.partial(_s5_sample_kernel, tc=dec, k0=tc - dec),
        out_shape=[jax.ShapeDtypeStruct((ts, SSM_WIDTH), F32), jax.ShapeDtypeStruct((n_s, sw), F32)],
        grid=(1,),
        in_specs=[pl.BlockSpec((ts, hw), lambda i: (0, 0)), pl.BlockSpec((ts, hw), lambda i: (0, 1)),
                  _const_spec((n_s, sw)),
                  lay(kb), lay(pb), lay(qb), lay(adec)],
        out_specs=[_const_spec((ts, SSM_WIDTH)), _const_spec((n_s, sw))],
        scratch_shapes=[pltpu.VMEM((ts, hw), F32), pltpu.VMEM((ts, hw), F32)],
        compiler_params=_cparams(("arbitrary",)),
        name="s5_sample",
    )(us, us, s0, kb, pb, qb, adec)
    half = g * p
    fsp = fsp.reshape(n_p, sw)
    st = lambda a, n: a.reshape(n, g, p)
    return (yp, ys, st(fsp[:, :half], n_p), st(fsp[:, half:], n_p),
            st(fss[:, :half], n_s), st(fss[:, half:], n_s))


def _lambda(lp_ref, lam_init):
    lp = lp_ref[...]
    s1 = jnp.sum(lp[0:1, :] * lp[1:2, :], axis=-1, keepdims=True)
    s2 = jnp.sum(lp[2:3, :] * lp[3:4, :], axis=-1, keepdims=True)
    return jnp.exp(s1) - jnp.exp(s2) + lam_init


def _attn_kernel(qi_ref, ki_ref, pt_ref, q_ref, k_ref, v_ref, lp_ref, hg_ref, *rest,
                 tile, lam_init, cast_blocks, n_seq, n_pages):
    del pt_ref
    n_cast = len(cast_blocks)
    n_pg = n_seq * n_pages
    cast_in = rest[:n_cast]
    sq_ref, skn_ref, svn_ref = rest[n_cast:n_cast + 3]
    kp = rest[n_cast + 3:n_cast + 3 + n_pg]
    vp = rest[n_cast + 3 + n_pg:n_cast + 3 + 2 * n_pg]
    o_ref, so_ref = rest[n_cast + 3 + 2 * n_pg:n_cast + 5 + 2 * n_pg]
    cast_out = rest[n_cast + 5 + 2 * n_pg:2 * n_cast + 5 + 2 * n_pg]
    m_scr, acc_scr = rest[2 * n_cast + 5 + 2 * n_pg:]
    dec = sq_ref.shape[1]
    t = pl.program_id(1)
    qi = qi_ref[t]
    ki = ki_ref[t]
    vd = ATT_V_DIM

    step = pl.program_id(0) * pl.num_programs(1) + t
    for src, dst, n_blocks in zip(cast_in, cast_out, cast_blocks):
        @pl.when(step < n_blocks)
        def _(src=src, dst=dst):
            dst[...] = src[...].astype(dst.dtype)

    @pl.when(ki == 0)
    def _():
        m_scr[...] = jnp.full(m_scr.shape, NEG_INF, F32)
        acc_scr[...] = jnp.zeros(acc_scr.shape, F32)

    def accumulate(masked):
        for u in range(n_seq):
            o_s = _attend_sample(sq_ref[u], skn_ref[u], svn_ref[u], kp[u * n_pages:(u + 1) * n_pages],
                                 vp[u * n_pages:(u + 1) * n_pages], lp_ref, hg_ref, dec, lam_init)
            so_ref[u] = o_s.astype(so_ref.dtype)
        lane = lax.broadcasted_iota(jnp.int32, (tile, vd), 1)
        ones = jnp.ones((tile, vd), BF16)
        if masked:
            mask = (lax.broadcasted_iota(jnp.int32, (tile, tile), 1)
                    <= lax.broadcasted_iota(jnp.int32, (tile, tile), 0))
        nt = (((1,), (1,)), ((), ()))
        for h in range(ATT_HEADS):
            cols = slice(h * vd, (h + 1) * vd)
            q = q_ref[:, cols]
            k = k_ref[:, cols]
            v1 = jnp.concatenate([v_ref[:, cols], ones], axis=1)
            zero = jnp.zeros_like(q)
            for c in range(2):
                qm = jnp.where((lane >= ATT_HEAD_DIM) == bool(c), q, zero)
                s = lax.dot_general(qm, k, nt, preferred_element_type=F32)
                if masked:
                    s = jnp.where(mask, s, NEG_INF)
                idx = 2 * h + c
                m_old = m_scr[idx]
                m_row = jnp.max(s, axis=-1, keepdims=True)
                m_new = jnp.maximum(m_old, jnp.broadcast_to(m_row, m_old.shape))
                alpha = jnp.exp2(m_old - m_new)
                p = jnp.exp2(s - jnp.concatenate([m_new] * (tile // vd), axis=1)).astype(BF16)
                acc_scr[idx] = jnp.concatenate([alpha, alpha], axis=1) * acc_scr[idx] + _bdot(p, v1)
                m_scr[idx] = m_new

    @pl.when(ki < qi)
    def _():
        accumulate(False)

    @pl.when(ki == qi)
    def _():
        accumulate(True)
        lam = _lambda(lp_ref, lam_init)
        hg = hg_ref[...]
        for h in range(ATT_HEADS):
            a1 = acc_scr[2 * h]
            a2 = acc_scr[2 * h + 1]
            o = a1[:, :vd] / a1[:, vd:] - lam * (a2[:, :vd] / a2[:, vd:])
            o_ref[:, h * vd:(h + 1) * vd] = (_rms_rows(o, hg) * (1.0 - lam_init)).astype(o_ref.dtype)


def _cast_block_rows(rows, n_steps):
    for br in range(BF16_SUBLANES, rows + 1, BF16_SUBLANES):
        if rows % br == 0 and rows // br <= n_steps:
            return br
    raise ValueError((rows, n_steps))


def _attention(q, k, v, sq, skn, svn, cache_kt, cache_vr, layer, page_table, lp, hg, n_p, seq, lam_init,
               casts=()):
    tile = min(ATT_TILE, seq)
    assert seq % tile == 0 and tile % ATT_V_DIM == 0
    nq = seq // tile
    pairs = [(i, j) for i in range(nq) for j in range(i + 1)]
    n_pairs = len(pairs)
    n_steps = n_p * n_pairs
    n_s, dec, _ = sq.shape
    n_pages = page_table.shape[1]
    page = cache_kt.shape[3]
    n_seq = next(d for d in range(1, n_s + 1) if n_s % d == 0 and n_s // d <= n_steps)
    n_groups = n_s // n_seq
    qi_tab = jnp.asarray([a for a, _ in pairs], jnp.int32)
    ki_tab = jnp.asarray([b for _, b in pairs], jnp.int32)
    step = lambda b, t: b * n_pairs + t
    qspec = pl.BlockSpec((tile, ATT_WIDTH), lambda b, t, qi, ki, pt: (b * nq + qi[t], 0))
    kspec = pl.BlockSpec((tile, ATT_WIDTH), lambda b, t, qi, ki, pt: (b * nq + ki[t], 0))
    tok = pl.BlockSpec((n_seq, dec, QK_WIDTH), lambda b, t, qi, ki, pt: (jnp.minimum(step(b, t), n_groups - 1), 0, 0))
    seq_of = lambda b, t, u: jnp.minimum(step(b, t), n_groups - 1) * n_seq + u
    kpages = [pl.BlockSpec((None, None, QK_WIDTH, page),
                           lambda b, t, qi, ki, pt, u=u, j=j: (layer, pt[seq_of(b, t, u), j], 0, 0))
              for u in range(n_seq) for j in range(n_pages)]
    vpages = [pl.BlockSpec((None, None, page * ATT_HEADS, ATT_V_DIM),
                           lambda b, t, qi, ki, pt, u=u, j=j: (layer, pt[seq_of(b, t, u), j], 0, 0))
              for u in range(n_seq) for j in range(n_pages)]
    cast_specs, cast_blocks = [], []
    for w in casts:
        br = _cast_block_rows(w.shape[0], n_steps)
        nblk = w.shape[0] // br
        cast_blocks.append(nblk)
        cast_specs.append(pl.BlockSpec(
            (br, w.shape[1]),
            lambda b, t, qi, ki, pt, nblk=nblk: (jnp.minimum(step(b, t), nblk - 1), 0)))
    grid_spec = pltpu.PrefetchScalarGridSpec(
        num_scalar_prefetch=3,
        grid=(n_p, n_pairs),
        in_specs=[qspec, kspec, kspec,
                  pl.BlockSpec((4, ATT_HEAD_DIM), lambda b, t, qi, ki, pt: (0, 0)),
                  pl.BlockSpec((1, ATT_V_DIM), lambda b, t, qi, ki, pt: (0, 0))]
        + cast_specs + [tok, tok, tok] + kpages + vpages,
        out_specs=[qspec, tok] + cast_specs,
        scratch_shapes=[pltpu.VMEM((2 * ATT_HEADS, tile, ATT_V_DIM), F32),
                        pltpu.VMEM((2 * ATT_HEADS, tile, 2 * ATT_V_DIM), F32)],
    )
    return pl.pallas_call(
        functools.partial(_attn_kernel, tile=tile, lam_init=lam_init, cast_blocks=tuple(cast_blocks),
                          n_seq=n_seq, n_pages=n_pages),
        out_shape=[jax.ShapeDtypeStruct((n_p * seq, ATT_WIDTH), BF16),
                   jax.ShapeDtypeStruct((n_s, dec, ATT_WIDTH), BF16)]
        + [jax.ShapeDtypeStruct(w.shape, BF16) for w in casts],
        grid_spec=grid_spec,
        compiler_params=_cparams(("arbitrary", "arbitrary")),
        name="attention",
    )(qi_tab, ki_tab, page_table, q, k, v, lp, hg, *casts, sq, skn, svn,
      *([cache_kt] * (n_seq * n_pages)), *([cache_vr] * (n_seq * n_pages)))


def _attend_sample(q, kn, vn, kp, vp, lp_ref, hg_ref, dec, lam_init):
    n_pages = len(kp)
    page = kp[0].shape[1]
    grp = 2 * dec
    n_rows = ATT_HEADS * grp
    q = q.astype(F32)
    qt = jnp.concatenate([q] * (ATT_HEADS * 2), axis=0)
    r = lax.broadcasted_iota(jnp.int32, (n_rows, QK_WIDTH), 0)
    c = lax.broadcasted_iota(jnp.int32, (n_rows, QK_WIDTH), 1)
    qb = jnp.where(r // dec == c // ATT_HEAD_DIM, qt, 0.0).astype(BF16)
    nt = (((1,), (1,)), ((), ()))
    kt_past = jnp.concatenate([kp[j][...].astype(BF16) for j in range(n_pages)], axis=1)
    s_past = _bdot(qb, kt_past)
    s_new = lax.dot_general(qb, kn, nt, preferred_element_type=F32)
    rn = lax.broadcasted_iota(jnp.int32, (n_rows, dec), 0) % dec
    cn = lax.broadcasted_iota(jnp.int32, (n_rows, dec), 1)
    s_new = jnp.where(cn <= rn, s_new, NEG_INF)
    m = jnp.maximum(jnp.max(s_new, axis=-1, keepdims=True), jnp.max(s_past, axis=-1, keepdims=True))
    p_new = jnp.exp2(s_new - m)
    p_past = jnp.exp2(s_past - m)
    l = jnp.sum(p_new, axis=-1, keepdims=True) + jnp.sum(p_past, axis=-1, keepdims=True)
    acc_new = _bdot(p_new.astype(BF16), vn)
    lam = _lambda(lp_ref, lam_init)
    hg = hg_ref[...]
    outs = []
    for h in range(ATT_HEADS):
        rows = slice(h * grp, (h + 1) * grp)
        cols = slice(h * ATT_V_DIM, (h + 1) * ATT_V_DIM)
        v_past = jnp.concatenate([vp[j][pl.ds(h, page, stride=ATT_HEADS), :].astype(BF16)
                                  for j in range(n_pages)], axis=0)
        acc = acc_new[rows, cols] + _bdot(p_past[rows, :].astype(BF16), v_past)
        acc = acc / l[rows, :]
        o = acc[:dec] - lam * acc[dec:]
        outs.append(_rms_rows(o, hg) * (1.0 - lam_init))
    return jnp.concatenate(outs, axis=1)


def _merge_kernel(xp_ref, xs_ref, up_ref, us_ref, cb_ref, vin_ref, halo_ref, gate_ref,
                  yrp_ref, yrs_ref, ybp_ref, ybs_ref, vm1s_ref, vm2s_ref,
                  d_ref, wglu_ref, bglu_ref, cw_ref, wssm_ref, watt_ref, wconv_ref, wout_ref,
                  op_ref, os_ref, *, n_prompt_tiles, tiles_per_seq):
    i = pl.program_id(0)
    is_prompt = i < n_prompt_tiles
    yraw = jnp.where(is_prompt, yrp_ref[...], yrs_ref[...])
    yb = jnp.where(is_prompt, ybp_ref[...], ybs_ref[...])
    y = jax.nn.gelu(yraw + d_ref[...] * _read_group(up_ref, us_ref, is_prompt))
    ya = y * jax.nn.sigmoid(_bdot(y.astype(BF16), wglu_ref[...]) + bglu_ref[...])

    vin = vin_ref[...]
    row = lax.broadcasted_iota(jnp.int32, vin.shape, 0)
    halo = jnp.where(i % tiles_per_seq == 0, 0.0, halo_ref[...])
    h1 = jnp.broadcast_to(halo[SUBLANES - 1:SUBLANES, :], vin.shape)
    h2 = jnp.broadcast_to(halo[SUBLANES - 2:SUBLANES - 1, :], vin.shape)
    vm1 = jnp.where(row == 0, h1, pltpu.roll(vin, 1, axis=0))
    vm2 = jnp.where(row == 0, h2, jnp.where(row == 1, h1, pltpu.roll(vin, 2, axis=0)))
    vm1 = jnp.where(is_prompt, vm1, vm1s_ref[...])
    vm2 = jnp.where(is_prompt, vm2, vm2s_ref[...])
    conv = vm2 * cw_ref[0:1, :] + vm1 * cw_ref[1:2, :] + vin * cw_ref[2:3, :]
    yc = cb_ref[...] * conv
    merged = (gate_ref[:, 0:D_MODEL] * _bdot(ya.astype(BF16), wssm_ref[...])
              + gate_ref[:, D_MODEL:2 * D_MODEL] * _bdot(yb, watt_ref[...])
              + gate_ref[:, 2 * D_MODEL:3 * D_MODEL] * _bdot(yc.astype(BF16), wconv_ref[...]))
    x_new = _read_group(xp_ref, xs_ref, is_prompt) + _bdot(merged.astype(BF16), wout_ref[...])
    _write_group(op_ref, os_ref, is_prompt, x_new)


def _merge(xp, xs, up, us, cb, vin, gates, yr_p, yr_s, yb_p, yb_s, vm1_s, vm2_s,
           d, wglu, bglu, cw, wssm, watt, wconv, wout, tiles_per_seq):
    tm = ROW_TILE
    tp, ts = xp.shape[0], xs.shape[0]
    npt = tp // tm
    row = lambda w: pl.BlockSpec((tm, w), lambda i: (i, 0))
    prow = lambda w: _group_specs(tm, w, npt)[0]
    srow = lambda w: _group_specs(tm, w, npt)[1]
    halo = pl.BlockSpec((SUBLANES, CONV_WIDTH), lambda i: (jnp.maximum(i * (tm // SUBLANES) - 1, 0), 0))
    return pl.pallas_call(
        functools.partial(_merge_kernel, n_prompt_tiles=npt, tiles_per_seq=tiles_per_seq),
        out_shape=[jax.ShapeDtypeStruct((tp, D_MODEL), F32), jax.ShapeDtypeStruct((ts, D_MODEL), F32)],
        grid=((tp + ts) // tm,),
        in_specs=[prow(D_MODEL), srow(D_MODEL), prow(SSM_WIDTH), srow(SSM_WIDTH),
                  row(CONV_WIDTH), row(CONV_WIDTH), halo,
                  row(N_BRANCH * D_MODEL),
                  prow(SSM_WIDTH), srow(SSM_WIDTH), prow(ATT_WIDTH), srow(ATT_WIDTH),
                  srow(CONV_WIDTH), srow(CONV_WIDTH),
                  _const_spec((1, SSM_WIDTH)), _const_spec((SSM_WIDTH, SSM_WIDTH)),
                  _const_spec((1, SSM_WIDTH)), _const_spec((CONV_K, CONV_WIDTH)),
                  _const_spec((SSM_WIDTH, D_MODEL)), _const_spec((ATT_WIDTH, D_MODEL)),
                  _const_spec((CONV_WIDTH, D_MODEL)), _const_spec((D_MODEL, D_MODEL))],
        out_specs=[prow(D_MODEL), srow(D_MODEL)],
        compiler_params=_cparams(("arbitrary",)),
        name="merge",
    )(xp, xs, up, us, cb, vin, vin, gates, yr_p, yr_s, yb_p, yb_s, vm1_s, vm2_s,
      d, wglu, bglu, cw, wssm, watt, wconv, wout)


def _swiglu(h, w1, w3, w2):
    a = _bdot(h, w1)
    b = _bdot(h, w3)
    return _bdot((jax.nn.silu(a) * b).astype(BF16), w2)


def _ffn_kernel(xp_ref, xs_ref, g_ref, w1_ref, w3_ref, w2_ref, op_ref, os_ref, *, n_prompt_tiles):
    is_prompt = pl.program_id(0) < n_prompt_tiles
    x = _read_group(xp_ref, xs_ref, is_prompt)
    h = _rms_rows(x, g_ref[...]).astype(BF16)
    _write_group(op_ref, os_ref, is_prompt, x + _swiglu(h, w1_ref[...], w3_ref[...], w2_ref[...]))


def _ffn(xp, xs, g, w1, w3, w2):
    tm = ROW_TILE
    tp, ts = xp.shape[0], xs.shape[0]
    npt = tp // tm
    d_ff = w1.shape[1]
    rows = _group_specs(tm, D_MODEL, npt)
    return pl.pallas_call(
        functools.partial(_ffn_kernel, n_prompt_tiles=npt),
        out_shape=[jax.ShapeDtypeStruct((tp, D_MODEL), F32), jax.ShapeDtypeStruct((ts, D_MODEL), F32)],
        grid=((tp + ts) // tm,),
        in_specs=rows + [_const_spec((1, D_MODEL)), _const_spec((D_MODEL, d_ff)),
                         _const_spec((D_MODEL, d_ff)), _const_spec((d_ff, D_MODEL))],
        out_specs=rows,
        compiler_params=_cparams(("arbitrary",)),
        name="ffn",
    )(xp, xs, g, w1, w3, w2)


def _router_kernel(xp_ref, xs_ref, g_ref, wr_ref, tri_ref, h_ref, meta_ref, gate_ref, cnt_ref, carry,
                   *, n_prompt_tiles):
    @pl.when(pl.program_id(0) == 0)
    def _():
        carry[...] = jnp.zeros(carry.shape, F32)

    x = _read_group(xp_ref, xs_ref, pl.program_id(0) < n_prompt_tiles)
    h = _rms_rows(x, g_ref[...])
    h_ref[...] = _pack_halves(h)
    logits = jnp.dot(h, wr_ref[...], preferred_element_type=F32, precision=lax.Precision.HIGHEST)
    lane = lax.broadcasted_iota(jnp.int32, logits.shape, 1)
    logits = jnp.where(lane < N_EXPERTS, logits, -jnp.inf)
    big = jnp.int32(logits.shape[1])
    m1 = jnp.max(logits, axis=-1, keepdims=True)
    i1 = jnp.min(jnp.where(logits == m1, lane, big), axis=-1, keepdims=True)
    rest = jnp.where(lane == i1, -jnp.inf, logits)
    m2 = jnp.max(rest, axis=-1, keepdims=True)
    i2 = jnp.min(jnp.where(rest == m2, lane, big), axis=-1, keepdims=True)
    e = jnp.exp(m2 - m1)
    g1 = 1.0 / (1.0 + e)
    g2 = e / (1.0 + e)
    o1 = lane == i1
    o2 = lane == i2
    chosen = jnp.where(o1 | o2, 1.0, 0.0)
    base = _bdot(tri_ref[...], chosen.astype(BF16)) + carry[...]
    r1 = jnp.sum(jnp.where(o1, base, 0.0), axis=-1, keepdims=True).astype(jnp.int32)
    r2 = jnp.sum(jnp.where(o2, base, 0.0), axis=-1, keepdims=True).astype(jnp.int32)
    carry[...] = carry[...] + jnp.sum(chosen, axis=0, keepdims=True)
    cnt_ref[...] = carry[...]
    meta_ref[...] = jnp.where(lane == 0, i1, jnp.where(lane == 1, i2,
                              jnp.where(lane == 2, r1, jnp.where(lane == 3, r2, 0))))
    gate_ref[...] = jnp.where(lane == 0, g1, jnp.where(lane == 1, g2, 0.0))


def _router(xp, xs, g, wr_pad):
    tm = ROW_TILE
    t = xp.shape[0] + xs.shape[0]
    npt = xp.shape[0] // tm
    lanes = wr_pad.shape[1]
    tri = jnp.tri(tm, k=-1, dtype=BF16)
    row = lambda w: pl.BlockSpec((tm, w), lambda i: (i, 0))
    return pl.pallas_call(
        functools.partial(_router_kernel, n_prompt_tiles=npt),
        out_shape=[jax.ShapeDtypeStruct((t, D_MODEL // 2), jnp.uint32),
                   jax.ShapeDtypeStruct((t, lanes), jnp.int32),
                   jax.ShapeDtypeStruct((t, lanes), F32),
                   jax.ShapeDtypeStruct((1, lanes), F32)],
        grid=(t // tm,),
        in_specs=_group_specs(tm, D_MODEL, npt) + [_const_spec((1, D_MODEL)), _const_spec((D_MODEL, lanes)),
                                                   _const_spec((tm, tm))],
        out_specs=[row(D_MODEL // 2), row(lanes), row(lanes), _const_spec((1, lanes))],
        scratch_shapes=[pltpu.VMEM((1, lanes), F32)],
        compiler_params=_cparams(("arbitrary",)),
        name="router",
    )(xp, xs, g, wr_pad, tri)


def _sc_mesh():
    return plsc.VectorSubcoreMesh(core_axis_name="c", subcore_axis_name="s")


def _sc_worker_base(per_worker):
    return (lax.axis_index("s") * SC_CORES + lax.axis_index("c")) * per_worker


def _sc_chunk_rows(n, row_bytes):
    assert n % (SC_WORKERS * SUBLANES) == 0
    per_worker = n // SC_WORKERS
    fits = [r for r in range(SUBLANES, per_worker + 1, SUBLANES)
            if per_worker % r == 0 and r * row_bytes <= SC_CHUNK_BYTES]
    return per_worker, fits[-1]


def _sc_scatter_rows(x, idx0, idx1, n_out):
    n, w = x.shape
    per_worker, chunk = _sc_chunk_rows(n, w * x.dtype.itemsize)

    @functools.partial(
        pl.kernel, mesh=_sc_mesh(), out_type=jax.ShapeDtypeStruct((n_out, w), x.dtype),
        scratch_types=[pltpu.VMEM((chunk,), jnp.int32), pltpu.VMEM((chunk,), jnp.int32),
                       pltpu.VMEM((chunk, w), x.dtype), pltpu.SemaphoreType.DMA])
    def scatter(x_hbm, i0_hbm, i1_hbm, out_hbm, i0_v, i1_v, rows_v, sem):
        start = _sc_worker_base(per_worker)

        @pl.loop(0, per_worker // chunk)
        def _(c):
            rows = pl.ds(pl.multiple_of(start + c * chunk, SUBLANES), chunk)
            pltpu.sync_copy(x_hbm.at[rows], rows_v)
            pltpu.sync_copy(i0_hbm.at[rows], i0_v)
            pltpu.sync_copy(i1_hbm.at[rows], i1_v)
            pltpu.async_copy(rows_v, out_hbm.at[i0_v], sem).wait()
            pltpu.async_copy(rows_v, out_hbm.at[i1_v], sem).wait()

    return scatter(x, idx0, idx1)


def _sc_gather_rows(table, idx):
    n, w = idx.shape[0], table.shape[1]
    per_worker, chunk = _sc_chunk_rows(n, w * table.dtype.itemsize)

    @functools.partial(
        pl.kernel, mesh=_sc_mesh(), out_type=jax.ShapeDtypeStruct((n, w), table.dtype),
        scratch_types=[pltpu.VMEM((chunk,), jnp.int32), pltpu.VMEM((chunk, w), table.dtype),
                       pltpu.SemaphoreType.DMA])
    def gather(table_hbm, idx_hbm, out_hbm, idx_v, rows_v, sem):
        start = _sc_worker_base(per_worker)

        @pl.loop(0, per_worker // chunk)
        def _(c):
            rows = pl.ds(pl.multiple_of(start + c * chunk, SUBLANES), chunk)
            pltpu.sync_copy(idx_hbm.at[rows], idx_v)
            pltpu.async_copy(table_hbm.at[idx_v], rows_v, sem).wait()
            pltpu.sync_copy(rows_v, out_hbm.at[rows])

    return gather(table, idx)


def _moe_kernel(be_ref, nv_ref, x_ref, w1_ref, w3_ref, w2_ref, o_ref):
    del be_ref

    @pl.when(pl.program_id(0) < nv_ref[0])
    def _():
        x = _unpack_halves(x_ref[...]).astype(BF16)
        o_ref[...] = _pack_halves(_swiglu(x, w1_ref[...], w3_ref[...], w2_ref[...]))


def _moe_blocks(xs, block_expert, n_valid, w1, w3, w2):
    cap = xs.shape[0]
    tb = MOE_TILE
    d_ff = w1.shape[2]
    grid_spec = pltpu.PrefetchScalarGridSpec(
        num_scalar_prefetch=2,
        grid=(cap // tb,),
        in_specs=[pl.BlockSpec((tb, D_MODEL // 2), lambda i, be, nv: (jnp.minimum(i, nv[0] - 1), 0)),
                  pl.BlockSpec((None, D_MODEL, d_ff), lambda i, be, nv: (be[i], 0, 0)),
                  pl.BlockSpec((None, D_MODEL, d_ff), lambda i, be, nv: (be[i], 0, 0)),
                  pl.BlockSpec((None, d_ff, D_MODEL), lambda i, be, nv: (be[i], 0, 0))],
        out_specs=pl.BlockSpec((tb, D_MODEL // 2), lambda i, be, nv: (i, 0)),
    )
    return pl.pallas_call(
        _moe_kernel,
        out_shape=jax.ShapeDtypeStruct((cap, D_MODEL // 2), jnp.uint32),
        grid_spec=grid_spec,
        compiler_params=_cparams(("arbitrary",)),
        name="moe",
    )(block_expert, n_valid, xs, w1, w3, w2)


def _combine_kernel(xp_ref, xs_ref, a0_ref, a1_ref, gate_ref, op_ref, os_ref, *, n_prompt_tiles):
    is_prompt = pl.program_id(0) < n_prompt_tiles
    gate = gate_ref[...]
    y = (_read_group(xp_ref, xs_ref, is_prompt)
         + gate[:, 0:1] * _unpack_halves(a0_ref[...]) + gate[:, 1:2] * _unpack_halves(a1_ref[...]))
    _write_group(op_ref, os_ref, is_prompt, y)


def _combine(xp, xs, picked, gate):
    tm = ROW_TILE
    tp, ts = xp.shape[0], xs.shape[0]
    nt = (tp + ts) // tm
    rows = _group_specs(tm, D_MODEL, tp // tm)
    return pl.pallas_call(
        functools.partial(_combine_kernel, n_prompt_tiles=tp // tm),
        out_shape=[jax.ShapeDtypeStruct((tp, D_MODEL), F32), jax.ShapeDtypeStruct((ts, D_MODEL), F32)],
        grid=(nt,),
        in_specs=rows + [pl.BlockSpec((tm, D_MODEL // 2), lambda i: (i, 0)),
                         pl.BlockSpec((tm, D_MODEL // 2), lambda i: (i + nt, 0)),
                         pl.BlockSpec((tm, gate.shape[1]), lambda i: (i, 0))],
        out_specs=rows,
        compiler_params=_cparams(("arbitrary",)),
        name="moe_combine",
    )(xp, xs, picked, picked, gate)


def _moe(xp, xs, g, wr, w1, w3, w2):
    t = xp.shape[0] + xs.shape[0]
    tb = MOE_TILE
    lanes = 128
    wr_pad = jnp.pad(wr, ((0, 0), (0, lanes - N_EXPERTS)))
    h, meta, gate, cnt = _router(xp, xs, g, wr_pad)
    counts = cnt[0, :N_EXPERTS].astype(jnp.int32)
    padded = (counts + tb - 1) // tb * tb
    pad_ends = jnp.cumsum(padded)
    pad_starts = pad_ends - padded
    experts = jnp.arange(N_EXPERTS, dtype=jnp.int32)
    slot = lambda e, r: jnp.sum(jnp.where(e[:, None] == experts, pad_starts, 0), axis=1) + r
    dest0 = slot(meta[:, 0], meta[:, 2])
    dest1 = slot(meta[:, 1], meta[:, 3])
    nb = -(-(t * TOP_K + N_EXPERTS * (tb - 1)) // tb)
    block_expert = jnp.minimum(
        jnp.sum(pad_ends[None, :] <= (jnp.arange(nb, dtype=jnp.int32) * tb)[:, None], axis=1),
        N_EXPERTS - 1).astype(jnp.int32)
    n_valid = (pad_ends[-1:] // tb).astype(jnp.int32)
    slots = _sc_scatter_rows(h, dest0, dest1, nb * tb)
    out = _moe_blocks(slots, block_expert, n_valid, w1, w3, w2)
    picked = _sc_gather_rows(out, jnp.concatenate([dest0, dest1]))
    return _combine(xp, xs, picked, gate)


def kernel(x_prompt, x_sample, cache_k, cache_v, page_table, state_ssm_re, state_ssm_im, state_conv,
           norm_mix_g, norm_ffn_g, w_in, ssm_a_re, ssm_a_im, ssm_log_dt, ssm_b_re, ssm_b_im,
           ssm_c_re, ssm_c_im, ssm_d, ssm_w_glu, ssm_b_glu, q_norm_g, k_norm_g,
           lambda_q1, lambda_k1, lambda_q2, lambda_k2, head_norm_g, conv_w,
           w_br_ssm, w_br_att, w_br_conv, w_out, ffn_w1, ffn_w3, ffn_w2,
           router_w, moe_w1, moe_w3, moe_w2):
    n_p, seq, _ = x_prompt.shape
    n_s, dec, _ = x_sample.shape
    depth = w_in.shape[0]
    tp = n_p * seq
    ts = n_s * dec
    assert seq % ROW_TILE == 0 and ts % ROW_TILE == 0 and seq >= CONV_K - 1
    pool, page = cache_k.shape[1], cache_k.shape[2]
    cache_kt = cache_k.reshape(depth, pool, page, QK_WIDTH).transpose(0, 1, 3, 2)
    cache_vr = cache_v.reshape(depth, pool, page * ATT_HEADS, ATT_V_DIM)
    seg = jnp.kron(jnp.eye(QK_WIDTH // ATT_HEAD_DIM, dtype=F32),
                   jnp.full((ATT_HEAD_DIM, ATT_HEAD_DIM), 1.0 / ATT_HEAD_DIM, F32)).astype(BF16)
    n_rep = QK_WIDTH // ATT_HEAD_DIM

    xp = x_prompt.reshape(tp, D_MODEL)
    xs = x_sample.reshape(ts, D_MODEL)
    w_in_bf = w_in.astype(BF16)
    all_tables = _s5_tables(ssm_a_re, ssm_a_im, ssm_log_dt, ssm_b_re, ssm_b_im, ssm_c_re, ssm_c_im, dec)
    kv_prompt = None
    srp, sip, cvp = [], [], []
    ks, vs, srs, sis, cvs = [], [], [], [], []
    for l in range(depth):
        lam_init = 0.8 - 0.6 * math.exp(-0.3 * l)
        q, kb, vb, cb, vin, gates, kt, vr, k_s, v_s, u_p, u_s = _inproj(
            xp, xs, norm_mix_g[l][None], w_in_bf,
            jnp.tile(q_norm_g[l], n_rep)[None], jnp.tile(k_norm_g[l], n_rep)[None], seg,
            l, depth, n_p, seq, kv_prompt)
        kv_prompt = (kt, vr)

        yr_p, yr_s, p_re, p_im, s_re, s_im = _s5(u_p, u_s, state_ssm_re[l], state_ssm_im[l], all_tables, l,
                                                  n_p, seq, n_s, dec)

        lp = jnp.stack([lambda_q1[l], lambda_k1[l], lambda_q2[l], lambda_k2[l]])
        hg = head_norm_g[l][None]
        casts = ()
        if l + 1 < depth and (l + 1) % 2 == 1:
            e = (l + 1) // 2
            casts = (moe_w1[e].reshape(-1, moe_w1.shape[-1]), moe_w3[e].reshape(-1, moe_w3.shape[-1]),
                     moe_w2[e].reshape(-1, moe_w2.shape[-1]))
        yb_p, yb_s, *cast_out = _attention(
            q, kb, vb, q[tp:].reshape(n_s, dec, QK_WIDTH), kb[tp:].reshape(n_s, dec, QK_WIDTH),
            vb[tp:].reshape(n_s, dec, ATT_WIDTH), cache_kt, cache_vr, l, page_table, lp, hg,
            n_p, seq, lam_init, casts)
        yb_s = yb_s.reshape(ts, ATT_WIDTH)
        if casts:
            moe_bf = [c.reshape(w.shape[1:]) for c, w in zip(cast_out, (moe_w1, moe_w3, moe_w2))]

        ext_s = jnp.concatenate([state_conv[l], vin[tp:].reshape(n_s, dec, CONV_WIDTH)], axis=1)
        vm1_s = ext_s[:, 1:1 + dec].reshape(ts, CONV_WIDTH)
        vm2_s = ext_s[:, 0:dec].reshape(ts, CONV_WIDTH)
        xp, xs = _merge(xp, xs, u_p, u_s, cb, vin, gates, yr_p, yr_s, yb_p, yb_s, vm1_s, vm2_s,
                        ssm_d[l][None], ssm_w_glu[l].astype(BF16), ssm_b_glu[l][None], conv_w[l],
                        w_br_ssm[l].astype(BF16), w_br_att[l].astype(BF16), w_br_conv[l].astype(BF16),
                        w_out[l].astype(BF16), seq // ROW_TILE)

        i = l // 2
        if l % 2 == 0:
            xp, xs = _ffn(xp, xs, norm_ffn_g[l][None], ffn_w1[i].astype(BF16), ffn_w3[i].astype(BF16),
                          ffn_w2[i].astype(BF16))
        else:
            xp, xs = _moe(xp, xs, norm_ffn_g[l][None], router_w[i], *moe_bf)

        srp.append(p_re); sip.append(p_im)
        cvp.append(jnp.stack([vin[(b + 1) * seq - (CONV_K - 1):(b + 1) * seq] for b in range(n_p)]))
        ks.append(k_s.reshape(n_s, dec, ATT_HEADS, 2, ATT_HEAD_DIM))
        vs.append(v_s.reshape(n_s, dec, ATT_HEADS, ATT_V_DIM))
        srs.append(s_re); sis.append(s_im); cvs.append(ext_s[:, dec:])

    kt, vr = kv_prompt
    k_prompt = kt.reshape(depth, n_p, ATT_HEADS, 2, ATT_HEAD_DIM, seq).transpose(0, 1, 5, 2, 3, 4)
    v_prompt = vr.reshape(depth, n_p, seq, ATT_HEADS, ATT_V_DIM)
    return (xp.reshape(n_p, seq, D_MODEL), xs.reshape(n_s, dec, D_MODEL),
            k_prompt, v_prompt, jnp.stack(srp), jnp.stack(sip), jnp.stack(cvp),
            jnp.stack(ks), jnp.stack(vs), jnp.stack(srs), jnp.stack(sis), jnp.stack(cvs))
```

```python
import functools
import math

import jax
import jax.numpy as jnp
from jax import lax
from jax.experimental import pallas as pl
from jax.experimental.pallas import tpu as pltpu
from jax.experimental.pallas import tpu_sc as plsc

F32 = jnp.float32
BF16 = jnp.bfloat16

D_MODEL = 1024
SSM_WIDTH = 256
SSM_GROUP = 16
SSM_GROUPS = SSM_WIDTH // SSM_GROUP
SSM_STATE = 64
ATT_HEADS = 4
ATT_HEAD_DIM = 64
ATT_V_DIM = 2 * ATT_HEAD_DIM
QK_WIDTH = ATT_HEADS * 2 * ATT_HEAD_DIM
ATT_WIDTH = ATT_HEADS * ATT_V_DIM
CONV_WIDTH = 256
CONV_K = 3
N_BRANCH = 3
N_EXPERTS = 8
TOP_K = 2
EPS = 1e-6
NEG_INF = -1e30
LOG2_E = 1.4426950408889634

C_U = 0
C_Q = C_U + SSM_WIDTH
C_K = C_Q + QK_WIDTH
C_V = C_K + QK_WIDTH
C_CB = C_V + ATT_WIDTH
C_CC = C_CB + CONV_WIDTH
C_CH = C_CC + CONV_WIDTH
C_G = C_CH + CONV_WIDTH
IN_COLS = C_G + N_BRANCH * D_MODEL

SSM_CHUNK = 8
SUBLANES = 8
BF16_SUBLANES = 16
ROW_TILE = 512
ATT_TILE = 512
MOE_TILE = 256
SC_CORES = 2
SC_WORKERS = SC_CORES * 16
SC_CHUNK_BYTES = 192 * 1024
VMEM_LIMIT = 56 * 1024 * 1024


def _cparams(sem):
    return pltpu.CompilerParams(dimension_semantics=sem, vmem_limit_bytes=VMEM_LIMIT)


def _const_spec(shape):
    nd = len(shape)
    return pl.BlockSpec(shape, lambda *_: (0,) * nd)


def _bdot(a, b):
    return jnp.dot(a, b, preferred_element_type=F32)


def _rms_rows(x, g):
    ms = jnp.mean(x * x, axis=-1, keepdims=True)
    return x * lax.rsqrt(ms + EPS) * g


def _group_specs(tm, w, n_prompt_tiles):
    return [pl.BlockSpec((tm, w), lambda i: (jnp.minimum(i, n_prompt_tiles - 1), 0)),
            pl.BlockSpec((tm, w), lambda i: (jnp.maximum(i - n_prompt_tiles, 0), 0))]


def _read_group(p_ref, s_ref, is_prompt):
    return jnp.where(is_prompt, p_ref[...], s_ref[...])


def _write_group(p_ref, s_ref, is_prompt, val):
    @pl.when(is_prompt)
    def _():
        p_ref[...] = val

    @pl.when(jnp.logical_not(is_prompt))
    def _():
        s_ref[...] = val


def _pack_halves(x):
    half = x.shape[1] // 2
    bits = lambda t: lax.bitcast_convert_type(t.astype(BF16).astype(F32), jnp.uint32)
    return (bits(x[:, :half]) >> 16) | (bits(x[:, half:]) & jnp.uint32(0xFFFF0000))


def _unpack_halves(p):
    lo = lax.bitcast_convert_type(p << 16, F32)
    hi = lax.bitcast_convert_type(p & jnp.uint32(0xFFFF0000), F32)
    return jnp.concatenate([lo, hi], axis=1)


def _segment_rms(z, g, seg):
    ms = _bdot((z * z).astype(BF16), seg)
    return z * lax.rsqrt(ms + EPS) * g


def _inproj_kernel(xp_ref, xs_ref, g_ref, w_ref, qg_ref, kg_ref, seg_ref, *rest,
                   n_prompt_tiles, n_prev, layer):
    (q_ref, kb_ref, vb_ref, cb_ref, vin_ref, gate_ref,
     kt_ref, vr_ref, ks_ref, vs_ref, up_ref, us_ref) = rest[n_prev:]
    tm = q_ref.shape[0]
    is_prompt = pl.program_id(0) < n_prompt_tiles
    h = _rms_rows(_read_group(xp_ref, xs_ref, is_prompt), g_ref[...]).astype(BF16)

    def proj(a, b):
        return _bdot(h, w_ref[:, a:b])

    seg = seg_ref[...]
    _write_group(up_ref, us_ref, is_prompt, proj(C_U, C_Q))
    qn = _segment_rms(proj(C_Q, C_K), qg_ref[...], seg)
    q_ref[...] = (qn * (ATT_HEAD_DIM ** -0.5 * LOG2_E)).astype(BF16)
    kn = _segment_rms(proj(C_K, C_V), kg_ref[...], seg)
    kb_ref[...] = kn.astype(BF16)
    v = proj(C_V, C_CB)
    vb_ref[...] = v.astype(BF16)
    cb_ref[...] = proj(C_CB, C_CC)
    vin_ref[...] = proj(C_CC, C_CH) * proj(C_CH, C_G)
    for j in range(N_BRANCH):
        a = C_G + j * D_MODEL
        gate_ref[:, j * D_MODEL:(j + 1) * D_MODEL] = jax.nn.sigmoid(proj(a, a + D_MODEL)).astype(gate_ref.dtype)

    @pl.when(is_prompt)
    def _():
        if n_prev:
            kt_l, vr_l = kt_ref, vr_ref
        else:
            kt_l, vr_l = kt_ref.at[layer], vr_ref.at[layer]
            for other in range(kt_ref.shape[0]):
                if other != layer:
                    kt_ref[other] = jnp.zeros(kt_ref.shape[1:], F32)
                    vr_ref[other] = jnp.zeros(vr_ref.shape[1:], F32)
        kt_l[...] = kn.T
        for hd in range(ATT_HEADS):
            vr_l[pl.ds(hd, tm, stride=ATT_HEADS), :] = v[:, hd * ATT_V_DIM:(hd + 1) * ATT_V_DIM]

    @pl.when(jnp.logical_not(is_prompt))
    def _():
        ks_ref[...] = kn
        vs_ref[...] = v


def _inproj(xp, xs, g, w_bf, qg, kg, seg, layer, depth, n_p, seq, prev):
    tm = ROW_TILE
    tp, ts = xp.shape[0], xs.shape[0]
    t = tp + ts
    npt = tp // tm
    tps = seq // tm
    row = lambda w: pl.BlockSpec((tm, w), lambda i: (i, 0))
    pc = lambda i: jnp.minimum(i, npt - 1)
    outs = [(QK_WIDTH, BF16), (QK_WIDTH, BF16), (ATT_WIDTH, BF16),
            (CONV_WIDTH, F32), (CONV_WIDTH, F32), (N_BRANCH * D_MODEL, BF16)]
    out_shape = ([jax.ShapeDtypeStruct((t, w), d) for w, d in outs]
                 + [jax.ShapeDtypeStruct((depth, n_p, QK_WIDTH, seq), F32),
                    jax.ShapeDtypeStruct((depth, tp * ATT_HEADS, ATT_V_DIM), F32),
                    jax.ShapeDtypeStruct((ts, QK_WIDTH), F32), jax.ShapeDtypeStruct((ts, ATT_WIDTH), F32),
                    jax.ShapeDtypeStruct((tp, SSM_WIDTH), F32), jax.ShapeDtypeStruct((ts, SSM_WIDTH), F32)])
    srow = _group_specs(tm, QK_WIDTH, npt)[1]
    prev = () if prev is None else tuple(prev)
    lead, at = (None, layer) if prev else (depth, 0)
    out_specs = ([row(w) for w, _ in outs]
                 + [pl.BlockSpec((lead, None, QK_WIDTH, tm), lambda i: (at, pc(i) // tps, 0, pc(i) % tps)),
                    pl.BlockSpec((lead, tm * ATT_HEADS, ATT_V_DIM), lambda i: (at, pc(i), 0)),
                    srow, srow] + _group_specs(tm, SSM_WIDTH, npt))
    n_in = 7
    return pl.pallas_call(
        functools.partial(_inproj_kernel, n_prompt_tiles=npt, n_prev=len(prev), layer=layer),
        out_shape=out_shape,
        grid=(t // tm,),
        in_specs=(_group_specs(tm, D_MODEL, npt)
                  + [_const_spec((1, D_MODEL)),
                     pl.BlockSpec((None, D_MODEL, IN_COLS), lambda i: (layer, 0, 0)),
                     _const_spec((1, QK_WIDTH)), _const_spec((1, QK_WIDTH)),
                     _const_spec((QK_WIDTH, QK_WIDTH))]
                  + [pl.BlockSpec(memory_space=pl.ANY)] * len(prev)),
        out_specs=out_specs,
        input_output_aliases={n_in + j: len(outs) + j for j in range(len(prev))},
        compiler_params=_cparams(("arbitrary",)),
        name="inproj",
    )(xp, xs, g, w_bf, qg, kg, seg, *prev)


def _s5_intra(u, kb_ref, tc):
    rowmod = lax.broadcasted_iota(jnp.int32, u.shape, 0) % tc
    y = _bdot(u.astype(BF16), kb_ref[0])
    for m in range(1, tc):
        um = jnp.where(rowmod >= m, pltpu.roll(u, m, axis=0), 0.0)
        y = y + _bdot(um.astype(BF16), kb_ref[m])
    return y


def _s5_inject(u_refs, pb_ref, k0, n_chunks, tc):
    w = None
    for k in range(tc):
        rows = pl.ds(k, n_chunks, stride=tc)
        uk = jnp.concatenate([r[rows, :] for r in u_refs], axis=1).astype(BF16)
        d = _bdot(uk, pb_ref[k0 + k])
        w = d if w is None else w + d
    return w


def _s5_readout(y_ref, y_scrs, y, s_in, qb_ref, n_chunks, tc):
    lanes = y_scrs[0].shape[1]
    for h, scr in enumerate(y_scrs):
        scr[...] = y[:, h * lanes:(h + 1) * lanes]
    sb = s_in.astype(BF16)
    for k in range(tc):
        rows = pl.ds(k, n_chunks, stride=tc)
        yk = _bdot(sb, qb_ref[k])
        for h, scr in enumerate(y_scrs):
            scr[rows, :] = scr[rows, :] + yk[:, h * lanes:(h + 1) * lanes]
    y_ref[...] = jnp.concatenate([scr[...] for scr in y_scrs], axis=1)


def _cmul_add(a_re, a_im, s, w, half):
    s_re, s_im = s[:, :half], s[:, half:]
    return jnp.concatenate([a_re * s_re - a_im * s_im, a_re * s_im + a_im * s_re], axis=1) + w


def _s5_prompt_kernel(ua_ref, ub_ref, kb_ref, pb_ref, qb_ref, a_ref, y_ref, fs_ref,
                      carry, w_fold, s_fold, ya_scr, yb_scr, *, tc):
    n_b, tm, hw = ua_ref.shape
    n_chunks = tm // tc
    rows = n_b * n_chunks
    fold = carry.shape[0] // n_b
    hf = fold // 2

    @pl.when(pl.program_id(0) == 0)
    def _():
        carry[...] = jnp.zeros(carry.shape, F32)

    u = jnp.concatenate([ua_ref[...].reshape(n_b * tm, hw), ub_ref[...].reshape(n_b * tm, hw)], axis=1)
    y = _s5_intra(u, kb_ref, tc)
    w = None
    for k in range(tc):
        at_k = pl.ds(k, n_chunks, stride=tc)
        uk = jnp.concatenate([jnp.concatenate([ua_ref[b, at_k, :], ub_ref[b, at_k, :]], axis=1)
                              for b in range(n_b)], axis=0).astype(BF16)
        d = _bdot(uk, pb_ref[k])
        w = d if w is None else w + d
    for r in range(fold):
        w_fold[pl.ds(r, rows, stride=fold), :] = w[:, r * hw:(r + 1) * hw]
    a_re = a_ref[0]
    a_im = a_ref[1]
    for b in range(n_b):
        s = carry[b * fold:(b + 1) * fold, :]
        for c in range(n_chunks):
            at = slice((b * n_chunks + c) * fold, (b * n_chunks + c + 1) * fold)
            s_fold[at, :] = s
            s_re, s_im = s[:hf], s[hf:]
            s = jnp.concatenate([a_re * s_re - a_im * s_im, a_re * s_im + a_im * s_re], axis=0) + w_fold[at, :]
        carry[b * fold:(b + 1) * fold, :] = s
    fs_ref[...] = carry[...]
    s_in = jnp.concatenate([s_fold[pl.ds(r, rows, stride=fold), :] for r in range(fold)], axis=1)
    for h, scr in enumerate((ya_scr, yb_scr)):
        scr[...] = y[:, h * hw:(h + 1) * hw]
    sb = s_in.astype(BF16)
    for k in range(tc):
        at_k = pl.ds(k, rows, stride=tc)
        yk = _bdot(sb, qb_ref[k])
        for h, scr in enumerate((ya_scr, yb_scr)):
            scr[at_k, :] = scr[at_k, :] + yk[:, h * hw:(h + 1) * hw]
    for b in range(n_b):
        y_ref[b] = jnp.concatenate([ya_scr[b * tm:(b + 1) * tm, :], yb_scr[b * tm:(b + 1) * tm, :]], axis=1)


def _s5_sample_kernel(ua_ref, ub_ref, s0_ref, kb_ref, pb_ref, qb_ref, a_ref, y_ref, fs_ref,
                      ya_scr, yb_scr, *, tc, k0):
    n_chunks = ua_ref.shape[0] // tc
    half = a_ref.shape[1]
    y = _s5_intra(jnp.concatenate([ua_ref[...], ub_ref[...]], axis=1), kb_ref, tc)
    w = _s5_inject((ua_ref, ub_ref), pb_ref, k0, n_chunks, tc)
    s0 = s0_ref[...]
    fs_ref[...] = _cmul_add(a_ref[2:3, :], a_ref[3:4, :], s0, w, half)
    _s5_readout(y_ref, (ya_scr, yb_scr), y, s0, qb_ref, n_chunks, tc)


def _s5_tables_kernel(are_ref, aim_ref, ldt_ref, bre_ref, bim_ref, cre_ref, cim_ref,
                      kb_ref, pb_ref, qb_ref, a_ref, *, tc, dec):
    are = are_ref[...]
    aim = aim_ref[...]
    dt = jnp.exp(ldt_ref[...])

    def power(m):
        mag = jnp.exp(are * dt * m)
        ang = aim * dt * m
        return mag * jnp.cos(ang), mag * jnp.sin(ang)

    ab_re, ab_im = power(1.0)
    den = are * are + aim * aim
    nr = ab_re - 1.0
    cr = (nr * are + ab_im * aim) / den
    ci = (ab_im * are - nr * aim) / den
    gj, gp = bre_ref.shape
    row_g = lax.broadcasted_iota(jnp.int32, (gj, gp), 0) // SSM_GROUP
    col_g = lax.broadcasted_iota(jnp.int32, (gj, gp), 1) // SSM_STATE
    diag = row_g == col_g
    bre = bre_ref[...]
    bim = bim_ref[...]
    bb_re = jnp.where(diag, cr * bre - ci * bim, 0.0)
    bb_im = jnp.where(diag, cr * bim + ci * bre, 0.0)
    cc_re = jnp.where(diag, cre_ref[...], 0.0)
    cc_im = jnp.where(diag, cim_ref[...], 0.0)
    c_blk = jnp.concatenate([cc_re, -cc_im], axis=1).T
    for m in range(tc):
        pr, pi = power(float(m))
        pm = jnp.concatenate([pr * bb_re - pi * bb_im, pr * bb_im + pi * bb_re], axis=1)
        pb_ref[tc - 1 - m] = pm.astype(pb_ref.dtype)
        kb_ref[m] = jnp.dot(pm, c_blk, preferred_element_type=F32,
                            precision=lax.Precision.HIGHEST).astype(kb_ref.dtype)
        qr, qi = power(float(m + 1))
        qm = jnp.concatenate([qr * cc_re - qi * cc_im, -(qr * cc_im + qi * cc_re)], axis=1)
        qb_ref[m] = qm.T.astype(qb_ref.dtype)
    a_ref[...] = jnp.concatenate(list(power(float(tc)) + power(float(dec))), axis=0)


def _s5_tables(a_re, a_im, log_dt, b_re, b_im, c_re, c_im, dec):
    depth = a_re.shape[0]
    tc = SSM_CHUNK
    g, p, j = SSM_GROUPS, SSM_STATE, SSM_GROUP
    gp, gj = g * p, g * j
    rowv = lambda t: t.reshape(depth, 1, gp)
    ldt = jnp.repeat(log_dt, p, axis=1).reshape(depth, 1, gp)
    bt = lambda t: jnp.tile(t.transpose(0, 1, 3, 2).reshape(depth, gj, p), (1, 1, g))
    ct = lambda t: jnp.tile(t.reshape(depth, gj, p), (1, 1, g))
    lay = lambda *shape: pl.BlockSpec((None,) + shape, lambda l: (l,) + (0,) * len(shape))
    return pl.pallas_call(
        functools.partial(_s5_tables_kernel, tc=tc, dec=dec),
        out_shape=[jax.ShapeDtypeStruct((depth, tc, gj, gj), BF16),
                   jax.ShapeDtypeStruct((depth, tc, gj, 2 * gp), BF16),
                   jax.ShapeDtypeStruct((depth, tc, 2 * gp, gj), BF16),
                   jax.ShapeDtypeStruct((depth, 4, gp), F32)],
        grid=(depth,),
        in_specs=[lay(1, gp)] * 3 + [lay(gj, gp)] * 4,
        out_specs=[lay(tc, gj, gj), lay(tc, gj, 2 * gp), lay(tc, 2 * gp, gj), lay(4, gp)],
        compiler_params=_cparams(("arbitrary",)),
        name="s5_tables",
    )(rowv(a_re), rowv(a_im), ldt, bt(b_re), bt(b_im), ct(c_re), ct(c_im))


def _s5(up, us, s0_re, s0_im, tables, layer, n_p, seq, n_s, dec):
    kb, pb, qb, adec = tables
    lay = lambda t: pl.BlockSpec((None,) + t.shape[1:], lambda i: (layer,) + (0,) * (t.ndim - 1))
    tc = SSM_CHUNK
    g, p = SSM_GROUPS, SSM_STATE
    sw = 2 * g * p
    tm = ROW_TILE
    tp = n_p * seq
    ts = n_s * dec
    hw = SSM_WIDTH // 2
    fold = sw // hw
    assert dec <= tc and seq % tm == 0 and tm % tc == 0
    up3 = up.reshape(n_p, seq, SSM_WIDTH)
    chunk_rows = n_p * (tm // tc)
    yp, fsp = pl.pallas_call(
        functools.partial(_s5_prompt_kernel, tc=tc),
        out_shape=[jax.ShapeDtypeStruct((n_p, seq, SSM_WIDTH), F32),
                   jax.ShapeDtypeStruct((n_p * fold, hw), F32)],
        grid=(seq // tm,),
        in_specs=[pl.BlockSpec((n_p, tm, hw), lambda i: (0, i, 0)),
                  pl.BlockSpec((n_p, tm, hw), lambda i: (0, i, 1)),
                  lay(kb), lay(pb), lay(qb),
                  pl.BlockSpec((None, 4, fold // 2, hw), lambda i: (layer, 0, 0, 0))],
        out_specs=[pl.BlockSpec((n_p, tm, SSM_WIDTH), lambda i: (0, i, 0)),
                   _const_spec((n_p * fold, hw))],
        scratch_shapes=[pltpu.VMEM((n_p * fold, hw), F32),
                        pltpu.VMEM((chunk_rows * fold, hw), F32), pltpu.VMEM((chunk_rows * fold, hw), F32),
                        pltpu.VMEM((n_p * tm, hw), F32), pltpu.VMEM((n_p * tm, hw), F32)],
        compiler_params=_cparams(("arbitrary",)),
        name="s5_prompt",
    )(up3, up3, kb, pb, qb, adec.reshape(adec.shape[0], 4, fold // 2, hw))
    yp = yp.reshape(tp, SSM_WIDTH)
    s0 = jnp.concatenate([s0_re.reshape(n_s, g * p), s0_im.reshape(n_s, g * p)], axis=1)
    ys, fss = pl.pallas_call(
        functools.partial(_s5_sample_kernel, tc=dec, k0=tc - dec),
        out_shape=[jax.ShapeDtypeStruct((ts, SSM_WIDTH), F32), jax.ShapeDtypeStruct((n_s, sw), F32)],
        grid=(1,),
        in_specs=[pl.BlockSpec((ts, hw), lambda i: (0, 0)), pl.BlockSpec((ts, hw), lambda i: (0, 1)),
                  _const_spec((n_s, sw)),
                  lay(kb), lay(pb), lay(qb), lay(adec)],
        out_specs=[_const_spec((ts, SSM_WIDTH)), _const_spec((n_s, sw))],
        scratch_shapes=[pltpu.VMEM((ts, hw), F32), pltpu.VMEM((ts, hw), F32)],
        compiler_params=_cparams(("arbitrary",)),
        name="s5_sample",
    )(us, us, s0, kb, pb, qb, adec)
    half = g * p
    fsp = fsp.reshape(n_p, sw)
    st = lambda a, n: a.reshape(n, g, p)
    return (yp, ys, st(fsp[:, :half], n_p), st(fsp[:, half:], n_p),
            st(fss[:, :half], n_s), st(fss[:, half:], n_s))


def _lambda(lp_ref, lam_init):
    lp = lp_ref[...]
    s1 = jnp.sum(lp[0:1, :] * lp[1:2, :], axis=-1, keepdims=True)
    s2 = jnp.sum(lp[2:3, :] * lp[3:4, :], axis=-1, keepdims=True)
    return jnp.exp(s1) - jnp.exp(s2) + lam_init


def _attn_kernel(qi_ref, ki_ref, pg_ref, q_ref, k_ref, v_ref, lp_ref, hg_ref, *rest,
                 tile, lam_init, cast_blocks, n_seq, n_pages, layer):
    n_cast = len(cast_blocks)
    n_pg = n_seq * n_pages
    cast_in = rest[:n_cast]
    sq_ref, skn_ref, svn_ref, ckt_hbm, cvr_hbm = rest[n_cast:n_cast + 5]
    o_ref, so_ref = rest[n_cast + 5:n_cast + 7]
    cast_out = rest[n_cast + 7:2 * n_cast + 7]
    m_scr, acc_scr, kbuf, vbuf, sem = rest[2 * n_cast + 7:]
    dec = sq_ref.shape[1]
    t = pl.program_id(1)
    qi = qi_ref[t]
    ki = ki_ref[t]
    vd = ATT_V_DIM
    step = pl.program_id(0) * pl.num_programs(1) + t
    last_step = pl.num_programs(0) * pl.num_programs(1) - 1
    slot = step % 2

    def page_copies(s, to_slot, c):
        pg = pg_ref[s * n_pg + c]
        return (pltpu.make_async_copy(ckt_hbm.at[layer, pg], kbuf.at[to_slot, c], sem.at[to_slot]),
                pltpu.make_async_copy(cvr_hbm.at[layer, pg], vbuf.at[to_slot, c], sem.at[to_slot]))

    @pl.when(step == 0)
    def _():
        for c in range(n_pg):
            for cp in page_copies(0, 0, c):
                cp.start()

    for c in range(n_pg):
        for cp in page_copies(step, slot, c):
            cp.wait()

    for src, dst, n_blocks in zip(cast_in, cast_out, cast_blocks):
        @pl.when(step < n_blocks)
        def _(src=src, dst=dst):
            dst[...] = src[...].astype(dst.dtype)

    @pl.when(ki == 0)
    def _():
        m_scr[...] = jnp.full(m_scr.shape, NEG_INF, F32)
        acc_scr[...] = jnp.zeros(acc_scr.shape, F32)

    def accumulate(masked):
        for u in range(n_seq):
            kp = [kbuf.at[slot, u * n_pages + j] for j in range(n_pages)]
            vp = [vbuf.at[slot, u * n_pages + j] for j in range(n_pages)]
            o_s = _attend_sample(sq_ref[u], skn_ref[u], svn_ref[u], kp, vp, lp_ref, hg_ref, dec, lam_init)
            so_ref[u] = o_s.astype(so_ref.dtype)
        lane = lax.broadcasted_iota(jnp.int32, (tile, vd), 1)
        ones = jnp.ones((tile, vd), BF16)
        if masked:
            mask = (lax.broadcasted_iota(jnp.int32, (tile, tile), 1)
                    <= lax.broadcasted_iota(jnp.int32, (tile, tile), 0))
        nt = (((1,), (1,)), ((), ()))
        per_head = -(-n_pg // ATT_HEADS)
        for h in range(ATT_HEADS):
            for c in range(h * per_head, min((h + 1) * per_head, n_pg)):
                for cp in page_copies(step + 1, 1 - slot, c):
                    cp.start()
            cols = slice(h * vd, (h + 1) * vd)
            q = q_ref[:, cols]
            k = k_ref[:, cols]
            v1 = jnp.concatenate([v_ref[:, cols], ones], axis=1)
            zero = jnp.zeros_like(q)
            for c in range(2):
                qm = jnp.where((lane >= ATT_HEAD_DIM) == bool(c), q, zero)
                s = lax.dot_general(qm, k, nt, preferred_element_type=F32)
                if masked:
                    s = jnp.where(mask, s, NEG_INF)
                idx = 2 * h + c
                m_old = m_scr[idx]
                m_row = jnp.max(s, axis=-1, keepdims=True)
                m_new = jnp.maximum(m_old, jnp.broadcast_to(m_row, m_old.shape))
                alpha = jnp.exp2(m_old - m_new)
                p = jnp.exp2(s - jnp.concatenate([m_new] * (tile // vd), axis=1)).astype(BF16)
                acc_scr[idx] = jnp.concatenate([alpha, alpha], axis=1) * acc_scr[idx] + _bdot(p, v1)
                m_scr[idx] = m_new

    @pl.when(ki < qi)
    def _():
        accumulate(False)

    @pl.when(ki == qi)
    def _():
        accumulate(True)
        lam = _lambda(lp_ref, lam_init)
        hg = hg_ref[...]
        for h in range(ATT_HEADS):
            a1 = acc_scr[2 * h]
            a2 = acc_scr[2 * h + 1]
            o = a1[:, :vd] / a1[:, vd:] - lam * (a2[:, :vd] / a2[:, vd:])
            o_ref[:, h * vd:(h + 1) * vd] = (_rms_rows(o, hg) * (1.0 - lam_init)).astype(o_ref.dtype)

    @pl.when(step == last_step)
    def _():
        for c in range(n_pg):
            for cp in page_copies(step + 1, 1 - slot, c):
                cp.wait()


def _cast_block_rows(rows, n_steps):
    for br in range(BF16_SUBLANES, rows + 1, BF16_SUBLANES):
        if rows % br == 0 and rows // br <= n_steps:
            return br
    raise ValueError((rows, n_steps))


def _attention(q, k, v, sq, skn, svn, cache_kt, cache_vr, layer, page_table, lp, hg, n_p, seq, lam_init,
               casts=()):
    tile = min(ATT_TILE, seq)
    assert seq % tile == 0 and tile % ATT_V_DIM == 0
    nq = seq // tile
    pairs = [(i, j) for i in range(nq) for j in range(i + 1)]
    n_pairs = len(pairs)
    n_steps = n_p * n_pairs
    n_s, dec, _ = sq.shape
    n_pages = page_table.shape[1]
    page = cache_kt.shape[3]
    n_seq = next(d for d in range(1, n_s + 1) if n_s % d == 0 and n_s // d <= n_steps)
    n_groups = n_s // n_seq
    qi_tab = jnp.asarray([a for a, _ in pairs], jnp.int32)
    ki_tab = jnp.asarray([b for _, b in pairs], jnp.int32)
    step = lambda b, t: b * n_pairs + t
    qspec = pl.BlockSpec((tile, ATT_WIDTH), lambda b, t, qi, ki, pt: (b * nq + qi[t], 0))
    kspec = pl.BlockSpec((tile, ATT_WIDTH), lambda b, t, qi, ki, pt: (b * nq + ki[t], 0))
    tok = pl.BlockSpec((n_seq, dec, QK_WIDTH), lambda b, t, qi, ki, pt: (jnp.minimum(step(b, t), n_groups - 1), 0, 0))
    groups = jnp.minimum(jnp.arange(n_steps + 1), n_groups - 1)
    step_pages = page_table.reshape(n_groups, n_seq * n_pages)[groups].reshape(-1)
    n_pg = n_seq * n_pages
    cast_specs, cast_blocks = [], []
    for w in casts:
        br = _cast_block_rows(w.shape[0], n_steps)
        nblk = w.shape[0] // br
        cast_blocks.append(nblk)
        cast_specs.append(pl.BlockSpec(
            (br, w.shape[1]),
            lambda b, t, qi, ki, pt, nblk=nblk: (jnp.minimum(step(b, t), nblk - 1), 0)))
    grid_spec = pltpu.PrefetchScalarGridSpec(
        num_scalar_prefetch=3,
        grid=(n_p, n_pairs),
        in_specs=[qspec, kspec, kspec,
                  pl.BlockSpec((4, ATT_HEAD_DIM), lambda b, t, qi, ki, pt: (0, 0)),
                  pl.BlockSpec((1, ATT_V_DIM), lambda b, t, qi, ki, pt: (0, 0))]
        + cast_specs + [tok, tok, tok, pl.BlockSpec(memory_space=pl.ANY), pl.BlockSpec(memory_space=pl.ANY)],
        out_specs=[qspec, tok] + cast_specs,
        scratch_shapes=[pltpu.VMEM((2 * ATT_HEADS, tile, ATT_V_DIM), F32),
                        pltpu.VMEM((2 * ATT_HEADS, tile, 2 * ATT_V_DIM), F32),
                        pltpu.VMEM((2, n_pg, QK_WIDTH, page), F32),
                        pltpu.VMEM((2, n_pg, page * ATT_HEADS, ATT_V_DIM), F32),
                        pltpu.SemaphoreType.DMA((2,))],
    )
    return pl.pallas_call(
        functools.partial(_attn_kernel, tile=tile, lam_init=lam_init, cast_blocks=tuple(cast_blocks),
                          n_seq=n_seq, n_pages=n_pages, layer=layer),
        out_shape=[jax.ShapeDtypeStruct((n_p * seq, ATT_WIDTH), BF16),
                   jax.ShapeDtypeStruct((n_s, dec, ATT_WIDTH), BF16)]
        + [jax.ShapeDtypeStruct(w.shape, BF16) for w in casts],
        grid_spec=grid_spec,
        compiler_params=_cparams(("arbitrary", "arbitrary")),
        name="attention",
    )(qi_tab, ki_tab, step_pages, q, k, v, lp, hg, *casts, sq, skn, svn, cache_kt, cache_vr)


def _attend_sample(q, kn, vn, kp, vp, lp_ref, hg_ref, dec, lam_init):
    n_pages = len(kp)
    page = kp[0].shape[1]
    grp = 2 * dec
    n_rows = ATT_HEADS * grp
    q = q.astype(F32)
    qt = jnp.concatenate([q] * (ATT_HEADS * 2), axis=0)
    r = lax.broadcasted_iota(jnp.int32, (n_rows, QK_WIDTH), 0)
    c = lax.broadcasted_iota(jnp.int32, (n_rows, QK_WIDTH), 1)
    qb = jnp.where(r // dec == c // ATT_HEAD_DIM, qt, 0.0).astype(BF16)
    nt = (((1,), (1,)), ((), ()))
    kt_past = jnp.concatenate([kp[j][...].astype(BF16) for j in range(n_pages)], axis=1)
    s_past = _bdot(qb, kt_past)
    s_new = lax.dot_general(qb, kn, nt, preferred_element_type=F32)
    rn = lax.broadcasted_iota(jnp.int32, (n_rows, dec), 0) % dec
    cn = lax.broadcasted_iota(jnp.int32, (n_rows, dec), 1)
    s_new = jnp.where(cn <= rn, s_new, NEG_INF)
    m = jnp.maximum(jnp.max(s_new, axis=-1, keepdims=True), jnp.max(s_past, axis=-1, keepdims=True))
    p_new = jnp.exp2(s_new - m)
    p_past = jnp.exp2(s_past - m)
    l = jnp.sum(p_new, axis=-1, keepdims=True) + jnp.sum(p_past, axis=-1, keepdims=True)
    acc_new = _bdot(p_new.astype(BF16), vn)
    lam = _lambda(lp_ref, lam_init)
    hg = hg_ref[...]
    outs = []
    for h in range(ATT_HEADS):
        rows = slice(h * grp, (h + 1) * grp)
        cols = slice(h * ATT_V_DIM, (h + 1) * ATT_V_DIM)
        v_past = jnp.concatenate([vp[j][pl.ds(h, page, stride=ATT_HEADS), :].astype(BF16)
                                  for j in range(n_pages)], axis=0)
        acc = acc_new[rows, cols] + _bdot(p_past[rows, :].astype(BF16), v_past)
        acc = acc / l[rows, :]
        o = acc[:dec] - lam * acc[dec:]
        outs.append(_rms_rows(o, hg) * (1.0 - lam_init))
    return jnp.concatenate(outs, axis=1)


def _merge_kernel(xp_ref, xs_ref, up_ref, us_ref, cb_ref, vin_ref, halo_ref, gate_ref,
                  yrp_ref, yrs_ref, ybp_ref, ybs_ref, vm1s_ref, vm2s_ref,
                  d_ref, wglu_ref, bglu_ref, cw_ref, wssm_ref, watt_ref, wconv_ref, wout_ref,
                  op_ref, os_ref, *, n_prompt_tiles, tiles_per_seq):
    i = pl.program_id(0)
    is_prompt = i < n_prompt_tiles
    yraw = jnp.where(is_prompt, yrp_ref[...], yrs_ref[...])
    yb = jnp.where(is_prompt, ybp_ref[...], ybs_ref[...])
    y = jax.nn.gelu(yraw + d_ref[...] * _read_group(up_ref, us_ref, is_prompt))
    ya = y * jax.nn.sigmoid(_bdot(y.astype(BF16), wglu_ref[...]) + bglu_ref[...])

    vin = vin_ref[...]
    row = lax.broadcasted_iota(jnp.int32, vin.shape, 0)
    halo = jnp.where(i % tiles_per_seq == 0, 0.0, halo_ref[...])
    h1 = jnp.broadcast_to(halo[SUBLANES - 1:SUBLANES, :], vin.shape)
    h2 = jnp.broadcast_to(halo[SUBLANES - 2:SUBLANES - 1, :], vin.shape)
    vm1 = jnp.where(row == 0, h1, pltpu.roll(vin, 1, axis=0))
    vm2 = jnp.where(row == 0, h2, jnp.where(row == 1, h1, pltpu.roll(vin, 2, axis=0)))
    vm1 = jnp.where(is_prompt, vm1, vm1s_ref[...])
    vm2 = jnp.where(is_prompt, vm2, vm2s_ref[...])
    conv = vm2 * cw_ref[0:1, :] + vm1 * cw_ref[1:2, :] + vin * cw_ref[2:3, :]
    yc = cb_ref[...] * conv
    merged = (gate_ref[:, 0:D_MODEL] * _bdot(ya.astype(BF16), wssm_ref[...])
              + gate_ref[:, D_MODEL:2 * D_MODEL] * _bdot(yb, watt_ref[...])
              + gate_ref[:, 2 * D_MODEL:3 * D_MODEL] * _bdot(yc.astype(BF16), wconv_ref[...]))
    x_new = _read_group(xp_ref, xs_ref, is_prompt) + _bdot(merged.astype(BF16), wout_ref[...])
    _write_group(op_ref, os_ref, is_prompt, x_new)


def _merge(xp, xs, up, us, cb, vin, gates, yr_p, yr_s, yb_p, yb_s, vm1_s, vm2_s,
           d, wglu, bglu, cw, wssm, watt, wconv, wout, tiles_per_seq):
    tm = ROW_TILE
    tp, ts = xp.shape[0], xs.shape[0]
    npt = tp // tm
    row = lambda w: pl.BlockSpec((tm, w), lambda i: (i, 0))
    prow = lambda w: _group_specs(tm, w, npt)[0]
    srow = lambda w: _group_specs(tm, w, npt)[1]
    halo = pl.BlockSpec((SUBLANES, CONV_WIDTH), lambda i: (jnp.maximum(i * (tm // SUBLANES) - 1, 0), 0))
    return pl.pallas_call(
        functools.partial(_merge_kernel, n_prompt_tiles=npt, tiles_per_seq=tiles_per_seq),
        out_shape=[jax.ShapeDtypeStruct((tp, D_MODEL), F32), jax.ShapeDtypeStruct((ts, D_MODEL), F32)],
        grid=((tp + ts) // tm,),
        in_specs=[prow(D_MODEL), srow(D_MODEL), prow(SSM_WIDTH), srow(SSM_WIDTH),
                  row(CONV_WIDTH), row(CONV_WIDTH), halo,
                  row(N_BRANCH * D_MODEL),
                  prow(SSM_WIDTH), srow(SSM_WIDTH), prow(ATT_WIDTH), srow(ATT_WIDTH),
                  srow(CONV_WIDTH), srow(CONV_WIDTH),
                  _const_spec((1, SSM_WIDTH)), _const_spec((SSM_WIDTH, SSM_WIDTH)),
                  _const_spec((1, SSM_WIDTH)), _const_spec((CONV_K, CONV_WIDTH)),
                  _const_spec((SSM_WIDTH, D_MODEL)), _const_spec((ATT_WIDTH, D_MODEL)),
                  _const_spec((CONV_WIDTH, D_MODEL)), _const_spec((D_MODEL, D_MODEL))],
        out_specs=[prow(D_MODEL), srow(D_MODEL)],
        compiler_params=_cparams(("arbitrary",)),
        name="merge",
    )(xp, xs, up, us, cb, vin, vin, gates, yr_p, yr_s, yb_p, yb_s, vm1_s, vm2_s,
      d, wglu, bglu, cw, wssm, watt, wconv, wout)


def _swiglu(h, w1, w3, w2):
    a = _bdot(h, w1)
    b = _bdot(h, w3)
    return _bdot((jax.nn.silu(a) * b).astype(BF16), w2)


def _ffn_kernel(xp_ref, xs_ref, g_ref, w1_ref, w3_ref, w2_ref, op_ref, os_ref, *, n_prompt_tiles):
    is_prompt = pl.program_id(0) < n_prompt_tiles
    x = _read_group(xp_ref, xs_ref, is_prompt)
    h = _rms_rows(x, g_ref[...]).astype(BF16)
    _write_group(op_ref, os_ref, is_prompt, x + _swiglu(h, w1_ref[...], w3_ref[...], w2_ref[...]))


def _ffn(xp, xs, g, w1, w3, w2):
    tm = ROW_TILE
    tp, ts = xp.shape[0], xs.shape[0]
    npt = tp // tm
    d_ff = w1.shape[1]
    rows = _group_specs(tm, D_MODEL, npt)
    return pl.pallas_call(
        functools.partial(_ffn_kernel, n_prompt_tiles=npt),
        out_shape=[jax.ShapeDtypeStruct((tp, D_MODEL), F32), jax.ShapeDtypeStruct((ts, D_MODEL), F32)],
        grid=((tp + ts) // tm,),
        in_specs=rows + [_const_spec((1, D_MODEL)), _const_spec((D_MODEL, d_ff)),
                         _const_spec((D_MODEL, d_ff)), _const_spec((d_ff, D_MODEL))],
        out_specs=rows,
        compiler_params=_cparams(("arbitrary",)),
        name="ffn",
    )(xp, xs, g, w1, w3, w2)


def _router_kernel(xp_ref, xs_ref, g_ref, wr_ref, tri_ref, h_ref, meta_ref, gate_ref, cnt_ref, carry,
                   *, n_prompt_tiles):
    @pl.when(pl.program_id(0) == 0)
    def _():
        carry[...] = jnp.zeros(carry.shape, F32)

    x = _read_group(xp_ref, xs_ref, pl.program_id(0) < n_prompt_tiles)
    h = _rms_rows(x, g_ref[...])
    h_ref[...] = _pack_halves(h)
    logits = jnp.dot(h, wr_ref[...], preferred_element_type=F32, precision=lax.Precision.HIGHEST)
    lane = lax.broadcasted_iota(jnp.int32, logits.shape, 1)
    logits = jnp.where(lane < N_EXPERTS, logits, -jnp.inf)
    big = jnp.int32(logits.shape[1])
    m1 = jnp.max(logits, axis=-1, keepdims=True)
    i1 = jnp.min(jnp.where(logits == m1, lane, big), axis=-1, keepdims=True)
    rest = jnp.where(lane == i1, -jnp.inf, logits)
    m2 = jnp.max(rest, axis=-1, keepdims=True)
    i2 = jnp.min(jnp.where(rest == m2, lane, big), axis=-1, keepdims=True)
    e = jnp.exp(m2 - m1)
    g1 = 1.0 / (1.0 + e)
    g2 = e / (1.0 + e)
    o1 = lane == i1
    o2 = lane == i2
    chosen = jnp.where(o1 | o2, 1.0, 0.0)
    base = _bdot(tri_ref[...], chosen.astype(BF16)) + carry[...]
    r1 = jnp.sum(jnp.where(o1, base, 0.0), axis=-1, keepdims=True).astype(jnp.int32)
    r2 = jnp.sum(jnp.where(o2, base, 0.0), axis=-1, keepdims=True).astype(jnp.int32)
    carry[...] = carry[...] + jnp.sum(chosen, axis=0, keepdims=True)
    cnt_ref[...] = carry[...]
    meta_ref[...] = jnp.where(lane == 0, i1, jnp.where(lane == 1, i2,
                              jnp.where(lane == 2, r1, jnp.where(lane == 3, r2, 0))))
    gate_ref[...] = jnp.where(lane == 0, g1, jnp.where(lane == 1, g2, 0.0))


def _router(xp, xs, g, wr_pad):
    tm = ROW_TILE
    t = xp.shape[0] + xs.shape[0]
    npt = xp.shape[0] // tm
    lanes = wr_pad.shape[1]
    tri = jnp.tri(tm, k=-1, dtype=BF16)
    row = lambda w: pl.BlockSpec((tm, w), lambda i: (i, 0))
    return pl.pallas_call(
        functools.partial(_router_kernel, n_prompt_tiles=npt),
        out_shape=[jax.ShapeDtypeStruct((t, D_MODEL // 2), jnp.uint32),
                   jax.ShapeDtypeStruct((t, lanes), jnp.int32),
                   jax.ShapeDtypeStruct((t, lanes), F32),
                   jax.ShapeDtypeStruct((1, lanes), F32)],
        grid=(t // tm,),
        in_specs=_group_specs(tm, D_MODEL, npt) + [_const_spec((1, D_MODEL)), _const_spec((D_MODEL, lanes)),
                                                   _const_spec((tm, tm))],
        out_specs=[row(D_MODEL // 2), row(lanes), row(lanes), _const_spec((1, lanes))],
        scratch_shapes=[pltpu.VMEM((1, lanes), F32)],
        compiler_params=_cparams(("arbitrary",)),
        name="router",
    )(xp, xs, g, wr_pad, tri)


def _sc_mesh():
    return plsc.VectorSubcoreMesh(core_axis_name="c", subcore_axis_name="s")


def _sc_worker_base(per_worker):
    return (lax.axis_index("s") * SC_CORES + lax.axis_index("c")) * per_worker


def _sc_chunk_rows(n, row_bytes):
    assert n % (SC_WORKERS * SUBLANES) == 0
    per_worker = n // SC_WORKERS
    fits = [r for r in range(SUBLANES, per_worker + 1, SUBLANES)
            if per_worker % r == 0 and r * row_bytes <= SC_CHUNK_BYTES]
    return per_worker, fits[-1]


def _sc_scatter_rows(x, idx0, idx1, n_out):
    n, w = x.shape
    per_worker, chunk = _sc_chunk_rows(n, w * x.dtype.itemsize)

    @functools.partial(
        pl.kernel, mesh=_sc_mesh(), out_type=jax.ShapeDtypeStruct((n_out, w), x.dtype),
        scratch_types=[pltpu.VMEM((chunk,), jnp.int32), pltpu.VMEM((chunk,), jnp.int32),
                       pltpu.VMEM((chunk, w), x.dtype), pltpu.SemaphoreType.DMA])
    def scatter(x_hbm, i0_hbm, i1_hbm, out_hbm, i0_v, i1_v, rows_v, sem):
        start = _sc_worker_base(per_worker)

        @pl.loop(0, per_worker // chunk)
        def _(c):
            rows = pl.ds(pl.multiple_of(start + c * chunk, SUBLANES), chunk)
            pltpu.sync_copy(x_hbm.at[rows], rows_v)
            pltpu.sync_copy(i0_hbm.at[rows], i0_v)
            pltpu.sync_copy(i1_hbm.at[rows], i1_v)
            pltpu.async_copy(rows_v, out_hbm.at[i0_v], sem).wait()
            pltpu.async_copy(rows_v, out_hbm.at[i1_v], sem).wait()

    return scatter(x, idx0, idx1)


def _sc_gather_rows(table, idx):
    n, w = idx.shape[0], table.shape[1]
    per_worker, chunk = _sc_chunk_rows(n, w * table.dtype.itemsize)

    @functools.partial(
        pl.kernel, mesh=_sc_mesh(), out_type=jax.ShapeDtypeStruct((n, w), table.dtype),
        scratch_types=[pltpu.VMEM((chunk,), jnp.int32), pltpu.VMEM((chunk, w), table.dtype),
                       pltpu.SemaphoreType.DMA])
    def gather(table_hbm, idx_hbm, out_hbm, idx_v, rows_v, sem):
        start = _sc_worker_base(per_worker)

        @pl.loop(0, per_worker // chunk)
        def _(c):
            rows = pl.ds(pl.multiple_of(start + c * chunk, SUBLANES), chunk)
            pltpu.sync_copy(idx_hbm.at[rows], idx_v)
            pltpu.async_copy(table_hbm.at[idx_v], rows_v, sem).wait()
            pltpu.sync_copy(rows_v, out_hbm.at[rows])

    return gather(table, idx)


def _moe_kernel(be_ref, nv_ref, x_ref, w1_ref, w3_ref, w2_ref, o_ref):
    del be_ref

    @pl.when(pl.program_id(0) < nv_ref[0])
    def _():
        x = _unpack_halves(x_ref[...]).astype(BF16)
        o_ref[...] = _pack_halves(_swiglu(x, w1_ref[...], w3_ref[...], w2_ref[...]))


def _moe_blocks(xs, block_expert, n_valid, w1, w3, w2):
    cap = xs.shape[0]
    tb = MOE_TILE
    d_ff = w1.shape[2]
    grid_spec = pltpu.PrefetchScalarGridSpec(
        num_scalar_prefetch=2,
        grid=(cap // tb,),
        in_specs=[pl.BlockSpec((tb, D_MODEL // 2), lambda i, be, nv: (jnp.minimum(i, nv[0] - 1), 0)),
                  pl.BlockSpec((None, D_MODEL, d_ff), lambda i, be, nv: (be[i], 0, 0)),
                  pl.BlockSpec((None, D_MODEL, d_ff), lambda i, be, nv: (be[i], 0, 0)),
                  pl.BlockSpec((None, d_ff, D_MODEL), lambda i, be, nv: (be[i], 0, 0))],
        out_specs=pl.BlockSpec((tb, D_MODEL // 2), lambda i, be, nv: (i, 0)),
    )
    return pl.pallas_call(
        _moe_kernel,
        out_shape=jax.ShapeDtypeStruct((cap, D_MODEL // 2), jnp.uint32),
        grid_spec=grid_spec,
        compiler_params=_cparams(("arbitrary",)),
        name="moe",
    )(block_expert, n_valid, xs, w1, w3, w2)


def _combine_kernel(xp_ref, xs_ref, a0_ref, a1_ref, gate_ref, op_ref, os_ref, *, n_prompt_tiles):
    is_prompt = pl.program_id(0) < n_prompt_tiles
    gate = gate_ref[...]
    y = (_read_group(xp_ref, xs_ref, is_prompt)
         + gate[:, 0:1] * _unpack_halves(a0_ref[...]) + gate[:, 1:2] * _unpack_halves(a1_ref[...]))
    _write_group(op_ref, os_ref, is_prompt, y)


def _combine(xp, xs, picked, gate):
    tm = ROW_TILE
    tp, ts = xp.shape[0], xs.shape[0]
    nt = (tp + ts) // tm
    rows = _group_specs(tm, D_MODEL, tp // tm)
    return pl.pallas_call(
        functools.partial(_combine_kernel, n_prompt_tiles=tp // tm),
        out_shape=[jax.ShapeDtypeStruct((tp, D_MODEL), F32), jax.ShapeDtypeStruct((ts, D_MODEL), F32)],
        grid=(nt,),
        in_specs=rows + [pl.BlockSpec((tm, D_MODEL // 2), lambda i: (i, 0)),
                         pl.BlockSpec((tm, D_MODEL // 2), lambda i: (i + nt, 0)),
                         pl.BlockSpec((tm, gate.shape[1]), lambda i: (i, 0))],
        out_specs=rows,
        compiler_params=_cparams(("arbitrary",)),
        name="moe_combine",
    )(xp, xs, picked, picked, gate)


def _moe(xp, xs, g, wr, w1, w3, w2):
    t = xp.shape[0] + xs.shape[0]
    tb = MOE_TILE
    lanes = 128
    wr_pad = jnp.pad(wr, ((0, 0), (0, lanes - N_EXPERTS)))
    h, meta, gate, cnt = _router(xp, xs, g, wr_pad)
    counts = cnt[0, :N_EXPERTS].astype(jnp.int32)
    padded = (counts + tb - 1) // tb * tb
    pad_ends = jnp.cumsum(padded)
    pad_starts = pad_ends - padded
    experts = jnp.arange(N_EXPERTS, dtype=jnp.int32)
    slot = lambda e, r: jnp.sum(jnp.where(e[:, None] == experts, pad_starts, 0), axis=1) + r
    dest0 = slot(meta[:, 0], meta[:, 2])
    dest1 = slot(meta[:, 1], meta[:, 3])
    nb = -(-(t * TOP_K + N_EXPERTS * (tb - 1)) // tb)
    block_expert = jnp.minimum(
        jnp.sum(pad_ends[None, :] <= (jnp.arange(nb, dtype=jnp.int32) * tb)[:, None], axis=1),
        N_EXPERTS - 1).astype(jnp.int32)
    n_valid = (pad_ends[-1:] // tb).astype(jnp.int32)
    slots = _sc_scatter_rows(h, dest0, dest1, nb * tb)
    out = _moe_blocks(slots, block_expert, n_valid, w1, w3, w2)
    picked = _sc_gather_rows(out, jnp.concatenate([dest0, dest1]))
    return _combine(xp, xs, picked, gate)


def kernel(x_prompt, x_sample, cache_k, cache_v, page_table, state_ssm_re, state_ssm_im, state_conv,
           norm_mix_g, norm_ffn_g, w_in, ssm_a_re, ssm_a_im, ssm_log_dt, ssm_b_re, ssm_b_im,
           ssm_c_re, ssm_c_im, ssm_d, ssm_w_glu, ssm_b_glu, q_norm_g, k_norm_g,
           lambda_q1, lambda_k1, lambda_q2, lambda_k2, head_norm_g, conv_w,
           w_br_ssm, w_br_att, w_br_conv, w_out, ffn_w1, ffn_w3, ffn_w2,
           router_w, moe_w1, moe_w3, moe_w2):
    n_p, seq, _ = x_prompt.shape
    n_s, dec, _ = x_sample.shape
    depth = w_in.shape[0]
    tp = n_p * seq
    ts = n_s * dec
    assert seq % ROW_TILE == 0 and ts % ROW_TILE == 0 and seq >= CONV_K - 1
    pool, page = cache_k.shape[1], cache_k.shape[2]
    cache_kt = cache_k.reshape(depth, pool, page, QK_WIDTH).transpose(0, 1, 3, 2)
    cache_vr = cache_v.reshape(depth, pool, page * ATT_HEADS, ATT_V_DIM)
    seg = jnp.kron(jnp.eye(QK_WIDTH // ATT_HEAD_DIM, dtype=F32),
                   jnp.full((ATT_HEAD_DIM, ATT_HEAD_DIM), 1.0 / ATT_HEAD_DIM, F32)).astype(BF16)
    n_rep = QK_WIDTH // ATT_HEAD_DIM

    xp = x_prompt.reshape(tp, D_MODEL)
    xs = x_sample.reshape(ts, D_MODEL)
    w_in_bf = w_in.astype(BF16)
    all_tables = _s5_tables(ssm_a_re, ssm_a_im, ssm_log_dt, ssm_b_re, ssm_b_im, ssm_c_re, ssm_c_im, dec)
    kv_prompt = None
    srp, sip, cvp = [], [], []
    ks, vs, srs, sis, cvs = [], [], [], [], []
    for l in range(depth):
        lam_init = 0.8 - 0.6 * math.exp(-0.3 * l)
        q, kb, vb, cb, vin, gates, kt, vr, k_s, v_s, u_p, u_s = _inproj(
            xp, xs, norm_mix_g[l][None], w_in_bf,
            jnp.tile(q_norm_g[l], n_rep)[None], jnp.tile(k_norm_g[l], n_rep)[None], seg,
            l, depth, n_p, seq, kv_prompt)
        kv_prompt = (kt, vr)

        yr_p, yr_s, p_re, p_im, s_re, s_im = _s5(u_p, u_s, state_ssm_re[l], state_ssm_im[l], all_tables, l,
                                                  n_p, seq, n_s, dec)

        lp = jnp.stack([lambda_q1[l], lambda_k1[l], lambda_q2[l], lambda_k2[l]])
        hg = head_norm_g[l][None]
        casts = ()
        if l + 1 < depth and (l + 1) % 2 == 1:
            e = (l + 1) // 2
            casts = (moe_w1[e].reshape(-1, moe_w1.shape[-1]), moe_w3[e].reshape(-1, moe_w3.shape[-1]),
                     moe_w2[e].reshape(-1, moe_w2.shape[-1]))
        yb_p, yb_s, *cast_out = _attention(
            q, kb, vb, q[tp:].reshape(n_s, dec, QK_WIDTH), kb[tp:].reshape(n_s, dec, QK_WIDTH),
            vb[tp:].reshape(n_s, dec, ATT_WIDTH), cache_kt, cache_vr, l, page_table, lp, hg,
            n_p, seq, lam_init, casts)
        yb_s = yb_s.reshape(ts, ATT_WIDTH)
        if casts:
            moe_bf = [c.reshape(w.shape[1:]) for c, w in zip(cast_out, (moe_w1, moe_w3, moe_w2))]

        ext_s = jnp.concatenate([state_conv[l], vin[tp:].reshape(n_s, dec, CONV_WIDTH)], axis=1)
        vm1_s = ext_s[:, 1:1 + dec].reshape(ts, CONV_WIDTH)
        vm2_s = ext_s[:, 0:dec].reshape(ts, CONV_WIDTH)
        xp, xs = _merge(xp, xs, u_p, u_s, cb, vin, gates, yr_p, yr_s, yb_p, yb_s, vm1_s, vm2_s,
                        ssm_d[l][None], ssm_w_glu[l].astype(BF16), ssm_b_glu[l][None], conv_w[l],
                        w_br_ssm[l].astype(BF16), w_br_att[l].astype(BF16), w_br_conv[l].astype(BF16),
                        w_out[l].astype(BF16), seq // ROW_TILE)

        i = l // 2
        if l % 2 == 0:
            xp, xs = _ffn(xp, xs, norm_ffn_g[l][None], ffn_w1[i].astype(BF16), ffn_w3[i].astype(BF16),
                          ffn_w2[i].astype(BF16))
        else:
            xp, xs = _moe(xp, xs, norm_ffn_g[l][None], router_w[i], *moe_bf)

        srp.append(p_re); sip.append(p_im)
        cvp.append(jnp.stack([vin[(b + 1) * seq - (CONV_K - 1):(b + 1) * seq] for b in range(n_p)]))
        ks.append(k_s.reshape(n_s, dec, ATT_HEADS, 2, ATT_HEAD_DIM))
        vs.append(v_s.reshape(n_s, dec, ATT_HEADS, ATT_V_DIM))
        srs.append(s_re); sis.append(s_im); cvs.append(ext_s[:, dec:])

    kt, vr = kv_prompt
    k_prompt = kt.reshape(depth, n_p, ATT_HEADS, 2, ATT_HEAD_DIM, seq).transpose(0, 1, 5, 2, 3, 4)
    v_prompt = vr.reshape(depth, n_p, seq, ATT_HEADS, ATT_V_DIM)
    return (xp.reshape(n_p, seq, D_MODEL), xs.reshape(n_s, dec, D_MODEL),
            k_prompt, v_prompt, jnp.stack(srp), jnp.stack(sip), jnp.stack(cvp),
            jnp.stack(ks), jnp.stack(vs), jnp.stack(srs), jnp.stack(sis), jnp.stack(cvs))
```

```python
import functools
import math

import jax
import jax.numpy as jnp
from jax import lax
from jax.experimental import pallas as pl
from jax.experimental.pallas import tpu as pltpu
from jax.experimental.pallas import tpu_sc as plsc

F32 = jnp.float32
BF16 = jnp.bfloat16

D_MODEL = 1024
SSM_WIDTH = 256
SSM_GROUP = 16
SSM_GROUPS = SSM_WIDTH // SSM_GROUP
SSM_STATE = 64
ATT_HEADS = 4
ATT_HEAD_DIM = 64
ATT_V_DIM = 2 * ATT_HEAD_DIM
QK_WIDTH = ATT_HEADS * 2 * ATT_HEAD_DIM
ATT_WIDTH = ATT_HEADS * ATT_V_DIM
CONV_WIDTH = 256
CONV_K = 3
N_BRANCH = 3
N_EXPERTS = 8
TOP_K = 2
EPS = 1e-6
NEG_INF = -1e30
LOG2_E = 1.4426950408889634

C_U = 0
C_Q = C_U + SSM_WIDTH
C_K = C_Q + QK_WIDTH
C_V = C_K + QK_WIDTH
C_CB = C_V + ATT_WIDTH
C_CC = C_CB + CONV_WIDTH
C_CH = C_CC + CONV_WIDTH
C_G = C_CH + CONV_WIDTH
IN_COLS = C_G + N_BRANCH * D_MODEL

SSM_CHUNK = 8
SUBLANES = 8
BF16_SUBLANES = 16
ROW_TILE = 512
ATT_TILE = 512
MOE_TILE = 256
SC_CORES = 2
SC_WORKERS = SC_CORES * 16
SC_CHUNK_BYTES = 192 * 1024
VMEM_LIMIT = 56 * 1024 * 1024


def _cparams(sem):
    return pltpu.CompilerParams(dimension_semantics=sem, vmem_limit_bytes=VMEM_LIMIT)


def _const_spec(shape):
    nd = len(shape)
    return pl.BlockSpec(shape, lambda *_: (0,) * nd)


def _bdot(a, b):
    return jnp.dot(a, b, preferred_element_type=F32)


def _rms_rows(x, g):
    ms = jnp.mean(x * x, axis=-1, keepdims=True)
    return x * lax.rsqrt(ms + EPS) * g


def _group_specs(tm, w, n_prompt_tiles):
    return [pl.BlockSpec((tm, w), lambda i: (jnp.minimum(i, n_prompt_tiles - 1), 0)),
            pl.BlockSpec((tm, w), lambda i: (jnp.maximum(i - n_prompt_tiles, 0), 0))]


def _read_group(p_ref, s_ref, is_prompt):
    return jnp.where(is_prompt, p_ref[...], s_ref[...])


def _write_group(p_ref, s_ref, is_prompt, val):
    @pl.when(is_prompt)
    def _():
        p_ref[...] = val

    @pl.when(jnp.logical_not(is_prompt))
    def _():
        s_ref[...] = val


def _pack_halves(x):
    half = x.shape[1] // 2
    bits = lambda t: lax.bitcast_convert_type(t.astype(BF16).astype(F32), jnp.uint32)
    return (bits(x[:, :half]) >> 16) | (bits(x[:, half:]) & jnp.uint32(0xFFFF0000))


def _unpack_halves(p):
    lo = lax.bitcast_convert_type(p << 16, F32)
    hi = lax.bitcast_convert_type(p & jnp.uint32(0xFFFF0000), F32)
    return jnp.concatenate([lo, hi], axis=1)


def _segment_rms(z, g, seg):
    ms = _bdot((z * z).astype(BF16), seg)
    return z * lax.rsqrt(ms + EPS) * g


def _inproj_kernel(xp_ref, xs_ref, g_ref, w_ref, qg_ref, kg_ref, seg_ref, *rest,
                   n_prompt_tiles, n_prev, layer):
    (q_ref, kb_ref, vb_ref, cb_ref, vin_ref, gate_ref,
     kt_ref, vr_ref, ks_ref, vs_ref, up_ref, us_ref) = rest[n_prev:]
    tm = q_ref.shape[0]
    is_prompt = pl.program_id(0) < n_prompt_tiles
    h = _rms_rows(_read_group(xp_ref, xs_ref, is_prompt), g_ref[...]).astype(BF16)

    def proj(a, b):
        return _bdot(h, w_ref[:, a:b])

    seg = seg_ref[...]
    _write_group(up_ref, us_ref, is_prompt, proj(C_U, C_Q))
    qn = _segment_rms(proj(C_Q, C_K), qg_ref[...], seg)
    q_ref[...] = (qn * (ATT_HEAD_DIM ** -0.5 * LOG2_E)).astype(BF16)
    kn = _segment_rms(proj(C_K, C_V), kg_ref[...], seg)
    kb_ref[...] = kn.astype(BF16)
    v = proj(C_V, C_CB)
    vb_ref[...] = v.astype(BF16)
    cb_ref[...] = proj(C_CB, C_CC)
    vin_ref[...] = proj(C_CC, C_CH) * proj(C_CH, C_G)
    for j in range(N_BRANCH):
        a = C_G + j * D_MODEL
        gate_ref[:, j * D_MODEL:(j + 1) * D_MODEL] = jax.nn.sigmoid(proj(a, a + D_MODEL)).astype(gate_ref.dtype)

    @pl.when(is_prompt)
    def _():
        if n_prev:
            kt_l, vr_l = kt_ref, vr_ref
        else:
            kt_l, vr_l = kt_ref.at[layer], vr_ref.at[layer]
            for other in range(kt_ref.shape[0]):
                if other != layer:
                    kt_ref[other] = jnp.zeros(kt_ref.shape[1:], F32)
                    vr_ref[other] = jnp.zeros(vr_ref.shape[1:], F32)
        kt_l[...] = kn.T
        for hd in range(ATT_HEADS):
            vr_l[pl.ds(hd, tm, stride=ATT_HEADS), :] = v[:, hd * ATT_V_DIM:(hd + 1) * ATT_V_DIM]

    @pl.when(jnp.logical_not(is_prompt))
    def _():
        ks_ref[...] = kn
        vs_ref[...] = v


def _inproj(xp, xs, g, w_bf, qg, kg, seg, layer, depth, n_p, seq, prev):
    tm = ROW_TILE
    tp, ts = xp.shape[0], xs.shape[0]
    t = tp + ts
    npt = tp // tm
    tps = seq // tm
    row = lambda w: pl.BlockSpec((tm, w), lambda i: (i, 0))
    pc = lambda i: jnp.minimum(i, npt - 1)
    outs = [(QK_WIDTH, BF16), (QK_WIDTH, BF16), (ATT_WIDTH, BF16),
            (CONV_WIDTH, F32), (CONV_WIDTH, F32), (N_BRANCH * D_MODEL, BF16)]
    out_shape = ([jax.ShapeDtypeStruct((t, w), d) for w, d in outs]
                 + [jax.ShapeDtypeStruct((depth, n_p, QK_WIDTH, seq), F32),
                    jax.ShapeDtypeStruct((depth, tp * ATT_HEADS, ATT_V_DIM), F32),
                    jax.ShapeDtypeStruct((ts, QK_WIDTH), F32), jax.ShapeDtypeStruct((ts, ATT_WIDTH), F32),
                    jax.ShapeDtypeStruct((tp, SSM_WIDTH), F32), jax.ShapeDtypeStruct((ts, SSM_WIDTH), F32)])
    srow = _group_specs(tm, QK_WIDTH, npt)[1]
    prev = () if prev is None else tuple(prev)
    lead, at = (None, layer) if prev else (depth, 0)
    out_specs = ([row(w) for w, _ in outs]
                 + [pl.BlockSpec((lead, None, QK_WIDTH, tm), lambda i: (at, pc(i) // tps, 0, pc(i) % tps)),
                    pl.BlockSpec((lead, tm * ATT_HEADS, ATT_V_DIM), lambda i: (at, pc(i), 0)),
                    srow, srow] + _group_specs(tm, SSM_WIDTH, npt))
    n_in = 7
    return pl.pallas_call(
        functools.partial(_inproj_kernel, n_prompt_tiles=npt, n_prev=len(prev), layer=layer),
        out_shape=out_shape,
        grid=(t // tm,),
        in_specs=(_group_specs(tm, D_MODEL, npt)
                  + [_const_spec((1, D_MODEL)),
                     pl.BlockSpec((None, D_MODEL, IN_COLS), lambda i: (layer, 0, 0)),
                     _const_spec((1, QK_WIDTH)), _const_spec((1, QK_WIDTH)),
                     _const_spec((QK_WIDTH, QK_WIDTH))]
                  + [pl.BlockSpec(memory_space=pl.ANY)] * len(prev)),
        out_specs=out_specs,
        input_output_aliases={n_in + j: len(outs) + j for j in range(len(prev))},
        compiler_params=_cparams(("arbitrary",)),
        name="inproj",
    )(xp, xs, g, w_bf, qg, kg, seg, *prev)


def _s5_intra(u, kb_ref, tc):
    rowmod = lax.broadcasted_iota(jnp.int32, u.shape, 0) % tc
    y = _bdot(u.astype(BF16), kb_ref[0])
    for m in range(1, tc):
        um = jnp.where(rowmod >= m, pltpu.roll(u, m, axis=0), 0.0)
        y = y + _bdot(um.astype(BF16), kb_ref[m])
    return y


def _s5_inject(u_refs, pb_ref, k0, n_chunks, tc):
    w = None
    for k in range(tc):
        rows = pl.ds(k, n_chunks, stride=tc)
        uk = jnp.concatenate([r[rows, :] for r in u_refs], axis=1).astype(BF16)
        d = _bdot(uk, pb_ref[k0 + k])
        w = d if w is None else w + d
    return w


def _s5_readout(y_ref, y_scrs, y, s_in, qb_ref, n_chunks, tc):
    lanes = y_scrs[0].shape[1]
    for h, scr in enumerate(y_scrs):
        scr[...] = y[:, h * lanes:(h + 1) * lanes]
    sb = s_in.astype(BF16)
    for k in range(tc):
        rows = pl.ds(k, n_chunks, stride=tc)
        yk = _bdot(sb, qb_ref[k])
        for h, scr in enumerate(y_scrs):
            scr[rows, :] = scr[rows, :] + yk[:, h * lanes:(h + 1) * lanes]
    y_ref[...] = jnp.concatenate([scr[...] for scr in y_scrs], axis=1)


def _cmul_add(a_re, a_im, s, w, half):
    s_re, s_im = s[:, :half], s[:, half:]
    return jnp.concatenate([a_re * s_re - a_im * s_im, a_re * s_im + a_im * s_re], axis=1) + w


def _s5_prompt_kernel(ua_ref, ub_ref, kb_ref, pb_ref, qb_ref, a_ref, y_ref, fs_ref,
                      carry, w_fold, s_fold, ya_scr, yb_scr, *, tc):
    n_b, tm, hw = ua_ref.shape
    n_chunks = tm // tc
    rows = n_b * n_chunks
    fold = carry.shape[0] // n_b
    hf = fold // 2

    @pl.when(pl.program_id(0) == 0)
    def _():
        carry[...] = jnp.zeros(carry.shape, F32)

    u = jnp.concatenate([ua_ref[...].reshape(n_b * tm, hw), ub_ref[...].reshape(n_b * tm, hw)], axis=1)
    y = _s5_intra(u, kb_ref, tc)
    w = None
    for k in range(tc):
        at_k = pl.ds(k, n_chunks, stride=tc)
        uk = jnp.concatenate([jnp.concatenate([ua_ref[b, at_k, :], ub_ref[b, at_k, :]], axis=1)
                              for b in range(n_b)], axis=0).astype(BF16)
        d = _bdot(uk, pb_ref[k])
        w = d if w is None else w + d
    for r in range(fold):
        w_fold[pl.ds(r, rows, stride=fold), :] = w[:, r * hw:(r + 1) * hw]
    a_re = a_ref[0]
    a_im = a_ref[1]
    for b in range(n_b):
        s = carry[b * fold:(b + 1) * fold, :]
        for c in range(n_chunks):
            at = slice((b * n_chunks + c) * fold, (b * n_chunks + c + 1) * fold)
            s_fold[at, :] = s
            s_re, s_im = s[:hf], s[hf:]
            s = jnp.concatenate([a_re * s_re - a_im * s_im, a_re * s_im + a_im * s_re], axis=0) + w_fold[at, :]
        carry[b * fold:(b + 1) * fold, :] = s
    fs_ref[...] = carry[...]
    s_in = jnp.concatenate([s_fold[pl.ds(r, rows, stride=fold), :] for r in range(fold)], axis=1)
    for h, scr in enumerate((ya_scr, yb_scr)):
        scr[...] = y[:, h * hw:(h + 1) * hw]
    sb = s_in.astype(BF16)
    for k in range(tc):
        at_k = pl.ds(k, rows, stride=tc)
        yk = _bdot(sb, qb_ref[k])
        for h, scr in enumerate((ya_scr, yb_scr)):
            scr[at_k, :] = scr[at_k, :] + yk[:, h * hw:(h + 1) * hw]
    for b in range(n_b):
        y_ref[b] = jnp.concatenate([ya_scr[b * tm:(b + 1) * tm, :], yb_scr[b * tm:(b + 1) * tm, :]], axis=1)


def _s5_sample_kernel(ua_ref, ub_ref, s0_ref, kb_ref, pb_ref, qb_ref, a_ref, y_ref, fs_ref,
                      ya_scr, yb_scr, *, tc, k0):
    n_chunks = ua_ref.shape[0] // tc
    half = a_ref.shape[1]
    y = _s5_intra(jnp.concatenate([ua_ref[...], ub_ref[...]], axis=1), kb_ref, tc)
    w = _s5_inject((ua_ref, ub_ref), pb_ref, k0, n_chunks, tc)
    s0 = s0_ref[...]
    fs_ref[...] = _cmul_add(a_ref[2:3, :], a_ref[3:4, :], s0, w, half)
    _s5_readout(y_ref, (ya_scr, yb_scr), y, s0, qb_ref, n_chunks, tc)


def _s5_tables_kernel(are_ref, aim_ref, ldt_ref, bre_ref, bim_ref, cre_ref, cim_ref,
                      kb_ref, pb_ref, qb_ref, a_ref, *, tc, dec):
    are = are_ref[...]
    aim = aim_ref[...]
    dt = jnp.exp(ldt_ref[...])

    def power(m):
        mag = jnp.exp(are * dt * m)
        ang = aim * dt * m
        return mag * jnp.cos(ang), mag * jnp.sin(ang)

    ab_re, ab_im = power(1.0)
    den = are * are + aim * aim
    nr = ab_re - 1.0
    cr = (nr * are + ab_im * aim) / den
    ci = (ab_im * are - nr * aim) / den
    gj, gp = bre_ref.shape
    row_g = lax.broadcasted_iota(jnp.int32, (gj, gp), 0) // SSM_GROUP
    col_g = lax.broadcasted_iota(jnp.int32, (gj, gp), 1) // SSM_STATE
    diag = row_g == col_g
    bre = bre_ref[...]
    bim = bim_ref[...]
    bb_re = jnp.where(diag, cr * bre - ci * bim, 0.0)
    bb_im = jnp.where(diag, cr * bim + ci * bre, 0.0)
    cc_re = jnp.where(diag, cre_ref[...], 0.0)
    cc_im = jnp.where(diag, cim_ref[...], 0.0)
    c_blk = jnp.concatenate([cc_re, -cc_im], axis=1).T
    for m in range(tc):
        pr, pi = power(float(m))
        pm = jnp.concatenate([pr * bb_re - pi * bb_im, pr * bb_im + pi * bb_re], axis=1)
        pb_ref[tc - 1 - m] = pm.astype(pb_ref.dtype)
        kb_ref[m] = jnp.dot(pm, c_blk, preferred_element_type=F32,
                            precision=lax.Precision.HIGHEST).astype(kb_ref.dtype)
        qr, qi = power(float(m + 1))
        qm = jnp.concatenate([qr * cc_re - qi * cc_im, -(qr * cc_im + qi * cc_re)], axis=1)
        qb_ref[m] = qm.T.astype(qb_ref.dtype)
    a_ref[...] = jnp.concatenate(list(power(float(tc)) + power(float(dec))), axis=0)


def _s5_tables(a_re, a_im, log_dt, b_re, b_im, c_re, c_im, dec):
    depth = a_re.shape[0]
    tc = SSM_CHUNK
    g, p, j = SSM_GROUPS, SSM_STATE, SSM_GROUP
    gp, gj = g * p, g * j
    rowv = lambda t: t.reshape(depth, 1, gp)
    ldt = jnp.repeat(log_dt, p, axis=1).reshape(depth, 1, gp)
    bt = lambda t: jnp.tile(t.transpose(0, 1, 3, 2).reshape(depth, gj, p), (1, 1, g))
    ct = lambda t: jnp.tile(t.reshape(depth, gj, p), (1, 1, g))
    lay = lambda *shape: pl.BlockSpec((None,) + shape, lambda l: (l,) + (0,) * len(shape))
    return pl.pallas_call(
        functools.partial(_s5_tables_kernel, tc=tc, dec=dec),
        out_shape=[jax.ShapeDtypeStruct((depth, tc, gj, gj), BF16),
                   jax.ShapeDtypeStruct((depth, tc, gj, 2 * gp), BF16),
                   jax.ShapeDtypeStruct((depth, tc, 2 * gp, gj), BF16),
                   jax.ShapeDtypeStruct((depth, 4, gp), F32)],
        grid=(depth,),
        in_specs=[lay(1, gp)] * 3 + [lay(gj, gp)] * 4,
        out_specs=[lay(tc, gj, gj), lay(tc, gj, 2 * gp), lay(tc, 2 * gp, gj), lay(4, gp)],
        compiler_params=_cparams(("arbitrary",)),
        name="s5_tables",
    )(rowv(a_re), rowv(a_im), ldt, bt(b_re), bt(b_im), ct(c_re), ct(c_im))


def _s5(up, us, s0_re, s0_im, tables, layer, n_p, seq, n_s, dec):
    kb, pb, qb, adec = tables
    lay = lambda t: pl.BlockSpec((None,) + t.shape[1:], lambda i: (layer,) + (0,) * (t.ndim - 1))
    tc = SSM_CHUNK
    g, p = SSM_GROUPS, SSM_STATE
    sw = 2 * g * p
    tm = ROW_TILE
    tp = n_p * seq
    ts = n_s * dec
    hw = SSM_WIDTH // 2
    fold = sw // hw
    assert dec <= tc and seq % tm == 0 and tm % tc == 0
    up3 = up.reshape(n_p, seq, SSM_WIDTH)
    chunk_rows = n_p * (tm // tc)
    yp, fsp = pl.pallas_call(
        functools.partial(_s5_prompt_kernel, tc=tc),
        out_shape=[jax.ShapeDtypeStruct((n_p, seq, SSM_WIDTH), F32),
                   jax.ShapeDtypeStruct((n_p * fold, hw), F32)],
        grid=(seq // tm,),
        in_specs=[pl.BlockSpec((n_p, tm, hw), lambda i: (0, i, 0)),
                  pl.BlockSpec((n_p, tm, hw), lambda i: (0, i, 1)),
                  lay(kb), lay(pb), lay(qb),
                  pl.BlockSpec((None, 4, fold // 2, hw), lambda i: (layer, 0, 0, 0))],
        out_specs=[pl.BlockSpec((n_p, tm, SSM_WIDTH), lambda i: (0, i, 0)),
                   _const_spec((n_p * fold, hw))],
        scratch_shapes=[pltpu.VMEM((n_p * fold, hw), F32),
                        pltpu.VMEM((chunk_rows * fold, hw), F32), pltpu.VMEM((chunk_rows * fold, hw), F32),
                        pltpu.VMEM((n_p * tm, hw), F32), pltpu.VMEM((n_p * tm, hw), F32)],
        compiler_params=_cparams(("arbitrary",)),
        name="s5_prompt",
    )(up3, up3, kb, pb, qb, adec.reshape(adec.shape[0], 4, fold // 2, hw))
    yp = yp.reshape(tp, SSM_WIDTH)
    s0 = jnp.concatenate([s0_re.reshape(n_s, g * p), s0_im.reshape(n_s, g * p)], axis=1)
    ys, fss = pl.pallas_call(
        functools.partial(_s5_sample_kernel, tc=dec, k0=tc - dec),
        out_shape=[jax.ShapeDtypeStruct((ts, SSM_WIDTH), F32), jax.ShapeDtypeStruct((n_s, sw), F32)],
        grid=(1,),
        in_specs=[pl.BlockSpec((ts, hw), lambda i: (0, 0)), pl.BlockSpec((ts, hw), lambda i: (0, 1)),
                  _const_spec((n_s, sw)),
                  lay(kb), lay(pb), lay(qb), lay(adec)],
        out_specs=[_const_spec((ts, SSM_WIDTH)), _const_spec((n_s, sw))],
        scratch_shapes=[pltpu.VMEM((ts, hw), F32), pltpu.VMEM((ts, hw), F32)],
        compiler_params=_cparams(("arbitrary",)),
        name="s5_sample",
    )(us, us, s0, kb, pb, qb, adec)
    half = g * p
    fsp = fsp.reshape(n_p, sw)
    st = lambda a, n: a.reshape(n, g, p)
    return (yp, ys, st(fsp[:, :half], n_p), st(fsp[:, half:], n_p),
            st(fss[:, :half], n_s), st(fss[:, half:], n_s))


def _lambda(lp_ref, lam_init):
    lp = lp_ref[...]
    s1 = jnp.sum(lp[0:1, :] * lp[1:2, :], axis=-1, keepdims=True)
    s2 = jnp.sum(lp[2:3, :] * lp[3:4, :], axis=-1, keepdims=True)
    return jnp.exp(s1) - jnp.exp(s2) + lam_init


def _attn_kernel(qi_ref, ki_ref, pg_ref, q_ref, k_ref, v_ref, lp_ref, hg_ref, *rest,
                 tile, lam_init, cast_blocks, n_seq, n_pages, layer):
    n_cast = len(cast_blocks)
    n_pg = n_seq * n_pages
    cast_in = rest[:n_cast]
    sq_ref, skn_ref, svn_ref, ckt_hbm, cvr_hbm = rest[n_cast:n_cast + 5]
    o_ref, so_ref = rest[n_cast + 5:n_cast + 7]
    cast_out = rest[n_cast + 7:2 * n_cast + 7]
    m_scr, acc_scr, kbuf, vbuf, sem = rest[2 * n_cast + 7:]
    dec = sq_ref.shape[1]
    t = pl.program_id(1)
    qi = qi_ref[t]
    ki = ki_ref[t]
    vd = ATT_V_DIM
    step = pl.program_id(0) * pl.num_programs(1) + t
    last_step = pl.num_programs(0) * pl.num_programs(1) - 1
    slot = step % 2

    def page_copies(s, to_slot, c):
        pg = pg_ref[s * n_pg + c]
        return (pltpu.make_async_copy(ckt_hbm.at[layer, pg], kbuf.at[to_slot, c], sem.at[to_slot]),
                pltpu.make_async_copy(cvr_hbm.at[layer, pg], vbuf.at[to_slot, c], sem.at[to_slot]))

    @pl.when(step == 0)
    def _():
        for c in range(n_pg):
            for cp in page_copies(0, 0, c):
                cp.start()

    for c in range(n_pg):
        for cp in page_copies(step, slot, c):
            cp.wait()
    for c in range(n_pg):
        for cp in page_copies(step + 1, 1 - slot, c):
            cp.start()

    for src, dst, n_blocks in zip(cast_in, cast_out, cast_blocks):
        @pl.when(step < n_blocks)
        def _(src=src, dst=dst):
            dst[...] = src[...].astype(dst.dtype)

    @pl.when(ki == 0)
    def _():
        m_scr[...] = jnp.full(m_scr.shape, NEG_INF, F32)
        acc_scr[...] = jnp.zeros(acc_scr.shape, F32)

    def accumulate(masked):
        for u in range(n_seq):
            kp = [kbuf.at[slot, u * n_pages + j] for j in range(n_pages)]
            vp = [vbuf.at[slot, u * n_pages + j] for j in range(n_pages)]
            o_s = _attend_sample(sq_ref[u], skn_ref[u], svn_ref[u], kp, vp, lp_ref, hg_ref, dec, lam_init)
            so_ref[u] = o_s.astype(so_ref.dtype)
        lane = lax.broadcasted_iota(jnp.int32, (tile, vd), 1)
        ones = jnp.ones((tile, vd), BF16)
        if masked:
            mask = (lax.broadcasted_iota(jnp.int32, (tile, tile), 1)
                    <= lax.broadcasted_iota(jnp.int32, (tile, tile), 0))
        nt = (((1,), (1,)), ((), ()))
        for h in range(ATT_HEADS):
            cols = slice(h * vd, (h + 1) * vd)
            q = q_ref[:, cols]
            k = k_ref[:, cols]
            v1 = jnp.concatenate([v_ref[:, cols], ones], axis=1)
            zero = jnp.zeros_like(q)
            for c in range(2):
                qm = jnp.where((lane >= ATT_HEAD_DIM) == bool(c), q, zero)
                s = lax.dot_general(qm, k, nt, preferred_element_type=F32)
                if masked:
                    s = jnp.where(mask, s, NEG_INF)
                idx = 2 * h + c
                m_old = m_scr[idx]
                m_row = jnp.max(s, axis=-1, keepdims=True)
                m_new = jnp.maximum(m_old, jnp.broadcast_to(m_row, m_old.shape))
                alpha = jnp.exp2(m_old - m_new)
                p = jnp.exp2(s - jnp.concatenate([m_new] * (tile // vd), axis=1)).astype(BF16)
                acc_scr[idx] = jnp.concatenate([alpha, alpha], axis=1) * acc_scr[idx] + _bdot(p, v1)
                m_scr[idx] = m_new

    @pl.when(ki < qi)
    def _():
        accumulate(False)

    @pl.when(ki == qi)
    def _():
        accumulate(True)
        lam = _lambda(lp_ref, lam_init)
        hg = hg_ref[...]
        for h in range(ATT_HEADS):
            a1 = acc_scr[2 * h]
            a2 = acc_scr[2 * h + 1]
            o = a1[:, :vd] / a1[:, vd:] - lam * (a2[:, :vd] / a2[:, vd:])
            o_ref[:, h * vd:(h + 1) * vd] = (_rms_rows(o, hg) * (1.0 - lam_init)).astype(o_ref.dtype)

    @pl.when(step == last_step)
    def _():
        for c in range(n_pg):
            for cp in page_copies(step + 1, 1 - slot, c):
                cp.wait()


def _cast_block_rows(rows, n_steps):
    for br in range(BF16_SUBLANES, rows + 1, BF16_SUBLANES):
        if rows % br == 0 and rows // br <= n_steps:
            return br
    raise ValueError((rows, n_steps))


def _attention(q, k, v, sq, skn, svn, cache_kt, cache_vr, layer, page_table, lp, hg, n_p, seq, lam_init,
               casts=()):
    tile = min(ATT_TILE, seq)
    assert seq % tile == 0 and tile % ATT_V_DIM == 0
    nq = seq // tile
    pairs = [(i, j) for i in range(nq) for j in range(i + 1)]
    n_pairs = len(pairs)
    n_steps = n_p * n_pairs
    n_s, dec, _ = sq.shape
    n_pages = page_table.shape[1]
    page = cache_kt.shape[3]
    n_seq = next(d for d in range(1, n_s + 1) if n_s % d == 0 and n_s // d <= n_steps)
    n_groups = n_s // n_seq
    qi_tab = jnp.asarray([a for a, _ in pairs], jnp.int32)
    ki_tab = jnp.asarray([b for _, b in pairs], jnp.int32)
    step = lambda b, t: b * n_pairs + t
    qspec = pl.BlockSpec((tile, ATT_WIDTH), lambda b, t, qi, ki, pt: (b * nq + qi[t], 0))
    kspec = pl.BlockSpec((tile, ATT_WIDTH), lambda b, t, qi, ki, pt: (b * nq + ki[t], 0))
    tok = pl.BlockSpec((n_seq, dec, QK_WIDTH), lambda b, t, qi, ki, pt: (jnp.minimum(step(b, t), n_groups - 1), 0, 0))
    groups = jnp.minimum(jnp.arange(n_steps + 1), n_groups - 1)
    step_pages = page_table.reshape(n_groups, n_seq * n_pages)[groups].reshape(-1)
    n_pg = n_seq * n_pages
    cast_specs, cast_blocks = [], []
    for w in casts:
        br = _cast_block_rows(w.shape[0], n_steps)
        nblk = w.shape[0] // br
        cast_blocks.append(nblk)
        cast_specs.append(pl.BlockSpec(
            (br, w.shape[1]),
            lambda b, t, qi, ki, pt, nblk=nblk: (jnp.minimum(step(b, t), nblk - 1), 0)))
    grid_spec = pltpu.PrefetchScalarGridSpec(
        num_scalar_prefetch=3,
        grid=(n_p, n_pairs),
        in_specs=[qspec, kspec, kspec,
                  pl.BlockSpec((4, ATT_HEAD_DIM), lambda b, t, qi, ki, pt: (0, 0)),
                  pl.BlockSpec((1, ATT_V_DIM), lambda b, t, qi, ki, pt: (0, 0))]
        + cast_specs + [tok, tok, tok, pl.BlockSpec(memory_space=pl.ANY), pl.BlockSpec(memory_space=pl.ANY)],
        out_specs=[qspec, tok] + cast_specs,
        scratch_shapes=[pltpu.VMEM((2 * ATT_HEADS, tile, ATT_V_DIM), F32),
                        pltpu.VMEM((2 * ATT_HEADS, tile, 2 * ATT_V_DIM), F32),
                        pltpu.VMEM((2, n_pg, QK_WIDTH, page), F32),
                        pltpu.VMEM((2, n_pg, page * ATT_HEADS, ATT_V_DIM), F32),
                        pltpu.SemaphoreType.DMA((2,))],
    )
    return pl.pallas_call(
        functools.partial(_attn_kernel, tile=tile, lam_init=lam_init, cast_blocks=tuple(cast_blocks),
                          n_seq=n_seq, n_pages=n_pages, layer=layer),
        out_shape=[jax.ShapeDtypeStruct((n_p * seq, ATT_WIDTH), BF16),
                   jax.ShapeDtypeStruct((n_s, dec, ATT_WIDTH), BF16)]
        + [jax.ShapeDtypeStruct(w.shape, BF16) for w in casts],
        grid_spec=grid_spec,
        compiler_params=_cparams(("arbitrary", "arbitrary")),
        name="attention",
    )(qi_tab, ki_tab, step_pages, q, k, v, lp, hg, *casts, sq, skn, svn, cache_kt, cache_vr)


def _attend_sample(q, kn, vn, kp, vp, lp_ref, hg_ref, dec, lam_init):
    n_pages = len(kp)
    page = kp[0].shape[1]
    grp = 2 * dec
    n_rows = ATT_HEADS * grp
    q = q.astype(F32)
    qt = jnp.concatenate([q] * (ATT_HEADS * 2), axis=0)
    r = lax.broadcasted_iota(jnp.int32, (n_rows, QK_WIDTH), 0)
    c = lax.broadcasted_iota(jnp.int32, (n_rows, QK_WIDTH), 1)
    qb = jnp.where(r // dec == c // ATT_HEAD_DIM, qt, 0.0).astype(BF16)
    nt = (((1,), (1,)), ((), ()))
    kt_past = jnp.concatenate([kp[j][...].astype(BF16) for j in range(n_pages)], axis=1)
    s_past = _bdot(qb, kt_past)
    s_new = lax.dot_general(qb, kn, nt, preferred_element_type=F32)
    rn = lax.broadcasted_iota(jnp.int32, (n_rows, dec), 0) % dec
    cn = lax.broadcasted_iota(jnp.int32, (n_rows, dec), 1)
    s_new = jnp.where(cn <= rn, s_new, NEG_INF)
    m = jnp.maximum(jnp.max(s_new, axis=-1, keepdims=True), jnp.max(s_past, axis=-1, keepdims=True))
    p_new = jnp.exp2(s_new - m)
    p_past = jnp.exp2(s_past - m)
    l = jnp.sum(p_new, axis=-1, keepdims=True) + jnp.sum(p_past, axis=-1, keepdims=True)
    acc_new = _bdot(p_new.astype(BF16), vn)
    lam = _lambda(lp_ref, lam_init)
    hg = hg_ref[...]
    outs = []
    for h in range(ATT_HEADS):
        rows = slice(h * grp, (h + 1) * grp)
        cols = slice(h * ATT_V_DIM, (h + 1) * ATT_V_DIM)
        v_past = jnp.concatenate([vp[j][pl.ds(h, page, stride=ATT_HEADS), :].astype(BF16)
                                  for j in range(n_pages)], axis=0)
        acc = acc_new[rows, cols] + _bdot(p_past[rows, :].astype(BF16), v_past)
        acc = acc / l[rows, :]
        o = acc[:dec] - lam * acc[dec:]
        outs.append(_rms_rows(o, hg) * (1.0 - lam_init))
    return jnp.concatenate(outs, axis=1)


def _merge_kernel(xp_ref, xs_ref, up_ref, us_ref, cb_ref, vin_ref, halo_ref, gate_ref,
                  yrp_ref, yrs_ref, ybp_ref, ybs_ref, vm1s_ref, vm2s_ref,
                  d_ref, wglu_ref, bglu_ref, cw_ref, wssm_ref, watt_ref, wconv_ref, wout_ref,
                  op_ref, os_ref, *, n_prompt_tiles, tiles_per_seq):
    i = pl.program_id(0)
    is_prompt = i < n_prompt_tiles
    yraw = jnp.where(is_prompt, yrp_ref[...], yrs_ref[...])
    yb = jnp.where(is_prompt, ybp_ref[...], ybs_ref[...])
    y = jax.nn.gelu(yraw + d_ref[...] * _read_group(up_ref, us_ref, is_prompt))
    ya = y * jax.nn.sigmoid(_bdot(y.astype(BF16), wglu_ref[...]) + bglu_ref[...])

    vin = vin_ref[...]
    row = lax.broadcasted_iota(jnp.int32, vin.shape, 0)
    halo = jnp.where(i % tiles_per_seq == 0, 0.0, halo_ref[...])
    h1 = jnp.broadcast_to(halo[SUBLANES - 1:SUBLANES, :], vin.shape)
    h2 = jnp.broadcast_to(halo[SUBLANES - 2:SUBLANES - 1, :], vin.shape)
    vm1 = jnp.where(row == 0, h1, pltpu.roll(vin, 1, axis=0))
    vm2 = jnp.where(row == 0, h2, jnp.where(row == 1, h1, pltpu.roll(vin, 2, axis=0)))
    vm1 = jnp.where(is_prompt, vm1, vm1s_ref[...])
    vm2 = jnp.where(is_prompt, vm2, vm2s_ref[...])
    conv = vm2 * cw_ref[0:1, :] + vm1 * cw_ref[1:2, :] + vin * cw_ref[2:3, :]
    yc = cb_ref[...] * conv
    merged = (gate_ref[:, 0:D_MODEL] * _bdot(ya.astype(BF16), wssm_ref[...])
              + gate_ref[:, D_MODEL:2 * D_MODEL] * _bdot(yb, watt_ref[...])
              + gate_ref[:, 2 * D_MODEL:3 * D_MODEL] * _bdot(yc.astype(BF16), wconv_ref[...]))
    x_new = _read_group(xp_ref, xs_ref, is_prompt) + _bdot(merged.astype(BF16), wout_ref[...])
    _write_group(op_ref, os_ref, is_prompt, x_new)


def _merge(xp, xs, up, us, cb, vin, gates, yr_p, yr_s, yb_p, yb_s, vm1_s, vm2_s,
           d, wglu, bglu, cw, wssm, watt, wconv, wout, tiles_per_seq):
    tm = ROW_TILE
    tp, ts = xp.shape[0], xs.shape[0]
    npt = tp // tm
    row = lambda w: pl.BlockSpec((tm, w), lambda i: (i, 0))
    prow = lambda w: _group_specs(tm, w, npt)[0]
    srow = lambda w: _group_specs(tm, w, npt)[1]
    halo = pl.BlockSpec((SUBLANES, CONV_WIDTH), lambda i: (jnp.maximum(i * (tm // SUBLANES) - 1, 0), 0))
    return pl.pallas_call(
        functools.partial(_merge_kernel, n_prompt_tiles=npt, tiles_per_seq=tiles_per_seq),
        out_shape=[jax.ShapeDtypeStruct((tp, D_MODEL), F32), jax.ShapeDtypeStruct((ts, D_MODEL), F32)],
        grid=((tp + ts) // tm,),
        in_specs=[prow(D_MODEL), srow(D_MODEL), prow(SSM_WIDTH), srow(SSM_WIDTH),
                  row(CONV_WIDTH), row(CONV_WIDTH), halo,
                  row(N_BRANCH * D_MODEL),
                  prow(SSM_WIDTH), srow(SSM_WIDTH), prow(ATT_WIDTH), srow(ATT_WIDTH),
                  srow(CONV_WIDTH), srow(CONV_WIDTH),
                  _const_spec((1, SSM_WIDTH)), _const_spec((SSM_WIDTH, SSM_WIDTH)),
                  _const_spec((1, SSM_WIDTH)), _const_spec((CONV_K, CONV_WIDTH)),
                  _const_spec((SSM_WIDTH, D_MODEL)), _const_spec((ATT_WIDTH, D_MODEL)),
                  _const_spec((CONV_WIDTH, D_MODEL)), _const_spec((D_MODEL, D_MODEL))],
        out_specs=[prow(D_MODEL), srow(D_MODEL)],
        compiler_params=_cparams(("arbitrary",)),
        name="merge",
    )(xp, xs, up, us, cb, vin, vin, gates, yr_p, yr_s, yb_p, yb_s, vm1_s, vm2_s,
      d, wglu, bglu, cw, wssm, watt, wconv, wout)


def _swiglu(h, w1, w3, w2):
    a = _bdot(h, w1)
    b = _bdot(h, w3)
    return _bdot((jax.nn.silu(a) * b).astype(BF16), w2)


def _ffn_kernel(xp_ref, xs_ref, g_ref, w1_ref, w3_ref, w2_ref, op_ref, os_ref, *, n_prompt_tiles):
    is_prompt = pl.program_id(0) < n_prompt_tiles
    x = _read_group(xp_ref, xs_ref, is_prompt)
    h = _rms_rows(x, g_ref[...]).astype(BF16)
    _write_group(op_ref, os_ref, is_prompt, x + _swiglu(h, w1_ref[...], w3_ref[...], w2_ref[...]))


def _ffn(xp, xs, g, w1, w3, w2):
    tm = ROW_TILE
    tp, ts = xp.shape[0], xs.shape[0]
    npt = tp // tm
    d_ff = w1.shape[1]
    rows = _group_specs(tm, D_MODEL, npt)
    return pl.pallas_call(
        functools.partial(_ffn_kernel, n_prompt_tiles=npt),
        out_shape=[jax.ShapeDtypeStruct((tp, D_MODEL), F32), jax.ShapeDtypeStruct((ts, D_MODEL), F32)],
        grid=((tp + ts) // tm,),
        in_specs=rows + [_const_spec((1, D_MODEL)), _const_spec((D_MODEL, d_ff)),
                         _const_spec((D_MODEL, d_ff)), _const_spec((d_ff, D_MODEL))],
        out_specs=rows,
        compiler_params=_cparams(("arbitrary",)),
        name="ffn",
    )(xp, xs, g, w1, w3, w2)


def _router_kernel(xp_ref, xs_ref, g_ref, wr_ref, tri_ref, h_ref, meta_ref, gate_ref, cnt_ref, carry,
                   *, n_prompt_tiles):
    @pl.when(pl.program_id(0) == 0)
    def _():
        carry[...] = jnp.zeros(carry.shape, F32)

    x = _read_group(xp_ref, xs_ref, pl.program_id(0) < n_prompt_tiles)
    h = _rms_rows(x, g_ref[...])
    h_ref[...] = _pack_halves(h)
    logits = jnp.dot(h, wr_ref[...], preferred_element_type=F32, precision=lax.Precision.HIGHEST)
    lane = lax.broadcasted_iota(jnp.int32, logits.shape, 1)
    logits = jnp.where(lane < N_EXPERTS, logits, -jnp.inf)
    big = jnp.int32(logits.shape[1])
    m1 = jnp.max(logits, axis=-1, keepdims=True)
    i1 = jnp.min(jnp.where(logits == m1, lane, big), axis=-1, keepdims=True)
    rest = jnp.where(lane == i1, -jnp.inf, logits)
    m2 = jnp.max(rest, axis=-1, keepdims=True)
    i2 = jnp.min(jnp.where(rest == m2, lane, big), axis=-1, keepdims=True)
    e = jnp.exp(m2 - m1)
    g1 = 1.0 / (1.0 + e)
    g2 = e / (1.0 + e)
    o1 = lane == i1
    o2 = lane == i2
    chosen = jnp.where(o1 | o2, 1.0, 0.0)
    base = _bdot(tri_ref[...], chosen.astype(BF16)) + carry[...]
    r1 = jnp.sum(jnp.where(o1, base, 0.0), axis=-1, keepdims=True).astype(jnp.int32)
    r2 = jnp.sum(jnp.where(o2, base, 0.0), axis=-1, keepdims=True).astype(jnp.int32)
    carry[...] = carry[...] + jnp.sum(chosen, axis=0, keepdims=True)
    cnt_ref[...] = carry[...]
    meta_ref[...] = jnp.where(lane == 0, i1, jnp.where(lane == 1, i2,
                              jnp.where(lane == 2, r1, jnp.where(lane == 3, r2, 0))))
    gate_ref[...] = jnp.where(lane == 0, g1, jnp.where(lane == 1, g2, 0.0))


def _router(xp, xs, g, wr_pad):
    tm = ROW_TILE
    t = xp.shape[0] + xs.shape[0]
    npt = xp.shape[0] // tm
    lanes = wr_pad.shape[1]
    tri = jnp.tri(tm, k=-1, dtype=BF16)
    row = lambda w: pl.BlockSpec((tm, w), lambda i: (i, 0))
    return pl.pallas_call(
        functools.partial(_router_kernel, n_prompt_tiles=npt),
        out_shape=[jax.ShapeDtypeStruct((t, D_MODEL // 2), jnp.uint32),
                   jax.ShapeDtypeStruct((t, lanes), jnp.int32),
                   jax.ShapeDtypeStruct((t, lanes), F32),
                   jax.ShapeDtypeStruct((1, lanes), F32)],
        grid=(t // tm,),
        in_specs=_group_specs(tm, D_MODEL, npt) + [_const_spec((1, D_MODEL)), _const_spec((D_MODEL, lanes)),
                                                   _const_spec((tm, tm))],
        out_specs=[row(D_MODEL // 2), row(lanes), row(lanes), _const_spec((1, lanes))],
        scratch_shapes=[pltpu.VMEM((1, lanes), F32)],
        compiler_params=_cparams(("arbitrary",)),
        name="router",
    )(xp, xs, g, wr_pad, tri)


def _sc_mesh():
    return plsc.VectorSubcoreMesh(core_axis_name="c", subcore_axis_name="s")


def _sc_worker_base(per_worker):
    return (lax.axis_index("s") * SC_CORES + lax.axis_index("c")) * per_worker


def _sc_chunk_rows(n, row_bytes):
    assert n % (SC_WORKERS * SUBLANES) == 0
    per_worker = n // SC_WORKERS
    fits = [r for r in range(SUBLANES, per_worker + 1, SUBLANES)
            if per_worker % r == 0 and r * row_bytes <= SC_CHUNK_BYTES]
    return per_worker, fits[-1]


def _sc_scatter_rows(x, idx0, idx1, n_out):
    n, w = x.shape
    per_worker, chunk = _sc_chunk_rows(n, w * x.dtype.itemsize)

    @functools.partial(
        pl.kernel, mesh=_sc_mesh(), out_type=jax.ShapeDtypeStruct((n_out, w), x.dtype),
        scratch_types=[pltpu.VMEM((chunk,), jnp.int32), pltpu.VMEM((chunk,), jnp.int32),
                       pltpu.VMEM((chunk, w), x.dtype), pltpu.SemaphoreType.DMA])
    def scatter(x_hbm, i0_hbm, i1_hbm, out_hbm, i0_v, i1_v, rows_v, sem):
        start = _sc_worker_base(per_worker)

        @pl.loop(0, per_worker // chunk)
        def _(c):
            rows = pl.ds(pl.multiple_of(start + c * chunk, SUBLANES), chunk)
            pltpu.sync_copy(x_hbm.at[rows], rows_v)
            pltpu.sync_copy(i0_hbm.at[rows], i0_v)
            pltpu.sync_copy(i1_hbm.at[rows], i1_v)
            pltpu.async_copy(rows_v, out_hbm.at[i0_v], sem).wait()
            pltpu.async_copy(rows_v, out_hbm.at[i1_v], sem).wait()

    return scatter(x, idx0, idx1)


def _sc_gather_rows(table, idx):
    n, w = idx.shape[0], table.shape[1]
    per_worker, chunk = _sc_chunk_rows(n, w * table.dtype.itemsize)

    @functools.partial(
        pl.kernel, mesh=_sc_mesh(), out_type=jax.ShapeDtypeStruct((n, w), table.dtype),
        scratch_types=[pltpu.VMEM((chunk,), jnp.int32), pltpu.VMEM((chunk, w), table.dtype),
                       pltpu.SemaphoreType.DMA])
    def gather(table_hbm, idx_hbm, out_hbm, idx_v, rows_v, sem):
        start = _sc_worker_base(per_worker)

        @pl.loop(0, per_worker // chunk)
        def _(c):
            rows = pl.ds(pl.multiple_of(start + c * chunk, SUBLANES), chunk)
            pltpu.sync_copy(idx_hbm.at[rows], idx_v)
            pltpu.async_copy(table_hbm.at[idx_v], rows_v, sem).wait()
            pltpu.sync_copy(rows_v, out_hbm.at[rows])

    return gather(table, idx)


def _moe_kernel(be_ref, nv_ref, x_ref, w1_ref, w3_ref, w2_ref, o_ref):
    del be_ref

    @pl.when(pl.program_id(0) < nv_ref[0])
    def _():
        x = _unpack_halves(x_ref[...]).astype(BF16)
        o_ref[...] = _pack_halves(_swiglu(x, w1_ref[...], w3_ref[...], w2_ref[...]))


def _moe_blocks(xs, block_expert, n_valid, w1, w3, w2):
    cap = xs.shape[0]
    tb = MOE_TILE
    d_ff = w1.shape[2]
    grid_spec = pltpu.PrefetchScalarGridSpec(
        num_scalar_prefetch=2,
        grid=(cap // tb,),
        in_specs=[pl.BlockSpec((tb, D_MODEL // 2), lambda i, be, nv: (jnp.minimum(i, nv[0] - 1), 0)),
                  pl.BlockSpec((None, D_MODEL, d_ff), lambda i, be, nv: (be[i], 0, 0)),
                  pl.BlockSpec((None, D_MODEL, d_ff), lambda i, be, nv: (be[i], 0, 0)),
                  pl.BlockSpec((None, d_ff, D_MODEL), lambda i, be, nv: (be[i], 0, 0))],
        out_specs=pl.BlockSpec((tb, D_MODEL // 2), lambda i, be, nv: (i, 0)),
    )
    return pl.pallas_call(
        _moe_kernel,
        out_shape=jax.ShapeDtypeStruct((cap, D_MODEL // 2), jnp.uint32),
        grid_spec=grid_spec,
        compiler_params=_cparams(("arbitrary",)),
        name="moe",
    )(block_expert, n_valid, xs, w1, w3, w2)


def _combine_kernel(xp_ref, xs_ref, a0_ref, a1_ref, gate_ref, op_ref, os_ref, *, n_prompt_tiles):
    is_prompt = pl.program_id(0) < n_prompt_tiles
    gate = gate_ref[...]
    y = (_read_group(xp_ref, xs_ref, is_prompt)
         + gate[:, 0:1] * _unpack_halves(a0_ref[...]) + gate[:, 1:2] * _unpack_halves(a1_ref[...]))
    _write_group(op_ref, os_ref, is_prompt, y)


def _combine(xp, xs, picked, gate):
    tm = ROW_TILE
    tp, ts = xp.shape[0], xs.shape[0]
    nt = (tp + ts) // tm
    rows = _group_specs(tm, D_MODEL, tp // tm)
    return pl.pallas_call(
        functools.partial(_combine_kernel, n_prompt_tiles=tp // tm),
        out_shape=[jax.ShapeDtypeStruct((tp, D_MODEL), F32), jax.ShapeDtypeStruct((ts, D_MODEL), F32)],
        grid=(nt,),
        in_specs=rows + [pl.BlockSpec((tm, D_MODEL // 2), lambda i: (i, 0)),
                         pl.BlockSpec((tm, D_MODEL // 2), lambda i: (i + nt, 0)),
                         pl.BlockSpec((tm, gate.shape[1]), lambda i: (i, 0))],
        out_specs=rows,
        compiler_params=_cparams(("arbitrary",)),
        name="moe_combine",
    )(xp, xs, picked, picked, gate)


def _moe(xp, xs, g, wr, w1, w3, w2):
    t = xp.shape[0] + xs.shape[0]
    tb = MOE_TILE
    lanes = 128
    wr_pad = jnp.pad(wr, ((0, 0), (0, lanes - N_EXPERTS)))
    h, meta, gate, cnt = _router(xp, xs, g, wr_pad)
    counts = cnt[0, :N_EXPERTS].astype(jnp.int32)
    padded = (counts + tb - 1) // tb * tb
    pad_ends = jnp.cumsum(padded)
    pad_starts = pad_ends - padded
    experts = jnp.arange(N_EXPERTS, dtype=jnp.int32)
    slot = lambda e, r: jnp.sum(jnp.where(e[:, None] == experts, pad_starts, 0), axis=1) + r
    dest0 = slot(meta[:, 0], meta[:, 2])
    dest1 = slot(meta[:, 1], meta[:, 3])
    nb = -(-(t * TOP_K + N_EXPERTS * (tb - 1)) // tb)
    block_expert = jnp.minimum(
        jnp.sum(pad_ends[None, :] <= (jnp.arange(nb, dtype=jnp.int32) * tb)[:, None], axis=1),
        N_EXPERTS - 1).astype(jnp.int32)
    n_valid = (pad_ends[-1:] // tb).astype(jnp.int32)
    slots = _sc_scatter_rows(h, dest0, dest1, nb * tb)
    out = _moe_blocks(slots, block_expert, n_valid, w1, w3, w2)
    picked = _sc_gather_rows(out, jnp.concatenate([dest0, dest1]))
    return _combine(xp, xs, picked, gate)


def kernel(x_prompt, x_sample, cache_k, cache_v, page_table, state_ssm_re, state_ssm_im, state_conv,
           norm_mix_g, norm_ffn_g, w_in, ssm_a_re, ssm_a_im, ssm_log_dt, ssm_b_re, ssm_b_im,
           ssm_c_re, ssm_c_im, ssm_d, ssm_w_glu, ssm_b_glu, q_norm_g, k_norm_g,
           lambda_q1, lambda_k1, lambda_q2, lambda_k2, head_norm_g, conv_w,
           w_br_ssm, w_br_att, w_br_conv, w_out, ffn_w1, ffn_w3, ffn_w2,
           router_w, moe_w1, moe_w3, moe_w2):
    n_p, seq, _ = x_prompt.shape
    n_s, dec, _ = x_sample.shape
    depth = w_in.shape[0]
    tp = n_p * seq
    ts = n_s * dec
    assert seq % ROW_TILE == 0 and ts % ROW_TILE == 0 and seq >= CONV_K - 1
    pool, page = cache_k.shape[1], cache_k.shape[2]
    cache_kt = cache_k.reshape(depth, pool, page, QK_WIDTH).transpose(0, 1, 3, 2)
    cache_vr = cache_v.reshape(depth, pool, page * ATT_HEADS, ATT_V_DIM)
    seg = jnp.kron(jnp.eye(QK_WIDTH // ATT_HEAD_DIM, dtype=F32),
                   jnp.full((ATT_HEAD_DIM, ATT_HEAD_DIM), 1.0 / ATT_HEAD_DIM, F32)).astype(BF16)
    n_rep = QK_WIDTH // ATT_HEAD_DIM

    xp = x_prompt.reshape(tp, D_MODEL)
    xs = x_sample.reshape(ts, D_MODEL)
    w_in_bf = w_in.astype(BF16)
    all_tables = _s5_tables(ssm_a_re, ssm_a_im, ssm_log_dt, ssm_b_re, ssm_b_im, ssm_c_re, ssm_c_im, dec)
    kv_prompt = None
    srp, sip, cvp = [], [], []
    ks, vs, srs, sis, cvs = [], [], [], [], []
    for l in range(depth):
        lam_init = 0.8 - 0.6 * math.exp(-0.3 * l)
        q, kb, vb, cb, vin, gates, kt, vr, k_s, v_s, u_p, u_s = _inproj(
            xp, xs, norm_mix_g[l][None], w_in_bf,
            jnp.tile(q_norm_g[l], n_rep)[None], jnp.tile(k_norm_g[l], n_rep)[None], seg,
            l, depth, n_p, seq, kv_prompt)
        kv_prompt = (kt, vr)

        yr_p, yr_s, p_re, p_im, s_re, s_im = _s5(u_p, u_s, state_ssm_re[l], state_ssm_im[l], all_tables, l,
                                                  n_p, seq, n_s, dec)

        lp = jnp.stack([lambda_q1[l], lambda_k1[l], lambda_q2[l], lambda_k2[l]])
        hg = head_norm_g[l][None]
        casts = ()
        if l + 1 < depth and (l + 1) % 2 == 1:
            e = (l + 1) // 2
            casts = (moe_w1[e].reshape(-1, moe_w1.shape[-1]), moe_w3[e].reshape(-1, moe_w3.shape[-1]),
                     moe_w2[e].reshape(-1, moe_w2.shape[-1]))
        yb_p, yb_s, *cast_out = _attention(
            q, kb, vb, q[tp:].reshape(n_s, dec, QK_WIDTH), kb[tp:].reshape(n_s, dec, QK_WIDTH),
            vb[tp:].reshape(n_s, dec, ATT_WIDTH), cache_kt, cache_vr, l, page_table, lp, hg,
            n_p, seq, lam_init, casts)
        yb_s = yb_s.reshape(ts, ATT_WIDTH)
        if casts:
            moe_bf = [c.reshape(w.shape[1:]) for c, w in zip(cast_out, (moe_w1, moe_w3, moe_w2))]

        ext_s = jnp.concatenate([state_conv[l], vin[tp:].reshape(n_s, dec, CONV_WIDTH)], axis=1)
        vm1_s = ext_s[:, 1:1 + dec].reshape(ts, CONV_WIDTH)
        vm2_s = ext_s[:, 0:dec].reshape(ts, CONV_WIDTH)
        xp, xs = _merge(xp, xs, u_p, u_s, cb, vin, gates, yr_p, yr_s, yb_p, yb_s, vm1_s, vm2_s,
                        ssm_d[l][None], ssm_w_glu[l].astype(BF16), ssm_b_glu[l][None], conv_w[l],
                        w_br_ssm[l].astype(BF16), w_br_att[l].astype(BF16), w_br_conv[l].astype(BF16),
                        w_out[l].astype(BF16), seq // ROW_TILE)

        i = l // 2
        if l % 2 == 0:
            xp, xs = _ffn(xp, xs, norm_ffn_g[l][None], ffn_w1[i].astype(BF16), ffn_w3[i].astype(BF16),
                          ffn_w2[i].astype(BF16))
        else:
            xp, xs = _moe(xp, xs, norm_ffn_g[l][None], router_w[i], *moe_bf)

        srp.append(p_re); sip.append(p_im)
        cvp.append(jnp.stack([vin[(b + 1) * seq - (CONV_K - 1):(b + 1) * seq] for b in range(n_p)]))
        ks.append(k_s.reshape(n_s, dec, ATT_HEADS, 2, ATT_HEAD_DIM))
        vs.append(v_s.reshape(n_s, dec, ATT_HEADS, ATT_V_DIM))
        srs.append(s_re); sis.append(s_im); cvs.append(ext_s[:, dec:])

    kt, vr = kv_prompt
    k_prompt = kt.reshape(depth, n_p, ATT_HEADS, 2, ATT_HEAD_DIM, seq).transpose(0, 1, 5, 2, 3, 4)
    v_prompt = vr.reshape(depth, n_p, seq, ATT_HEADS, ATT_V_DIM)
    return (xp.reshape(n_p, seq, D_MODEL), xs.reshape(n_s, dec, D_MODEL),
            k_prompt, v_prompt, jnp.stack(srp), jnp.stack(sip), jnp.stack(cvp),
            jnp.stack(ks), jnp.stack(vs), jnp.stack(srs), jnp.stack(sis), jnp.stack(cvs))
```

```python
import functools
import math

import jax
import jax.numpy as jnp
from jax import lax
from jax.experimental import pallas as pl
from jax.experimental.pallas import tpu as pltpu
from jax.experimental.pallas import tpu_sc as plsc

F32 = jnp.float32
BF16 = jnp.bfloat16

D_MODEL = 1024
SSM_WIDTH = 256
SSM_GROUP = 16
SSM_GROUPS = SSM_WIDTH // SSM_GROUP
SSM_STATE = 64
ATT_HEADS = 4
ATT_HEAD_DIM = 64
ATT_V_DIM = 2 * ATT_HEAD_DIM
QK_WIDTH = ATT_HEADS * 2 * ATT_HEAD_DIM
ATT_WIDTH = ATT_HEADS * ATT_V_DIM
CONV_WIDTH = 256
CONV_K = 3
N_BRANCH = 3
N_EXPERTS = 8
TOP_K = 2
EPS = 1e-6
NEG_INF = -1e30
LOG2_E = 1.4426950408889634

C_U = 0
C_Q = C_U + SSM_WIDTH
C_K = C_Q + QK_WIDTH
C_V = C_K + QK_WIDTH
C_CB = C_V + ATT_WIDTH
C_CC = C_CB + CONV_WIDTH
C_CH = C_CC + CONV_WIDTH
C_G = C_CH + CONV_WIDTH
IN_COLS = C_G + N_BRANCH * D_MODEL

SSM_CHUNK = 8
SUBLANES = 8
BF16_SUBLANES = 16
ROW_TILE = 512
ATT_TILE = 512
MOE_TILE = 256
SC_CORES = 2
SC_WORKERS = SC_CORES * 16
SC_CHUNK_BYTES = 192 * 1024
VMEM_LIMIT = 56 * 1024 * 1024


def _cparams(sem):
    return pltpu.CompilerParams(dimension_semantics=sem, vmem_limit_bytes=VMEM_LIMIT)


def _const_spec(shape):
    nd = len(shape)
    return pl.BlockSpec(shape, lambda *_: (0,) * nd)


def _bdot(a, b):
    return jnp.dot(a, b, preferred_element_type=F32)


def _rms_rows(x, g):
    ms = jnp.mean(x * x, axis=-1, keepdims=True)
    return x * lax.rsqrt(ms + EPS) * g


def _group_specs(tm, w, n_prompt_tiles):
    return [pl.BlockSpec((tm, w), lambda i: (jnp.minimum(i, n_prompt_tiles - 1), 0)),
            pl.BlockSpec((tm, w), lambda i: (jnp.maximum(i - n_prompt_tiles, 0), 0))]


def _read_group(p_ref, s_ref, is_prompt):
    return jnp.where(is_prompt, p_ref[...], s_ref[...])


def _write_group(p_ref, s_ref, is_prompt, val):
    @pl.when(is_prompt)
    def _():
        p_ref[...] = val

    @pl.when(jnp.logical_not(is_prompt))
    def _():
        s_ref[...] = val


def _pack_halves(x):
    half = x.shape[1] // 2
    bits = lambda t: lax.bitcast_convert_type(t.astype(BF16).astype(F32), jnp.uint32)
    return (bits(x[:, :half]) >> 16) | (bits(x[:, half:]) & jnp.uint32(0xFFFF0000))


def _unpack_halves(p):
    lo = lax.bitcast_convert_type(p << 16, F32)
    hi = lax.bitcast_convert_type(p & jnp.uint32(0xFFFF0000), F32)
    return jnp.concatenate([lo, hi], axis=1)


def _segment_rms(z, g, seg):
    ms = _bdot((z * z).astype(BF16), seg)
    return z * lax.rsqrt(ms + EPS) * g


def _inproj_kernel(xp_ref, xs_ref, g_ref, w_ref, qg_ref, kg_ref, seg_ref, *rest,
                   n_prompt_tiles, n_prev, layer):
    (q_ref, kb_ref, vb_ref, cb_ref, vin_ref, gate_ref,
     kt_ref, vr_ref, ks_ref, vs_ref, up_ref, us_ref) = rest[n_prev:]
    tm = q_ref.shape[0]
    is_prompt = pl.program_id(0) < n_prompt_tiles
    h = _rms_rows(_read_group(xp_ref, xs_ref, is_prompt), g_ref[...]).astype(BF16)

    def proj(a, b):
        return _bdot(h, w_ref[:, a:b])

    seg = seg_ref[...]
    _write_group(up_ref, us_ref, is_prompt, proj(C_U, C_Q))
    qn = _segment_rms(proj(C_Q, C_K), qg_ref[...], seg)
    q_ref[...] = (qn * (ATT_HEAD_DIM ** -0.5 * LOG2_E)).astype(BF16)
    kn = _segment_rms(proj(C_K, C_V), kg_ref[...], seg)
    kb_ref[...] = kn.astype(BF16)
    v = proj(C_V, C_CB)
    vb_ref[...] = v.astype(BF16)
    cb_ref[...] = proj(C_CB, C_CC)
    vin_ref[...] = proj(C_CC, C_CH) * proj(C_CH, C_G)
    for j in range(N_BRANCH):
        a = C_G + j * D_MODEL
        gate_ref[:, j * D_MODEL:(j + 1) * D_MODEL] = jax.nn.sigmoid(proj(a, a + D_MODEL)).astype(gate_ref.dtype)

    @pl.when(is_prompt)
    def _():
        if n_prev:
            kt_l, vr_l = kt_ref, vr_ref
        else:
            kt_l, vr_l = kt_ref.at[layer], vr_ref.at[layer]
            for other in range(kt_ref.shape[0]):
                if other != layer:
                    kt_ref[other] = jnp.zeros(kt_ref.shape[1:], F32)
                    vr_ref[other] = jnp.zeros(vr_ref.shape[1:], F32)
        kt_l[...] = kn.T
        for hd in range(ATT_HEADS):
            vr_l[pl.ds(hd, tm, stride=ATT_HEADS), :] = v[:, hd * ATT_V_DIM:(hd + 1) * ATT_V_DIM]

    @pl.when(jnp.logical_not(is_prompt))
    def _():
        ks_ref[...] = kn
        vs_ref[...] = v


def _inproj(xp, xs, g, w_bf, qg, kg, seg, layer, depth, n_p, seq, prev):
    tm = ROW_TILE
    tp, ts = xp.shape[0], xs.shape[0]
    t = tp + ts
    npt = tp // tm
    tps = seq // tm
    row = lambda w: pl.BlockSpec((tm, w), lambda i: (i, 0))
    pc = lambda i: jnp.minimum(i, npt - 1)
    outs = [(QK_WIDTH, BF16), (QK_WIDTH, BF16), (ATT_WIDTH, BF16),
            (CONV_WIDTH, F32), (CONV_WIDTH, F32), (N_BRANCH * D_MODEL, BF16)]
    out_shape = ([jax.ShapeDtypeStruct((t, w), d) for w, d in outs]
                 + [jax.ShapeDtypeStruct((depth, n_p, QK_WIDTH, seq), F32),
                    jax.ShapeDtypeStruct((depth, tp * ATT_HEADS, ATT_V_DIM), F32),
                    jax.ShapeDtypeStruct((ts, QK_WIDTH), F32), jax.ShapeDtypeStruct((ts, ATT_WIDTH), F32),
                    jax.ShapeDtypeStruct((tp, SSM_WIDTH), F32), jax.ShapeDtypeStruct((ts, SSM_WIDTH), F32)])
    srow = _group_specs(tm, QK_WIDTH, npt)[1]
    prev = () if prev is None else tuple(prev)
    lead, at = (None, layer) if prev else (depth, 0)
    out_specs = ([row(w) for w, _ in outs]
                 + [pl.BlockSpec((lead, None, QK_WIDTH, tm), lambda i: (at, pc(i) // tps, 0, pc(i) % tps)),
                    pl.BlockSpec((lead, tm * ATT_HEADS, ATT_V_DIM), lambda i: (at, pc(i), 0)),
                    srow, srow] + _group_specs(tm, SSM_WIDTH, npt))
    n_in = 7
    return pl.pallas_call(
        functools.partial(_inproj_kernel, n_prompt_tiles=npt, n_prev=len(prev), layer=layer),
        out_shape=out_shape,
        grid=(t // tm,),
        in_specs=(_group_specs(tm, D_MODEL, npt)
                  + [_const_spec((1, D_MODEL)),
                     pl.BlockSpec((None, D_MODEL, IN_COLS), lambda i: (layer, 0, 0)),
                     _const_spec((1, QK_WIDTH)), _const_spec((1, QK_WIDTH)),
                     _const_spec((QK_WIDTH, QK_WIDTH))]
                  + [pl.BlockSpec(memory_space=pl.ANY)] * len(prev)),
        out_specs=out_specs,
        input_output_aliases={n_in + j: len(outs) + j for j in range(len(prev))},
        compiler_params=_cparams(("arbitrary",)),
        name="inproj",
    )(xp, xs, g, w_bf, qg, kg, seg, *prev)


def _s5_intra(u, kb_ref, tc):
    rowmod = lax.broadcasted_iota(jnp.int32, u.shape, 0) % tc
    y = _bdot(u.astype(BF16), kb_ref[0])
    for m in range(1, tc):
        um = jnp.where(rowmod >= m, pltpu.roll(u, m, axis=0), 0.0)
        y = y + _bdot(um.astype(BF16), kb_ref[m])
    return y


def _s5_inject(u_refs, pb_ref, k0, n_chunks, tc):
    w = None
    for k in range(tc):
        rows = pl.ds(k, n_chunks, stride=tc)
        uk = jnp.concatenate([r[rows, :] for r in u_refs], axis=1).astype(BF16)
        d = _bdot(uk, pb_ref[k0 + k])
        w = d if w is None else w + d
    return w


def _s5_readout(y_ref, y_scrs, y, s_in, qb_ref, n_chunks, tc):
    lanes = y_scrs[0].shape[1]
    for h, scr in enumerate(y_scrs):
        scr[...] = y[:, h * lanes:(h + 1) * lanes]
    sb = s_in.astype(BF16)
    for k in range(tc):
        rows = pl.ds(k, n_chunks, stride=tc)
        yk = _bdot(sb, qb_ref[k])
        for h, scr in enumerate(y_scrs):
            scr[rows, :] = scr[rows, :] + yk[:, h * lanes:(h + 1) * lanes]
    y_ref[...] = jnp.concatenate([scr[...] for scr in y_scrs], axis=1)


def _cmul_add(a_re, a_im, s, w, half):
    s_re, s_im = s[:, :half], s[:, half:]
    return jnp.concatenate([a_re * s_re - a_im * s_im, a_re * s_im + a_im * s_re], axis=1) + w


def _s5_prompt_kernel(ua_ref, ub_ref, kb_ref, pb_ref, qb_ref, a_ref, y_ref, fs_ref,
                      carry, w_fold, s_fold, ya_scr, yb_scr, *, tc):
    n_b, tm, hw = ua_ref.shape
    n_chunks = tm // tc
    rows = n_b * n_chunks
    fold = carry.shape[0] // n_b
    hf = fold // 2

    @pl.when(pl.program_id(0) == 0)
    def _():
        carry[...] = jnp.zeros(carry.shape, F32)

    u = jnp.concatenate([ua_ref[...].reshape(n_b * tm, hw), ub_ref[...].reshape(n_b * tm, hw)], axis=1)
    y = _s5_intra(u, kb_ref, tc)
    w = None
    for k in range(tc):
        at_k = pl.ds(k, n_chunks, stride=tc)
        uk = jnp.concatenate([jnp.concatenate([ua_ref[b, at_k, :], ub_ref[b, at_k, :]], axis=1)
                              for b in range(n_b)], axis=0).astype(BF16)
        d = _bdot(uk, pb_ref[k])
        w = d if w is None else w + d
    for r in range(fold):
        w_fold[pl.ds(r, rows, stride=fold), :] = w[:, r * hw:(r + 1) * hw]
    a_re = a_ref[0]
    a_im = a_ref[1]
    for b in range(n_b):
        s = carry[b * fold:(b + 1) * fold, :]
        for c in range(n_chunks):
            at = slice((b * n_chunks + c) * fold, (b * n_chunks + c + 1) * fold)
            s_fold[at, :] = s
            s_re, s_im = s[:hf], s[hf:]
            s = jnp.concatenate([a_re * s_re - a_im * s_im, a_re * s_im + a_im * s_re], axis=0) + w_fold[at, :]
        carry[b * fold:(b + 1) * fold, :] = s
    fs_ref[...] = carry[...]
    s_in = jnp.concatenate([s_fold[pl.ds(r, rows, stride=fold), :] for r in range(fold)], axis=1)
    for h, scr in enumerate((ya_scr, yb_scr)):
        scr[...] = y[:, h * hw:(h + 1) * hw]
    sb = s_in.astype(BF16)
    for k in range(tc):
        at_k = pl.ds(k, rows, stride=tc)
        yk = _bdot(sb, qb_ref[k])
        for h, scr in enumerate((ya_scr, yb_scr)):
            scr[at_k, :] = scr[at_k, :] + yk[:, h * hw:(h + 1) * hw]
    for b in range(n_b):
        y_ref[b] = jnp.concatenate([ya_scr[b * tm:(b + 1) * tm, :], yb_scr[b * tm:(b + 1) * tm, :]], axis=1)


def _s5_sample_kernel(ua_ref, ub_ref, s0_ref, kb_ref, pb_ref, qb_ref, a_ref, y_ref, fs_ref,
                      ya_scr, yb_scr, *, tc, k0):
    n_chunks = ua_ref.shape[0] // tc
    half = a_ref.shape[1]
    y = _s5_intra(jnp.concatenate([ua_ref[...], ub_ref[...]], axis=1), kb_ref, tc)
    w = _s5_inject((ua_ref, ub_ref), pb_ref, k0, n_chunks, tc)
    s0 = s0_ref[...]
    fs_ref[...] = _cmul_add(a_ref[2:3, :], a_ref[3:4, :], s0, w, half)
    _s5_readout(y_ref, (ya_scr, yb_scr), y, s0, qb_ref, n_chunks, tc)


def _s5_tables_kernel(are_ref, aim_ref, ldt_ref, bre_ref, bim_ref, cre_ref, cim_ref,
                      kb_ref, pb_ref, qb_ref, a_ref, *, tc, dec):
    are = are_ref[...]
    aim = aim_ref[...]
    dt = jnp.exp(ldt_ref[...])

    def power(m):
        mag = jnp.exp(are * dt * m)
        ang = aim * dt * m
        return mag * jnp.cos(ang), mag * jnp.sin(ang)

    ab_re, ab_im = power(1.0)
    den = are * are + aim * aim
    nr = ab_re - 1.0
    cr = (nr * are + ab_im * aim) / den
    ci = (ab_im * are - nr * aim) / den
    gj, gp = bre_ref.shape
    row_g = lax.broadcasted_iota(jnp.int32, (gj, gp), 0) // SSM_GROUP
    col_g = lax.broadcasted_iota(jnp.int32, (gj, gp), 1) // SSM_STATE
    diag = row_g == col_g
    bre = bre_ref[...]
    bim = bim_ref[...]
    bb_re = jnp.where(diag, cr * bre - ci * bim, 0.0)
    bb_im = jnp.where(diag, cr * bim + ci * bre, 0.0)
    cc_re = jnp.where(diag, cre_ref[...], 0.0)
    cc_im = jnp.where(diag, cim_ref[...], 0.0)
    c_blk = jnp.concatenate([cc_re, -cc_im], axis=1).T
    for m in range(tc):
        pr, pi = power(float(m))
        pm = jnp.concatenate([pr * bb_re - pi * bb_im, pr * bb_im + pi * bb_re], axis=1)
        pb_ref[tc - 1 - m] = pm.astype(pb_ref.dtype)
        kb_ref[m] = jnp.dot(pm, c_blk, preferred_element_type=F32,
                            precision=lax.Precision.HIGHEST).astype(kb_ref.dtype)
        qr, qi = power(float(m + 1))
        qm = jnp.concatenate([qr * cc_re - qi * cc_im, -(qr * cc_im + qi * cc_re)], axis=1)
        qb_ref[m] = qm.T.astype(qb_ref.dtype)
    a_ref[...] = jnp.concatenate(list(power(float(tc)) + power(float(dec))), axis=0)


def _s5_tables(a_re, a_im, log_dt, b_re, b_im, c_re, c_im, dec):
    depth = a_re.shape[0]
    tc = SSM_CHUNK
    g, p, j = SSM_GROUPS, SSM_STATE, SSM_GROUP
    gp, gj = g * p, g * j
    rowv = lambda t: t.reshape(depth, 1, gp)
    ldt = jnp.repeat(log_dt, p, axis=1).reshape(depth, 1, gp)
    bt = lambda t: jnp.tile(t.transpose(0, 1, 3, 2).reshape(depth, gj, p), (1, 1, g))
    ct = lambda t: jnp.tile(t.reshape(depth, gj, p), (1, 1, g))
    lay = lambda *shape: pl.BlockSpec((None,) + shape, lambda l: (l,) + (0,) * len(shape))
    return pl.pallas_call(
        functools.partial(_s5_tables_kernel, tc=tc, dec=dec),
        out_shape=[jax.ShapeDtypeStruct((depth, tc, gj, gj), BF16),
                   jax.ShapeDtypeStruct((depth, tc, gj, 2 * gp), BF16),
                   jax.ShapeDtypeStruct((depth, tc, 2 * gp, gj), BF16),
                   jax.ShapeDtypeStruct((depth, 4, gp), F32)],
        grid=(depth,),
        in_specs=[lay(1, gp)] * 3 + [lay(gj, gp)] * 4,
        out_specs=[lay(tc, gj, gj), lay(tc, gj, 2 * gp), lay(tc, 2 * gp, gj), lay(4, gp)],
        compiler_params=_cparams(("arbitrary",)),
        name="s5_tables",
    )(rowv(a_re), rowv(a_im), ldt, bt(b_re), bt(b_im), ct(c_re), ct(c_im))


def _s5(up, us, s0_re, s0_im, tables, layer, n_p, seq, n_s, dec):
    kb, pb, qb, adec = tables
    lay = lambda t: pl.BlockSpec((None,) + t.shape[1:], lambda i: (layer,) + (0,) * (t.ndim - 1))
    tc = SSM_CHUNK
    g, p = SSM_GROUPS, SSM_STATE
    sw = 2 * g * p
    tm = ROW_TILE
    tp = n_p * seq
    ts = n_s * dec
    hw = SSM_WIDTH // 2
    fold = sw // hw
    assert dec <= tc and seq % tm == 0 and tm % tc == 0
    up3 = up.reshape(n_p, seq, SSM_WIDTH)
    chunk_rows = n_p * (tm // tc)
    yp, fsp = pl.pallas_call(
        functools.partial(_s5_prompt_kernel, tc=tc),
        out_shape=[jax.ShapeDtypeStruct((n_p, seq, SSM_WIDTH), F32),
                   jax.ShapeDtypeStruct((n_p * fold, hw), F32)],
        grid=(seq // tm,),
        in_specs=[pl.BlockSpec((n_p, tm, hw), lambda i: (0, i, 0)),
                  pl.BlockSpec((n_p, tm, hw), lambda i: (0, i, 1)),
                  lay(kb), lay(pb), lay(qb),
                  pl.BlockSpec((None, 4, fold // 2, hw), lambda i: (layer, 0, 0, 0))],
        out_specs=[pl.BlockSpec((n_p, tm, SSM_WIDTH), lambda i: (0, i, 0)),
                   _const_spec((n_p * fold, hw))],
        scratch_shapes=[pltpu.VMEM((n_p * fold, hw), F32),
                        pltpu.VMEM((chunk_rows * fold, hw), F32), pltpu.VMEM((chunk_rows * fold, hw), F32),
                        pltpu.VMEM((n_p * tm, hw), F32), pltpu.VMEM((n_p * tm, hw), F32)],
        compiler_params=_cparams(("arbitrary",)),
        name="s5_prompt",
    )(up3, up3, kb, pb, qb, adec.reshape(adec.shape[0], 4, fold // 2, hw))
    yp = yp.reshape(tp, SSM_WIDTH)
    s0 = jnp.concatenate([s0_re.reshape(n_s, g * p), s0_im.reshape(n_s, g * p)], axis=1)
    ys, fss = pl.pallas_call(
        functools.partial(_s5_sample_kernel, tc=dec, k0=tc - dec),
        out_shape=[jax.ShapeDtypeStruct((ts, SSM_WIDTH), F32), jax.ShapeDtypeStruct((n_s, sw), F32)],
        grid=(1,),
        in_specs=[pl.BlockSpec((ts, hw), lambda i: (0, 0)), pl.BlockSpec((ts, hw), lambda i: (0, 1)),
                  _const_spec((n_s, sw)),
                  lay(kb), lay(pb), lay(qb), lay(adec)],
        out_specs=[_const_spec((ts, SSM_WIDTH)), _const_spec((n_s, sw))],
        scratch_shapes=[pltpu.VMEM((ts, hw), F32), pltpu.VMEM((ts, hw), F32)],
        compiler_params=_cparams(("arbitrary",)),
        name="s5_sample",
    )(us, us, s0, kb, pb, qb, adec)
    half = g * p
    fsp = fsp.reshape(n_p, sw)
    st = lambda a, n: a.reshape(n, g, p)
    return (yp, ys, st(fsp[:, :half], n_p), st(fsp[:, half:], n_p),
            st(fss[:, :half], n_s), st(fss[:, half:], n_s))


def _lambda(lp_ref, lam_init):
    lp = lp_ref[...]
    s1 = jnp.sum(lp[0:1, :] * lp[1:2, :], axis=-1, keepdims=True)
    s2 = jnp.sum(lp[2:3, :] * lp[3:4, :], axis=-1, keepdims=True)
    return jnp.exp(s1) - jnp.exp(s2) + lam_init


def _attn_kernel(qi_ref, ki_ref, pg_ref, q_ref, k_ref, v_ref, lp_ref, hg_ref, *rest,
                 tile, lam_init, cast_blocks, n_seq, n_pages, layer):
    n_cast = len(cast_blocks)
    n_pg = n_seq * n_pages
    cast_in = rest[:n_cast]
    sq_ref, skn_ref, svn_ref, ckt_hbm, cvr_hbm = rest[n_cast:n_cast + 5]
    o_ref, so_ref = rest[n_cast + 5:n_cast + 7]
    cast_out = rest[n_cast + 7:2 * n_cast + 7]
    m_scr, acc_scr, kbuf, vbuf, sem = rest[2 * n_cast + 7:]
    dec = sq_ref.shape[1]
    t = pl.program_id(1)
    qi = qi_ref[t]
    ki = ki_ref[t]
    vd = ATT_V_DIM
    step = pl.program_id(0) * pl.num_programs(1) + t
    last_step = pl.num_programs(0) * pl.num_programs(1) - 1
    slot = step % 2

    def page_copies(s, to_slot, c):
        pg = pg_ref[s * n_pg + c]
        return (pltpu.make_async_copy(ckt_hbm.at[layer, pg], kbuf.at[to_slot, c], sem.at[to_slot]),
                pltpu.make_async_copy(cvr_hbm.at[layer, pg], vbuf.at[to_slot, c], sem.at[to_slot]))

    @pl.when(step == 0)
    def _():
        for c in range(n_pg):
            for cp in page_copies(0, 0, c):
                cp.start()

    for c in range(n_pg):
        for cp in page_copies(step, slot, c):
            cp.wait()
    for c in range(n_pg):
        for cp in page_copies(step + 1, 1 - slot, c):
            cp.start()

    for src, dst, n_blocks in zip(cast_in, cast_out, cast_blocks):
        @pl.when(step < n_blocks)
        def _(src=src, dst=dst):
            dst[...] = src[...].astype(dst.dtype)

    @pl.when(ki == 0)
    def _():
        m_scr[...] = jnp.full(m_scr.shape, NEG_INF, F32)
        acc_scr[...] = jnp.zeros(acc_scr.shape, F32)

    def sample_group():
        for u in range(n_seq):
            kp = [kbuf.at[slot, u * n_pages + j] for j in range(n_pages)]
            vp = [vbuf.at[slot, u * n_pages + j] for j in range(n_pages)]
            o_s = _attend_sample(sq_ref[u], skn_ref[u], svn_ref[u], kp, vp, lp_ref, hg_ref, dec, lam_init)
            so_ref[u] = o_s.astype(so_ref.dtype)

    def accumulate(r0, nr, nk, masked):
        rows = slice(r0, r0 + nr)
        lane = lax.broadcasted_iota(jnp.int32, (nr, vd), 1)
        ones = jnp.ones((nk, vd), BF16)
        if masked:
            mask = (lax.broadcasted_iota(jnp.int32, (nr, nk), 1)
                    <= r0 + lax.broadcasted_iota(jnp.int32, (nr, nk), 0))
        nt = (((1,), (1,)), ((), ()))
        for h in range(ATT_HEADS):
            cols = slice(h * vd, (h + 1) * vd)
            q = q_ref[rows, cols]
            k = k_ref[0:nk, cols]
            v1 = jnp.concatenate([v_ref[0:nk, cols], ones], axis=1)
            zero = jnp.zeros_like(q)
            for c in range(2):
                qm = jnp.where((lane >= ATT_HEAD_DIM) == bool(c), q, zero)
                s = lax.dot_general(qm, k, nt, preferred_element_type=F32)
                if masked:
                    s = jnp.where(mask, s, NEG_INF)
                idx = 2 * h + c
                m_old = m_scr[idx, rows, :]
                m_row = jnp.max(s, axis=-1, keepdims=True)
                m_new = jnp.maximum(m_old, jnp.broadcast_to(m_row, m_old.shape))
                alpha = jnp.exp2(m_old - m_new)
                p = jnp.exp2(s - jnp.concatenate([m_new] * (nk // vd), axis=1)).astype(BF16)
                acc_scr[idx, rows, :] = (jnp.concatenate([alpha, alpha], axis=1) * acc_scr[idx, rows, :]
                                         + _bdot(p, v1))
                m_scr[idx, rows, :] = m_new

    @pl.when(ki < qi)
    def _():
        sample_group()
        accumulate(0, tile, tile, False)

    @pl.when(ki == qi)
    def _():
        sample_group()
        accumulate(0, tile // 2, tile // 2, True)
        accumulate(tile // 2, tile // 2, tile, True)
        lam = _lambda(lp_ref, lam_init)
        hg = hg_ref[...]
        for h in range(ATT_HEADS):
            a1 = acc_scr[2 * h]
            a2 = acc_scr[2 * h + 1]
            o = a1[:, :vd] / a1[:, vd:] - lam * (a2[:, :vd] / a2[:, vd:])
            o_ref[:, h * vd:(h + 1) * vd] = (_rms_rows(o, hg) * (1.0 - lam_init)).astype(o_ref.dtype)

    @pl.when(step == last_step)
    def _():
        for c in range(n_pg):
            for cp in page_copies(step + 1, 1 - slot, c):
                cp.wait()


def _cast_block_rows(rows, n_steps):
    for br in range(BF16_SUBLANES, rows + 1, BF16_SUBLANES):
        if rows % br == 0 and rows // br <= n_steps:
            return br
    raise ValueError((rows, n_steps))


def _attention(q, k, v, sq, skn, svn, cache_kt, cache_vr, layer, page_table, lp, hg, n_p, seq, lam_init,
               casts=()):
    tile = min(ATT_TILE, seq)
    assert seq % tile == 0 and tile % ATT_V_DIM == 0
    nq = seq // tile
    pairs = [(i, j) for i in range(nq) for j in range(i + 1)]
    n_pairs = len(pairs)
    n_steps = n_p * n_pairs
    n_s, dec, _ = sq.shape
    n_pages = page_table.shape[1]
    page = cache_kt.shape[3]
    n_seq = next(d for d in range(1, n_s + 1) if n_s % d == 0 and n_s // d <= n_steps)
    n_groups = n_s // n_seq
    qi_tab = jnp.asarray([a for a, _ in pairs], jnp.int32)
    ki_tab = jnp.asarray([b for _, b in pairs], jnp.int32)
    step = lambda b, t: b * n_pairs + t
    qspec = pl.BlockSpec((tile, ATT_WIDTH), lambda b, t, qi, ki, pt: (b * nq + qi[t], 0))
    kspec = pl.BlockSpec((tile, ATT_WIDTH), lambda b, t, qi, ki, pt: (b * nq + ki[t], 0))
    tok = pl.BlockSpec((n_seq, dec, QK_WIDTH), lambda b, t, qi, ki, pt: (jnp.minimum(step(b, t), n_groups - 1), 0, 0))
    groups = jnp.minimum(jnp.arange(n_steps + 1), n_groups - 1)
    step_pages = page_table.reshape(n_groups, n_seq * n_pages)[groups].reshape(-1)
    n_pg = n_seq * n_pages
    cast_specs, cast_blocks = [], []
    for w in casts:
        br = _cast_block_rows(w.shape[0], n_steps)
        nblk = w.shape[0] // br
        cast_blocks.append(nblk)
        cast_specs.append(pl.BlockSpec(
            (br, w.shape[1]),
            lambda b, t, qi, ki, pt, nblk=nblk: (jnp.minimum(step(b, t), nblk - 1), 0)))
    grid_spec = pltpu.PrefetchScalarGridSpec(
        num_scalar_prefetch=3,
        grid=(n_p, n_pairs),
        in_specs=[qspec, kspec, kspec,
                  pl.BlockSpec((4, ATT_HEAD_DIM), lambda b, t, qi, ki, pt: (0, 0)),
                  pl.BlockSpec((1, ATT_V_DIM), lambda b, t, qi, ki, pt: (0, 0))]
        + cast_specs + [tok, tok, tok, pl.BlockSpec(memory_space=pl.ANY), pl.BlockSpec(memory_space=pl.ANY)],
        out_specs=[qspec, tok] + cast_specs,
        scratch_shapes=[pltpu.VMEM((2 * ATT_HEADS, tile, ATT_V_DIM), F32),
                        pltpu.VMEM((2 * ATT_HEADS, tile, 2 * ATT_V_DIM), F32),
                        pltpu.VMEM((2, n_pg, QK_WIDTH, page), F32),
                        pltpu.VMEM((2, n_pg, page * ATT_HEADS, ATT_V_DIM), F32),
                        pltpu.SemaphoreType.DMA((2,))],
    )
    return pl.pallas_call(
        functools.partial(_attn_kernel, tile=tile, lam_init=lam_init, cast_blocks=tuple(cast_blocks),
                          n_seq=n_seq, n_pages=n_pages, layer=layer),
        out_shape=[jax.ShapeDtypeStruct((n_p * seq, ATT_WIDTH), BF16),
                   jax.ShapeDtypeStruct((n_s, dec, ATT_WIDTH), BF16)]
        + [jax.ShapeDtypeStruct(w.shape, BF16) for w in casts],
        grid_spec=grid_spec,
        compiler_params=_cparams(("arbitrary", "arbitrary")),
        name="attention",
    )(qi_tab, ki_tab, step_pages, q, k, v, lp, hg, *casts, sq, skn, svn, cache_kt, cache_vr)


def _attend_sample(q, kn, vn, kp, vp, lp_ref, hg_ref, dec, lam_init):
    n_pages = len(kp)
    page = kp[0].shape[1]
    grp = 2 * dec
    n_rows = ATT_HEADS * grp
    q = q.astype(F32)
    qt = jnp.concatenate([q] * (ATT_HEADS * 2), axis=0)
    r = lax.broadcasted_iota(jnp.int32, (n_rows, QK_WIDTH), 0)
    c = lax.broadcasted_iota(jnp.int32, (n_rows, QK_WIDTH), 1)
    qb = jnp.where(r // dec == c // ATT_HEAD_DIM, qt, 0.0).astype(BF16)
    nt = (((1,), (1,)), ((), ()))
    kt_past = jnp.concatenate([kp[j][...].astype(BF16) for j in range(n_pages)], axis=1)
    s_past = _bdot(qb, kt_past)
    s_new = lax.dot_general(qb, kn, nt, preferred_element_type=F32)
    rn = lax.broadcasted_iota(jnp.int32, (n_rows, dec), 0) % dec
    cn = lax.broadcasted_iota(jnp.int32, (n_rows, dec), 1)
    s_new = jnp.where(cn <= rn, s_new, NEG_INF)
    m = jnp.maximum(jnp.max(s_new, axis=-1, keepdims=True), jnp.max(s_past, axis=-1, keepdims=True))
    p_new = jnp.exp2(s_new - m)
    p_past = jnp.exp2(s_past - m)
    l = jnp.sum(p_new, axis=-1, keepdims=True) + jnp.sum(p_past, axis=-1, keepdims=True)
    acc_new = _bdot(p_new.astype(BF16), vn)
    lam = _lambda(lp_ref, lam_init)
    hg = hg_ref[...]
    outs = []
    for h in range(ATT_HEADS):
        rows = slice(h * grp, (h + 1) * grp)
        cols = slice(h * ATT_V_DIM, (h + 1) * ATT_V_DIM)
        v_past = jnp.concatenate([vp[j][pl.ds(h, page, stride=ATT_HEADS), :].astype(BF16)
                                  for j in range(n_pages)], axis=0)
        acc = acc_new[rows, cols] + _bdot(p_past[rows, :].astype(BF16), v_past)
        acc = acc / l[rows, :]
        o = acc[:dec] - lam * acc[dec:]
        outs.append(_rms_rows(o, hg) * (1.0 - lam_init))
    return jnp.concatenate(outs, axis=1)


def _merge_kernel(xp_ref, xs_ref, up_ref, us_ref, cb_ref, vin_ref, halo_ref, gate_ref,
                  yrp_ref, yrs_ref, ybp_ref, ybs_ref, vm1s_ref, vm2s_ref,
                  d_ref, wglu_ref, bglu_ref, cw_ref, wssm_ref, watt_ref, wconv_ref, wout_ref,
                  op_ref, os_ref, *, n_prompt_tiles, tiles_per_seq):
    i = pl.program_id(0)
    is_prompt = i < n_prompt_tiles
    yraw = jnp.where(is_prompt, yrp_ref[...], yrs_ref[...])
    yb = jnp.where(is_prompt, ybp_ref[...], ybs_ref[...])
    y = jax.nn.gelu(yraw + d_ref[...] * _read_group(up_ref, us_ref, is_prompt))
    ya = y * jax.nn.sigmoid(_bdot(y.astype(BF16), wglu_ref[...]) + bglu_ref[...])

    vin = vin_ref[...]
    row = lax.broadcasted_iota(jnp.int32, vin.shape, 0)
    halo = jnp.where(i % tiles_per_seq == 0, 0.0, halo_ref[...])
    h1 = jnp.broadcast_to(halo[SUBLANES - 1:SUBLANES, :], vin.shape)
    h2 = jnp.broadcast_to(halo[SUBLANES - 2:SUBLANES - 1, :], vin.shape)
    vm1 = jnp.where(row == 0, h1, pltpu.roll(vin, 1, axis=0))
    vm2 = jnp.where(row == 0, h2, jnp.where(row == 1, h1, pltpu.roll(vin, 2, axis=0)))
    vm1 = jnp.where(is_prompt, vm1, vm1s_ref[...])
    vm2 = jnp.where(is_prompt, vm2, vm2s_ref[...])
    conv = vm2 * cw_ref[0:1, :] + vm1 * cw_ref[1:2, :] + vin * cw_ref[2:3, :]
    yc = cb_ref[...] * conv
    merged = (gate_ref[:, 0:D_MODEL] * _bdot(ya.astype(BF16), wssm_ref[...])
              + gate_ref[:, D_MODEL:2 * D_MODEL] * _bdot(yb, watt_ref[...])
              + gate_ref[:, 2 * D_MODEL:3 * D_MODEL] * _bdot(yc.astype(BF16), wconv_ref[...]))
    x_new = _read_group(xp_ref, xs_ref, is_prompt) + _bdot(merged.astype(BF16), wout_ref[...])
    _write_group(op_ref, os_ref, is_prompt, x_new)


def _merge(xp, xs, up, us, cb, vin, gates, yr_p, yr_s, yb_p, yb_s, vm1_s, vm2_s,
           d, wglu, bglu, cw, wssm, watt, wconv, wout, tiles_per_seq):
    tm = ROW_TILE
    tp, ts = xp.shape[0], xs.shape[0]
    npt = tp // tm
    row = lambda w: pl.BlockSpec((tm, w), lambda i: (i, 0))
    prow = lambda w: _group_specs(tm, w, npt)[0]
    srow = lambda w: _group_specs(tm, w, npt)[1]
    halo = pl.BlockSpec((SUBLANES, CONV_WIDTH), lambda i: (jnp.maximum(i * (tm // SUBLANES) - 1, 0), 0))
    return pl.pallas_call(
        functools.partial(_merge_kernel, n_prompt_tiles=npt, tiles_per_seq=tiles_per_seq),
        out_shape=[jax.ShapeDtypeStruct((tp, D_MODEL), F32), jax.ShapeDtypeStruct((ts, D_MODEL), F32)],
        grid=((tp + ts) // tm,),
        in_specs=[prow(D_MODEL), srow(D_MODEL), prow(SSM_WIDTH), srow(SSM_WIDTH),
                  row(CONV_WIDTH), row(CONV_WIDTH), halo,
                  row(N_BRANCH * D_MODEL),
                  prow(SSM_WIDTH), srow(SSM_WIDTH), prow(ATT_WIDTH), srow(ATT_WIDTH),
                  srow(CONV_WIDTH), srow(CONV_WIDTH),
                  _const_spec((1, SSM_WIDTH)), _const_spec((SSM_WIDTH, SSM_WIDTH)),
                  _const_spec((1, SSM_WIDTH)), _const_spec((CONV_K, CONV_WIDTH)),
                  _const_spec((SSM_WIDTH, D_MODEL)), _const_spec((ATT_WIDTH, D_MODEL)),
                  _const_spec((CONV_WIDTH, D_MODEL)), _const_spec((D_MODEL, D_MODEL))],
        out_specs=[prow(D_MODEL), srow(D_MODEL)],
        compiler_params=_cparams(("arbitrary",)),
        name="merge",
    )(xp, xs, up, us, cb, vin, vin, gates, yr_p, yr_s, yb_p, yb_s, vm1_s, vm2_s,
      d, wglu, bglu, cw, wssm, watt, wconv, wout)


def _swiglu(h, w1, w3, w2):
    a = _bdot(h, w1)
    b = _bdot(h, w3)
    return _bdot((jax.nn.silu(a) * b).astype(BF16), w2)


def _ffn_kernel(xp_ref, xs_ref, g_ref, w1_ref, w3_ref, w2_ref, op_ref, os_ref, *, n_prompt_tiles):
    is_prompt = pl.program_id(0) < n_prompt_tiles
    x = _read_group(xp_ref, xs_ref, is_prompt)
    h = _rms_rows(x, g_ref[...]).astype(BF16)
    _write_group(op_ref, os_ref, is_prompt, x + _swiglu(h, w1_ref[...], w3_ref[...], w2_ref[...]))


def _ffn(xp, xs, g, w1, w3, w2):
    tm = ROW_TILE
    tp, ts = xp.shape[0], xs.shape[0]
    npt = tp // tm
    d_ff = w1.shape[1]
    rows = _group_specs(tm, D_MODEL, npt)
    return pl.pallas_call(
        functools.partial(_ffn_kernel, n_prompt_tiles=npt),
        out_shape=[jax.ShapeDtypeStruct((tp, D_MODEL), F32), jax.ShapeDtypeStruct((ts, D_MODEL), F32)],
        grid=((tp + ts) // tm,),
        in_specs=rows + [_const_spec((1, D_MODEL)), _const_spec((D_MODEL, d_ff)),
                         _const_spec((D_MODEL, d_ff)), _const_spec((d_ff, D_MODEL))],
        out_specs=rows,
        compiler_params=_cparams(("arbitrary",)),
        name="ffn",
    )(xp, xs, g, w1, w3, w2)


def _router_kernel(xp_ref, xs_ref, g_ref, wr_ref, tri_ref, h_ref, meta_ref, gate_ref, cnt_ref, carry,
                   *, n_prompt_tiles):
    @pl.when(pl.program_id(0) == 0)
    def _():
        carry[...] = jnp.zeros(carry.shape, F32)

    x = _read_group(xp_ref, xs_ref, pl.program_id(0) < n_prompt_tiles)
    h = _rms_rows(x, g_ref[...])
    h_ref[...] = _pack_halves(h)
    logits = jnp.dot(h, wr_ref[...], preferred_element_type=F32, precision=lax.Precision.HIGHEST)
    lane = lax.broadcasted_iota(jnp.int32, logits.shape, 1)
    logits = jnp.where(lane < N_EXPERTS, logits, -jnp.inf)
    big = jnp.int32(logits.shape[1])
    m1 = jnp.max(logits, axis=-1, keepdims=True)
    i1 = jnp.min(jnp.where(logits == m1, lane, big), axis=-1, keepdims=True)
    rest = jnp.where(lane == i1, -jnp.inf, logits)
    m2 = jnp.max(rest, axis=-1, keepdims=True)
    i2 = jnp.min(jnp.where(rest == m2, lane, big), axis=-1, keepdims=True)
    e = jnp.exp(m2 - m1)
    g1 = 1.0 / (1.0 + e)
    g2 = e / (1.0 + e)
    o1 = lane == i1
    o2 = lane == i2
    chosen = jnp.where(o1 | o2, 1.0, 0.0)
    base = _bdot(tri_ref[...], chosen.astype(BF16)) + carry[...]
    r1 = jnp.sum(jnp.where(o1, base, 0.0), axis=-1, keepdims=True).astype(jnp.int32)
    r2 = jnp.sum(jnp.where(o2, base, 0.0), axis=-1, keepdims=True).astype(jnp.int32)
    carry[...] = carry[...] + jnp.sum(chosen, axis=0, keepdims=True)
    cnt_ref[...] = carry[...]
    meta_ref[...] = jnp.where(lane == 0, i1, jnp.where(lane == 1, i2,
                              jnp.where(lane == 2, r1, jnp.where(lane == 3, r2, 0))))
    gate_ref[...] = jnp.where(lane == 0, g1, jnp.where(lane == 1, g2, 0.0))


def _router(xp, xs, g, wr_pad):
    tm = ROW_TILE
    t = xp.shape[0] + xs.shape[0]
    npt = xp.shape[0] // tm
    lanes = wr_pad.shape[1]
    tri = jnp.tri(tm, k=-1, dtype=BF16)
    row = lambda w: pl.BlockSpec((tm, w), lambda i: (i, 0))
    return pl.pallas_call(
        functools.partial(_router_kernel, n_prompt_tiles=npt),
        out_shape=[jax.ShapeDtypeStruct((t, D_MODEL // 2), jnp.uint32),
                   jax.ShapeDtypeStruct((t, lanes), jnp.int32),
                   jax.ShapeDtypeStruct((t, lanes), F32),
                   jax.ShapeDtypeStruct((1, lanes), F32)],
        grid=(t // tm,),
        in_specs=_group_specs(tm, D_MODEL, npt) + [_const_spec((1, D_MODEL)), _const_spec((D_MODEL, lanes)),
                                                   _const_spec((tm, tm))],
        out_specs=[row(D_MODEL // 2), row(lanes), row(lanes), _const_spec((1, lanes))],
        scratch_shapes=[pltpu.VMEM((1, lanes), F32)],
        compiler_params=_cparams(("arbitrary",)),
        name="router",
    )(xp, xs, g, wr_pad, tri)


def _sc_mesh():
    return plsc.VectorSubcoreMesh(core_axis_name="c", subcore_axis_name="s")


def _sc_worker_base(per_worker):
    return (lax.axis_index("s") * SC_CORES + lax.axis_index("c")) * per_worker


def _sc_chunk_rows(n, row_bytes):
    assert n % (SC_WORKERS * SUBLANES) == 0
    per_worker = n // SC_WORKERS
    fits = [r for r in range(SUBLANES, per_worker + 1, SUBLANES)
            if per_worker % r == 0 and r * row_bytes <= SC_CHUNK_BYTES]
    return per_worker, fits[-1]


def _sc_scatter_rows(x, idx0, idx1, n_out):
    n, w = x.shape
    per_worker, chunk = _sc_chunk_rows(n, w * x.dtype.itemsize)

    @functools.partial(
        pl.kernel, mesh=_sc_mesh(), out_type=jax.ShapeDtypeStruct((n_out, w), x.dtype),
        scratch_types=[pltpu.VMEM((chunk,), jnp.int32), pltpu.VMEM((chunk,), jnp.int32),
                       pltpu.VMEM((chunk, w), x.dtype), pltpu.SemaphoreType.DMA])
    def scatter(x_hbm, i0_hbm, i1_hbm, out_hbm, i0_v, i1_v, rows_v, sem):
        start = _sc_worker_base(per_worker)

        @pl.loop(0, per_worker // chunk)
        def _(c):
            rows = pl.ds(pl.multiple_of(start + c * chunk, SUBLANES), chunk)
            pltpu.sync_copy(x_hbm.at[rows], rows_v)
            pltpu.sync_copy(i0_hbm.at[rows], i0_v)
            pltpu.sync_copy(i1_hbm.at[rows], i1_v)
            pltpu.async_copy(rows_v, out_hbm.at[i0_v], sem).wait()
            pltpu.async_copy(rows_v, out_hbm.at[i1_v], sem).wait()

    return scatter(x, idx0, idx1)


def _sc_gather_rows(table, idx):
    n, w = idx.shape[0], table.shape[1]
    per_worker, chunk = _sc_chunk_rows(n, w * table.dtype.itemsize)

    @functools.partial(
        pl.kernel, mesh=_sc_mesh(), out_type=jax.ShapeDtypeStruct((n, w), table.dtype),
        scratch_types=[pltpu.VMEM((chunk,), jnp.int32), pltpu.VMEM((chunk, w), table.dtype),
                       pltpu.SemaphoreType.DMA])
    def gather(table_hbm, idx_hbm, out_hbm, idx_v, rows_v, sem):
        start = _sc_worker_base(per_worker)

        @pl.loop(0, per_worker // chunk)
        def _(c):
            rows = pl.ds(pl.multiple_of(start + c * chunk, SUBLANES), chunk)
            pltpu.sync_copy(idx_hbm.at[rows], idx_v)
            pltpu.async_copy(table_hbm.at[idx_v], rows_v, sem).wait()
            pltpu.sync_copy(rows_v, out_hbm.at[rows])

    return gather(table, idx)


def _moe_kernel(be_ref, nv_ref, x_ref, w1_ref, w3_ref, w2_ref, o_ref):
    del be_ref

    @pl.when(pl.program_id(0) < nv_ref[0])
    def _():
        x = _unpack_halves(x_ref[...]).astype(BF16)
        o_ref[...] = _pack_halves(_swiglu(x, w1_ref[...], w3_ref[...], w2_ref[...]))


def _moe_blocks(xs, block_expert, n_valid, w1, w3, w2):
    cap = xs.shape[0]
    tb = MOE_TILE
    d_ff = w1.shape[2]
    grid_spec = pltpu.PrefetchScalarGridSpec(
        num_scalar_prefetch=2,
        grid=(cap // tb,),
        in_specs=[pl.BlockSpec((tb, D_MODEL // 2), lambda i, be, nv: (jnp.minimum(i, nv[0] - 1), 0)),
                  pl.BlockSpec((None, D_MODEL, d_ff), lambda i, be, nv: (be[i], 0, 0)),
                  pl.BlockSpec((None, D_MODEL, d_ff), lambda i, be, nv: (be[i], 0, 0)),
                  pl.BlockSpec((None, d_ff, D_MODEL), lambda i, be, nv: (be[i], 0, 0))],
        out_specs=pl.BlockSpec((tb, D_MODEL // 2), lambda i, be, nv: (i, 0)),
    )
    return pl.pallas_call(
        _moe_kernel,
        out_shape=jax.ShapeDtypeStruct((cap, D_MODEL // 2), jnp.uint32),
        grid_spec=grid_spec,
        compiler_params=_cparams(("arbitrary",)),
        name="moe",
    )(block_expert, n_valid, xs, w1, w3, w2)


def _combine_kernel(xp_ref, xs_ref, a0_ref, a1_ref, gate_ref, op_ref, os_ref, *, n_prompt_tiles):
    is_prompt = pl.program_id(0) < n_prompt_tiles
    gate = gate_ref[...]
    y = (_read_group(xp_ref, xs_ref, is_prompt)
         + gate[:, 0:1] * _unpack_halves(a0_ref[...]) + gate[:, 1:2] * _unpack_halves(a1_ref[...]))
    _write_group(op_ref, os_ref, is_prompt, y)


def _combine(xp, xs, picked, gate):
    tm = ROW_TILE
    tp, ts = xp.shape[0], xs.shape[0]
    nt = (tp + ts) // tm
    rows = _group_specs(tm, D_MODEL, tp // tm)
    return pl.pallas_call(
        functools.partial(_combine_kernel, n_prompt_tiles=tp // tm),
        out_shape=[jax.ShapeDtypeStruct((tp, D_MODEL), F32), jax.ShapeDtypeStruct((ts, D_MODEL), F32)],
        grid=(nt,),
        in_specs=rows + [pl.BlockSpec((tm, D_MODEL // 2), lambda i: (i, 0)),
                         pl.BlockSpec((tm, D_MODEL // 2), lambda i: (i + nt, 0)),
                         pl.BlockSpec((tm, gate.shape[1]), lambda i: (i, 0))],
        out_specs=rows,
        compiler_params=_cparams(("arbitrary",)),
        name="moe_combine",
    )(xp, xs, picked, picked, gate)


def _moe(xp, xs, g, wr, w1, w3, w2):
    t = xp.shape[0] + xs.shape[0]
    tb = MOE_TILE
    lanes = 128
    wr_pad = jnp.pad(wr, ((0, 0), (0, lanes - N_EXPERTS)))
    h, meta, gate, cnt = _router(xp, xs, g, wr_pad)
    counts = cnt[0, :N_EXPERTS].astype(jnp.int32)
    padded = (counts + tb - 1) // tb * tb
    pad_ends = jnp.cumsum(padded)
    pad_starts = pad_ends - padded
    experts = jnp.arange(N_EXPERTS, dtype=jnp.int32)
    slot = lambda e, r: jnp.sum(jnp.where(e[:, None] == experts, pad_starts, 0), axis=1) + r
    dest0 = slot(meta[:, 0], meta[:, 2])
    dest1 = slot(meta[:, 1], meta[:, 3])
    nb = -(-(t * TOP_K + N_EXPERTS * (tb - 1)) // tb)
    block_expert = jnp.minimum(
        jnp.sum(pad_ends[None, :] <= (jnp.arange(nb, dtype=jnp.int32) * tb)[:, None], axis=1),
        N_EXPERTS - 1).astype(jnp.int32)
    n_valid = (pad_ends[-1:] // tb).astype(jnp.int32)
    slots = _sc_scatter_rows(h, dest0, dest1, nb * tb)
    out = _moe_blocks(slots, block_expert, n_valid, w1, w3, w2)
    picked = _sc_gather_rows(out, jnp.concatenate([dest0, dest1]))
    return _combine(xp, xs, picked, gate)


def kernel(x_prompt, x_sample, cache_k, cache_v, page_table, state_ssm_re, state_ssm_im, state_conv,
           norm_mix_g, norm_ffn_g, w_in, ssm_a_re, ssm_a_im, ssm_log_dt, ssm_b_re, ssm_b_im,
           ssm_c_re, ssm_c_im, ssm_d, ssm_w_glu, ssm_b_glu, q_norm_g, k_norm_g,
           lambda_q1, lambda_k1, lambda_q2, lambda_k2, head_norm_g, conv_w,
           w_br_ssm, w_br_att, w_br_conv, w_out, ffn_w1, ffn_w3, ffn_w2,
           router_w, moe_w1, moe_w3, moe_w2):
    n_p, seq, _ = x_prompt.shape
    n_s, dec, _ = x_sample.shape
    depth = w_in.shape[0]
    tp = n_p * seq
    ts = n_s * dec
    assert seq % ROW_TILE == 0 and ts % ROW_TILE == 0 and seq >= CONV_K - 1
    pool, page = cache_k.shape[1], cache_k.shape[2]
    cache_kt = cache_k.reshape(depth, pool, page, QK_WIDTH).transpose(0, 1, 3, 2)
    cache_vr = cache_v.reshape(depth, pool, page * ATT_HEADS, ATT_V_DIM)
    seg = jnp.kron(jnp.eye(QK_WIDTH // ATT_HEAD_DIM, dtype=F32),
                   jnp.full((ATT_HEAD_DIM, ATT_HEAD_DIM), 1.0 / ATT_HEAD_DIM, F32)).astype(BF16)
    n_rep = QK_WIDTH // ATT_HEAD_DIM

    xp = x_prompt.reshape(tp, D_MODEL)
    xs = x_sample.reshape(ts, D_MODEL)
    w_in_bf = w_in.astype(BF16)
    all_tables = _s5_tables(ssm_a_re, ssm_a_im, ssm_log_dt, ssm_b_re, ssm_b_im, ssm_c_re, ssm_c_im, dec)
    kv_prompt = None
    srp, sip, cvp = [], [], []
    ks, vs, srs, sis, cvs = [], [], [], [], []
    for l in range(depth):
        lam_init = 0.8 - 0.6 * math.exp(-0.3 * l)
        q, kb, vb, cb, vin, gates, kt, vr, k_s, v_s, u_p, u_s = _inproj(
            xp, xs, norm_mix_g[l][None], w_in_bf,
            jnp.tile(q_norm_g[l], n_rep)[None], jnp.tile(k_norm_g[l], n_rep)[None], seg,
            l, depth, n_p, seq, kv_prompt)
        kv_prompt = (kt, vr)

        yr_p, yr_s, p_re, p_im, s_re, s_im = _s5(u_p, u_s, state_ssm_re[l], state_ssm_im[l], all_tables, l,
                                                  n_p, seq, n_s, dec)

        lp = jnp.stack([lambda_q1[l], lambda_k1[l], lambda_q2[l], lambda_k2[l]])
        hg = head_norm_g[l][None]
        casts = ()
        if l + 1 < depth and (l + 1) % 2 == 1:
            e = (l + 1) // 2
            casts = (moe_w1[e].reshape(-1, moe_w1.shape[-1]), moe_w3[e].reshape(-1, moe_w3.shape[-1]),
                     moe_w2[e].reshape(-1, moe_w2.shape[-1]))
        yb_p, yb_s, *cast_out = _attention(
            q, kb, vb, q[tp:].reshape(n_s, dec, QK_WIDTH), kb[tp:].reshape(n_s, dec, QK_WIDTH),
            vb[tp:].reshape(n_s, dec, ATT_WIDTH), cache_kt, cache_vr, l, page_table, lp, hg,
            n_p, seq, lam_init, casts)
        yb_s = yb_s.reshape(ts, ATT_WIDTH)
        if casts:
            moe_bf = [c.reshape(w.shape[1:]) for c, w in zip(cast_out, (moe_w1, moe_w3, moe_w2))]

        ext_s = jnp.concatenate([state_conv[l], vin[tp:].reshape(n_s, dec, CONV_WIDTH)], axis=1)
        vm1_s = ext_s[:, 1:1 + dec].reshape(ts, CONV_WIDTH)
        vm2_s = ext_s[:, 0:dec].reshape(ts, CONV_WIDTH)
        xp, xs = _merge(xp, xs, u_p, u_s, cb, vin, gates, yr_p, yr_s, yb_p, yb_s, vm1_s, vm2_s,
                        ssm_d[l][None], ssm_w_glu[l].astype(BF16), ssm_b_glu[l][None], conv_w[l],
                        w_br_ssm[l].astype(BF16), w_br_att[l].astype(BF16), w_br_conv[l].astype(BF16),
                        w_out[l].astype(BF16), seq // ROW_TILE)

        i = l // 2
        if l % 2 == 0:
            xp, xs = _ffn(xp, xs, norm_ffn_g[l][None], ffn_w1[i].astype(BF16), ffn_w3[i].astype(BF16),
                          ffn_w2[i].astype(BF16))
        else:
            xp, xs = _moe(xp, xs, norm_ffn_g[l][None], router_w[i], *moe_bf)

        srp.append(p_re); sip.append(p_im)
        cvp.append(jnp.stack([vin[(b + 1) * seq - (CONV_K - 1):(b + 1) * seq] for b in range(n_p)]))
        ks.append(k_s.reshape(n_s, dec, ATT_HEADS, 2, ATT_HEAD_DIM))
        vs.append(v_s.reshape(n_s, dec, ATT_HEADS, ATT_V_DIM))
        srs.append(s_re); sis.append(s_im); cvs.append(ext_s[:, dec:])

    kt, vr = kv_prompt
    k_prompt = kt.reshape(depth, n_p, ATT_HEADS, 2, ATT_HEAD_DIM, seq).transpose(0, 1, 5, 2, 3, 4)
    v_prompt = vr.reshape(depth, n_p, seq, ATT_HEADS, ATT_V_DIM)
    return (xp.reshape(n_p, seq, D_MODEL), xs.reshape(n_s, dec, D_MODEL),
            k_prompt, v_prompt, jnp.stack(srp), jnp.stack(sip), jnp.stack(cvp),
            jnp.stack(ks), jnp.stack(vs), jnp.stack(srs), jnp.stack(sis), jnp.stack(cvs))
```

```python
import functools
import math

import jax
import jax.numpy as jnp
from jax import lax
from jax.experimental import pallas as pl
from jax.experimental.pallas import tpu as pltpu
from jax.experimental.pallas import tpu_sc as plsc

F32 = jnp.float32
BF16 = jnp.bfloat16

D_MODEL = 1024
SSM_WIDTH = 256
SSM_GROUP = 16
SSM_GROUPS = SSM_WIDTH // SSM_GROUP
SSM_STATE = 64
ATT_HEADS = 4
ATT_HEAD_DIM = 64
ATT_V_DIM = 2 * ATT_HEAD_DIM
QK_WIDTH = ATT_HEADS * 2 * ATT_HEAD_DIM
ATT_WIDTH = ATT_HEADS * ATT_V_DIM
CONV_WIDTH = 256
CONV_K = 3
N_BRANCH = 3
N_EXPERTS = 8
TOP_K = 2
EPS = 1e-6
NEG_INF = -1e30
LOG2_E = 1.4426950408889634

C_U = 0
C_Q = C_U + SSM_WIDTH
C_K = C_Q + QK_WIDTH
C_V = C_K + QK_WIDTH
C_CB = C_V + ATT_WIDTH
C_CC = C_CB + CONV_WIDTH
C_CH = C_CC + CONV_WIDTH
C_G = C_CH + CONV_WIDTH
IN_COLS = C_G + N_BRANCH * D_MODEL

SSM_CHUNK = 8
SUBLANES = 8
BF16_SUBLANES = 16
ROW_TILE = 512
ATT_TILE = 512
MOE_TILE = 256
SC_CORES = 2
SC_WORKERS = SC_CORES * 16
SC_CHUNK_BYTES = 192 * 1024
VMEM_LIMIT = 56 * 1024 * 1024


def _cparams(sem):
    return pltpu.CompilerParams(dimension_semantics=sem, vmem_limit_bytes=VMEM_LIMIT)


def _const_spec(shape):
    nd = len(shape)
    return pl.BlockSpec(shape, lambda *_: (0,) * nd)


def _bdot(a, b):
    return jnp.dot(a, b, preferred_element_type=F32)


def _rms_rows(x, g):
    ms = jnp.mean(x * x, axis=-1, keepdims=True)
    return x * lax.rsqrt(ms + EPS) * g


def _group_specs(tm, w, n_prompt_tiles):
    return [pl.BlockSpec((tm, w), lambda i: (jnp.minimum(i, n_prompt_tiles - 1), 0)),
            pl.BlockSpec((tm, w), lambda i: (jnp.maximum(i - n_prompt_tiles, 0), 0))]


def _read_group(p_ref, s_ref, is_prompt):
    return jnp.where(is_prompt, p_ref[...], s_ref[...])


def _write_group(p_ref, s_ref, is_prompt, val):
    @pl.when(is_prompt)
    def _():
        p_ref[...] = val

    @pl.when(jnp.logical_not(is_prompt))
    def _():
        s_ref[...] = val


def _pack_halves(x):
    half = x.shape[1] // 2
    bits = lambda t: lax.bitcast_convert_type(t.astype(BF16).astype(F32), jnp.uint32)
    return (bits(x[:, :half]) >> 16) | (bits(x[:, half:]) & jnp.uint32(0xFFFF0000))


def _unpack_halves(p):
    lo = lax.bitcast_convert_type(p << 16, F32)
    hi = lax.bitcast_convert_type(p & jnp.uint32(0xFFFF0000), F32)
    return jnp.concatenate([lo, hi], axis=1)


def _segment_rms(z, g, seg):
    ms = _bdot((z * z).astype(BF16), seg)
    return z * lax.rsqrt(ms + EPS) * g


def _inproj_kernel(xp_ref, xs_ref, g_ref, w_ref, qg_ref, kg_ref, seg_ref, *rest,
                   n_prompt_tiles, n_prev, layer):
    (q_ref, kb_ref, vb_ref, cb_ref, vin_ref, gate_ref,
     kt_ref, vr_ref, ks_ref, vs_ref, up_ref, us_ref) = rest[n_prev:]
    tm = q_ref.shape[0]
    is_prompt = pl.program_id(0) < n_prompt_tiles
    h = _rms_rows(_read_group(xp_ref, xs_ref, is_prompt), g_ref[...]).astype(BF16)

    def proj(a, b):
        return _bdot(h, w_ref[:, a:b])

    seg = seg_ref[...]
    _write_group(up_ref, us_ref, is_prompt, proj(C_U, C_Q))
    qn = _segment_rms(proj(C_Q, C_K), qg_ref[...], seg)
    q_ref[...] = (qn * (ATT_HEAD_DIM ** -0.5 * LOG2_E)).astype(BF16)
    kn = _segment_rms(proj(C_K, C_V), kg_ref[...], seg)
    kb_ref[...] = kn.astype(BF16)
    v = proj(C_V, C_CB)
    vb_ref[...] = v.astype(BF16)
    cb_ref[...] = proj(C_CB, C_CC)
    vin_ref[...] = proj(C_CC, C_CH) * proj(C_CH, C_G)
    for j in range(N_BRANCH):
        a = C_G + j * D_MODEL
        gate_ref[:, j * D_MODEL:(j + 1) * D_MODEL] = jax.nn.sigmoid(proj(a, a + D_MODEL)).astype(gate_ref.dtype)

    @pl.when(is_prompt)
    def _():
        if n_prev:
            kt_l, vr_l = kt_ref, vr_ref
        else:
            kt_l, vr_l = kt_ref.at[layer], vr_ref.at[layer]
            for other in range(kt_ref.shape[0]):
                if other != layer:
                    kt_ref[other] = jnp.zeros(kt_ref.shape[1:], F32)
                    vr_ref[other] = jnp.zeros(vr_ref.shape[1:], F32)
        kt_l[...] = kn.T
        for hd in range(ATT_HEADS):
            vr_l[pl.ds(hd, tm, stride=ATT_HEADS), :] = v[:, hd * ATT_V_DIM:(hd + 1) * ATT_V_DIM]

    @pl.when(jnp.logical_not(is_prompt))
    def _():
        ks_ref[...] = kn
        vs_ref[...] = v


def _inproj(xp, xs, g, w_bf, qg, kg, seg, layer, depth, n_p, seq, prev):
    tm = ROW_TILE
    tp, ts = xp.shape[0], xs.shape[0]
    t = tp + ts
    npt = tp // tm
    tps = seq // tm
    row = lambda w: pl.BlockSpec((tm, w), lambda i: (i, 0))
    pc = lambda i: jnp.minimum(i, npt - 1)
    outs = [(QK_WIDTH, BF16), (QK_WIDTH, BF16), (ATT_WIDTH, BF16),
            (CONV_WIDTH, F32), (CONV_WIDTH, F32), (N_BRANCH * D_MODEL, BF16)]
    out_shape = ([jax.ShapeDtypeStruct((t, w), d) for w, d in outs]
                 + [jax.ShapeDtypeStruct((depth, n_p, QK_WIDTH, seq), F32),
                    jax.ShapeDtypeStruct((depth, tp * ATT_HEADS, ATT_V_DIM), F32),
                    jax.ShapeDtypeStruct((ts, QK_WIDTH), F32), jax.ShapeDtypeStruct((ts, ATT_WIDTH), F32),
                    jax.ShapeDtypeStruct((tp, SSM_WIDTH), F32), jax.ShapeDtypeStruct((ts, SSM_WIDTH), F32)])
    srow = _group_specs(tm, QK_WIDTH, npt)[1]
    prev = () if prev is None else tuple(prev)
    lead, at = (None, layer) if prev else (depth, 0)
    out_specs = ([row(w) for w, _ in outs]
                 + [pl.BlockSpec((lead, None, QK_WIDTH, tm), lambda i: (at, pc(i) // tps, 0, pc(i) % tps)),
                    pl.BlockSpec((lead, tm * ATT_HEADS, ATT_V_DIM), lambda i: (at, pc(i), 0)),
                    srow, srow] + _group_specs(tm, SSM_WIDTH, npt))
    n_in = 7
    return pl.pallas_call(
        functools.partial(_inproj_kernel, n_prompt_tiles=npt, n_prev=len(prev), layer=layer),
        out_shape=out_shape,
        grid=(t // tm,),
        in_specs=(_group_specs(tm, D_MODEL, npt)
                  + [_const_spec((1, D_MODEL)),
                     pl.BlockSpec((None, D_MODEL, IN_COLS), lambda i: (layer, 0, 0)),
                     _const_spec((1, QK_WIDTH)), _const_spec((1, QK_WIDTH)),
                     _const_spec((QK_WIDTH, QK_WIDTH))]
                  + [pl.BlockSpec(memory_space=pl.ANY)] * len(prev)),
        out_specs=out_specs,
        input_output_aliases={n_in + j: len(outs) + j for j in range(len(prev))},
        compiler_params=_cparams(("arbitrary",)),
        name="inproj",
    )(xp, xs, g, w_bf, qg, kg, seg, *prev)


def _s5_intra(u, kb_ref, tc):
    rowmod = lax.broadcasted_iota(jnp.int32, u.shape, 0) % tc
    y = _bdot(u.astype(BF16), kb_ref[0])
    for m in range(1, tc):
        um = jnp.where(rowmod >= m, pltpu.roll(u, m, axis=0), 0.0)
        y = y + _bdot(um.astype(BF16), kb_ref[m])
    return y


def _s5_inject(u_refs, pb_ref, k0, n_chunks, tc):
    w = None
    for k in range(tc):
        rows = pl.ds(k, n_chunks, stride=tc)
        uk = jnp.concatenate([r[rows, :] for r in u_refs], axis=1).astype(BF16)
        d = _bdot(uk, pb_ref[k0 + k])
        w = d if w is None else w + d
    return w


def _s5_readout(y_ref, y_scrs, y, s_in, qb_ref, n_chunks, tc):
    lanes = y_scrs[0].shape[1]
    for h, scr in enumerate(y_scrs):
        scr[...] = y[:, h * lanes:(h + 1) * lanes]
    sb = s_in.astype(BF16)
    for k in range(tc):
        rows = pl.ds(k, n_chunks, stride=tc)
        yk = _bdot(sb, qb_ref[k])
        for h, scr in enumerate(y_scrs):
            scr[rows, :] = scr[rows, :] + yk[:, h * lanes:(h + 1) * lanes]
    y_ref[...] = jnp.concatenate([scr[...] for scr in y_scrs], axis=1)


def _cmul_add(a_re, a_im, s, w, half):
    s_re, s_im = s[:, :half], s[:, half:]
    return jnp.concatenate([a_re * s_re - a_im * s_im, a_re * s_im + a_im * s_re], axis=1) + w


def _s5_prompt_kernel(ua_ref, ub_ref, kb_ref, pb_ref, qb_ref, a_ref, y_ref, fs_ref,
                      carry, w_fold, s_fold, ya_scr, yb_scr, *, tc):
    n_b, tm, hw = ua_ref.shape
    n_chunks = tm // tc
    rows = n_b * n_chunks
    fold = carry.shape[0] // n_b
    hf = fold // 2

    @pl.when(pl.program_id(0) == 0)
    def _():
        carry[...] = jnp.zeros(carry.shape, F32)

    u = jnp.concatenate([ua_ref[...].reshape(n_b * tm, hw), ub_ref[...].reshape(n_b * tm, hw)], axis=1)
    y = _s5_intra(u, kb_ref, tc)
    w = None
    for k in range(tc):
        at_k = pl.ds(k, n_chunks, stride=tc)
        uk = jnp.concatenate([jnp.concatenate([ua_ref[b, at_k, :], ub_ref[b, at_k, :]], axis=1)
                              for b in range(n_b)], axis=0).astype(BF16)
        d = _bdot(uk, pb_ref[k])
        w = d if w is None else w + d
    for r in range(fold):
        w_fold[pl.ds(r, rows, stride=fold), :] = w[:, r * hw:(r + 1) * hw]
    a_re = a_ref[0]
    a_im = a_ref[1]
    for b in range(n_b):
        s = carry[b * fold:(b + 1) * fold, :]
        for c in range(n_chunks):
            at = slice((b * n_chunks + c) * fold, (b * n_chunks + c + 1) * fold)
            s_fold[at, :] = s
            s_re, s_im = s[:hf], s[hf:]
            s = jnp.concatenate([a_re * s_re - a_im * s_im, a_re * s_im + a_im * s_re], axis=0) + w_fold[at, :]
        carry[b * fold:(b + 1) * fold, :] = s
    fs_ref[...] = carry[...]
    s_in = jnp.concatenate([s_fold[pl.ds(r, rows, stride=fold), :] for r in range(fold)], axis=1)
    for h, scr in enumerate((ya_scr, yb_scr)):
        scr[...] = y[:, h * hw:(h + 1) * hw]
    sb = s_in.astype(BF16)
    for k in range(tc):
        at_k = pl.ds(k, rows, stride=tc)
        yk = _bdot(sb, qb_ref[k])
        for h, scr in enumerate((ya_scr, yb_scr)):
            scr[at_k, :] = scr[at_k, :] + yk[:, h * hw:(h + 1) * hw]
    for b in range(n_b):
        y_ref[b] = jnp.concatenate([ya_scr[b * tm:(b + 1) * tm, :], yb_scr[b * tm:(b + 1) * tm, :]], axis=1)


def _s5_sample_kernel(ua_ref, ub_ref, s0_ref, kb_ref, pb_ref, qb_ref, a_ref, y_ref, fs_ref,
                      ya_scr, yb_scr, *, tc, k0):
    n_chunks = ua_ref.shape[0] // tc
    half = a_ref.shape[1]
    y = _s5_intra(jnp.concatenate([ua_ref[...], ub_ref[...]], axis=1), kb_ref, tc)
    w = _s5_inject((ua_ref, ub_ref), pb_ref, k0, n_chunks, tc)
    s0 = s0_ref[...]
    fs_ref[...] = _cmul_add(a_ref[2:3, :], a_ref[3:4, :], s0, w, half)
    _s5_readout(y_ref, (ya_scr, yb_scr), y, s0, qb_ref, n_chunks, tc)


def _s5_tables_kernel(are_ref, aim_ref, ldt_ref, bre_ref, bim_ref, cre_ref, cim_ref,
                      kb_ref, pb_ref, qb_ref, a_ref, *, tc, dec):
    are = are_ref[...]
    aim = aim_ref[...]
    dt = jnp.exp(ldt_ref[...])

    def power(m):
        mag = jnp.exp(are * dt * m)
        ang = aim * dt * m
        return mag * jnp.cos(ang), mag * jnp.sin(ang)

    ab_re, ab_im = power(1.0)
    den = are * are + aim * aim
    nr = ab_re - 1.0
    cr = (nr * are + ab_im * aim) / den
    ci = (ab_im * are - nr * aim) / den
    gj, gp = bre_ref.shape
    row_g = lax.broadcasted_iota(jnp.int32, (gj, gp), 0) // SSM_GROUP
    col_g = lax.broadcasted_iota(jnp.int32, (gj, gp), 1) // SSM_STATE
    diag = row_g == col_g
    bre = bre_ref[...]
    bim = bim_ref[...]
    bb_re = jnp.where(diag, cr * bre - ci * bim, 0.0)
    bb_im = jnp.where(diag, cr * bim + ci * bre, 0.0)
    cc_re = jnp.where(diag, cre_ref[...], 0.0)
    cc_im = jnp.where(diag, cim_ref[...], 0.0)
    c_blk = jnp.concatenate([cc_re, -cc_im], axis=1).T
    for m in range(tc):
        pr, pi = power(float(m))
        pm = jnp.concatenate([pr * bb_re - pi * bb_im, pr * bb_im + pi * bb_re], axis=1)
        pb_ref[tc - 1 - m] = pm.astype(pb_ref.dtype)
        kb_ref[m] = jnp.dot(pm, c_blk, preferred_element_type=F32,
                            precision=lax.Precision.HIGHEST).astype(kb_ref.dtype)
        qr, qi = power(float(m + 1))
        qm = jnp.concatenate([qr * cc_re - qi * cc_im, -(qr * cc_im + qi * cc_re)], axis=1)
        qb_ref[m] = qm.T.astype(qb_ref.dtype)
    a_ref[...] = jnp.concatenate(list(power(float(tc)) + power(float(dec))), axis=0)


def _s5_tables(a_re, a_im, log_dt, b_re, b_im, c_re, c_im, dec):
    depth = a_re.shape[0]
    tc = SSM_CHUNK
    g, p, j = SSM_GROUPS, SSM_STATE, SSM_GROUP
    gp, gj = g * p, g * j
    rowv = lambda t: t.reshape(depth, 1, gp)
    ldt = jnp.repeat(log_dt, p, axis=1).reshape(depth, 1, gp)
    bt = lambda t: jnp.tile(t.transpose(0, 1, 3, 2).reshape(depth, gj, p), (1, 1, g))
    ct = lambda t: jnp.tile(t.reshape(depth, gj, p), (1, 1, g))
    lay = lambda *shape: pl.BlockSpec((None,) + shape, lambda l: (l,) + (0,) * len(shape))
    return pl.pallas_call(
        functools.partial(_s5_tables_kernel, tc=tc, dec=dec),
        out_shape=[jax.ShapeDtypeStruct((depth, tc, gj, gj), BF16),
                   jax.ShapeDtypeStruct((depth, tc, gj, 2 * gp), BF16),
                   jax.ShapeDtypeStruct((depth, tc, 2 * gp, gj), BF16),
                   jax.ShapeDtypeStruct((depth, 4, gp), F32)],
        grid=(depth,),
        in_specs=[lay(1, gp)] * 3 + [lay(gj, gp)] * 4,
        out_specs=[lay(tc, gj, gj), lay(tc, gj, 2 * gp), lay(tc, 2 * gp, gj), lay(4, gp)],
        compiler_params=_cparams(("arbitrary",)),
        name="s5_tables",
    )(rowv(a_re), rowv(a_im), ldt, bt(b_re), bt(b_im), ct(c_re), ct(c_im))


def _s5(up, us, s0_re, s0_im, tables, layer, n_p, seq, n_s, dec):
    kb, pb, qb, adec = tables
    lay = lambda t: pl.BlockSpec((None,) + t.shape[1:], lambda i: (layer,) + (0,) * (t.ndim - 1))
    tc = SSM_CHUNK
    g, p = SSM_GROUPS, SSM_STATE
    sw = 2 * g * p
    tm = ROW_TILE
    tp = n_p * seq
    ts = n_s * dec
    hw = SSM_WIDTH // 2
    fold = sw // hw
    assert dec <= tc and seq % tm == 0 and tm % tc == 0
    up3 = up.reshape(n_p, seq, SSM_WIDTH)
    chunk_rows = n_p * (tm // tc)
    yp, fsp = pl.pallas_call(
        functools.partial(_s5_prompt_kernel, tc=tc),
        out_shape=[jax.ShapeDtypeStruct((n_p, seq, SSM_WIDTH), F32),
                   jax.ShapeDtypeStruct((n_p * fold, hw), F32)],
        grid=(seq // tm,),
        in_specs=[pl.BlockSpec((n_p, tm, hw), lambda i: (0, i, 0)),
                  pl.BlockSpec((n_p, tm, hw), lambda i: (0, i, 1)),
                  lay(kb), lay(pb), lay(qb),
                  pl.BlockSpec((None, 4, fold // 2, hw), lambda i: (layer, 0, 0, 0))],
        out_specs=[pl.BlockSpec((n_p, tm, SSM_WIDTH), lambda i: (0, i, 0)),
                   _const_spec((n_p * fold, hw))],
        scratch_shapes=[pltpu.VMEM((n_p * fold, hw), F32),
                        pltpu.VMEM((chunk_rows * fold, hw), F32), pltpu.VMEM((chunk_rows * fold, hw), F32),
                        pltpu.VMEM((n_p * tm, hw), F32), pltpu.VMEM((n_p * tm, hw), F32)],
        compiler_params=_cparams(("arbitrary",)),
        name="s5_prompt",
    )(up3, up3, kb, pb, qb, adec.reshape(adec.shape[0], 4, fold // 2, hw))
    yp = yp.reshape(tp, SSM_WIDTH)
    s0 = jnp.concatenate([s0_re.reshape(n_s, g * p), s0_im.reshape(n_s, g * p)], axis=1)
    ys, fss = pl.pallas_call(
        functools.partial(_s5_sample_kernel, tc=dec, k0=tc - dec),
        out_shape=[jax.ShapeDtypeStruct((ts, SSM_WIDTH), F32), jax.ShapeDtypeStruct((n_s, sw), F32)],
        grid=(1,),
        in_specs=[pl.BlockSpec((ts, hw), lambda i: (0, 0)), pl.BlockSpec((ts, hw), lambda i: (0, 1)),
                  _const_spec((n_s, sw)),
                  lay(kb), lay(pb), lay(qb), lay(adec)],
        out_specs=[_const_spec((ts, SSM_WIDTH)), _const_spec((n_s, sw))],
        scratch_shapes=[pltpu.VMEM((ts, hw), F32), pltpu.VMEM((ts, hw), F32)],
        compiler_params=_cparams(("arbitrary",)),
        name="s5_sample",
    )(us, us, s0, kb, pb, qb, adec)
    half = g * p
    fsp = fsp.reshape(n_p, sw)
    st = lambda a, n: a.reshape(n, g, p)
    return (yp, ys, st(fsp[:, :half], n_p), st(fsp[:, half:], n_p),
            st(fss[:, :half], n_s), st(fss[:, half:], n_s))


def _lambda(lp_ref, lam_init):
    lp = lp_ref[...]
    s1 = jnp.sum(lp[0:1, :] * lp[1:2, :], axis=-1, keepdims=True)
    s2 = jnp.sum(lp[2:3, :] * lp[3:4, :], axis=-1, keepdims=True)
    return jnp.exp(s1) - jnp.exp(s2) + lam_init


def _attn_kernel(qi_ref, ki_ref, pg_ref, q_ref, k_ref, v_ref, lp_ref, hg_ref, *rest,
                 tile, lam_init, cast_blocks, n_seq, n_pages, layer):
    n_cast = len(cast_blocks)
    n_pg = n_seq * n_pages
    cast_in = rest[:n_cast]
    sq_ref, skn_ref, svn_ref, ckt_hbm, cvr_hbm = rest[n_cast:n_cast + 5]
    o_ref, so_ref = rest[n_cast + 5:n_cast + 7]
    cast_out = rest[n_cast + 7:2 * n_cast + 7]
    m_scr, acc_scr, kbuf, vbuf, sem = rest[2 * n_cast + 7:]
    dec = sq_ref.shape[1]
    t = pl.program_id(1)
    qi = qi_ref[t]
    ki = ki_ref[t]
    vd = ATT_V_DIM
    step = pl.program_id(0) * pl.num_programs(1) + t
    last_step = pl.num_programs(0) * pl.num_programs(1) - 1
    slot = step % 2

    def page_copies(s, to_slot, c):
        pg = pg_ref[s * n_pg + c]
        return (pltpu.make_async_copy(ckt_hbm.at[layer, pg], kbuf.at[to_slot, c], sem.at[to_slot]),
                pltpu.make_async_copy(cvr_hbm.at[layer, pg], vbuf.at[to_slot, c], sem.at[to_slot]))

    @pl.when(step == 0)
    def _():
        for c in range(n_pg):
            for cp in page_copies(0, 0, c):
                cp.start()

    for c in range(n_pg):
        for cp in page_copies(step, slot, c):
            cp.wait()
    for c in range(n_pg):
        for cp in page_copies(step + 1, 1 - slot, c):
            cp.start()

    for src, dst, n_blocks in zip(cast_in, cast_out, cast_blocks):
        @pl.when(step < n_blocks)
        def _(src=src, dst=dst):
            dst[...] = src[...].astype(dst.dtype)

    @pl.when(ki == 0)
    def _():
        m_scr[...] = jnp.full(m_scr.shape, NEG_INF, F32)
        acc_scr[...] = jnp.zeros(acc_scr.shape, F32)

    def sample_group():
        for u in range(n_seq):
            kp = [kbuf.at[slot, u * n_pages + j] for j in range(n_pages)]
            vp = [vbuf.at[slot, u * n_pages + j] for j in range(n_pages)]
            o_s = _attend_sample(sq_ref[u], skn_ref[u], svn_ref[u], kp, vp, lp_ref, hg_ref, dec, lam_init)
            so_ref[u] = o_s.astype(so_ref.dtype)

    def accumulate(r0, nr, nk, masked):
        rows = slice(r0, r0 + nr)
        lane = lax.broadcasted_iota(jnp.int32, (nr, vd), 1)
        ones = jnp.ones((nk, vd), BF16)
        if masked:
            mask = (lax.broadcasted_iota(jnp.int32, (nr, nk), 1)
                    <= r0 + lax.broadcasted_iota(jnp.int32, (nr, nk), 0))
        nt = (((1,), (1,)), ((), ()))
        for h in range(ATT_HEADS):
            cols = slice(h * vd, (h + 1) * vd)
            q = q_ref[rows, cols]
            k = k_ref[0:nk, cols]
            v1 = jnp.concatenate([v_ref[0:nk, cols], ones], axis=1)
            zero = jnp.zeros_like(q)
            for c in range(2):
                qm = jnp.where((lane >= ATT_HEAD_DIM) == bool(c), q, zero)
                s = lax.dot_general(qm, k, nt, preferred_element_type=F32)
                if masked:
                    s = jnp.where(mask, s, NEG_INF)
                idx = 2 * h + c
                m_old = m_scr[idx, rows, :]
                m_row = jnp.max(s, axis=-1, keepdims=True)
                m_new = jnp.maximum(m_old, jnp.broadcast_to(m_row, m_old.shape))
                alpha = jnp.exp2(m_old - m_new)
                p = jnp.exp2(s - jnp.concatenate([m_new] * (nk // vd), axis=1)).astype(BF16)
                acc_scr[idx, rows, :] = (jnp.concatenate([alpha, alpha], axis=1) * acc_scr[idx, rows, :]
                                         + _bdot(p, v1))
                m_scr[idx, rows, :] = m_new

    @pl.when(ki < qi)
    def _():
        sample_group()
        accumulate(0, tile, tile, False)

    @pl.when(ki == qi)
    def _():
        sample_group()
        accumulate(0, tile // 2, tile // 2, True)
        accumulate(tile // 2, tile // 2, tile, True)
        lam = _lambda(lp_ref, lam_init)
        hg = hg_ref[...]
        for h in range(ATT_HEADS):
            a1 = acc_scr[2 * h]
            a2 = acc_scr[2 * h + 1]
            o = a1[:, :vd] / a1[:, vd:] - lam * (a2[:, :vd] / a2[:, vd:])
            o_ref[:, h * vd:(h + 1) * vd] = (_rms_rows(o, hg) * (1.0 - lam_init)).astype(o_ref.dtype)

    @pl.when(step == last_step)
    def _():
        for c in range(n_pg):
            for cp in page_copies(step + 1, 1 - slot, c):
                cp.wait()


def _cast_block_rows(rows, n_steps):
    for br in range(BF16_SUBLANES, rows + 1, BF16_SUBLANES):
        if rows % br == 0 and rows // br <= n_steps:
            return br
    raise ValueError((rows, n_steps))


def _attention(q, k, v, sq, skn, svn, cache_kt, cache_vr, layer, page_table, lp, hg, n_p, seq, lam_init,
               casts=()):
    tile = min(ATT_TILE, seq)
    assert seq % tile == 0 and tile % ATT_V_DIM == 0
    nq = seq // tile
    pairs = [(i, j) for i in range(nq) for j in range(i + 1)]
    n_pairs = len(pairs)
    n_steps = n_p * n_pairs
    n_s, dec, _ = sq.shape
    n_pages = page_table.shape[1]
    page = cache_kt.shape[3]
    n_seq = next(d for d in range(1, n_s + 1) if n_s % d == 0 and n_s // d <= n_steps)
    n_groups = n_s // n_seq
    qi_tab = jnp.asarray([a for a, _ in pairs], jnp.int32)
    ki_tab = jnp.asarray([b for _, b in pairs], jnp.int32)
    step = lambda b, t: b * n_pairs + t
    qspec = pl.BlockSpec((tile, ATT_WIDTH), lambda b, t, qi, ki, pt: (b * nq + qi[t], 0))
    kspec = pl.BlockSpec((tile, ATT_WIDTH), lambda b, t, qi, ki, pt: (b * nq + ki[t], 0))
    tok = pl.BlockSpec((n_seq, dec, QK_WIDTH), lambda b, t, qi, ki, pt: (jnp.minimum(step(b, t), n_groups - 1), 0, 0))
    groups = jnp.minimum(jnp.arange(n_steps + 1), n_groups - 1)
    step_pages = page_table.reshape(n_groups, n_seq * n_pages)[groups].reshape(-1)
    n_pg = n_seq * n_pages
    cast_specs, cast_blocks = [], []
    for w in casts:
        br = _cast_block_rows(w.shape[0], n_steps)
        nblk = w.shape[0] // br
        cast_blocks.append(nblk)
        cast_specs.append(pl.BlockSpec(
            (br, w.shape[1]),
            lambda b, t, qi, ki, pt, nblk=nblk: (jnp.minimum(step(b, t), nblk - 1), 0)))
    grid_spec = pltpu.PrefetchScalarGridSpec(
        num_scalar_prefetch=3,
        grid=(n_p, n_pairs),
        in_specs=[qspec, kspec, kspec,
                  pl.BlockSpec((4, ATT_HEAD_DIM), lambda b, t, qi, ki, pt: (0, 0)),
                  pl.BlockSpec((1, ATT_V_DIM), lambda b, t, qi, ki, pt: (0, 0))]
        + cast_specs + [tok, tok, tok, pl.BlockSpec(memory_space=pl.ANY), pl.BlockSpec(memory_space=pl.ANY)],
        out_specs=[qspec, tok] + cast_specs,
        scratch_shapes=[pltpu.VMEM((2 * ATT_HEADS, tile, ATT_V_DIM), F32),
                        pltpu.VMEM((2 * ATT_HEADS, tile, 2 * ATT_V_DIM), F32),
                        pltpu.VMEM((2, n_pg, QK_WIDTH, page), F32),
                        pltpu.VMEM((2, n_pg, page * ATT_HEADS, ATT_V_DIM), F32),
                        pltpu.SemaphoreType.DMA((2,))],
    )
    return pl.pallas_call(
        functools.partial(_attn_kernel, tile=tile, lam_init=lam_init, cast_blocks=tuple(cast_blocks),
                          n_seq=n_seq, n_pages=n_pages, layer=layer),
        out_shape=[jax.ShapeDtypeStruct((n_p * seq, ATT_WIDTH), BF16),
                   jax.ShapeDtypeStruct((n_s, dec, ATT_WIDTH), BF16)]
        + [jax.ShapeDtypeStruct(w.shape, BF16) for w in casts],
        grid_spec=grid_spec,
        compiler_params=_cparams(("arbitrary", "arbitrary")),
        name="attention",
    )(qi_tab, ki_tab, step_pages, q, k, v, lp, hg, *casts, sq, skn, svn, cache_kt, cache_vr)


def _attend_sample(q, kn, vn, kp, vp, lp_ref, hg_ref, dec, lam_init):
    n_pages = len(kp)
    page = kp[0].shape[1]
    grp = 2 * dec
    n_rows = ATT_HEADS * grp
    q = q.astype(F32)
    qt = jnp.concatenate([q] * (ATT_HEADS * 2), axis=0)
    r = lax.broadcasted_iota(jnp.int32, (n_rows, QK_WIDTH), 0)
    c = lax.broadcasted_iota(jnp.int32, (n_rows, QK_WIDTH), 1)
    qb = jnp.where(r // dec == c // ATT_HEAD_DIM, qt, 0.0).astype(BF16)
    nt = (((1,), (1,)), ((), ()))
    kt_past = jnp.concatenate([kp[j][...].astype(BF16) for j in range(n_pages)], axis=1)
    s_past = _bdot(qb, kt_past)
    s_new = lax.dot_general(qb, kn, nt, preferred_element_type=F32)
    rn = lax.broadcasted_iota(jnp.int32, (n_rows, dec), 0) % dec
    cn = lax.broadcasted_iota(jnp.int32, (n_rows, dec), 1)
    s_new = jnp.where(cn <= rn, s_new, NEG_INF)
    m = jnp.maximum(jnp.max(s_new, axis=-1, keepdims=True), jnp.max(s_past, axis=-1, keepdims=True))
    p_new = jnp.exp2(s_new - m)
    p_past = jnp.exp2(s_past - m)
    l = jnp.sum(p_new, axis=-1, keepdims=True) + jnp.sum(p_past, axis=-1, keepdims=True)
    acc_new = _bdot(p_new.astype(BF16), vn)
    lam = _lambda(lp_ref, lam_init)
    hg = hg_ref[...]
    outs = []
    for h in range(ATT_HEADS):
        rows = slice(h * grp, (h + 1) * grp)
        cols = slice(h * ATT_V_DIM, (h + 1) * ATT_V_DIM)
        v_past = jnp.concatenate([vp[j][pl.ds(h, page, stride=ATT_HEADS), :].astype(BF16)
                                  for j in range(n_pages)], axis=0)
        acc = acc_new[rows, cols] + _bdot(p_past[rows, :].astype(BF16), v_past)
        acc = acc / l[rows, :]
        o = acc[:dec] - lam * acc[dec:]
        outs.append(_rms_rows(o, hg) * (1.0 - lam_init))
    return jnp.concatenate(outs, axis=1)


def _merge_kernel(xp_ref, xs_ref, up_ref, us_ref, cb_ref, vin_ref, halo_ref, gate_ref,
                  yrp_ref, yrs_ref, ybp_ref, ybs_ref, vm1s_ref, vm2s_ref,
                  d_ref, wglu_ref, bglu_ref, cw_ref, wssm_ref, watt_ref, wconv_ref, wout_ref,
                  op_ref, os_ref, *, n_prompt_tiles, tiles_per_seq):
    i = pl.program_id(0)
    is_prompt = i < n_prompt_tiles
    yraw = jnp.where(is_prompt, yrp_ref[...], yrs_ref[...])
    yb = jnp.where(is_prompt, ybp_ref[...], ybs_ref[...])
    y = jax.nn.gelu(yraw + d_ref[...] * _read_group(up_ref, us_ref, is_prompt))
    ya = y * jax.nn.sigmoid(_bdot(y.astype(BF16), wglu_ref[...]) + bglu_ref[...])

    vin = vin_ref[...]
    row = lax.broadcasted_iota(jnp.int32, vin.shape, 0)
    halo = jnp.where(i % tiles_per_seq == 0, 0.0, halo_ref[...])
    h1 = jnp.broadcast_to(halo[SUBLANES - 1:SUBLANES, :], vin.shape)
    h2 = jnp.broadcast_to(halo[SUBLANES - 2:SUBLANES - 1, :], vin.shape)
    vm1 = jnp.where(row == 0, h1, pltpu.roll(vin, 1, axis=0))
    vm2 = jnp.where(row == 0, h2, jnp.where(row == 1, h1, pltpu.roll(vin, 2, axis=0)))
    vm1 = jnp.where(is_prompt, vm1, vm1s_ref[...])
    vm2 = jnp.where(is_prompt, vm2, vm2s_ref[...])
    conv = vm2 * cw_ref[0:1, :] + vm1 * cw_ref[1:2, :] + vin * cw_ref[2:3, :]
    yc = cb_ref[...] * conv
    merged = (gate_ref[:, 0:D_MODEL] * _bdot(ya.astype(BF16), wssm_ref[...])
              + gate_ref[:, D_MODEL:2 * D_MODEL] * _bdot(yb, watt_ref[...])
              + gate_ref[:, 2 * D_MODEL:3 * D_MODEL] * _bdot(yc.astype(BF16), wconv_ref[...]))
    x_new = _read_group(xp_ref, xs_ref, is_prompt) + _bdot(merged.astype(BF16), wout_ref[...])
    _write_group(op_ref, os_ref, is_prompt, x_new)


def _merge(xp, xs, up, us, cb, vin, gates, yr_p, yr_s, yb_p, yb_s, vm1_s, vm2_s,
           d, wglu, bglu, cw, wssm, watt, wconv, wout, tiles_per_seq):
    tm = ROW_TILE
    tp, ts = xp.shape[0], xs.shape[0]
    npt = tp // tm
    row = lambda w: pl.BlockSpec((tm, w), lambda i: (i, 0))
    prow = lambda w: _group_specs(tm, w, npt)[0]
    srow = lambda w: _group_specs(tm, w, npt)[1]
    halo = pl.BlockSpec((SUBLANES, CONV_WIDTH), lambda i: (jnp.maximum(i * (tm // SUBLANES) - 1, 0), 0))
    return pl.pallas_call(
        functools.partial(_merge_kernel, n_prompt_tiles=npt, tiles_per_seq=tiles_per_seq),
        out_shape=[jax.ShapeDtypeStruct((tp, D_MODEL), F32), jax.ShapeDtypeStruct((ts, D_MODEL), F32)],
        grid=((tp + ts) // tm,),
        in_specs=[prow(D_MODEL), srow(D_MODEL), prow(SSM_WIDTH), srow(SSM_WIDTH),
                  row(CONV_WIDTH), row(CONV_WIDTH), halo,
                  row(N_BRANCH * D_MODEL),
                  prow(SSM_WIDTH), srow(SSM_WIDTH), prow(ATT_WIDTH), srow(ATT_WIDTH),
                  srow(CONV_WIDTH), srow(CONV_WIDTH),
                  _const_spec((1, SSM_WIDTH)), _const_spec((SSM_WIDTH, SSM_WIDTH)),
                  _const_spec((1, SSM_WIDTH)), _const_spec((CONV_K, CONV_WIDTH)),
                  _const_spec((SSM_WIDTH, D_MODEL)), _const_spec((ATT_WIDTH, D_MODEL)),
                  _const_spec((CONV_WIDTH, D_MODEL)), _const_spec((D_MODEL, D_MODEL))],
        out_specs=[prow(D_MODEL), srow(D_MODEL)],
        compiler_params=_cparams(("arbitrary",)),
        name="merge",
    )(xp, xs, up, us, cb, vin, vin, gates, yr_p, yr_s, yb_p, yb_s, vm1_s, vm2_s,
      d, wglu, bglu, cw, wssm, watt, wconv, wout)


def _swiglu(h, w1, w3, w2):
    a = _bdot(h, w1)
    b = _bdot(h, w3)
    return _bdot((jax.nn.silu(a) * b).astype(BF16), w2)


def _ffn_kernel(xp_ref, xs_ref, g_ref, w1_ref, w3_ref, w2_ref, op_ref, os_ref, *, n_prompt_tiles):
    is_prompt = pl.program_id(0) < n_prompt_tiles
    x = _read_group(xp_ref, xs_ref, is_prompt)
    h = _rms_rows(x, g_ref[...]).astype(BF16)
    _write_group(op_ref, os_ref, is_prompt, x + _swiglu(h, w1_ref[...], w3_ref[...], w2_ref[...]))


def _ffn(xp, xs, g, w1, w3, w2):
    tm = ROW_TILE
    tp, ts = xp.shape[0], xs.shape[0]
    npt = tp // tm
    d_ff = w1.shape[1]
    rows = _group_specs(tm, D_MODEL, npt)
    return pl.pallas_call(
        functools.partial(_ffn_kernel, n_prompt_tiles=npt),
        out_shape=[jax.ShapeDtypeStruct((tp, D_MODEL), F32), jax.ShapeDtypeStruct((ts, D_MODEL), F32)],
        grid=((tp + ts) // tm,),
        in_specs=rows + [_const_spec((1, D_MODEL)), _const_spec((D_MODEL, d_ff)),
                         _const_spec((D_MODEL, d_ff)), _const_spec((d_ff, D_MODEL))],
        out_specs=rows,
        compiler_params=_cparams(("arbitrary",)),
        name="ffn",
    )(xp, xs, g, w1, w3, w2)


def _router_kernel(xp_ref, xs_ref, g_ref, wr_ref, tri_ref, h_ref, meta_ref, gate_ref, cnt_ref, carry,
                   *, n_prompt_tiles):
    @pl.when(pl.program_id(0) == 0)
    def _():
        carry[...] = jnp.zeros(carry.shape, F32)

    x = _read_group(xp_ref, xs_ref, pl.program_id(0) < n_prompt_tiles)
    h = _rms_rows(x, g_ref[...])
    h_ref[...] = _pack_halves(h)
    logits = jnp.dot(h, wr_ref[...], preferred_element_type=F32, precision=lax.Precision.HIGHEST)
    lane = lax.broadcasted_iota(jnp.int32, logits.shape, 1)
    logits = jnp.where(lane < N_EXPERTS, logits, -jnp.inf)
    big = jnp.int32(logits.shape[1])
    m1 = jnp.max(logits, axis=-1, keepdims=True)
    i1 = jnp.min(jnp.where(logits == m1, lane, big), axis=-1, keepdims=True)
    rest = jnp.where(lane == i1, -jnp.inf, logits)
    m2 = jnp.max(rest, axis=-1, keepdims=True)
    i2 = jnp.min(jnp.where(rest == m2, lane, big), axis=-1, keepdims=True)
    e = jnp.exp(m2 - m1)
    g1 = 1.0 / (1.0 + e)
    g2 = e / (1.0 + e)
    o1 = lane == i1
    o2 = lane == i2
    chosen = jnp.where(o1 | o2, 1.0, 0.0)
    base = _bdot(tri_ref[...], chosen.astype(BF16)) + carry[...]
    r1 = jnp.sum(jnp.where(o1, base, 0.0), axis=-1, keepdims=True).astype(jnp.int32)
    r2 = jnp.sum(jnp.where(o2, base, 0.0), axis=-1, keepdims=True).astype(jnp.int32)
    carry[...] = carry[...] + jnp.sum(chosen, axis=0, keepdims=True)
    cnt_ref[...] = carry[...]
    meta_ref[...] = jnp.where(lane == 0, i1, jnp.where(lane == 1, i2,
                              jnp.where(lane == 2, r1, jnp.where(lane == 3, r2, 0))))
    gate_ref[...] = jnp.where(lane == 0, g1, jnp.where(lane == 1, g2, 0.0))


def _router(xp, xs, g, wr_pad):
    tm = ROW_TILE
    t = xp.shape[0] + xs.shape[0]
    npt = xp.shape[0] // tm
    lanes = wr_pad.shape[1]
    tri = jnp.tri(tm, k=-1, dtype=BF16)
    row = lambda w: pl.BlockSpec((tm, w), lambda i: (i, 0))
    return pl.pallas_call(
        functools.partial(_router_kernel, n_prompt_tiles=npt),
        out_shape=[jax.ShapeDtypeStruct((t, D_MODEL // 2), jnp.uint32),
                   jax.ShapeDtypeStruct((t, lanes), jnp.int32),
                   jax.ShapeDtypeStruct((t, lanes), F32),
                   jax.ShapeDtypeStruct((1, lanes), F32)],
        grid=(t // tm,),
        in_specs=_group_specs(tm, D_MODEL, npt) + [_const_spec((1, D_MODEL)), _const_spec((D_MODEL, lanes)),
                                                   _const_spec((tm, tm))],
        out_specs=[row(D_MODEL // 2), row(lanes), row(lanes), _const_spec((1, lanes))],
        scratch_shapes=[pltpu.VMEM((1, lanes), F32)],
        compiler_params=_cparams(("arbitrary",)),
        name="router",
    )(xp, xs, g, wr_pad, tri)


def _sc_mesh():
    return plsc.VectorSubcoreMesh(core_axis_name="c", subcore_axis_name="s")


def _sc_worker_base(per_worker):
    return (lax.axis_index("s") * SC_CORES + lax.axis_index("c")) * per_worker


def _sc_chunk_rows(n, row_bytes):
    assert n % (SC_WORKERS * SUBLANES) == 0
    per_worker = n // SC_WORKERS
    fits = [r for r in range(SUBLANES, per_worker + 1, SUBLANES)
            if per_worker % r == 0 and r * row_bytes <= SC_CHUNK_BYTES]
    return per_worker, fits[-1]


def _sc_scatter_rows(x, idx0, idx1, n_out):
    n, w = x.shape
    per_worker, chunk = _sc_chunk_rows(n, w * x.dtype.itemsize)

    @functools.partial(
        pl.kernel, mesh=_sc_mesh(), out_type=jax.ShapeDtypeStruct((n_out, w), x.dtype),
        scratch_types=[pltpu.VMEM((chunk,), jnp.int32), pltpu.VMEM((chunk,), jnp.int32),
                       pltpu.VMEM((chunk, w), x.dtype), pltpu.SemaphoreType.DMA])
    def scatter(x_hbm, i0_hbm, i1_hbm, out_hbm, i0_v, i1_v, rows_v, sem):
        start = _sc_worker_base(per_worker)

        @pl.loop(0, per_worker // chunk)
        def _(c):
            rows = pl.ds(pl.multiple_of(start + c * chunk, SUBLANES), chunk)
            pltpu.sync_copy(x_hbm.at[rows], rows_v)
            pltpu.sync_copy(i0_hbm.at[rows], i0_v)
            pltpu.sync_copy(i1_hbm.at[rows], i1_v)
            pltpu.async_copy(rows_v, out_hbm.at[i0_v], sem).wait()
            pltpu.async_copy(rows_v, out_hbm.at[i1_v], sem).wait()

    return scatter(x, idx0, idx1)


def _sc_gather_rows(table, idx):
    n, w = idx.shape[0], table.shape[1]
    per_worker, chunk = _sc_chunk_rows(n, w * table.dtype.itemsize)

    @functools.partial(
        pl.kernel, mesh=_sc_mesh(), out_type=jax.ShapeDtypeStruct((n, w), table.dtype),
        scratch_types=[pltpu.VMEM((chunk,), jnp.int32), pltpu.VMEM((chunk, w), table.dtype),
                       pltpu.SemaphoreType.DMA])
    def gather(table_hbm, idx_hbm, out_hbm, idx_v, rows_v, sem):
        start = _sc_worker_base(per_worker)

        @pl.loop(0, per_worker // chunk)
        def _(c):
            rows = pl.ds(pl.multiple_of(start + c * chunk, SUBLANES), chunk)
            pltpu.sync_copy(idx_hbm.at[rows], idx_v)
            pltpu.async_copy(table_hbm.at[idx_v], rows_v, sem).wait()
            pltpu.sync_copy(rows_v, out_hbm.at[rows])

    return gather(table, idx)


def _moe_kernel(be_ref, nv_ref, x_ref, w1_ref, w3_ref, w2_ref, o_ref):
    del be_ref

    @pl.when(pl.program_id(0) < nv_ref[0])
    def _():
        x = _unpack_halves(x_ref[...]).astype(BF16)
        o_ref[...] = _pack_halves(_swiglu(x, w1_ref[...], w3_ref[...], w2_ref[...]))


def _moe_blocks(xs, block_expert, n_valid, w1, w3, w2):
    cap = xs.shape[0]
    tb = MOE_TILE
    d_ff = w1.shape[2]
    grid_spec = pltpu.PrefetchScalarGridSpec(
        num_scalar_prefetch=2,
        grid=(cap // tb,),
        in_specs=[pl.BlockSpec((tb, D_MODEL // 2), lambda i, be, nv: (jnp.minimum(i, nv[0] - 1), 0)),
                  pl.BlockSpec((None, D_MODEL, d_ff), lambda i, be, nv: (be[i], 0, 0)),
                  pl.BlockSpec((None, D_MODEL, d_ff), lambda i, be, nv: (be[i], 0, 0)),
                  pl.BlockSpec((None, d_ff, D_MODEL), lambda i, be, nv: (be[i], 0, 0))],
        out_specs=pl.BlockSpec((tb, D_MODEL // 2), lambda i, be, nv: (i, 0)),
    )
    return pl.pallas_call(
        _moe_kernel,
        out_shape=jax.ShapeDtypeStruct((cap, D_MODEL // 2), jnp.uint32),
        grid_spec=grid_spec,
        compiler_params=_cparams(("arbitrary",)),
        name="moe",
    )(block_expert, n_valid, xs, w1, w3, w2)


def _combine_kernel(xp_ref, xs_ref, a0_ref, a1_ref, gate_ref, op_ref, os_ref, *, n_prompt_tiles):
    is_prompt = pl.program_id(0) < n_prompt_tiles
    gate = gate_ref[...]
    y = (_read_group(xp_ref, xs_ref, is_prompt)
         + gate[:, 0:1] * _unpack_halves(a0_ref[...]) + gate[:, 1:2] * _unpack_halves(a1_ref[...]))
    _write_group(op_ref, os_ref, is_prompt, y)


def _combine(xp, xs, picked, gate):
    tm = ROW_TILE
    tp, ts = xp.shape[0], xs.shape[0]
    nt = (tp + ts) // tm
    rows = _group_specs(tm, D_MODEL, tp // tm)
    return pl.pallas_call(
        functools.partial(_combine_kernel, n_prompt_tiles=tp // tm),
        out_shape=[jax.ShapeDtypeStruct((tp, D_MODEL), F32), jax.ShapeDtypeStruct((ts, D_MODEL), F32)],
        grid=(nt,),
        in_specs=rows + [pl.BlockSpec((tm, D_MODEL // 2), lambda i: (i, 0)),
                         pl.BlockSpec((tm, D_MODEL // 2), lambda i: (i + nt, 0)),
                         pl.BlockSpec((tm, gate.shape[1]), lambda i: (i, 0))],
        out_specs=rows,
        compiler_params=_cparams(("arbitrary",)),
        name="moe_combine",
    )(xp, xs, picked, picked, gate)


def _moe(xp, xs, g, wr, w1, w3, w2):
    t = xp.shape[0] + xs.shape[0]
    tb = MOE_TILE
    lanes = 128
    wr_pad = jnp.pad(wr, ((0, 0), (0, lanes - N_EXPERTS)))
    h, meta, gate, cnt = _router(xp, xs, g, wr_pad)
    counts = cnt[0, :N_EXPERTS].astype(jnp.int32)
    padded = (counts + tb - 1) // tb * tb
    pad_ends = jnp.cumsum(padded)
    pad_starts = pad_ends - padded
    experts = jnp.arange(N_EXPERTS, dtype=jnp.int32)
    slot = lambda e, r: jnp.sum(jnp.where(e[:, None] == experts, pad_starts, 0), axis=1) + r
    dest0 = slot(meta[:, 0], meta[:, 2])
    dest1 = slot(meta[:, 1], meta[:, 3])
    nb = -(-(t * TOP_K + N_EXPERTS * (tb - 1)) // tb)
    block_expert = jnp.minimum(
        jnp.sum(pad_ends[None, :] <= (jnp.arange(nb, dtype=jnp.int32) * tb)[:, None], axis=1),
        N_EXPERTS - 1).astype(jnp.int32)
    n_valid = (pad_ends[-1:] // tb).astype(jnp.int32)
    slots = _sc_scatter_rows(h, dest0, dest1, nb * tb)
    out = _moe_blocks(slots, block_expert, n_valid, w1, w3, w2)
    picked = _sc_gather_rows(out, jnp.concatenate([dest0, dest1]))
    return _combine(xp, xs, picked, gate)


def kernel(x_prompt, x_sample, cache_k, cache_v, page_table, state_ssm_re, state_ssm_im, state_conv,
           norm_mix_g, norm_ffn_g, w_in, ssm_a_re, ssm_a_im, ssm_log_dt, ssm_b_re, ssm_b_im,
           ssm_c_re, ssm_c_im, ssm_d, ssm_w_glu, ssm_b_glu, q_norm_g, k_norm_g,
           lambda_q1, lambda_k1, lambda_q2, lambda_k2, head_norm_g, conv_w,
           w_br_ssm, w_br_att, w_br_conv, w_out, ffn_w1, ffn_w3, ffn_w2,
           router_w, moe_w1, moe_w3, moe_w2):
    n_p, seq, _ = x_prompt.shape
    n_s, dec, _ = x_sample.shape
    depth = w_in.shape[0]
    tp = n_p * seq
    ts = n_s * dec
    assert seq % ROW_TILE == 0 and ts % ROW_TILE == 0 and seq >= CONV_K - 1
    pool, page = cache_k.shape[1], cache_k.shape[2]
    cache_kt = cache_k.reshape(depth, pool, page, QK_WIDTH).transpose(0, 1, 3, 2)
    cache_vr = cache_v.reshape(depth, pool, page * ATT_HEADS, ATT_V_DIM)
    seg = jnp.kron(jnp.eye(QK_WIDTH // ATT_HEAD_DIM, dtype=F32),
                   jnp.full((ATT_HEAD_DIM, ATT_HEAD_DIM), 1.0 / ATT_HEAD_DIM, F32)).astype(BF16)
    n_rep = QK_WIDTH // ATT_HEAD_DIM

    xp = x_prompt.reshape(tp, D_MODEL)
    xs = x_sample.reshape(ts, D_MODEL)
    w_in_bf = w_in.astype(BF16)
    all_tables = _s5_tables(ssm_a_re, ssm_a_im, ssm_log_dt, ssm_b_re, ssm_b_im, ssm_c_re, ssm_c_im, dec)
    kv_prompt = None
    moe_bf = [None] * 3
    srp, sip, cvp = [], [], []
    ks, vs, srs, sis, cvs = [], [], [], [], []
    for l in range(depth):
        lam_init = 0.8 - 0.6 * math.exp(-0.3 * l)
        q, kb, vb, cb, vin, gates, kt, vr, k_s, v_s, u_p, u_s = _inproj(
            xp, xs, norm_mix_g[l][None], w_in_bf,
            jnp.tile(q_norm_g[l], n_rep)[None], jnp.tile(k_norm_g[l], n_rep)[None], seg,
            l, depth, n_p, seq, kv_prompt)
        kv_prompt = (kt, vr)

        yr_p, yr_s, p_re, p_im, s_re, s_im = _s5(u_p, u_s, state_ssm_re[l], state_ssm_im[l], all_tables, l,
                                                  n_p, seq, n_s, dec)

        lp = jnp.stack([lambda_q1[l], lambda_k1[l], lambda_q2[l], lambda_k2[l]])
        hg = head_norm_g[l][None]
        casts = []
        if l + 1 < depth and (l + 1) % 2 == 1:
            casts += [(w, (l + 1) // 2, k) for k, w in ((0, moe_w1), (1, moe_w3))]
        if l % 2 == 1:
            casts += [(moe_w2, l // 2, 2)]
        yb_p, yb_s, *cast_out = _attention(
            q, kb, vb, q[tp:].reshape(n_s, dec, QK_WIDTH), kb[tp:].reshape(n_s, dec, QK_WIDTH),
            vb[tp:].reshape(n_s, dec, ATT_WIDTH), cache_kt, cache_vr, l, page_table, lp, hg,
            n_p, seq, lam_init, tuple(w[e].reshape(-1, w.shape[-1]) for w, e, _ in casts))
        yb_s = yb_s.reshape(ts, ATT_WIDTH)
        for c, (w, _, k) in zip(cast_out, casts):
            moe_bf[k] = c.reshape(w.shape[1:])

        ext_s = jnp.concatenate([state_conv[l], vin[tp:].reshape(n_s, dec, CONV_WIDTH)], axis=1)
        vm1_s = ext_s[:, 1:1 + dec].reshape(ts, CONV_WIDTH)
        vm2_s = ext_s[:, 0:dec].reshape(ts, CONV_WIDTH)
        xp, xs = _merge(xp, xs, u_p, u_s, cb, vin, gates, yr_p, yr_s, yb_p, yb_s, vm1_s, vm2_s,
                        ssm_d[l][None], ssm_w_glu[l].astype(BF16), ssm_b_glu[l][None], conv_w[l],
                        w_br_ssm[l].astype(BF16), w_br_att[l].astype(BF16), w_br_conv[l].astype(BF16),
                        w_out[l].astype(BF16), seq // ROW_TILE)

        i = l // 2
        if l % 2 == 0:
            xp, xs = _ffn(xp, xs, norm_ffn_g[l][None], ffn_w1[i].astype(BF16), ffn_w3[i].astype(BF16),
                          ffn_w2[i].astype(BF16))
        else:
            xp, xs = _moe(xp, xs, norm_ffn_g[l][None], router_w[i], *moe_bf)

        srp.append(p_re); sip.append(p_im)
        cvp.append(jnp.stack([vin[(b + 1) * seq - (CONV_K - 1):(b + 1) * seq] for b in range(n_p)]))
        ks.append(k_s.reshape(n_s, dec, ATT_HEADS, 2, ATT_HEAD_DIM))
        vs.append(v_s.reshape(n_s, dec, ATT_HEADS, ATT_V_DIM))
        srs.append(s_re); sis.append(s_im); cvs.append(ext_s[:, dec:])

    kt, vr = kv_prompt
    k_prompt = kt.reshape(depth, n_p, ATT_HEADS, 2, ATT_HEAD_DIM, seq).transpose(0, 1, 5, 2, 3, 4)
    v_prompt = vr.reshape(depth, n_p, seq, ATT_HEADS, ATT_V_DIM)
    return (xp.reshape(n_p, seq, D_MODEL), xs.reshape(n_s, dec, D_MODEL),
            k_prompt, v_prompt, jnp.stack(srp), jnp.stack(sip), jnp.stack(cvp),
            jnp.stack(ks), jnp.stack(vs), jnp.stack(srs), jnp.stack(sis), jnp.stack(cvs))
```

```python
import functools
import math

import jax
import jax.numpy as jnp
from jax import lax
from jax.experimental import pallas as pl
from jax.experimental.pallas import tpu as pltpu
from jax.experimental.pallas import tpu_sc as plsc

F32 = jnp.float32
BF16 = jnp.bfloat16

D_MODEL = 1024
SSM_WIDTH = 256
SSM_GROUP = 16
SSM_GROUPS = SSM_WIDTH // SSM_GROUP
SSM_STATE = 64
ATT_HEADS = 4
ATT_HEAD_DIM = 64
ATT_V_DIM = 2 * ATT_HEAD_DIM
QK_WIDTH = ATT_HEADS * 2 * ATT_HEAD_DIM
ATT_WIDTH = ATT_HEADS * ATT_V_DIM
CONV_WIDTH = 256
CONV_K = 3
N_BRANCH = 3
N_EXPERTS = 8
TOP_K = 2
EPS = 1e-6
NEG_INF = -1e30
LOG2_E = 1.4426950408889634

C_U = 0
C_Q = C_U + SSM_WIDTH
C_K = C_Q + QK_WIDTH
C_V = C_K + QK_WIDTH
C_CB = C_V + ATT_WIDTH
C_CC = C_CB + CONV_WIDTH
C_CH = C_CC + CONV_WIDTH
C_G = C_CH + CONV_WIDTH
IN_COLS = C_G + N_BRANCH * D_MODEL

SSM_CHUNK = 8
SUBLANES = 8
BF16_SUBLANES = 16
ROW_TILE = 512
ATT_TILE = 512
MOE_TILE = 256
SC_CORES = 2
SC_WORKERS = SC_CORES * 16
SC_CHUNK_BYTES = 192 * 1024
VMEM_LIMIT = 56 * 1024 * 1024


def _cparams(sem):
    return pltpu.CompilerParams(dimension_semantics=sem, vmem_limit_bytes=VMEM_LIMIT)


def _const_spec(shape):
    nd = len(shape)
    return pl.BlockSpec(shape, lambda *_: (0,) * nd)


def _bdot(a, b):
    return jnp.dot(a, b, preferred_element_type=F32)


def _rms_rows(x, g):
    ms = jnp.mean(x * x, axis=-1, keepdims=True)
    return x * lax.rsqrt(ms + EPS) * g


def _group_specs(tm, w, n_prompt_tiles):
    return [pl.BlockSpec((tm, w), lambda i: (jnp.minimum(i, n_prompt_tiles - 1), 0)),
            pl.BlockSpec((tm, w), lambda i: (jnp.maximum(i - n_prompt_tiles, 0), 0))]


def _read_group(p_ref, s_ref, is_prompt):
    return jnp.where(is_prompt, p_ref[...], s_ref[...])


def _write_group(p_ref, s_ref, is_prompt, val):
    @pl.when(is_prompt)
    def _():
        p_ref[...] = val

    @pl.when(jnp.logical_not(is_prompt))
    def _():
        s_ref[...] = val


def _pack_halves(x):
    half = x.shape[1] // 2
    bits = lambda t: lax.bitcast_convert_type(t.astype(BF16).astype(F32), jnp.uint32)
    return (bits(x[:, :half]) >> 16) | (bits(x[:, half:]) & jnp.uint32(0xFFFF0000))


def _unpack_halves(p):
    lo = lax.bitcast_convert_type(p << 16, F32)
    hi = lax.bitcast_convert_type(p & jnp.uint32(0xFFFF0000), F32)
    return jnp.concatenate([lo, hi], axis=1)


def _segment_rms(z, g, seg):
    ms = _bdot((z * z).astype(BF16), seg)
    return z * lax.rsqrt(ms + EPS) * g


def _inproj_kernel(xp_ref, xs_ref, g_ref, w_ref, qg_ref, kg_ref, seg_ref, *rest,
                   n_prompt_tiles, n_prev, layer):
    (q_ref, kb_ref, vb_ref, cb_ref, vin_ref, gate_ref,
     kt_ref, vr_ref, ks_ref, vs_ref, up_ref, us_ref) = rest[n_prev:]
    tm = q_ref.shape[0]
    is_prompt = pl.program_id(0) < n_prompt_tiles
    h = _rms_rows(_read_group(xp_ref, xs_ref, is_prompt), g_ref[...]).astype(BF16)

    def proj(a, b):
        return _bdot(h, w_ref[:, a:b])

    seg = seg_ref[...]
    _write_group(up_ref, us_ref, is_prompt, proj(C_U, C_Q))
    qn = _segment_rms(proj(C_Q, C_K), qg_ref[...], seg)
    q_ref[...] = (qn * (ATT_HEAD_DIM ** -0.5 * LOG2_E)).astype(BF16)
    kn = _segment_rms(proj(C_K, C_V), kg_ref[...], seg)
    kb_ref[...] = kn.astype(BF16)
    v = proj(C_V, C_CB)
    vb_ref[...] = v.astype(BF16)
    cb_ref[...] = proj(C_CB, C_CC)
    vin_ref[...] = proj(C_CC, C_CH) * proj(C_CH, C_G)
    for j in range(N_BRANCH):
        a = C_G + j * D_MODEL
        gate_ref[:, j * D_MODEL:(j + 1) * D_MODEL] = jax.nn.sigmoid(proj(a, a + D_MODEL)).astype(gate_ref.dtype)

    @pl.when(is_prompt)
    def _():
        if n_prev:
            kt_l, vr_l = kt_ref, vr_ref
        else:
            kt_l, vr_l = kt_ref.at[layer], vr_ref.at[layer]
            for other in range(kt_ref.shape[0]):
                if other != layer:
                    kt_ref[other] = jnp.zeros(kt_ref.shape[1:], F32)
                    vr_ref[other] = jnp.zeros(vr_ref.shape[1:], F32)
        kt_l[...] = kn.T
        for hd in range(ATT_HEADS):
            vr_l[pl.ds(hd, tm, stride=ATT_HEADS), :] = v[:, hd * ATT_V_DIM:(hd + 1) * ATT_V_DIM]

    @pl.when(jnp.logical_not(is_prompt))
    def _():
        ks_ref[...] = kn
        vs_ref[...] = v


def _inproj(xp, xs, g, w_bf, qg, kg, seg, layer, depth, n_p, seq, prev):
    tm = ROW_TILE
    tp, ts = xp.shape[0], xs.shape[0]
    t = tp + ts
    npt = tp // tm
    tps = seq // tm
    row = lambda w: pl.BlockSpec((tm, w), lambda i: (i, 0))
    pc = lambda i: jnp.minimum(i, npt - 1)
    outs = [(QK_WIDTH, BF16), (QK_WIDTH, BF16), (ATT_WIDTH, BF16),
            (CONV_WIDTH, F32), (CONV_WIDTH, F32), (N_BRANCH * D_MODEL, BF16)]
    out_shape = ([jax.ShapeDtypeStruct((t, w), d) for w, d in outs]
                 + [jax.ShapeDtypeStruct((depth, n_p, QK_WIDTH, seq), F32),
                    jax.ShapeDtypeStruct((depth, tp * ATT_HEADS, ATT_V_DIM), F32),
                    jax.ShapeDtypeStruct((ts, QK_WIDTH), F32), jax.ShapeDtypeStruct((ts, ATT_WIDTH), F32),
                    jax.ShapeDtypeStruct((tp, SSM_WIDTH), F32), jax.ShapeDtypeStruct((ts, SSM_WIDTH), F32)])
    srow = _group_specs(tm, QK_WIDTH, npt)[1]
    prev = () if prev is None else tuple(prev)
    lead, at = (None, layer) if prev else (depth, 0)
    out_specs = ([row(w) for w, _ in outs]
                 + [pl.BlockSpec((lead, None, QK_WIDTH, tm), lambda i: (at, pc(i) // tps, 0, pc(i) % tps)),
                    pl.BlockSpec((lead, tm * ATT_HEADS, ATT_V_DIM), lambda i: (at, pc(i), 0)),
                    srow, srow] + _group_specs(tm, SSM_WIDTH, npt))
    n_in = 7
    return pl.pallas_call(
        functools.partial(_inproj_kernel, n_prompt_tiles=npt, n_prev=len(prev), layer=layer),
        out_shape=out_shape,
        grid=(t // tm,),
        in_specs=(_group_specs(tm, D_MODEL, npt)
                  + [_const_spec((1, D_MODEL)),
                     pl.BlockSpec((None, D_MODEL, IN_COLS), lambda i: (layer, 0, 0)),
                     _const_spec((1, QK_WIDTH)), _const_spec((1, QK_WIDTH)),
                     _const_spec((QK_WIDTH, QK_WIDTH))]
                  + [pl.BlockSpec(memory_space=pl.ANY)] * len(prev)),
        out_specs=out_specs,
        input_output_aliases={n_in + j: len(outs) + j for j in range(len(prev))},
        compiler_params=_cparams(("arbitrary",)),
        name="inproj",
    )(xp, xs, g, w_bf, qg, kg, seg, *prev)


def _s5_intra(u, kb_ref, tc):
    rowmod = lax.broadcasted_iota(jnp.int32, u.shape, 0) % tc
    y = _bdot(u.astype(BF16), kb_ref[0])
    for m in range(1, tc):
        um = jnp.where(rowmod >= m, pltpu.roll(u, m, axis=0), 0.0)
        y = y + _bdot(um.astype(BF16), kb_ref[m])
    return y


def _s5_inject(u_refs, pb_ref, k0, n_chunks, tc):
    w = None
    for k in range(tc):
        rows = pl.ds(k, n_chunks, stride=tc)
        uk = jnp.concatenate([r[rows, :] for r in u_refs], axis=1).astype(BF16)
        d = _bdot(uk, pb_ref[k0 + k])
        w = d if w is None else w + d
    return w


def _s5_readout(y_ref, y_scrs, y, s_in, qb_ref, n_chunks, tc):
    lanes = y_scrs[0].shape[1]
    for h, scr in enumerate(y_scrs):
        scr[...] = y[:, h * lanes:(h + 1) * lanes]
    sb = s_in.astype(BF16)
    for k in range(tc):
        rows = pl.ds(k, n_chunks, stride=tc)
        yk = _bdot(sb, qb_ref[k])
        for h, scr in enumerate(y_scrs):
            scr[rows, :] = scr[rows, :] + yk[:, h * lanes:(h + 1) * lanes]
    y_ref[...] = jnp.concatenate([scr[...] for scr in y_scrs], axis=1)


def _cmul_add(a_re, a_im, s, w, half):
    s_re, s_im = s[:, :half], s[:, half:]
    return jnp.concatenate([a_re * s_re - a_im * s_im, a_re * s_im + a_im * s_re], axis=1) + w


def _s5_prompt_kernel(ua_ref, ub_ref, kb_ref, pb_ref, qb_ref, a_ref, y_ref, fs_ref,
                      carry, w_fold, s_fold, ya_scr, yb_scr, *, tc):
    n_b, tm, hw = ua_ref.shape
    n_chunks = tm // tc
    rows = n_b * n_chunks
    fold = carry.shape[0] // n_b
    hf = fold // 2

    @pl.when(pl.program_id(0) == 0)
    def _():
        carry[...] = jnp.zeros(carry.shape, F32)

    u = jnp.concatenate([ua_ref[...].reshape(n_b * tm, hw), ub_ref[...].reshape(n_b * tm, hw)], axis=1)
    y = _s5_intra(u, kb_ref, tc)
    w = None
    for k in range(tc):
        at_k = pl.ds(k, n_chunks, stride=tc)
        uk = jnp.concatenate([jnp.concatenate([ua_ref[b, at_k, :], ub_ref[b, at_k, :]], axis=1)
                              for b in range(n_b)], axis=0).astype(BF16)
        d = _bdot(uk, pb_ref[k])
        w = d if w is None else w + d
    for r in range(fold):
        w_fold[pl.ds(r, rows, stride=fold), :] = w[:, r * hw:(r + 1) * hw]
    a_re = a_ref[0]
    a_im = a_ref[1]
    for b in range(n_b):
        s = carry[b * fold:(b + 1) * fold, :]
        for c in range(n_chunks):
            at = slice((b * n_chunks + c) * fold, (b * n_chunks + c + 1) * fold)
            s_fold[at, :] = s
            s_re, s_im = s[:hf], s[hf:]
            s = jnp.concatenate([a_re * s_re - a_im * s_im, a_re * s_im + a_im * s_re], axis=0) + w_fold[at, :]
        carry[b * fold:(b + 1) * fold, :] = s
    fs_ref[...] = carry[...]
    s_in = jnp.concatenate([s_fold[pl.ds(r, rows, stride=fold), :] for r in range(fold)], axis=1)
    for h, scr in enumerate((ya_scr, yb_scr)):
        scr[...] = y[:, h * hw:(h + 1) * hw]
    sb = s_in.astype(BF16)
    for k in range(tc):
        at_k = pl.ds(k, rows, stride=tc)
        yk = _bdot(sb, qb_ref[k])
        for h, scr in enumerate((ya_scr, yb_scr)):
            scr[at_k, :] = scr[at_k, :] + yk[:, h * hw:(h + 1) * hw]
    for b in range(n_b):
        y_ref[b] = jnp.concatenate([ya_scr[b * tm:(b + 1) * tm, :], yb_scr[b * tm:(b + 1) * tm, :]], axis=1)


def _s5_sample_kernel(ua_ref, ub_ref, s0_ref, kb_ref, pb_ref, qb_ref, a_ref, y_ref, fs_ref,
                      ya_scr, yb_scr, *, tc, k0):
    n_chunks = ua_ref.shape[0] // tc
    half = a_ref.shape[1]
    y = _s5_intra(jnp.concatenate([ua_ref[...], ub_ref[...]], axis=1), kb_ref, tc)
    w = _s5_inject((ua_ref, ub_ref), pb_ref, k0, n_chunks, tc)
    s0 = s0_ref[...]
    fs_ref[...] = _cmul_add(a_ref[2:3, :], a_ref[3:4, :], s0, w, half)
    _s5_readout(y_ref, (ya_scr, yb_scr), y, s0, qb_ref, n_chunks, tc)


def _s5_tables_kernel(are_ref, aim_ref, ldt_ref, bre_ref, bim_ref, cre_ref, cim_ref,
                      kb_ref, pb_ref, qb_ref, a_ref, *, tc, dec):
    are = are_ref[...]
    aim = aim_ref[...]
    dt = jnp.exp(ldt_ref[...])

    def power(m):
        mag = jnp.exp(are * dt * m)
        ang = aim * dt * m
        return mag * jnp.cos(ang), mag * jnp.sin(ang)

    ab_re, ab_im = power(1.0)
    den = are * are + aim * aim
    nr = ab_re - 1.0
    cr = (nr * are + ab_im * aim) / den
    ci = (ab_im * are - nr * aim) / den
    gj, gp = bre_ref.shape
    row_g = lax.broadcasted_iota(jnp.int32, (gj, gp), 0) // SSM_GROUP
    col_g = lax.broadcasted_iota(jnp.int32, (gj, gp), 1) // SSM_STATE
    diag = row_g == col_g
    bre = bre_ref[...]
    bim = bim_ref[...]
    bb_re = jnp.where(diag, cr * bre - ci * bim, 0.0)
    bb_im = jnp.where(diag, cr * bim + ci * bre, 0.0)
    cc_re = jnp.where(diag, cre_ref[...], 0.0)
    cc_im = jnp.where(diag, cim_ref[...], 0.0)
    c_blk = jnp.concatenate([cc_re, -cc_im], axis=1).T
    for m in range(tc):
        pr, pi = power(float(m))
        pm = jnp.concatenate([pr * bb_re - pi * bb_im, pr * bb_im + pi * bb_re], axis=1)
        pb_ref[tc - 1 - m] = pm.astype(pb_ref.dtype)
        kb_ref[m] = jnp.dot(pm, c_blk, preferred_element_type=F32,
                            precision=lax.Precision.HIGHEST).astype(kb_ref.dtype)
        qr, qi = power(float(m + 1))
        qm = jnp.concatenate([qr * cc_re - qi * cc_im, -(qr * cc_im + qi * cc_re)], axis=1)
        qb_ref[m] = qm.T.astype(qb_ref.dtype)
    a_ref[...] = jnp.concatenate(list(power(float(tc)) + power(float(dec))), axis=0)


def _s5_tables(a_re, a_im, log_dt, b_re, b_im, c_re, c_im, dec):
    depth = a_re.shape[0]
    tc = SSM_CHUNK
    g, p, j = SSM_GROUPS, SSM_STATE, SSM_GROUP
    gp, gj = g * p, g * j
    rowv = lambda t: t.reshape(depth, 1, gp)
    ldt = jnp.repeat(log_dt, p, axis=1).reshape(depth, 1, gp)
    bt = lambda t: jnp.tile(t.transpose(0, 1, 3, 2).reshape(depth, gj, p), (1, 1, g))
    ct = lambda t: jnp.tile(t.reshape(depth, gj, p), (1, 1, g))
    lay = lambda *shape: pl.BlockSpec((None,) + shape, lambda l: (l,) + (0,) * len(shape))
    return pl.pallas_call(
        functools.partial(_s5_tables_kernel, tc=tc, dec=dec),
        out_shape=[jax.ShapeDtypeStruct((depth, tc, gj, gj), BF16),
                   jax.ShapeDtypeStruct((depth, tc, gj, 2 * gp), BF16),
                   jax.ShapeDtypeStruct((depth, tc, 2 * gp, gj), BF16),
                   jax.ShapeDtypeStruct((depth, 4, gp), F32)],
        grid=(depth,),
        in_specs=[lay(1, gp)] * 3 + [lay(gj, gp)] * 4,
        out_specs=[lay(tc, gj, gj), lay(tc, gj, 2 * gp), lay(tc, 2 * gp, gj), lay(4, gp)],
        compiler_params=_cparams(("arbitrary",)),
        name="s5_tables",
    )(rowv(a_re), rowv(a_im), ldt, bt(b_re), bt(b_im), ct(c_re), ct(c_im))


def _s5(up, us, s0_re, s0_im, tables, layer, n_p, seq, n_s, dec):
    kb, pb, qb, adec = tables
    lay = lambda t: pl.BlockSpec((None,) + t.shape[1:], lambda i: (layer,) + (0,) * (t.ndim - 1))
    tc = SSM_CHUNK
    g, p = SSM_GROUPS, SSM_STATE
    sw = 2 * g * p
    tm = ROW_TILE
    tp = n_p * seq
    ts = n_s * dec
    hw = SSM_WIDTH // 2
    fold = sw // hw
    assert dec <= tc and seq % tm == 0 and tm % tc == 0
    up3 = up.reshape(n_p, seq, SSM_WIDTH)
    chunk_rows = n_p * (tm // tc)
    yp, fsp = pl.pallas_call(
        functools.partial(_s5_prompt_kernel, tc=tc),
        out_shape=[jax.ShapeDtypeStruct((n_p, seq, SSM_WIDTH), F32),
                   jax.ShapeDtypeStruct((n_p * fold, hw), F32)],
        grid=(seq // tm,),
        in_specs=[pl.BlockSpec((n_p, tm, hw), lambda i: (0, i, 0)),
                  pl.BlockSpec((n_p, tm, hw), lambda i: (0, i, 1)),
                  lay(kb), lay(pb), lay(qb),
                  pl.BlockSpec((None, 4, fold // 2, hw), lambda i: (layer, 0, 0, 0))],
        out_specs=[pl.BlockSpec((n_p, tm, SSM_WIDTH), lambda i: (0, i, 0)),
                   _const_spec((n_p * fold, hw))],
        scratch_shapes=[pltpu.VMEM((n_p * fold, hw), F32),
                        pltpu.VMEM((chunk_rows * fold, hw), F32), pltpu.VMEM((chunk_rows * fold, hw), F32),
                        pltpu.VMEM((n_p * tm, hw), F32), pltpu.VMEM((n_p * tm, hw), F32)],
        compiler_params=_cparams(("arbitrary",)),
        name="s5_prompt",
    )(up3, up3, kb, pb, qb, adec.reshape(adec.shape[0], 4, fold // 2, hw))
    yp = yp.reshape(tp, SSM_WIDTH)
    s0 = jnp.concatenate([s0_re.reshape(n_s, g * p), s0_im.reshape(n_s, g * p)], axis=1)
    ys, fss = pl.pallas_call(
        functools.partial(_s5_sample_kernel, tc=dec, k0=tc - dec),
        out_shape=[jax.ShapeDtypeStruct((ts, SSM_WIDTH), F32), jax.ShapeDtypeStruct((n_s, sw), F32)],
        grid=(1,),
        in_specs=[pl.BlockSpec((ts, hw), lambda i: (0, 0)), pl.BlockSpec((ts, hw), lambda i: (0, 1)),
                  _const_spec((n_s, sw)),
                  lay(kb), lay(pb), lay(qb), lay(adec)],
        out_specs=[_const_spec((ts, SSM_WIDTH)), _const_spec((n_s, sw))],
        scratch_shapes=[pltpu.VMEM((ts, hw), F32), pltpu.VMEM((ts, hw), F32)],
        compiler_params=_cparams(("arbitrary",)),
        name="s5_sample",
    )(us, us, s0, kb, pb, qb, adec)
    half = g * p
    fsp = fsp.reshape(n_p, sw)
    st = lambda a, n: a.reshape(n, g, p)
    return (yp, ys, st(fsp[:, :half], n_p), st(fsp[:, half:], n_p),
            st(fss[:, :half], n_s), st(fss[:, half:], n_s))


def _lambda(lp_ref, lam_init):
    lp = lp_ref[...]
    s1 = jnp.sum(lp[0:1, :] * lp[1:2, :], axis=-1, keepdims=True)
    s2 = jnp.sum(lp[2:3, :] * lp[3:4, :], axis=-1, keepdims=True)
    return jnp.exp(s1) - jnp.exp(s2) + lam_init


def _attn_kernel(qi_ref, ki_ref, pg_ref, q_ref, k_ref, v_ref, lp_ref, hg_ref, *rest,
                 tile, lam_init, cast_blocks, n_seq, n_pages, layer):
    n_cast = len(cast_blocks)
    n_pg = n_seq * n_pages
    cast_in = rest[:n_cast]
    sq_ref, skn_ref, svn_ref, ckt_hbm, cvr_hbm = rest[n_cast:n_cast + 5]
    o_ref, so_ref = rest[n_cast + 5:n_cast + 7]
    cast_out = rest[n_cast + 7:2 * n_cast + 7]
    m_scr, acc_scr, kbuf, vbuf, sem = rest[2 * n_cast + 7:]
    dec = sq_ref.shape[1]
    t = pl.program_id(1)
    qi = qi_ref[t]
    ki = ki_ref[t]
    vd = ATT_V_DIM
    step = pl.program_id(0) * pl.num_programs(1) + t
    last_step = pl.num_programs(0) * pl.num_programs(1) - 1
    slot = step % 2

    def page_copies(s, to_slot, c):
        pg = pg_ref[s * n_pg + c]
        return (pltpu.make_async_copy(ckt_hbm.at[layer, pg], kbuf.at[to_slot, c], sem.at[to_slot]),
                pltpu.make_async_copy(cvr_hbm.at[layer, pg], vbuf.at[to_slot, c], sem.at[to_slot]))

    @pl.when(step == 0)
    def _():
        for c in range(n_pg):
            for cp in page_copies(0, 0, c):
                cp.start()

    for c in range(n_pg):
        for cp in page_copies(step, slot, c):
            cp.wait()
    for c in range(n_pg):
        for cp in page_copies(step + 1, 1 - slot, c):
            cp.start()

    for src, dst, n_blocks in zip(cast_in, cast_out, cast_blocks):
        @pl.when(step < n_blocks)
        def _(src=src, dst=dst):
            dst[...] = src[...].astype(dst.dtype)

    @pl.when(ki == 0)
    def _():
        m_scr[...] = jnp.full(m_scr.shape, NEG_INF, F32)
        acc_scr[...] = jnp.zeros(acc_scr.shape, F32)

    def sample_group():
        outs = [[] for _ in range(n_seq)]
        stages = [_attend_sample(sq_ref[u], skn_ref[u], svn_ref[u],
                                 [kbuf.at[slot, u * n_pages + j] for j in range(n_pages)],
                                 [vbuf.at[slot, u * n_pages + j] for j in range(n_pages)],
                                 lp_ref, hg_ref, dec, lam_init, outs[u]) for u in range(n_seq)]

        def advance(finish=False):
            for u, st in enumerate(stages):
                if finish:
                    for _ in st:
                        pass
                    so_ref[u] = outs[u][0].astype(so_ref.dtype)
                else:
                    next(st, None)
        return advance

    def accumulate(r0, nr, nk, masked, advance=None):
        rows = slice(r0, r0 + nr)
        lane = lax.broadcasted_iota(jnp.int32, (nr, vd), 1)
        ones = jnp.ones((nk, vd), BF16)
        if masked:
            mask = (lax.broadcasted_iota(jnp.int32, (nr, nk), 1)
                    <= r0 + lax.broadcasted_iota(jnp.int32, (nr, nk), 0))
        nt = (((1,), (1,)), ((), ()))
        for h in range(ATT_HEADS):
            if advance is not None:
                advance()
            cols = slice(h * vd, (h + 1) * vd)
            q = q_ref[rows, cols]
            k = k_ref[0:nk, cols]
            v1 = jnp.concatenate([v_ref[0:nk, cols], ones], axis=1)
            zero = jnp.zeros_like(q)
            for c in range(2):
                qm = jnp.where((lane >= ATT_HEAD_DIM) == bool(c), q, zero)
                s = lax.dot_general(qm, k, nt, preferred_element_type=F32)
                if masked:
                    s = jnp.where(mask, s, NEG_INF)
                idx = 2 * h + c
                m_old = m_scr[idx, rows, :]
                m_row = jnp.max(s, axis=-1, keepdims=True)
                m_new = jnp.maximum(m_old, jnp.broadcast_to(m_row, m_old.shape))
                alpha = jnp.exp2(m_old - m_new)
                p = jnp.exp2(s - jnp.concatenate([m_new] * (nk // vd), axis=1)).astype(BF16)
                acc_scr[idx, rows, :] = (jnp.concatenate([alpha, alpha], axis=1) * acc_scr[idx, rows, :]
                                         + _bdot(p, v1))
                m_scr[idx, rows, :] = m_new

    @pl.when(ki < qi)
    def _():
        advance = sample_group()
        accumulate(0, tile, tile, False, advance)
        advance(finish=True)

    @pl.when(ki == qi)
    def _():
        advance = sample_group()
        accumulate(0, tile // 2, tile // 2, True, advance)
        accumulate(tile // 2, tile // 2, tile, True)
        advance(finish=True)
        lam = _lambda(lp_ref, lam_init)
        hg = hg_ref[...]
        for h in range(ATT_HEADS):
            a1 = acc_scr[2 * h]
            a2 = acc_scr[2 * h + 1]
            o = a1[:, :vd] / a1[:, vd:] - lam * (a2[:, :vd] / a2[:, vd:])
            o_ref[:, h * vd:(h + 1) * vd] = (_rms_rows(o, hg) * (1.0 - lam_init)).astype(o_ref.dtype)

    @pl.when(step == last_step)
    def _():
        for c in range(n_pg):
            for cp in page_copies(step + 1, 1 - slot, c):
                cp.wait()


def _cast_block_rows(rows, n_steps):
    for br in range(BF16_SUBLANES, rows + 1, BF16_SUBLANES):
        if rows % br == 0 and rows // br <= n_steps:
            return br
    raise ValueError((rows, n_steps))


def _attention(q, k, v, sq, skn, svn, cache_kt, cache_vr, layer, page_table, lp, hg, n_p, seq, lam_init,
               casts=()):
    tile = min(ATT_TILE, seq)
    assert seq % tile == 0 and tile % ATT_V_DIM == 0
    nq = seq // tile
    pairs = [(i, j) for i in range(nq) for j in range(i + 1)]
    n_pairs = len(pairs)
    n_steps = n_p * n_pairs
    n_s, dec, _ = sq.shape
    n_pages = page_table.shape[1]
    page = cache_kt.shape[3]
    n_seq = next(d for d in range(1, n_s + 1) if n_s % d == 0 and n_s // d <= n_steps)
    n_groups = n_s // n_seq
    qi_tab = jnp.asarray([a for a, _ in pairs], jnp.int32)
    ki_tab = jnp.asarray([b for _, b in pairs], jnp.int32)
    step = lambda b, t: b * n_pairs + t
    qspec = pl.BlockSpec((tile, ATT_WIDTH), lambda b, t, qi, ki, pt: (b * nq + qi[t], 0))
    kspec = pl.BlockSpec((tile, ATT_WIDTH), lambda b, t, qi, ki, pt: (b * nq + ki[t], 0))
    tok = pl.BlockSpec((n_seq, dec, QK_WIDTH), lambda b, t, qi, ki, pt: (jnp.minimum(step(b, t), n_groups - 1), 0, 0))
    groups = jnp.minimum(jnp.arange(n_steps + 1), n_groups - 1)
    step_pages = page_table.reshape(n_groups, n_seq * n_pages)[groups].reshape(-1)
    n_pg = n_seq * n_pages
    cast_specs, cast_blocks = [], []
    for w in casts:
        br = _cast_block_rows(w.shape[0], n_steps)
        nblk = w.shape[0] // br
        cast_blocks.append(nblk)
        cast_specs.append(pl.BlockSpec(
            (br, w.shape[1]),
            lambda b, t, qi, ki, pt, nblk=nblk: (jnp.minimum(step(b, t), nblk - 1), 0)))
    grid_spec = pltpu.PrefetchScalarGridSpec(
        num_scalar_prefetch=3,
        grid=(n_p, n_pairs),
        in_specs=[qspec, kspec, kspec,
                  pl.BlockSpec((4, ATT_HEAD_DIM), lambda b, t, qi, ki, pt: (0, 0)),
                  pl.BlockSpec((1, ATT_V_DIM), lambda b, t, qi, ki, pt: (0, 0))]
        + cast_specs + [tok, tok, tok, pl.BlockSpec(memory_space=pl.ANY), pl.BlockSpec(memory_space=pl.ANY)],
        out_specs=[qspec, tok] + cast_specs,
        scratch_shapes=[pltpu.VMEM((2 * ATT_HEADS, tile, ATT_V_DIM), F32),
                        pltpu.VMEM((2 * ATT_HEADS, tile, 2 * ATT_V_DIM), F32),
                        pltpu.VMEM((2, n_pg, QK_WIDTH, page), F32),
                        pltpu.VMEM((2, n_pg, page * ATT_HEADS, ATT_V_DIM), F32),
                        pltpu.SemaphoreType.DMA((2,))],
    )
    return pl.pallas_call(
        functools.partial(_attn_kernel, tile=tile, lam_init=lam_init, cast_blocks=tuple(cast_blocks),
                          n_seq=n_seq, n_pages=n_pages, layer=layer),
        out_shape=[jax.ShapeDtypeStruct((n_p * seq, ATT_WIDTH), BF16),
                   jax.ShapeDtypeStruct((n_s, dec, ATT_WIDTH), BF16)]
        + [jax.ShapeDtypeStruct(w.shape, BF16) for w in casts],
        grid_spec=grid_spec,
        compiler_params=_cparams(("arbitrary", "arbitrary")),
        name="attention",
    )(qi_tab, ki_tab, step_pages, q, k, v, lp, hg, *casts, sq, skn, svn, cache_kt, cache_vr)


def _attend_sample(q, kn, vn, kp, vp, lp_ref, hg_ref, dec, lam_init, out):
    n_pages = len(kp)
    page = kp[0].shape[1]
    grp = 2 * dec
    n_rows = ATT_HEADS * grp
    q = q.astype(F32)
    qt = jnp.concatenate([q] * (ATT_HEADS * 2), axis=0)
    r = lax.broadcasted_iota(jnp.int32, (n_rows, QK_WIDTH), 0)
    c = lax.broadcasted_iota(jnp.int32, (n_rows, QK_WIDTH), 1)
    qb = jnp.where(r // dec == c // ATT_HEAD_DIM, qt, 0.0).astype(BF16)
    nt = (((1,), (1,)), ((), ()))
    s_parts = []
    for lo, hi in ((0, n_pages // 2), (n_pages // 2, n_pages)):
        kt = jnp.concatenate([kp[j][...].astype(BF16) for j in range(lo, hi)], axis=1)
        s_parts.append(_bdot(qb, kt))
        yield
    s_past = jnp.concatenate(s_parts, axis=1)
    s_new = lax.dot_general(qb, kn, nt, preferred_element_type=F32)
    rn = lax.broadcasted_iota(jnp.int32, (n_rows, dec), 0) % dec
    cn = lax.broadcasted_iota(jnp.int32, (n_rows, dec), 1)
    s_new = jnp.where(cn <= rn, s_new, NEG_INF)
    m = jnp.maximum(jnp.max(s_new, axis=-1, keepdims=True), jnp.max(s_past, axis=-1, keepdims=True))
    p_new = jnp.exp2(s_new - m)
    p_past = jnp.exp2(s_past - m)
    l = jnp.sum(p_new, axis=-1, keepdims=True) + jnp.sum(p_past, axis=-1, keepdims=True)
    acc_new = _bdot(p_new.astype(BF16), vn)
    lam = _lambda(lp_ref, lam_init)
    hg = hg_ref[...]
    outs = []
    for h in range(ATT_HEADS):
        rows = slice(h * grp, (h + 1) * grp)
        cols = slice(h * ATT_V_DIM, (h + 1) * ATT_V_DIM)
        v_past = jnp.concatenate([vp[j][pl.ds(h, page, stride=ATT_HEADS), :].astype(BF16)
                                  for j in range(n_pages)], axis=0)
        acc = acc_new[rows, cols] + _bdot(p_past[rows, :].astype(BF16), v_past)
        acc = acc / l[rows, :]
        o = acc[:dec] - lam * acc[dec:]
        outs.append(_rms_rows(o, hg) * (1.0 - lam_init))
        if h % 2 == 1 and h + 1 < ATT_HEADS:
            yield
    out.append(jnp.concatenate(outs, axis=1))


def _merge_kernel(xp_ref, xs_ref, up_ref, us_ref, cb_ref, vin_ref, halo_ref, gate_ref,
                  yrp_ref, yrs_ref, ybp_ref, ybs_ref, vm1s_ref, vm2s_ref,
                  d_ref, wglu_ref, bglu_ref, cw_ref, wssm_ref, watt_ref, wconv_ref, wout_ref,
                  op_ref, os_ref, *, n_prompt_tiles, tiles_per_seq):
    i = pl.program_id(0)
    is_prompt = i < n_prompt_tiles
    yraw = jnp.where(is_prompt, yrp_ref[...], yrs_ref[...])
    yb = jnp.where(is_prompt, ybp_ref[...], ybs_ref[...])
    y = jax.nn.gelu(yraw + d_ref[...] * _read_group(up_ref, us_ref, is_prompt))
    ya = y * jax.nn.sigmoid(_bdot(y.astype(BF16), wglu_ref[...]) + bglu_ref[...])

    vin = vin_ref[...]
    row = lax.broadcasted_iota(jnp.int32, vin.shape, 0)
    halo = jnp.where(i % tiles_per_seq == 0, 0.0, halo_ref[...])
    h1 = jnp.broadcast_to(halo[SUBLANES - 1:SUBLANES, :], vin.shape)
    h2 = jnp.broadcast_to(halo[SUBLANES - 2:SUBLANES - 1, :], vin.shape)
    vm1 = jnp.where(row == 0, h1, pltpu.roll(vin, 1, axis=0))
    vm2 = jnp.where(row == 0, h2, jnp.where(row == 1, h1, pltpu.roll(vin, 2, axis=0)))
    vm1 = jnp.where(is_prompt, vm1, vm1s_ref[...])
    vm2 = jnp.where(is_prompt, vm2, vm2s_ref[...])
    conv = vm2 * cw_ref[0:1, :] + vm1 * cw_ref[1:2, :] + vin * cw_ref[2:3, :]
    yc = cb_ref[...] * conv
    merged = (gate_ref[:, 0:D_MODEL] * _bdot(ya.astype(BF16), wssm_ref[...])
              + gate_ref[:, D_MODEL:2 * D_MODEL] * _bdot(yb, watt_ref[...])
              + gate_ref[:, 2 * D_MODEL:3 * D_MODEL] * _bdot(yc.astype(BF16), wconv_ref[...]))
    x_new = _read_group(xp_ref, xs_ref, is_prompt) + _bdot(merged.astype(BF16), wout_ref[...])
    _write_group(op_ref, os_ref, is_prompt, x_new)


def _merge(xp, xs, up, us, cb, vin, gates, yr_p, yr_s, yb_p, yb_s, vm1_s, vm2_s,
           d, wglu, bglu, cw, wssm, watt, wconv, wout, tiles_per_seq):
    tm = ROW_TILE
    tp, ts = xp.shape[0], xs.shape[0]
    npt = tp // tm
    row = lambda w: pl.BlockSpec((tm, w), lambda i: (i, 0))
    prow = lambda w: _group_specs(tm, w, npt)[0]
    srow = lambda w: _group_specs(tm, w, npt)[1]
    halo = pl.BlockSpec((SUBLANES, CONV_WIDTH), lambda i: (jnp.maximum(i * (tm // SUBLANES) - 1, 0), 0))
    return pl.pallas_call(
        functools.partial(_merge_kernel, n_prompt_tiles=npt, tiles_per_seq=tiles_per_seq),
        out_shape=[jax.ShapeDtypeStruct((tp, D_MODEL), F32), jax.ShapeDtypeStruct((ts, D_MODEL), F32)],
        grid=((tp + ts) // tm,),
        in_specs=[prow(D_MODEL), srow(D_MODEL), prow(SSM_WIDTH), srow(SSM_WIDTH),
                  row(CONV_WIDTH), row(CONV_WIDTH), halo,
                  row(N_BRANCH * D_MODEL),
                  prow(SSM_WIDTH), srow(SSM_WIDTH), prow(ATT_WIDTH), srow(ATT_WIDTH),
                  srow(CONV_WIDTH), srow(CONV_WIDTH),
                  _const_spec((1, SSM_WIDTH)), _const_spec((SSM_WIDTH, SSM_WIDTH)),
                  _const_spec((1, SSM_WIDTH)), _const_spec((CONV_K, CONV_WIDTH)),
                  _const_spec((SSM_WIDTH, D_MODEL)), _const_spec((ATT_WIDTH, D_MODEL)),
                  _const_spec((CONV_WIDTH, D_MODEL)), _const_spec((D_MODEL, D_MODEL))],
        out_specs=[prow(D_MODEL), srow(D_MODEL)],
        compiler_params=_cparams(("arbitrary",)),
        name="merge",
    )(xp, xs, up, us, cb, vin, vin, gates, yr_p, yr_s, yb_p, yb_s, vm1_s, vm2_s,
      d, wglu, bglu, cw, wssm, watt, wconv, wout)


def _swiglu(h, w1, w3, w2):
    a = _bdot(h, w1)
    b = _bdot(h, w3)
    return _bdot((jax.nn.silu(a) * b).astype(BF16), w2)


def _ffn_kernel(xp_ref, xs_ref, g_ref, w1_ref, w3_ref, w2_ref, op_ref, os_ref, *, n_prompt_tiles):
    is_prompt = pl.program_id(0) < n_prompt_tiles
    x = _read_group(xp_ref, xs_ref, is_prompt)
    h = _rms_rows(x, g_ref[...]).astype(BF16)
    _write_group(op_ref, os_ref, is_prompt, x + _swiglu(h, w1_ref[...], w3_ref[...], w2_ref[...]))


def _ffn(xp, xs, g, w1, w3, w2):
    tm = ROW_TILE
    tp, ts = xp.shape[0], xs.shape[0]
    npt = tp // tm
    d_ff = w1.shape[1]
    rows = _group_specs(tm, D_MODEL, npt)
    return pl.pallas_call(
        functools.partial(_ffn_kernel, n_prompt_tiles=npt),
        out_shape=[jax.ShapeDtypeStruct((tp, D_MODEL), F32), jax.ShapeDtypeStruct((ts, D_MODEL), F32)],
        grid=((tp + ts) // tm,),
        in_specs=rows + [_const_spec((1, D_MODEL)), _const_spec((D_MODEL, d_ff)),
                         _const_spec((D_MODEL, d_ff)), _const_spec((d_ff, D_MODEL))],
        out_specs=rows,
        compiler_params=_cparams(("arbitrary",)),
        name="ffn",
    )(xp, xs, g, w1, w3, w2)


def _router_kernel(xp_ref, xs_ref, g_ref, wr_ref, tri_ref, h_ref, meta_ref, gate_ref, cnt_ref, carry,
                   *, n_prompt_tiles):
    @pl.when(pl.program_id(0) == 0)
    def _():
        carry[...] = jnp.zeros(carry.shape, F32)

    x = _read_group(xp_ref, xs_ref, pl.program_id(0) < n_prompt_tiles)
    h = _rms_rows(x, g_ref[...])
    h_ref[...] = _pack_halves(h)
    logits = jnp.dot(h, wr_ref[...], preferred_element_type=F32, precision=lax.Precision.HIGHEST)
    lane = lax.broadcasted_iota(jnp.int32, logits.shape, 1)
    logits = jnp.where(lane < N_EXPERTS, logits, -jnp.inf)
    big = jnp.int32(logits.shape[1])
    m1 = jnp.max(logits, axis=-1, keepdims=True)
    i1 = jnp.min(jnp.where(logits == m1, lane, big), axis=-1, keepdims=True)
    rest = jnp.where(lane == i1, -jnp.inf, logits)
    m2 = jnp.max(rest, axis=-1, keepdims=True)
    i2 = jnp.min(jnp.where(rest == m2, lane, big), axis=-1, keepdims=True)
    e = jnp.exp(m2 - m1)
    g1 = 1.0 / (1.0 + e)
    g2 = e / (1.0 + e)
    o1 = lane == i1
    o2 = lane == i2
    chosen = jnp.where(o1 | o2, 1.0, 0.0)
    base = _bdot(tri_ref[...], chosen.astype(BF16)) + carry[...]
    r1 = jnp.sum(jnp.where(o1, base, 0.0), axis=-1, keepdims=True).astype(jnp.int32)
    r2 = jnp.sum(jnp.where(o2, base, 0.0), axis=-1, keepdims=True).astype(jnp.int32)
    carry[...] = carry[...] + jnp.sum(chosen, axis=0, keepdims=True)
    cnt_ref[...] = carry[...]
    meta_ref[...] = jnp.where(lane == 0, i1, jnp.where(lane == 1, i2,
                              jnp.where(lane == 2, r1, jnp.where(lane == 3, r2, 0))))
    gate_ref[...] = jnp.where(lane == 0, g1, jnp.where(lane == 1, g2, 0.0))


def _router(xp, xs, g, wr_pad):
    tm = ROW_TILE
    t = xp.shape[0] + xs.shape[0]
    npt = xp.shape[0] // tm
    lanes = wr_pad.shape[1]
    tri = jnp.tri(tm, k=-1, dtype=BF16)
    row = lambda w: pl.BlockSpec((tm, w), lambda i: (i, 0))
    return pl.pallas_call(
        functools.partial(_router_kernel, n_prompt_tiles=npt),
        out_shape=[jax.ShapeDtypeStruct((t, D_MODEL // 2), jnp.uint32),
                   jax.ShapeDtypeStruct((t, lanes), jnp.int32),
                   jax.ShapeDtypeStruct((t, lanes), F32),
                   jax.ShapeDtypeStruct((1, lanes), F32)],
        grid=(t // tm,),
        in_specs=_group_specs(tm, D_MODEL, npt) + [_const_spec((1, D_MODEL)), _const_spec((D_MODEL, lanes)),
                                                   _const_spec((tm, tm))],
        out_specs=[row(D_MODEL // 2), row(lanes), row(lanes), _const_spec((1, lanes))],
        scratch_shapes=[pltpu.VMEM((1, lanes), F32)],
        compiler_params=_cparams(("arbitrary",)),
        name="router",
    )(xp, xs, g, wr_pad, tri)


def _sc_mesh():
    return plsc.VectorSubcoreMesh(core_axis_name="c", subcore_axis_name="s")


def _sc_worker_base(per_worker):
    return (lax.axis_index("s") * SC_CORES + lax.axis_index("c")) * per_worker


def _sc_chunk_rows(n, row_bytes):
    assert n % (SC_WORKERS * SUBLANES) == 0
    per_worker = n // SC_WORKERS
    fits = [r for r in range(SUBLANES, per_worker + 1, SUBLANES)
            if per_worker % r == 0 and r * row_bytes <= SC_CHUNK_BYTES]
    return per_worker, fits[-1]


def _sc_scatter_rows(x, idx0, idx1, n_out):
    n, w = x.shape
    per_worker, chunk = _sc_chunk_rows(n, w * x.dtype.itemsize)

    @functools.partial(
        pl.kernel, mesh=_sc_mesh(), out_type=jax.ShapeDtypeStruct((n_out, w), x.dtype),
        scratch_types=[pltpu.VMEM((chunk,), jnp.int32), pltpu.VMEM((chunk,), jnp.int32),
                       pltpu.VMEM((chunk, w), x.dtype), pltpu.SemaphoreType.DMA])
    def scatter(x_hbm, i0_hbm, i1_hbm, out_hbm, i0_v, i1_v, rows_v, sem):
        start = _sc_worker_base(per_worker)

        @pl.loop(0, per_worker // chunk)
        def _(c):
            rows = pl.ds(pl.multiple_of(start + c * chunk, SUBLANES), chunk)
            pltpu.sync_copy(x_hbm.at[rows], rows_v)
            pltpu.sync_copy(i0_hbm.at[rows], i0_v)
            pltpu.sync_copy(i1_hbm.at[rows], i1_v)
            pltpu.async_copy(rows_v, out_hbm.at[i0_v], sem).wait()
            pltpu.async_copy(rows_v, out_hbm.at[i1_v], sem).wait()

    return scatter(x, idx0, idx1)


def _sc_gather_rows(table, idx):
    n, w = idx.shape[0], table.shape[1]
    per_worker, chunk = _sc_chunk_rows(n, w * table.dtype.itemsize)

    @functools.partial(
        pl.kernel, mesh=_sc_mesh(), out_type=jax.ShapeDtypeStruct((n, w), table.dtype),
        scratch_types=[pltpu.VMEM((chunk,), jnp.int32), pltpu.VMEM((chunk, w), table.dtype),
                       pltpu.SemaphoreType.DMA])
    def gather(table_hbm, idx_hbm, out_hbm, idx_v, rows_v, sem):
        start = _sc_worker_base(per_worker)

        @pl.loop(0, per_worker // chunk)
        def _(c):
            rows = pl.ds(pl.multiple_of(start + c * chunk, SUBLANES), chunk)
            pltpu.sync_copy(idx_hbm.at[rows], idx_v)
            pltpu.async_copy(table_hbm.at[idx_v], rows_v, sem).wait()
            pltpu.sync_copy(rows_v, out_hbm.at[rows])

    return gather(table, idx)


def _moe_kernel(be_ref, nv_ref, x_ref, w1_ref, w3_ref, w2_ref, o_ref):
    del be_ref

    @pl.when(pl.program_id(0) < nv_ref[0])
    def _():
        x = _unpack_halves(x_ref[...]).astype(BF16)
        o_ref[...] = _pack_halves(_swiglu(x, w1_ref[...], w3_ref[...], w2_ref[...]))


def _moe_blocks(xs, block_expert, n_valid, w1, w3, w2):
    cap = xs.shape[0]
    tb = MOE_TILE
    d_ff = w1.shape[2]
    grid_spec = pltpu.PrefetchScalarGridSpec(
        num_scalar_prefetch=2,
        grid=(cap // tb,),
        in_specs=[pl.BlockSpec((tb, D_MODEL // 2), lambda i, be, nv: (jnp.minimum(i, nv[0] - 1), 0)),
                  pl.BlockSpec((None, D_MODEL, d_ff), lambda i, be, nv: (be[i], 0, 0)),
                  pl.BlockSpec((None, D_MODEL, d_ff), lambda i, be, nv: (be[i], 0, 0)),
                  pl.BlockSpec((None, d_ff, D_MODEL), lambda i, be, nv: (be[i], 0, 0))],
        out_specs=pl.BlockSpec((tb, D_MODEL // 2), lambda i, be, nv: (i, 0)),
    )
    return pl.pallas_call(
        _moe_kernel,
        out_shape=jax.ShapeDtypeStruct((cap, D_MODEL // 2), jnp.uint32),
        grid_spec=grid_spec,
        compiler_params=_cparams(("arbitrary",)),
        name="moe",
    )(block_expert, n_valid, xs, w1, w3, w2)


def _combine_kernel(xp_ref, xs_ref, a0_ref, a1_ref, gate_ref, op_ref, os_ref, *, n_prompt_tiles):
    is_prompt = pl.program_id(0) < n_prompt_tiles
    gate = gate_ref[...]
    y = (_read_group(xp_ref, xs_ref, is_prompt)
         + gate[:, 0:1] * _unpack_halves(a0_ref[...]) + gate[:, 1:2] * _unpack_halves(a1_ref[...]))
    _write_group(op_ref, os_ref, is_prompt, y)


def _combine(xp, xs, picked, gate):
    tm = ROW_TILE
    tp, ts = xp.shape[0], xs.shape[0]
    nt = (tp + ts) // tm
    rows = _group_specs(tm, D_MODEL, tp // tm)
    return pl.pallas_call(
        functools.partial(_combine_kernel, n_prompt_tiles=tp // tm),
        out_shape=[jax.ShapeDtypeStruct((tp, D_MODEL), F32), jax.ShapeDtypeStruct((ts, D_MODEL), F32)],
        grid=(nt,),
        in_specs=rows + [pl.BlockSpec((tm, D_MODEL // 2), lambda i: (i, 0)),
                         pl.BlockSpec((tm, D_MODEL // 2), lambda i: (i + nt, 0)),
                         pl.BlockSpec((tm, gate.shape[1]), lambda i: (i, 0))],
        out_specs=rows,
        compiler_params=_cparams(("arbitrary",)),
        name="moe_combine",
    )(xp, xs, picked, picked, gate)


def _moe(xp, xs, g, wr, w1, w3, w2):
    t = xp.shape[0] + xs.shape[0]
    tb = MOE_TILE
    lanes = 128
    wr_pad = jnp.pad(wr, ((0, 0), (0, lanes - N_EXPERTS)))
    h, meta, gate, cnt = _router(xp, xs, g, wr_pad)
    counts = cnt[0, :N_EXPERTS].astype(jnp.int32)
    padded = (counts + tb - 1) // tb * tb
    pad_ends = jnp.cumsum(padded)
    pad_starts = pad_ends - padded
    experts = jnp.arange(N_EXPERTS, dtype=jnp.int32)
    slot = lambda e, r: jnp.sum(jnp.where(e[:, None] == experts, pad_starts, 0), axis=1) + r
    dest0 = slot(meta[:, 0], meta[:, 2])
    dest1 = slot(meta[:, 1], meta[:, 3])
    nb = -(-(t * TOP_K + N_EXPERTS * (tb - 1)) // tb)
    block_expert = jnp.minimum(
        jnp.sum(pad_ends[None, :] <= (jnp.arange(nb, dtype=jnp.int32) * tb)[:, None], axis=1),
        N_EXPERTS - 1).astype(jnp.int32)
    n_valid = (pad_ends[-1:] // tb).astype(jnp.int32)
    slots = _sc_scatter_rows(h, dest0, dest1, nb * tb)
    out = _moe_blocks(slots, block_expert, n_valid, w1, w3, w2)
    picked = _sc_gather_rows(out, jnp.concatenate([dest0, dest1]))
    return _combine(xp, xs, picked, gate)


def kernel(x_prompt, x_sample, cache_k, cache_v, page_table, state_ssm_re, state_ssm_im, state_conv,
           norm_mix_g, norm_ffn_g, w_in, ssm_a_re, ssm_a_im, ssm_log_dt, ssm_b_re, ssm_b_im,
           ssm_c_re, ssm_c_im, ssm_d, ssm_w_glu, ssm_b_glu, q_norm_g, k_norm_g,
           lambda_q1, lambda_k1, lambda_q2, lambda_k2, head_norm_g, conv_w,
           w_br_ssm, w_br_att, w_br_conv, w_out, ffn_w1, ffn_w3, ffn_w2,
           router_w, moe_w1, moe_w3, moe_w2):
    n_p, seq, _ = x_prompt.shape
    n_s, dec, _ = x_sample.shape
    depth = w_in.shape[0]
    tp = n_p * seq
    ts = n_s * dec
    assert seq % ROW_TILE == 0 and ts % ROW_TILE == 0 and seq >= CONV_K - 1
    pool, page = cache_k.shape[1], cache_k.shape[2]
    cache_kt = cache_k.reshape(depth, pool, page, QK_WIDTH).transpose(0, 1, 3, 2)
    cache_vr = cache_v.reshape(depth, pool, page * ATT_HEADS, ATT_V_DIM)
    seg = jnp.kron(jnp.eye(QK_WIDTH // ATT_HEAD_DIM, dtype=F32),
                   jnp.full((ATT_HEAD_DIM, ATT_HEAD_DIM), 1.0 / ATT_HEAD_DIM, F32)).astype(BF16)
    n_rep = QK_WIDTH // ATT_HEAD_DIM

    xp = x_prompt.reshape(tp, D_MODEL)
    xs = x_sample.reshape(ts, D_MODEL)
    w_in_bf = w_in.astype(BF16)
    all_tables = _s5_tables(ssm_a_re, ssm_a_im, ssm_log_dt, ssm_b_re, ssm_b_im, ssm_c_re, ssm_c_im, dec)
    kv_prompt = None
    moe_bf = [None] * 3
    srp, sip, cvp = [], [], []
    ks, vs, srs, sis, cvs = [], [], [], [], []
    for l in range(depth):
        lam_init = 0.8 - 0.6 * math.exp(-0.3 * l)
        q, kb, vb, cb, vin, gates, kt, vr, k_s, v_s, u_p, u_s = _inproj(
            xp, xs, norm_mix_g[l][None], w_in_bf,
            jnp.tile(q_norm_g[l], n_rep)[None], jnp.tile(k_norm_g[l], n_rep)[None], seg,
            l, depth, n_p, seq, kv_prompt)
        kv_prompt = (kt, vr)

        yr_p, yr_s, p_re, p_im, s_re, s_im = _s5(u_p, u_s, state_ssm_re[l], state_ssm_im[l], all_tables, l,
                                                  n_p, seq, n_s, dec)

        lp = jnp.stack([lambda_q1[l], lambda_k1[l], lambda_q2[l], lambda_k2[l]])
        hg = head_norm_g[l][None]
        casts = []
        if l + 1 < depth and (l + 1) % 2 == 1:
            casts += [(w, (l + 1) // 2, k) for k, w in ((0, moe_w1), (1, moe_w3))]
        if l % 2 == 1:
            casts += [(moe_w2, l // 2, 2)]
        yb_p, yb_s, *cast_out = _attention(
            q, kb, vb, q[tp:].reshape(n_s, dec, QK_WIDTH), kb[tp:].reshape(n_s, dec, QK_WIDTH),
            vb[tp:].reshape(n_s, dec, ATT_WIDTH), cache_kt, cache_vr, l, page_table, lp, hg,
            n_p, seq, lam_init, tuple(w[e].reshape(-1, w.shape[-1]) for w, e, _ in casts))
        yb_s = yb_s.reshape(ts, ATT_WIDTH)
        for c, (w, _, k) in zip(cast_out, casts):
            moe_bf[k] = c.reshape(w.shape[1:])

        ext_s = jnp.concatenate([state_conv[l], vin[tp:].reshape(n_s, dec, CONV_WIDTH)], axis=1)
        vm1_s = ext_s[:, 1:1 + dec].reshape(ts, CONV_WIDTH)
        vm2_s = ext_s[:, 0:dec].reshape(ts, CONV_WIDTH)
        xp, xs = _merge(xp, xs, u_p, u_s, cb, vin, gates, yr_p, yr_s, yb_p, yb_s, vm1_s, vm2_s,
                        ssm_d[l][None], ssm_w_glu[l].astype(BF16), ssm_b_glu[l][None], conv_w[l],
                        w_br_ssm[l].astype(BF16), w_br_att[l].astype(BF16), w_br_conv[l].astype(BF16),
                        w_out[l].astype(BF16), seq // ROW_TILE)

        i = l // 2
        if l % 2 == 0:
            xp, xs = _ffn(xp, xs, norm_ffn_g[l][None], ffn_w1[i].astype(BF16), ffn_w3[i].astype(BF16),
                          ffn_w2[i].astype(BF16))
        else:
            xp, xs = _moe(xp, xs, norm_ffn_g[l][None], router_w[i], *moe_bf)

        srp.append(p_re); sip.append(p_im)
        cvp.append(jnp.stack([vin[(b + 1) * seq - (CONV_K - 1):(b + 1) * seq] for b in range(n_p)]))
        ks.append(k_s.reshape(n_s, dec, ATT_HEADS, 2, ATT_HEAD_DIM))
        vs.append(v_s.reshape(n_s, dec, ATT_HEADS, ATT_V_DIM))
        srs.append(s_re); sis.append(s_im); cvs.append(ext_s[:, dec:])

    kt, vr = kv_prompt
    k_prompt = kt.reshape(depth, n_p, ATT_HEADS, 2, ATT_HEAD_DIM, seq).transpose(0, 1, 5, 2, 3, 4)
    v_prompt = vr.reshape(depth, n_p, seq, ATT_HEADS, ATT_V_DIM)
    return (xp.reshape(n_p, seq, D_MODEL), xs.reshape(n_s, dec, D_MODEL),
            k_prompt, v_prompt, jnp.stack(srp), jnp.stack(sip), jnp.stack(cvp),
            jnp.stack(ks), jnp.stack(vs), jnp.stack(srs), jnp.stack(sis), jnp.stack(cvs))
```
